```python
import math
import jax, jax.numpy as jnp
from jax import lax
import numpy as np

D_MODEL = 2048
BATCH = 2
SEQ = 4096
DEPTH = 2
DEC_BATCH = 128
DEC_SEQ = 4
PAST_LEN = 2048
PAGE_SIZE = 128

N_AB_LAYERS = (DEPTH + 1) // 2
N_C_LAYERS = DEPTH // 2
ALPHA = (2.0 * DEPTH) ** 0.25
BETA = (8.0 * DEPTH) ** -0.25
LN_EPS = 1e-5
C_A = D_MODEL // 2
HS_A = 64
H_A = C_A // HS_A
LORA_W = 64
LORA_A = 64
LORA_G = 160
A_SIZES = (C_A, C_A, C_A, LORA_W, LORA_A, LORA_G)
A_PROJ = sum(A_SIZES)
GN_EPS_A = 64e-5
C_B = D_MODEL - C_A
S5_P = 16
S5_GROUPS = C_B // S5_P
S5_N = 64
DH_C = 128
H_C = D_MODEL // DH_C
KVH_C = 4
GROUP_C = H_C // KVH_C
H_IDX = 16
D_IDX = 64
TOPK_MAX = 256
Q_BLOCK = 128
ROPE_THETA = 10000.0
C_SIZES = (H_C * DH_C, KVH_C * DH_C, KVH_C * DH_C, H_IDX * D_IDX, D_IDX, H_IDX)
C_PROJ = sum(C_SIZES)
N_EXPERTS = 64
TOP_K = 8
N_EXPERT_GROUPS = 8
TOPK_GROUPS = 4
D_EXPERT = 512
D_SHARED = 512
ROUTED_SCALE = 2.5
EXPERT_BLOCK = 128

kernel_name = 'rwkv7_s5_dsa_moe_deepnorm_step'


def split_cols(a, sizes):
    parts, start = [], 0
    for s in sizes:
        parts.append(a[..., start:start + s])
        start += s
    return parts


def layer_norm(x, g, b, eps=LN_EPS):
    xf = x.astype(jnp.float32)
    mu = xf.mean(-1, keepdims=True)
    var = jnp.square(xf - mu).mean(-1, keepdims=True)
    y = (xf - mu) * lax.rsqrt(var + eps) * g.astype(jnp.float32) + b.astype(jnp.float32)
    return y.astype(x.dtype)


def rope(x, pos):
    d = x.shape[-1]
    inv = ROPE_THETA ** (-jnp.arange(0, d, 2, dtype=jnp.float32) / d)
    ang = pos.astype(jnp.float32)[:, None] * inv[None, :]
    cos, sin = jnp.cos(ang)[:, None, :], jnp.sin(ang)[:, None, :]
    xf = x.astype(jnp.float32)
    x1, x2 = xf[..., : d // 2], xf[..., d // 2:]
    return jnp.concatenate([x1 * cos - x2 * sin, x2 * cos + x1 * sin], axis=-1).astype(x.dtype)


gather_rows = jax.vmap(lambda a, i: a[i])


def rwkv7_mix(r, k, v, lw, la, lg, wkv0, w0, w2, a0, a2, g2, k_k, k_a, r_k, lnx_g, lnx_b):
    f32 = jnp.float32
    Bn, T, _ = r.shape
    dt = r.dtype
    w_raw = w0 + jnp.tanh(lw) @ w2
    decay = jnp.exp(-jnp.exp(-jax.nn.softplus(-w_raw.astype(f32)) - 0.5))
    a = jax.nn.sigmoid((a0 + la @ a2).astype(f32))
    g = jax.nn.sigmoid(lg) @ g2

    def heads(t):
        return t.astype(f32).reshape(Bn, T, H_A, HS_A)

    kk = heads(k * k_k)
    kk = kk / jnp.maximum(jnp.linalg.norm(kk, axis=-1, keepdims=True), 1e-12)
    k_mod = heads(k.astype(f32) * (1.0 + (a - 1.0) * k_a.astype(f32)))
    rh, vh, wh, ah = heads(r), heads(v), heads(decay), heads(a)

    def step(S, inp):
        r_t, w_t, k_t, v_t, kk_t, a_t = inp
        sa = jnp.einsum('bhij,bhj->bhi', S, -kk_t)
        S = (S * w_t[:, :, None, :] + sa[:, :, :, None] * (kk_t * a_t)[:, :, None, :]
             + v_t[:, :, :, None] * k_t[:, :, None, :])
        return S, jnp.einsum('bhij,bhj->bhi', S, r_t)

    def tm(t):
        return jnp.moveaxis(t, 1, 0)

    S_T, o = lax.scan(step, wkv0.astype(f32), (tm(rh), tm(wh), tm(k_mod), tm(vh), tm(kk), tm(ah)))
    o = jnp.moveaxis(o, 0, 1)
    mu = o.mean(-1, keepdims=True)
    var = jnp.square(o - mu).mean(-1, keepdims=True)
    o = (o - mu) * lax.rsqrt(var + GN_EPS_A)
    o = o * lnx_g.astype(f32).reshape(H_A, HS_A) + lnx_b.astype(f32).reshape(H_A, HS_A)
    bonus = (rh * k_mod * r_k.astype(f32)).sum(-1, keepdims=True) * vh
    out = (o + bonus).reshape(Bn, T, C_A).astype(dt) * g
    return out, S_T


def complex_affine_combine(e1, e2):
    a1r, a1i, b1r, b1i = e1
    a2r, a2i, b2r, b2i = e2
    ar = a2r * a1r - a2i * a1i
    ai = a2r * a1i + a2i * a1r
    br = a2r * b1r - a2i * b1i + b2r
    bi = a2r * b1i + a2i * b1r + b2i
    return ar, ai, br, bi


def s5_mix(u, h0_re, h0_im, lam_re, lam_im, log_dt, b_re, b_im, c_re, c_im, d_skip, w_glu, b_glu):
    f32 = jnp.float32
    Bn, T, _ = u.shape
    lr = jnp.minimum(lam_re.astype(f32), -1e-4)
    li = lam_im.astype(f32)
    dt = jnp.exp(log_dt.astype(f32))[:, None]
    mag = jnp.exp(lr * dt)
    lbr, lbi = mag * jnp.cos(li * dt), mag * jnp.sin(li * dt)
    den = lr * lr + li * li
    pr, pim = lbr - 1.0, lbi
    fr = (pr * lr + pim * li) / den
    fi = (pim * lr - pr * li) / den
    br, bi = b_re.astype(f32), b_im.astype(f32)
    bbr = fr[..., None] * br - fi[..., None] * bi
    bbi = fr[..., None] * bi + fi[..., None] * br
    ug = u.astype(f32).reshape(Bn, T, S5_GROUPS, S5_P)
    bur = jnp.einsum('btgp,gnp->btgn', ug, bbr)
    bui = jnp.einsum('btgp,gnp->btgn', ug, bbi)
    h0r, h0i = h0_re.astype(f32), h0_im.astype(f32)
    bur = bur.at[:, 0].add(lbr * h0r - lbi * h0i)
    bui = bui.at[:, 0].add(lbr * h0i + lbi * h0r)
    ar = jnp.broadcast_to(lbr, bur.shape)
    ai = jnp.broadcast_to(lbi, bui.shape)
    _, _, hr, hi = lax.associative_scan(complex_affine_combine, (ar, ai, bur, bui), axis=1)
    y = (jnp.einsum('gpn,btgn->btgp', c_re.astype(f32), hr)
         - jnp.einsum('gpn,btgn->btgp', c_im.astype(f32), hi)
         + d_skip.astype(f32) * ug)
    z = jax.nn.gelu(y.reshape(Bn, T, C_B)).astype(u.dtype)
    out = z * jax.nn.sigmoid(z @ w_glu + b_glu)
    return out, hr[:, -1], hi[:, -1]


def ab_mixer(x, shift0, wkv0, ssm_re0, ssm_im0, w_in, mu, w0, w2, a0, a2, g2, k_k, k_a, r_k, lnx_g, lnx_b,
             lam_re, lam_im, log_dt, b_re, b_im, c_re, c_im, d_skip, w_glu, b_glu, w_out):
    proj = x @ w_in
    pa, u = proj[..., :A_PROJ], proj[..., A_PROJ:]
    prev = jnp.concatenate([shift0[:, None, :].astype(pa.dtype), pa[:, :-1]], axis=1)
    pa_mix = pa + (prev - pa) * mu
    r, k, v, lw, la, lg = split_cols(pa_mix, A_SIZES)
    o_a, wkv_new = rwkv7_mix(r, k, v, lw, la, lg, wkv0, w0, w2, a0, a2, g2, k_k, k_a, r_k, lnx_g, lnx_b)
    o_b, ssm_re_new, ssm_im_new = s5_mix(u, ssm_re0, ssm_im0, lam_re, lam_im, log_dt, b_re, b_im,
                                         c_re, c_im, d_skip, w_glu, b_glu)
    out = jnp.concatenate([o_a, o_b.astype(o_a.dtype)], axis=-1) @ w_out
    return out, pa[:, -1], wkv_new, ssm_re_new, ssm_im_new


def c_project(x, pos, w_in, kidx_g, kidx_b):
    Bn, T, _ = x.shape
    q, k, v, qi, ki, wi = split_cols(x @ w_in, C_SIZES)
    q = rope(q.reshape(Bn, T, H_C, DH_C), pos)
    k = rope(k.reshape(Bn, T, KVH_C, DH_C), pos)
    v = v.reshape(Bn, T, KVH_C, DH_C)
    qi = rope(qi.reshape(Bn, T, H_IDX, D_IDX), pos)
    ki = rope(layer_norm(ki, kidx_g, kidx_b)[:, :, None, :], pos)[:, :, 0]
    wi = wi * (H_IDX ** -0.5)
    return q, k, v, qi, ki, wi


def indexer_scores(qi, wi, ki):
    f32 = jnp.float32
    dots = jnp.einsum('bthd,bsd->bths', qi.astype(f32), ki.astype(f32)) * (D_IDX ** -0.5)
    return jnp.einsum('bth,bths->bts', wi.astype(f32), jax.nn.relu(dots))


def attend_selected(q, k_sel, v_sel, valid):
    Bn, T = q.shape[:2]
    qg = q.reshape(Bn, T, KVH_C, GROUP_C, DH_C)
    s = jnp.einsum('btkgd,btskd->btkgs', qg, k_sel).astype(jnp.float32) * (DH_C ** -0.5)
    s = jnp.where(valid[:, :, None, None, :], s, -jnp.inf)
    p = jax.nn.softmax(s, axis=-1).astype(v_sel.dtype)
    o = jnp.einsum('btkgs,btskd->btkgd', p, v_sel)
    return o.reshape(Bn, T, H_C * DH_C)


def c_mixer_prompt(x, w_in, kidx_g, kidx_b, w_out):
    Bn, T, _ = x.shape
    pos = jnp.arange(T, dtype=jnp.int32)
    q, k, v, qi, ki, wi = c_project(x, pos, w_in, kidx_g, kidx_b)
    topk = min(TOPK_MAX, T // 4)
    n_blocks = T // Q_BLOCK

    def block(args):
        q_b, qi_b, wi_b, pos_b = args
        scores = indexer_scores(qi_b, wi_b, ki)
        causal = pos[None, :] <= pos_b[:, None]
        scores = jnp.where(causal[None], scores, -jnp.inf)
        _, idx = lax.top_k(scores, topk)
        valid = idx <= pos_b[None, :, None]
        return attend_selected(q_b, gather_rows(k, idx), gather_rows(v, idx), valid)

    def to_blocks(a):
        return jnp.moveaxis(a.reshape((Bn, n_blocks, Q_BLOCK) + a.shape[2:]), 1, 0)

    o = lax.map(block, (to_blocks(q), to_blocks(qi), to_blocks(wi), pos.reshape(n_blocks, Q_BLOCK)))
    o = jnp.moveaxis(o, 0, 1).reshape(Bn, T, H_C * DH_C)
    return o @ w_out, k, v, ki


def c_mixer_sample(x, cache_k, cache_v, cache_kidx, page_table, w_in, kidx_g, kidx_b, w_out):
    Bn, T, _ = x.shape
    n_pages = page_table.shape[1]
    page = cache_k.shape[1]
    past = n_pages * page
    L = past + T
    pos = past + jnp.arange(T, dtype=jnp.int32)
    q, k, v, qi, ki, wi = c_project(x, pos, w_in, kidx_g, kidx_b)
    ki_past = cache_kidx[page_table].reshape(Bn, past, D_IDX).astype(ki.dtype)
    ki_all = jnp.concatenate([ki_past, ki], axis=1)
    topk = min(TOPK_MAX, L // 4)
    scores = indexer_scores(qi, wi, ki_all)
    causal = jnp.arange(L, dtype=jnp.int32)[None, :] <= pos[:, None]
    scores = jnp.where(causal[None], scores, -jnp.inf)
    _, idx = lax.top_k(scores, topk)
    valid = idx <= pos[None, :, None]
    is_new = idx >= past
    s_past = jnp.minimum(idx, past - 1)
    phys = jax.vmap(lambda pt, s: pt[s // page])(page_table, s_past)
    rows = phys * page + s_past % page
    s_new = jnp.clip(idx - past, 0, T - 1)

    def select(pool, new):
        flat = pool.reshape((-1,) + pool.shape[2:])
        return jnp.where(is_new[..., None, None], gather_rows(new, s_new), flat[rows].astype(new.dtype))

    o = attend_selected(q, select(cache_k, k), select(cache_v, v), valid)
    return o @ w_out, k, v, ki


def moe_ffn(x2d, w_router, b_router, w_gate, w_up, w_down, ws_gate, ws_up, ws_down):
    f32 = jnp.float32
    T = x2d.shape[0]
    scores = jax.nn.sigmoid(jnp.dot(x2d.astype(f32), w_router.astype(f32)))
    biased = scores + b_router.astype(f32)
    per_group = N_EXPERTS // N_EXPERT_GROUPS
    grp_score = lax.top_k(biased.reshape(T, N_EXPERT_GROUPS, per_group), 2)[0].sum(-1)
    _, top_grp = lax.top_k(grp_score, TOPK_GROUPS)
    keep = jnp.any(top_grp[:, :, None] == jnp.arange(N_EXPERT_GROUPS)[None, None, :], axis=1)
    biased = jnp.where(jnp.repeat(keep, per_group, axis=1), biased, -jnp.inf)
    _, top_e = lax.top_k(biased, TOP_K)
    top_s = jnp.take_along_axis(scores, top_e, axis=1)
    gates = top_s / top_s.sum(-1, keepdims=True) * ROUTED_SCALE
    n_asg = T * TOP_K
    e_flat = top_e.reshape(n_asg).astype(jnp.int32)
    tok_flat = jnp.arange(n_asg, dtype=jnp.int32) // TOP_K
    g_flat = gates.reshape(n_asg)
    order = jnp.argsort(e_flat)
    e_s, tok_s, g_s = e_flat[order], tok_flat[order], g_flat[order]
    counts = jnp.bincount(e_flat, length=N_EXPERTS).astype(jnp.int32)
    starts = jnp.cumsum(counts) - counts
    padded = (counts + EXPERT_BLOCK - 1) // EXPERT_BLOCK * EXPERT_BLOCK
    pad_ends = jnp.cumsum(padded)
    pad_starts = pad_ends - padded
    dest = pad_starts[e_s] + jnp.arange(n_asg, dtype=jnp.int32) - starts[e_s]
    n_blocks = -(-(n_asg + N_EXPERTS * (EXPERT_BLOCK - 1)) // EXPERT_BLOCK)
    n_rows = n_blocks * EXPERT_BLOCK
    buf_tok = jnp.zeros((n_rows,), jnp.int32).at[dest].set(tok_s)
    buf_gate = jnp.zeros((n_rows,), f32).at[dest].set(g_s)
    blk_e = jnp.searchsorted(pad_ends, jnp.arange(n_blocks, dtype=jnp.int32) * EXPERT_BLOCK, side='right')
    blk_e = jnp.minimum(blk_e, N_EXPERTS - 1)

    def expert_block(args):
        tok, gate, e = args
        xb = x2d[tok]
        h = jax.nn.silu(xb @ w_gate[e]) * (xb @ w_up[e])
        return (h @ w_down[e]) * gate[:, None].astype(x2d.dtype)

    out = lax.map(expert_block, (buf_tok.reshape(n_blocks, EXPERT_BLOCK),
                                 buf_gate.reshape(n_blocks, EXPERT_BLOCK), blk_e))
    routed = jax.ops.segment_sum(out.reshape(n_rows, x2d.shape[1]), buf_tok, num_segments=T)
    shared = (jax.nn.silu(x2d @ ws_gate) * (x2d @ ws_up)) @ ws_down
    return routed + shared


def setup_inputs(seed: int = 0) -> dict:
    key = jax.random.key(seed)
    ks = iter(jax.random.split(key, 64))
    f32 = jnp.float32

    def nrm(shape, scale=1.0):
        return jax.random.normal(next(ks), shape, f32) * scale

    def unif(shape, lo, hi):
        return jax.random.uniform(next(ks), shape, f32, lo, hi)

    n_pages = PAST_LEN // PAGE_SIZE
    n_used = DEC_BATCH * n_pages
    n_pool = n_used + max(1, n_used // 4)
    page_table = jax.random.permutation(next(ks), n_pool)[:n_used].reshape(DEC_BATCH, n_pages).astype(jnp.int32)
    NA, NC = N_AB_LAYERS, N_C_LAYERS
    D = D_MODEL
    return {
        'x_prompt': nrm((BATCH, SEQ, D)),
        'x_sample': nrm((DEC_BATCH, DEC_SEQ, D)),
        'state_shift': nrm((NA, DEC_BATCH, A_PROJ)),
        'state_wkv': nrm((NA, DEC_BATCH, H_A, HS_A, HS_A), 0.3),
        'state_ssm_re': nrm((NA, DEC_BATCH, S5_GROUPS, S5_N), 0.3),
        'state_ssm_im': nrm((NA, DEC_BATCH, S5_GROUPS, S5_N), 0.3),
        'cache_k': nrm((NC, n_pool, PAGE_SIZE, KVH_C, DH_C)),
        'cache_v': nrm((NC, n_pool, PAGE_SIZE, KVH_C, DH_C)),
        'cache_kidx': nrm((NC, n_pool, PAGE_SIZE, D_IDX)),
        'page_table': page_table,
        'ab_w_in': nrm((NA, D, A_PROJ + C_B), D ** -0.5),
        'ab_mu': unif((NA, A_PROJ), 0.0, 1.0),
        'rwkv_w0': unif((NA, C_A), -5.0, 1.0),
        'rwkv_w2': nrm((NA, LORA_W, C_A), 0.1),
        'rwkv_a0': nrm((NA, C_A), 0.5),
        'rwkv_a2': nrm((NA, LORA_A, C_A), 0.1),
        'rwkv_g2': nrm((NA, LORA_G, C_A), LORA_G ** -0.5),
        'rwkv_k_k': 0.85 + nrm((NA, C_A), 0.05),
        'rwkv_k_a': 1.0 + nrm((NA, C_A), 0.05),
        'rwkv_r_k': nrm((NA, H_A, HS_A), 0.1),
        'rwkv_ln_g': 1.0 + nrm((NA, C_A), 0.05),
        'rwkv_ln_b': nrm((NA, C_A), 0.02),
        's5_lam_re': -0.5 + nrm((NA, S5_GROUPS, S5_N), 0.01),
        's5_lam_im': jnp.pi * jnp.arange(S5_N, dtype=f32) + nrm((NA, S5_GROUPS, S5_N), 0.01),
        's5_log_dt': unif((NA, S5_GROUPS), math.log(1e-3), math.log(1e-1)),
        's5_b_re': nrm((NA, S5_GROUPS, S5_N, S5_P), (2 * S5_P) ** -0.5),
        's5_b_im': nrm((NA, S5_GROUPS, S5_N, S5_P), (2 * S5_P) ** -0.5),
        's5_c_re': nrm((NA, S5_GROUPS, S5_P, S5_N), (2 * S5_N) ** -0.5),
        's5_c_im': nrm((NA, S5_GROUPS, S5_P, S5_N), (2 * S5_N) ** -0.5),
        's5_d': nrm((NA, S5_GROUPS, S5_P), 0.5),
        's5_w_glu': nrm((NA, C_B, C_B), C_B ** -0.5),
        's5_b_glu': nrm((NA, C_B), 0.02),
        'ab_w_out': nrm((NA, D, D), BETA * D ** -0.5),
        'c_w_in': nrm((NC, D, C_PROJ), D ** -0.5),
        'c_kidx_ln_g': 1.0 + nrm((NC, D_IDX), 0.05),
        'c_kidx_ln_b': nrm((NC, D_IDX), 0.02),
        'c_w_out': nrm((NC, D, D), BETA * D ** -0.5),
        'ln_mix_g': 1.0 + nrm((DEPTH, D), 0.05),
        'ln_mix_b': nrm((DEPTH, D), 0.02),
        'ln_ffn_g': 1.0 + nrm((DEPTH, D), 0.05),
        'ln_ffn_b': nrm((DEPTH, D), 0.02),
        'moe_w_router': nrm((DEPTH, D, N_EXPERTS), D ** -0.5),
        'moe_b_router': nrm((DEPTH, N_EXPERTS), 0.01),
        'moe_w_gate': nrm((DEPTH, N_EXPERTS, D, D_EXPERT), D ** -0.5),
        'moe_w_up': nrm((DEPTH, N_EXPERTS, D, D_EXPERT), D ** -0.5),
        'moe_w_down': nrm((DEPTH, N_EXPERTS, D_EXPERT, D), BETA * D_EXPERT ** -0.5),
        'moe_ws_gate': nrm((DEPTH, D, D_SHARED), D ** -0.5),
        'moe_ws_up': nrm((DEPTH, D, D_SHARED), D ** -0.5),
        'moe_ws_down': nrm((DEPTH, D_SHARED, D), BETA * D_SHARED ** -0.5),
    }


def reference(x_prompt, x_sample, state_shift, state_wkv, state_ssm_re, state_ssm_im,
              cache_k, cache_v, cache_kidx, page_table,
              ab_w_in, ab_mu, rwkv_w0, rwkv_w2, rwkv_a0, rwkv_a2, rwkv_g2, rwkv_k_k, rwkv_k_a, rwkv_r_k,
              rwkv_ln_g, rwkv_ln_b, s5_lam_re, s5_lam_im, s5_log_dt, s5_b_re, s5_b_im, s5_c_re, s5_c_im,
              s5_d, s5_w_glu, s5_b_glu, ab_w_out,
              c_w_in, c_kidx_ln_g, c_kidx_ln_b, c_w_out,
              ln_mix_g, ln_mix_b, ln_ffn_g, ln_ffn_b,
              moe_w_router, moe_b_router, moe_w_gate, moe_w_up, moe_w_down, moe_ws_gate, moe_ws_up, moe_ws_down):
    f32 = jnp.float32
    y_p, y_s = x_prompt, x_sample
    Bp = x_prompt.shape[0]
    kp_l, vp_l, kip_l, ks_l, vs_l, kis_l = [], [], [], [], [], []
    shp_l, shs_l, wkvp_l, wkvs_l, srp_l, sip_l, srs_l, sis_l = [], [], [], [], [], [], [], []
    for layer in range(DEPTH):
        li = layer // 2
        if layer % 2 == 0:
            ab_params = (ab_w_in[li], ab_mu[li], rwkv_w0[li], rwkv_w2[li], rwkv_a0[li], rwkv_a2[li], rwkv_g2[li],
                         rwkv_k_k[li], rwkv_k_a[li], rwkv_r_k[li], rwkv_ln_g[li], rwkv_ln_b[li],
                         s5_lam_re[li], s5_lam_im[li], s5_log_dt[li], s5_b_re[li], s5_b_im[li],
                         s5_c_re[li], s5_c_im[li], s5_d[li], s5_w_glu[li], s5_b_glu[li], ab_w_out[li])
            zero_shift = jnp.zeros((Bp, A_PROJ), x_prompt.dtype)
            zero_wkv = jnp.zeros((Bp, H_A, HS_A, HS_A), f32)
            zero_ssm = jnp.zeros((Bp, S5_GROUPS, S5_N), f32)
            h_p, sh_p, wkv_p, sr_p, si_p = ab_mixer(y_p, zero_shift, zero_wkv, zero_ssm, zero_ssm, *ab_params)
            h_s, sh_s, wkv_s, sr_s, si_s = ab_mixer(y_s, state_shift[li], state_wkv[li], state_ssm_re[li],
                                                    state_ssm_im[li], *ab_params)
            shp_l.append(sh_p); shs_l.append(sh_s); wkvp_l.append(wkv_p); wkvs_l.append(wkv_s)
            srp_l.append(sr_p); sip_l.append(si_p); srs_l.append(sr_s); sis_l.append(si_s)
        else:
            c_params = (c_w_in[li], c_kidx_ln_g[li], c_kidx_ln_b[li], c_w_out[li])
            h_p, k_p, v_p, ki_p = c_mixer_prompt(y_p, *c_params)
            h_s, k_s, v_s, ki_s = c_mixer_sample(y_s, cache_k[li], cache_v[li], cache_kidx[li], page_table, *c_params)
            kp_l.append(k_p); vp_l.append(v_p); kip_l.append(ki_p)
            ks_l.append(k_s); vs_l.append(v_s); kis_l.append(ki_s)
        y_p = layer_norm(ALPHA * y_p + h_p, ln_mix_g[layer], ln_mix_b[layer])
        y_s = layer_norm(ALPHA * y_s + h_s, ln_mix_g[layer], ln_mix_b[layer])
        moe_params = (moe_w_router[layer], moe_b_router[layer], moe_w_gate[layer], moe_w_up[layer],
                      moe_w_down[layer], moe_ws_gate[layer], moe_ws_up[layer], moe_ws_down[layer])
        f_p = moe_ffn(y_p.reshape(-1, D_MODEL), *moe_params).reshape(y_p.shape)
        f_s = moe_ffn(y_s.reshape(-1, D_MODEL), *moe_params).reshape(y_s.shape)
        y_p = layer_norm(ALPHA * y_p + f_p, ln_ffn_g[layer], ln_ffn_b[layer])
        y_s = layer_norm(ALPHA * y_s + f_s, ln_ffn_g[layer], ln_ffn_b[layer])
    new_k_prompt, new_v_prompt, new_kidx_prompt = jnp.stack(kp_l), jnp.stack(vp_l), jnp.stack(kip_l)
    new_k_sample, new_v_sample, new_kidx_sample = jnp.stack(ks_l), jnp.stack(vs_l), jnp.stack(kis_l)
    new_shift_prompt, new_shift_sample = jnp.stack(shp_l), jnp.stack(shs_l)
    new_wkv_prompt, new_wkv_sample = jnp.stack(wkvp_l), jnp.stack(wkvs_l)
    new_ssm_re_prompt, new_ssm_im_prompt = jnp.stack(srp_l), jnp.stack(sip_l)
    new_ssm_re_sample, new_ssm_im_sample = jnp.stack(srs_l), jnp.stack(sis_l)
    return (y_p, y_s, new_k_prompt, new_v_prompt, new_kidx_prompt, new_k_sample, new_v_sample, new_kidx_sample,
            new_shift_prompt, new_shift_sample, new_wkv_prompt, new_wkv_sample,
            new_ssm_re_prompt, new_ssm_im_prompt, new_ssm_re_sample, new_ssm_im_sample)
```

```python
import functools
import math

import jax
import jax.numpy as jnp
from jax import lax
from jax.experimental import pallas as pl
from jax.experimental.pallas import tpu as pltpu

F32, BF16, I32 = jnp.float32, jnp.bfloat16, jnp.int32

D_MODEL = 2048
DEPTH = 2
ALPHA = (2.0 * DEPTH) ** 0.25
LN_EPS = 1e-5
C_A = 1024
HS_A = 64
H_A = 16
LORA_W, LORA_A, LORA_G = 64, 64, 160
A_PROJ = 3 * C_A + LORA_W + LORA_A + LORA_G
GN_EPS_A = 64e-5
C_B = 1024
S5_P = 16
S5_GROUPS = 64
S5_N = 64
S5_STATE = S5_GROUPS * S5_N
DH_C = 128
H_C = 16
KVH_C = 4
GROUP_C = 4
H_IDX = 16
D_IDX = 64
TOPK_MAX = 256
ROPE_THETA = 10000.0
N_EXPERTS = 64
TOP_K = 8
N_EXPERT_GROUPS = 8
TOPK_GROUPS = 4
D_EXPERT = 512
ROUTED_SCALE = 2.5

LANES = 128
SUBLANES = 8
VMEM_LIMIT = 56 * 1024 * 1024

LORA_PAD = 512
EXP_M05 = math.exp(-0.5)


def _cparams(sem):
    return pltpu.CompilerParams(dimension_semantics=sem, vmem_limit_bytes=VMEM_LIMIT)


def _dot(a, b):
    return jnp.dot(a, b, preferred_element_type=F32)


def _split3(a):
    a1 = a.astype(BF16)
    r1 = a - a1.astype(F32)
    a2 = r1.astype(BF16)
    a3 = (r1 - a2.astype(F32)).astype(BF16)
    return a1, a2, a3


def _dot_sel(a, sel_bf16):
    a1, a2, a3 = _split3(a)
    return _dot(a1, sel_bf16) + _dot(a2, sel_bf16) + _dot(a3, sel_bf16)


def _dot3(a, b):
    a1 = a.astype(BF16)
    a2 = (a - a1.astype(F32)).astype(BF16)
    b1 = b.astype(BF16)
    b2 = (b - b1.astype(F32)).astype(BF16)
    return _dot(a1, b1) + _dot(a1, b2) + _dot(a2, b1)


def _mm_kernel(*refs, n_lhs):
    o_ref = refs[-1]
    acc = None
    for i in range(n_lhs):
        d = _dot(refs[i][...], refs[n_lhs + i][...])
        acc = d if acc is None else acc + d
    o_ref[...] = acc.astype(o_ref.dtype)


def matmul(xs, ws, tm, tn, out_dtype=F32, name="mm"):
    m = xs[0].shape[0]
    n = ws[0].shape[1]
    assert m % tm == 0 and n % tn == 0, (m, tm, n, tn)
    in_specs = ([pl.BlockSpec((tm, x.shape[1]), lambda j, i: (i, 0)) for x in xs]
                + [pl.BlockSpec((w.shape[0], tn), lambda j, i: (0, j)) for w in ws])
    return pl.pallas_call(
        functools.partial(_mm_kernel, n_lhs=len(xs)),
        grid=(n // tn, m // tm),
        in_specs=in_specs,
        out_specs=pl.BlockSpec((tm, tn), lambda j, i: (i, j)),
        out_shape=jax.ShapeDtypeStruct((m, n), out_dtype),
        compiler_params=_cparams(("parallel", "parallel")),
        name=name,
    )(*xs, *ws)


def _add_ln_kernel(x_ref, h_ref, g_ref, b_ref, o_ref, ob_ref):
    z = ALPHA * x_ref[...] + h_ref[...]
    mu = jnp.mean(z, axis=-1, keepdims=True)
    d = z - mu
    var = jnp.mean(d * d, axis=-1, keepdims=True)
    y = d * lax.rsqrt(var + LN_EPS) * g_ref[...] + b_ref[...]
    o_ref[...] = y
    ob_ref[...] = y.astype(BF16)


def add_ln(x, h, g, b, tm):
    t, d = x.shape
    row = pl.BlockSpec((tm, d), lambda i: (i, 0))
    vec = pl.BlockSpec((1, d), lambda i: (0, 0))
    return pl.pallas_call(
        _add_ln_kernel,
        grid=(t // tm,),
        in_specs=[row, row, vec, vec],
        out_specs=[row, row],
        out_shape=[jax.ShapeDtypeStruct((t, d), F32), jax.ShapeDtypeStruct((t, d), BF16)],
        compiler_params=_cparams(("parallel",)),
        name="add_ln",
    )(x, h, g.reshape(1, d), b.reshape(1, d))


def _rwkv_prep_kernel(rkv_ref, rkvp_ref, lo_ref, lop_ref, mu1_ref, mu2_ref, w0_ref, w2_ref, a0_ref, a2_ref,
                      g2_ref, kk_ref, ka_ref, rk_ref, gsel_ref, gselt_ref,
                      r_o, w_o, k_o, kk_o, b_o, vt_o, g_o, bonus_o):
    x = rkv_ref[...]
    m = x + (rkvp_ref[...] - x) * mu1_ref[...]
    r = m[:, :C_A]
    k = m[:, C_A:2 * C_A]
    v = m[:, 2 * C_A:]
    l = lo_ref[...]
    lm = l + (lop_ref[...] - l) * mu2_ref[...]
    lw = lm[:, :LANES]
    la = lm[:, LANES:2 * LANES]
    lg = lm[:, 2 * LANES:]
    w_raw = w0_ref[...] + _dot3(jnp.tanh(lw), w2_ref[...])
    decay = jnp.exp(-EXP_M05 * jax.nn.sigmoid(w_raw))
    a = jax.nn.sigmoid(a0_ref[...] + _dot3(la, a2_ref[...]))
    g = _dot3(jax.nn.sigmoid(lg), g2_ref[...])
    gsel = gsel_ref[...]
    gselt = gselt_ref[...]
    kk = k * kk_ref[...]
    nrm = jnp.sqrt(_dot_sel(kk * kk, gsel))
    inv = 1.0 / jnp.maximum(nrm, 1e-12)
    kk = kk * _dot_sel(inv, gselt)
    k_mod = k * (1.0 + (a - 1.0) * ka_ref[...])
    sb = _dot_sel(r * k_mod * rk_ref[...], gsel)
    bonus_o[...] = _dot_sel(sb, gselt) * v
    g_o[...] = g
    b = kk * a
    for h in range(H_A):
        sl = slice(h * HS_A, (h + 1) * HS_A)
        r_o[h] = r[:, sl]
        w_o[h] = decay[:, sl]
        k_o[h] = k_mod[:, sl]
        kk_o[h] = kk[:, sl]
        b_o[h] = b[:, sl]
    tm = v.shape[0]
    vt_o[...] = v.T.reshape(H_A, HS_A, tm)


def rwkv_prep(rkv, rkv_prev, lora, lora_prev, p, tm):
    t = rkv.shape[0]
    row = lambda w: pl.BlockSpec((tm, w), lambda i: (i, 0))
    full = lambda a: pl.BlockSpec(a.shape, lambda i: (0,) * a.ndim)
    consts = [p["mu_rkv"], p["mu_lora"], p["w0"], p["w2"], p["a0"], p["a2"], p["g2"], p["k_k"], p["k_a"], p["r_k"],
              p["gsel"], p["gselt"]]
    hm = pl.BlockSpec((H_A, tm, HS_A), lambda i: (0, i, 0))
    hm_shape = jax.ShapeDtypeStruct((H_A, t, HS_A), F32)
    return pl.pallas_call(
        _rwkv_prep_kernel,
        grid=(t // tm,),
        in_specs=[row(3 * C_A), row(3 * C_A), row(LORA_PAD), row(LORA_PAD)] + [full(c) for c in consts],
        out_specs=[hm, hm, hm, hm, hm, pl.BlockSpec((H_A, HS_A, tm), lambda i: (0, 0, i)), row(C_A), row(C_A)],
        out_shape=[hm_shape] * 5 + [jax.ShapeDtypeStruct((H_A, HS_A, t), F32),
                                    jax.ShapeDtypeStruct((t, C_A), F32), jax.ShapeDtypeStruct((t, C_A), F32)],
        compiler_params=_cparams(("parallel",)),
        name="rwkv_prep",
    )(rkv, rkv_prev, lora, lora_prev, *consts)


def _rwkv_scan_kernel(r_ref, w_ref, k_ref, kk_ref, b_ref, vt_ref, s0_ref, ot_ref, sout_ref, s_scr,
                      *, seq_len, blk, spb):
    long_mode = seq_len >= blk
    steps = blk if long_mode else seq_len
    lane = lax.broadcasted_iota(I32, (1, 1, blk), 2)
    c = pl.program_id(1)

    if long_mode:
        ot_ref[...] = jnp.zeros(ot_ref.shape, F32)
    else:
        @pl.when(c == 0)
        def _():
            ot_ref[...] = jnp.zeros(ot_ref.shape, F32)

    def run_seq(row_base):
        def step(t, carry):
            idx = row_base + t
            rr = r_ref[:, pl.ds(idx, 1), :]
            ww = w_ref[:, pl.ds(idx, 1), :]
            kr = k_ref[:, pl.ds(idx, 1), :]
            kkr = kk_ref[:, pl.ds(idx, 1), :]
            br = b_ref[:, pl.ds(idx, 1), :]
            hit = lane == idx
            vcol = jnp.sum(jnp.where(hit, vt_ref[...], 0.0), axis=-1, keepdims=True)
            s = s_scr[...]
            sa = -jnp.sum(s * kkr, axis=-1, keepdims=True)
            s = s * ww + sa * br + vcol * kr
            s_scr[...] = s
            o = jnp.sum(s * rr, axis=-1, keepdims=True)
            ot_ref[...] = jnp.where(hit, o, ot_ref[...])
            return carry

        lax.fori_loop(0, steps, step, 0)

    if long_mode:
        @pl.when(c == 0)
        def _():
            s_scr[...] = s0_ref[0]

        run_seq(0)
        sout_ref[0] = s_scr[...]
    else:
        def seq_body(q, carry):
            s_scr[...] = s0_ref[q]
            run_seq((c * spb + q) * seq_len)
            sout_ref[q] = s_scr[...]
            return carry

        lax.fori_loop(0, spb, seq_body, 0)


RWKV_SPB = 8


def rwkv_scan(hm, vt, s0, *, row0, n_seq, seq_len, blk):
    long_mode = seq_len >= blk
    assert row0 % blk == 0
    b0 = row0 // blk
    if long_mode:
        cps = seq_len // blk
        spb = 1
        grid = (n_seq, cps)
        rmap = lambda s, c: (0, b0 + s * cps + c, 0)
        vmap_ = lambda s, c: (0, 0, b0 + s * cps + c)
        smap = lambda s, c: (s, 0, 0, 0)
        omap = lambda s, c: (0, 0, s * cps + c)
    else:
        nsb = blk // seq_len
        spb = min(RWKV_SPB, nsb)
        assert n_seq % nsb == 0 and nsb % spb == 0
        sub = nsb // spb
        grid = (n_seq // nsb, sub)
        rmap = lambda s, c: (0, b0 + s, 0)
        vmap_ = lambda s, c: (0, 0, b0 + s)
        smap = lambda s, c: (s * sub + c, 0, 0, 0)
        omap = lambda s, c: (0, 0, s)
    t_out = n_seq * seq_len
    rspec = pl.BlockSpec((H_A, blk, HS_A), rmap)
    sspec = pl.BlockSpec((spb, H_A, HS_A, HS_A), smap)
    return pl.pallas_call(
        functools.partial(_rwkv_scan_kernel, seq_len=seq_len, blk=blk, spb=spb),
        grid=grid,
        in_specs=[rspec] * 5 + [pl.BlockSpec((H_A, HS_A, blk), vmap_), sspec],
        out_specs=[pl.BlockSpec((H_A, HS_A, blk), omap), sspec],
        out_shape=[jax.ShapeDtypeStruct((H_A, HS_A, t_out), F32),
                   jax.ShapeDtypeStruct((n_seq, H_A, HS_A, HS_A), F32)],
        scratch_shapes=[pltpu.VMEM((H_A, HS_A, HS_A), F32)],
        compiler_params=_cparams(("parallel", "arbitrary")),
        name="rwkv_scan_long" if long_mode else "rwkv_scan_short",
    )(*hm, vt, s0)


def _rwkv_post_kernel(ot_ref, bonus_ref, g_ref, lng_ref, lnb_ref, gsel_ref, gselt_ref, o_ref):
    tm = bonus_ref.shape[0]
    o = ot_ref[...].reshape(C_A, tm).T
    gsel = gsel_ref[...]
    gselt = gselt_ref[...]
    mu = _dot_sel(o, gsel) * (1.0 / HS_A)
    d = o - _dot_sel(mu, gselt)
    var = _dot_sel(d * d, gsel) * (1.0 / HS_A)
    rstd = lax.rsqrt(var + GN_EPS_A)
    y = d * _dot_sel(rstd, gselt) * lng_ref[...] + lnb_ref[...]
    o_ref[...] = ((y + bonus_ref[...]) * g_ref[...]).astype(o_ref.dtype)


def rwkv_post(ot, bonus, g, p, tm):
    t = bonus.shape[0]
    row = pl.BlockSpec((tm, C_A), lambda i: (i, 0))
    full = lambda a: pl.BlockSpec(a.shape, lambda i: (0,) * a.ndim)
    consts = [p["ln_g"], p["ln_b"], p["gsel"], p["gselt"]]
    return pl.pallas_call(
        _rwkv_post_kernel,
        grid=(t // tm,),
        in_specs=[pl.BlockSpec((H_A, HS_A, tm), lambda i: (0, 0, i)), row, row] + [full(c) for c in consts],
        out_specs=row,
        out_shape=jax.ShapeDtypeStruct((t, C_A), BF16),
        compiler_params=_cparams(("parallel",)),
        name="rwkv_post",
    )(ot, bonus, g, *consts)


S5_KT = 4
S5_KU = C_B // S5_KT
S5_KH = S5_STATE // S5_KT
S5_LC = 512


def _gelu_tanh(x):
    return 0.5 * x * (1.0 + jnp.tanh(math.sqrt(2.0 / math.pi) * (x + 0.044715 * (x * x * x))))


def _s5_kernel(u_ref, wbr_ref, wbi_ref, lamk_ref, pre_ref, pim_ref, h0r_ref, h0i_ref, wcr_ref, wci_ref,
               dsk_ref, wglu_ref, bglu_ref, ob_ref, hr_out, hi_out, hre, him, car_re, car_im, y_scr,
               *, seq_len, blk):
    long_mode = seq_len >= blk
    period = SUBLANES if long_mode else seq_len
    u = u_ref[...]
    ub = u.astype(BF16)
    for kt in range(S5_KT):
        uk = ub[:, kt * S5_KU:(kt + 1) * S5_KU]
        hre[:, kt * S5_KH:(kt + 1) * S5_KH] = _dot(uk, wbr_ref[kt])
        him[:, kt * S5_KH:(kt + 1) * S5_KH] = _dot(uk, wbi_ref[kt])

    if long_mode:
        @pl.when(pl.program_id(1) == 0)
        def _():
            car_re[...] = jnp.zeros(car_re.shape, F32)
            car_im[...] = jnp.zeros(car_im.shape, F32)

    rowi = lax.broadcasted_iota(I32, (SUBLANES, 1), 0) % period
    for lc in range(S5_STATE // S5_LC):
        ls = slice(lc * S5_LC, (lc + 1) * S5_LC)
        lam = [(lamk_ref[2 * i:2 * i + 1, ls], lamk_ref[2 * i + 1:2 * i + 2, ls]) for i in range(3)]
        pre = pre_ref[:, ls]
        pim = pim_ref[:, ls]

        def tile(i, carry, ls=ls, lam=lam, pre=pre, pim=pim):
            r0 = pl.multiple_of(i * SUBLANES, SUBLANES)
            xr = hre[pl.ds(r0, SUBLANES), ls]
            xi = him[pl.ds(r0, SUBLANES), ls]
            for lvl, sh in enumerate((1, 2, 4)):
                if sh >= period:
                    break
                lr, li = lam[lvl]
                keep = rowi >= sh
                sr = jnp.where(keep, pltpu.roll(xr, sh, 0), 0.0)
                si = jnp.where(keep, pltpu.roll(xi, sh, 0), 0.0)
                xr, xi = xr + lr * sr - li * si, xi + lr * si + li * sr
            if long_mode:
                hr, hi = carry
            else:
                hr = h0r_ref[pl.ds(r0, SUBLANES), ls]
                hi = h0i_ref[pl.ds(r0, SUBLANES), ls]
            xr, xi = xr + pre * hr - pim * hi, xi + pre * hi + pim * hr
            hre[pl.ds(r0, SUBLANES), ls] = xr
            him[pl.ds(r0, SUBLANES), ls] = xi
            if long_mode:
                last_r = jnp.broadcast_to(xr[SUBLANES - 1:SUBLANES, :], xr.shape)
                last_i = jnp.broadcast_to(xi[SUBLANES - 1:SUBLANES, :], xi.shape)
                return (last_r, last_i)
            return carry

        if long_mode:
            init = (car_re[:, ls], car_im[:, ls])
            fin = lax.fori_loop(0, blk // SUBLANES, tile, init)
            car_re[:, ls] = fin[0]
            car_im[:, ls] = fin[1]
        else:
            lax.fori_loop(0, blk // SUBLANES, tile, 0)

    if long_mode:
        hr_out[0] = car_re[...]
        hi_out[0] = car_im[...]
    else:
        hr_out[...] = hre[...]
        hi_out[...] = him[...]

    for kt in range(S5_KT):
        hs = slice(kt * S5_KH, (kt + 1) * S5_KH)
        y_scr[:, kt * S5_KU:(kt + 1) * S5_KU] = (_dot(hre[:, hs].astype(BF16), wcr_ref[kt])
                                                 - _dot(him[:, hs].astype(BF16), wci_ref[kt]))
    y = y_scr[...] + dsk_ref[...] * u
    z = _gelu_tanh(y)
    gate = jax.nn.sigmoid(_dot(z.astype(BF16), wglu_ref[...]) + bglu_ref[...])
    ob_ref[...] = (z * gate).astype(ob_ref.dtype)


def s5_mix(u, p, h0r_rows, h0i_rows, *, row0, n_seq, seq_len, blk):
    long_mode = seq_len >= blk
    assert row0 % blk == 0
    b0 = row0 // blk
    t_out = n_seq * seq_len
    if long_mode:
        cps = seq_len // blk
        grid = (n_seq, cps)
        umap = lambda s, c: (b0 + s * cps + c, 0)
        omap = lambda s, c: (s * cps + c, 0)
        hspec = pl.BlockSpec((1, SUBLANES, S5_STATE), lambda s, c: (s, 0, 0))
        hshape = jax.ShapeDtypeStruct((n_seq, SUBLANES, S5_STATE), F32)
        h0spec = pl.BlockSpec((SUBLANES, S5_STATE), lambda s, c: (0, 0))
    else:
        assert t_out % blk == 0
        grid = (t_out // blk, 1)
        umap = lambda s, c: (b0 + s, 0)
        omap = lambda s, c: (s, 0)
        hspec = pl.BlockSpec((blk, S5_STATE), omap)
        hshape = jax.ShapeDtypeStruct((t_out, S5_STATE), F32)
        h0spec = pl.BlockSpec((blk, S5_STATE), omap)
    full = lambda a: pl.BlockSpec(a.shape, lambda s, c: (0,) * a.ndim)
    pre, pim = (p["pow_re8"], p["pow_im8"]) if long_mode else (p["pow_re_s"], p["pow_im_s"])
    args = [u, p["wb_re"], p["wb_im"], p["lamk"], pre, pim, h0r_rows, h0i_rows, p["wc_re"], p["wc_im"],
            p["d_skip"], p["w_glu"], p["b_glu"]]
    in_specs = [pl.BlockSpec((blk, C_B), umap)] + [full(a) for a in args[1:6]] + [h0spec, h0spec] \
        + [full(a) for a in args[8:]]
    return pl.pallas_call(
        functools.partial(_s5_kernel, seq_len=seq_len, blk=blk),
        grid=grid,
        in_specs=in_specs,
        out_specs=[pl.BlockSpec((blk, C_B), omap), hspec, hspec],
        out_shape=[jax.ShapeDtypeStruct((t_out, C_B), BF16), hshape, hshape],
        scratch_shapes=[pltpu.VMEM((blk, S5_STATE), F32), pltpu.VMEM((blk, S5_STATE), F32),
                        pltpu.VMEM((SUBLANES, S5_STATE), F32), pltpu.VMEM((SUBLANES, S5_STATE), F32),
                        pltpu.VMEM((blk, C_B), F32)],
        compiler_params=_cparams(("parallel", "arbitrary")),
        name="s5_long" if long_mode else "s5_short",
    )(*args)


def _pad_cols(a, n):
    return jnp.pad(a, ((0, 0), (0, n - a.shape[1])))


def _pad_rows(a, n):
    return jnp.pad(a, ((0, n - a.shape[0]), (0, 0)))


def _lora_layout(a):
    o = 3 * C_A
    return jnp.concatenate([_pad_cols(a[:, o:o + LORA_W], LANES),
                            _pad_cols(a[:, o + LORA_W:o + LORA_W + LORA_A], LANES),
                            _pad_cols(a[:, o + LORA_W + LORA_A:A_PROJ], 2 * LANES)], axis=1)


def _lora_unlayout(a):
    return jnp.concatenate([a[:, :LORA_W], a[:, LANES:LANES + LORA_A], a[:, 2 * LANES:2 * LANES + LORA_G]], axis=1)


def _cpow_table(lbr, lbi, n):
    res_r, res_i = [lbr], [lbi]
    for _ in range(n - 1):
        pr, pi = res_r[-1], res_i[-1]
        res_r.append(pr * lbr - pi * lbi)
        res_i.append(pr * lbi + pi * lbr)
    return (jnp.stack([x.reshape(-1) for x in res_r]), jnp.stack([x.reshape(-1) for x in res_i]))


def ab_params(li, w, sample_len):
    p = {}
    w_in = w["ab_w_in"][li]
    p["w_rkv"] = w_in[:, :3 * C_A].astype(BF16)
    p["w_lora"] = _lora_layout(w_in).astype(BF16)
    p["w_u"] = w_in[:, A_PROJ:].astype(BF16)
    mu = w["ab_mu"][li][None, :]
    p["mu_rkv"] = mu[:, :3 * C_A]
    p["mu_lora"] = _lora_layout(mu)
    row = lambda a: a.reshape(1, -1)
    p["w0"] = row(w["rwkv_w0"][li])
    p["w2"] = _pad_rows(w["rwkv_w2"][li], LANES)
    p["a0"] = row(w["rwkv_a0"][li])
    p["a2"] = _pad_rows(w["rwkv_a2"][li], LANES)
    p["g2"] = _pad_rows(w["rwkv_g2"][li], 2 * LANES)
    p["k_k"] = row(w["rwkv_k_k"][li])
    p["k_a"] = row(w["rwkv_k_a"][li])
    p["r_k"] = row(w["rwkv_r_k"][li])
    p["ln_g"] = row(w["rwkv_ln_g"][li])
    p["ln_b"] = row(w["rwkv_ln_b"][li])
    head_of_col = jnp.arange(C_A) // HS_A
    gsel = (head_of_col[:, None] == jnp.arange(LANES)[None, :])
    p["gsel"] = gsel.astype(BF16)
    p["gselt"] = gsel.T.astype(BF16)
    lr = jnp.minimum(w["s5_lam_re"][li], -1e-4)
    lim = w["s5_lam_im"][li]
    dt = jnp.exp(w["s5_log_dt"][li])[:, None]
    mag = jnp.exp(lr * dt)
    lbr, lbi = mag * jnp.cos(lim * dt), mag * jnp.sin(lim * dt)
    den = lr * lr + lim * lim
    pr, pim = lbr - 1.0, lbi
    fr = (pr * lr + pim * lim) / den
    fi = (pim * lr - pr * lim) / den
    br, bi = w["s5_b_re"][li], w["s5_b_im"][li]
    bbr = fr[..., None] * br - fi[..., None] * bi
    bbi = fr[..., None] * bi + fi[..., None] * br
    gpc = S5_GROUPS // S5_KT
    eye = jnp.eye(gpc, dtype=F32)

    def blockdiag_in(bb):
        b4 = bb.reshape(S5_KT, gpc, S5_N, S5_P)
        return jnp.einsum("kgnp,gh->kgphn", b4, eye).reshape(S5_KT, S5_KU, S5_KH).astype(BF16)

    def blockdiag_out(cc):
        c4 = cc.reshape(S5_KT, gpc, S5_P, S5_N)
        return jnp.einsum("kgpn,gh->kgnhp", c4, eye).reshape(S5_KT, S5_KH, S5_KU).astype(BF16)

    p["wb_re"], p["wb_im"] = blockdiag_in(bbr), blockdiag_in(bbi)
    p["wc_re"], p["wc_im"] = blockdiag_out(w["s5_c_re"][li]), blockdiag_out(w["s5_c_im"][li])
    pw_r, pw_i = _cpow_table(lbr, lbi, SUBLANES)
    p["pow_re8"], p["pow_im8"] = pw_r, pw_i
    reps = SUBLANES // sample_len
    p["pow_re_s"] = jnp.tile(pw_r[:sample_len], (reps, 1))
    p["pow_im_s"] = jnp.tile(pw_i[:sample_len], (reps, 1))
    p["lamk"] = jnp.stack([pw_r[0], pw_i[0], pw_r[1], pw_i[1], pw_r[3], pw_i[3]])
    p["d_skip"] = row(w["s5_d"][li])
    p["w_glu"] = w["s5_w_glu"][li].astype(BF16)
    p["b_glu"] = row(w["s5_b_glu"][li])
    w_out = w["ab_w_out"][li]
    p["w_out_a"] = w_out[:C_A].astype(BF16)
    p["w_out_b"] = w_out[C_A:].astype(BF16)
    return p


def _shift_rows(a, first_sample, geo):
    n_p, l_p, n_s, l_s = geo
    wd = a.shape[1]
    ap = a[:n_p * l_p].reshape(n_p, l_p, wd)
    prev_p = jnp.concatenate([jnp.zeros((n_p, 1, wd), a.dtype), ap[:, :-1]], axis=1)
    as_ = a[n_p * l_p:].reshape(n_s, l_s, wd)
    prev_s = jnp.concatenate([first_sample[:, None, :], as_[:, :-1]], axis=1)
    return jnp.concatenate([prev_p.reshape(-1, wd), prev_s.reshape(-1, wd)], axis=0)


def _last_rows(a, geo):
    n_p, l_p, n_s, l_s = geo
    wd = a.shape[1]
    return (a[:n_p * l_p].reshape(n_p, l_p, wd)[:, -1], a[n_p * l_p:].reshape(n_s, l_s, wd)[:, -1])


def ab_layer(xb, geo, p, state_shift, state_wkv, state_re, state_im, *, tm, blk):
    n_p, l_p, n_s, l_s = geo
    tp, ts = n_p * l_p, n_s * l_s
    rkv = matmul([xb], [p["w_rkv"]], tm, 1024, name="ab_in_rkv")
    lora = matmul([xb], [p["w_lora"]], tm, LORA_PAD, name="ab_in_lora")
    u = matmul([xb], [p["w_u"]], tm, C_B, name="ab_in_u")
    rkv_prev = _shift_rows(rkv, state_shift[:, :3 * C_A], geo)
    lora_prev = _shift_rows(lora, _lora_layout(state_shift), geo)
    r, wd, k, kk, b, vt, g, bonus = rwkv_prep(rkv, rkv_prev, lora, lora_prev, p, tm)
    hm = [r, wd, k, kk, b]
    zeros_wkv = jnp.zeros((n_p, H_A, HS_A, HS_A), F32)
    ot_p, wkv_p = rwkv_scan(hm, vt, zeros_wkv, row0=0, n_seq=n_p, seq_len=l_p, blk=blk)
    ot_s, wkv_s = rwkv_scan(hm, vt, state_wkv, row0=tp, n_seq=n_s, seq_len=l_s, blk=blk)
    ot = jnp.concatenate([ot_p, ot_s], axis=2)
    o_a = rwkv_post(ot, bonus, g, p, tm)
    dummy = jnp.zeros((SUBLANES, S5_STATE), F32)
    ob_p, hr_p, hi_p = s5_mix(u, p, dummy, dummy, row0=0, n_seq=n_p, seq_len=l_p, blk=blk)
    h0r = jnp.repeat(state_re.reshape(n_s, S5_STATE), l_s, axis=0)
    h0i = jnp.repeat(state_im.reshape(n_s, S5_STATE), l_s, axis=0)
    ob_s, hr_s, hi_s = s5_mix(u, p, h0r, h0i, row0=tp, n_seq=n_s, seq_len=l_s, blk=blk)
    o_b = jnp.concatenate([ob_p, ob_s], axis=0)
    h = matmul([o_a, o_b], [p["w_out_a"], p["w_out_b"]], tm, 1024, name="ab_out")
    rkv_lp, rkv_ls = _last_rows(rkv, geo)
    lo_lp, lo_ls = _last_rows(lora, geo)
    shift_p = jnp.concatenate([rkv_lp, _lora_unlayout(lo_lp)], axis=1)
    shift_s = jnp.concatenate([rkv_ls, _lora_unlayout(lo_ls)], axis=1)
    grp = lambda a, n: a.reshape(n, S5_GROUPS, S5_N)
    ssm_p = (grp(hr_p[:, SUBLANES - 1], n_p), grp(hi_p[:, SUBLANES - 1], n_p))
    ssm_s = (grp(hr_s[l_s - 1::l_s], n_s), grp(hi_s[l_s - 1::l_s], n_s))
    return h, (shift_p, shift_s), (wkv_p, wkv_s), ssm_p, ssm_s


def _router_kernel(x_ref, wr_ref, br_ref, ltri_ref, eidx_o, gate_o, rank_o, cnt_o, carry):
    i = pl.program_id(0)

    @pl.when(i == 0)
    def _():
        carry[...] = jnp.zeros(carry.shape, F32)

    tm = x_ref.shape[0]
    neg = -jnp.inf
    lane = lax.broadcasted_iota(I32, (tm, LANES), 1)
    real = lane < N_EXPERTS
    per_group = N_EXPERTS // N_EXPERT_GROUPS
    grp = lane // per_group
    scores = jax.nn.sigmoid(_dot3(x_ref[...], wr_ref[...]))
    biased = jnp.where(real, scores + br_ref[...], neg)

    gs = []
    for g in range(N_EXPERT_GROUPS):
        xg = jnp.where(grp == g, biased, neg)
        m1 = jnp.max(xg, axis=-1, keepdims=True)
        i1 = jnp.min(jnp.where(xg == m1, lane, LANES), axis=-1, keepdims=True)
        m2 = jnp.max(jnp.where(lane == i1, neg, xg), axis=-1, keepdims=True)
        gs.append(m1 + m2)
    keep = jnp.zeros((tm, LANES), jnp.bool_)
    for g in range(N_EXPERT_GROUPS):
        better = jnp.zeros((tm, 1), I32)
        for g2 in range(N_EXPERT_GROUPS):
            if g2 == g:
                continue
            wins = (gs[g2] > gs[g]) | ((gs[g2] == gs[g]) & (g2 < g))
            better = better + wins.astype(I32)
        keep = keep | ((grp == g) & (better < TOPK_GROUPS))
    masked = jnp.where(keep, biased, neg)

    cur = masked
    sel = jnp.zeros((tm, LANES), F32)
    es, ss = [], []
    for _ in range(TOP_K):
        m = jnp.max(cur, axis=-1, keepdims=True)
        ik = jnp.min(jnp.where(cur == m, lane, LANES), axis=-1, keepdims=True)
        hit = lane == ik
        ss.append(jnp.sum(jnp.where(hit, scores, 0.0), axis=-1, keepdims=True))
        es.append(ik)
        cur = jnp.where(hit, neg, cur)
        sel = jnp.where(hit, 1.0, sel)
    tot = ss[0]
    for k in range(1, TOP_K):
        tot = tot + ss[k]

    prefix = _dot(ltri_ref[...], sel.astype(BF16)) + carry[0:1, :]
    eidx = jnp.zeros((tm, LANES), I32)
    gate = jnp.zeros((tm, LANES), F32)
    rank = jnp.zeros((tm, LANES), I32)
    for k in range(TOP_K):
        slot = lane == k
        rk = jnp.sum(jnp.where(lane == es[k], prefix, 0.0), axis=-1, keepdims=True)
        eidx = jnp.where(slot, es[k], eidx)
        gate = jnp.where(slot, ss[k] / tot * ROUTED_SCALE, gate)
        rank = jnp.where(slot, rk.astype(I32), rank)
    eidx_o[...] = eidx
    gate_o[...] = gate
    rank_o[...] = rank
    carry[...] = carry[...] + jnp.sum(sel, axis=0, keepdims=True)
    cnt_o[...] = carry[...]


def moe_router(x, w_router_pad, b_router_pad, tm):
    t, d = x.shape
    ltri = (jnp.arange(tm)[:, None] > jnp.arange(tm)[None, :]).astype(BF16)
    row = pl.BlockSpec((tm, LANES), lambda i: (i, 0))
    out = jax.ShapeDtypeStruct((t, LANES), I32)
    return pl.pallas_call(
        _router_kernel,
        grid=(t // tm,),
        in_specs=[pl.BlockSpec((tm, d), lambda i: (i, 0)), pl.BlockSpec((d, LANES), lambda i: (0, 0)),
                  pl.BlockSpec((1, LANES), lambda i: (0, 0)), pl.BlockSpec((tm, tm), lambda i: (0, 0))],
        out_specs=[row, row, row, pl.BlockSpec((SUBLANES, LANES), lambda i: (0, 0))],
        out_shape=[out, jax.ShapeDtypeStruct((t, LANES), F32), out,
                   jax.ShapeDtypeStruct((SUBLANES, LANES), F32)],
        scratch_shapes=[pltpu.VMEM((SUBLANES, LANES), F32)],
        compiler_params=_cparams(("arbitrary",)),
        name="moe_router",
    )(x, w_router_pad, b_router_pad, ltri)


MOE_BM = 256
DMA_RING = 32


def _row_copy(src, s, dst, d, sem):
    return pltpu.make_async_copy(src.at[pl.ds(s, 1), :], dst.at[pl.ds(d, 1), :], sem)


def _dispatch_kernel(dest_ref, zb_ref, x_hbm, xs_hbm, zbuf, zsem, sems, *, tm):
    i = pl.program_id(0)
    n_asg = tm * TOP_K

    @pl.when(i == 0)
    def _():
        zbuf[...] = jnp.zeros(zbuf.shape, F32)
        for e in range(N_EXPERTS):
            @pl.when(zb_ref[e] >= 0)
            def _():
                r0 = pl.multiple_of(zb_ref[e] * MOE_BM, MOE_BM)
                pltpu.make_async_copy(zbuf, xs_hbm.at[pl.ds(r0, MOE_BM), :], zsem.at[e]).start()
        for e in range(N_EXPERTS):
            @pl.when(zb_ref[e] >= 0)
            def _():
                r0 = pl.multiple_of(zb_ref[e] * MOE_BM, MOE_BM)
                pltpu.make_async_copy(zbuf, xs_hbm.at[pl.ds(r0, MOE_BM), :], zsem.at[e]).wait()

        def fill(blk, c):
            r0 = pl.multiple_of(blk * MOE_BM, MOE_BM)
            cp = pltpu.make_async_copy(zbuf, xs_hbm.at[pl.ds(r0, MOE_BM), :], zsem.at[0])
            cp.start()
            cp.wait()
            return c

        lax.fori_loop(zb_ref[N_EXPERTS], xs_hbm.shape[0] // MOE_BM, fill, 0)

    base = i * n_asg

    def start(j):
        tok = i * tm + j // TOP_K
        _row_copy(x_hbm, tok, xs_hbm, dest_ref[base + j], sems.at[j % DMA_RING]).start()

    def wait(j):
        _row_copy(x_hbm, 0, xs_hbm, 0, sems.at[j % DMA_RING]).wait()

    def head(j, c):
        start(j)
        return c

    def steady(j, c):
        wait(j)
        start(j)
        return c

    def tail(j, c):
        wait(j)
        return c

    lax.fori_loop(0, DMA_RING, head, 0)
    lax.fori_loop(DMA_RING, n_asg, steady, 0)
    lax.fori_loop(0, DMA_RING, tail, 0)


def moe_dispatch(x, dest_flat, zero_blk, n_rows, tm):
    t, d = x.shape
    return pl.pallas_call(
        functools.partial(_dispatch_kernel, tm=tm),
        grid_spec=pltpu.PrefetchScalarGridSpec(
            num_scalar_prefetch=2,
            grid=(t // tm,),
            in_specs=[pl.BlockSpec(memory_space=pl.ANY)],
            out_specs=pl.BlockSpec(memory_space=pl.ANY),
            scratch_shapes=[pltpu.VMEM((MOE_BM, d), F32), pltpu.SemaphoreType.DMA((N_EXPERTS,)),
                            pltpu.SemaphoreType.DMA((DMA_RING,))],
        ),
        out_shape=jax.ShapeDtypeStruct((n_rows, d), F32),
        compiler_params=_cparams(("arbitrary",)),
        name="moe_dispatch",
    )(dest_flat, zero_blk, x)


def _expert_kernel(blk_e_ref, nused_ref, xs_ref, wg_ref, wu_ref, wd_ref, ys_ref, wgb, wub, wdb):
    i = pl.program_id(0)

    @pl.when(i < nused_ref[0])
    def _():
        e = blk_e_ref[i]
        prev = blk_e_ref[jnp.maximum(i - 1, 0)]

        @pl.when((i == 0) | (e != prev))
        def _():
            wgb[...] = wg_ref[0].astype(BF16)
            wub[...] = wu_ref[0].astype(BF16)
            wdb[...] = wd_ref[0].astype(BF16)

        x = xs_ref[...].astype(BF16)
        hg = _dot(x, wgb[...])
        hu = _dot(x, wub[...])
        h = hg * jax.nn.sigmoid(hg) * hu
        ys_ref[...] = _dot(h.astype(BF16), wdb[...])

    @pl.when(i >= nused_ref[0])
    def _():
        ys_ref[...] = jnp.zeros(ys_ref.shape, F32)


def moe_experts(xs, blk_e, n_used, w_gate, w_up, w_down):
    n_rows, d = xs.shape
    n_blocks = n_rows // MOE_BM
    de = w_gate.shape[2]
    last = lambda i, nu: jnp.minimum(i, nu[0] - 1)
    return pl.pallas_call(
        _expert_kernel,
        grid_spec=pltpu.PrefetchScalarGridSpec(
            num_scalar_prefetch=2,
            grid=(n_blocks,),
            in_specs=[pl.BlockSpec((MOE_BM, d), lambda i, be, nu: (last(i, nu), 0)),
                      pl.BlockSpec((1, d, de), lambda i, be, nu: (be[last(i, nu)], 0, 0)),
                      pl.BlockSpec((1, d, de), lambda i, be, nu: (be[last(i, nu)], 0, 0)),
                      pl.BlockSpec((1, de, d), lambda i, be, nu: (be[last(i, nu)], 0, 0))],
            out_specs=pl.BlockSpec((MOE_BM, d), lambda i, be, nu: (i, 0)),
            scratch_shapes=[pltpu.VMEM((d, de), BF16), pltpu.VMEM((d, de), BF16), pltpu.VMEM((de, d), BF16)],
        ),
        out_shape=jax.ShapeDtypeStruct((n_rows, d), F32),
        compiler_params=_cparams(("arbitrary",)),
        name="moe_experts",
    )(blk_e, n_used, xs, w_gate, w_up, w_down)


def _shared_ffn_kernel(x_ref, wg_ref, wu_ref, wd_ref, o_ref):
    x = x_ref[...]
    hg = _dot(x, wg_ref[...])
    hu = _dot(x, wu_ref[...])
    h = hg * jax.nn.sigmoid(hg) * hu
    o_ref[...] = _dot(h.astype(BF16), wd_ref[...])


def shared_ffn(xb, wg, wu, wd, tm):
    t, d = xb.shape
    full = lambda a: pl.BlockSpec(a.shape, lambda i: (0, 0))
    row = pl.BlockSpec((tm, d), lambda i: (i, 0))
    return pl.pallas_call(
        _shared_ffn_kernel,
        grid=(t // tm,),
        in_specs=[row, full(wg), full(wu), full(wd)],
        out_specs=row,
        out_shape=jax.ShapeDtypeStruct((t, d), F32),
        compiler_params=_cparams(("parallel",)),
        name="shared_ffn",
    )(xb, wg, wu, wd)


def _combine_kernel(dest_ref, ys_hbm, gate_ref, x_ref, sh_ref, g_ref, b_ref, o_ref, ob_ref, buf, sems, *, tm):
    i = pl.program_id(0)
    n_asg = tm * TOP_K
    base = i * n_asg

    def start(j):
        row = (j % TOP_K) * tm + j // TOP_K
        _row_copy(ys_hbm, dest_ref[base + j], buf, row, sems.at[j % DMA_RING]).start()

    def wait(j):
        _row_copy(ys_hbm, 0, buf, 0, sems.at[j % DMA_RING]).wait()

    def head(j, c):
        start(j)
        return c

    def steady(j, c):
        wait(j)
        start(j)
        return c

    def tail(j, c):
        wait(j)
        return c

    lax.fori_loop(0, DMA_RING, head, 0)
    lax.fori_loop(DMA_RING, n_asg, steady, 0)
    lax.fori_loop(0, DMA_RING, tail, 0)

    gate = gate_ref[...]
    f = sh_ref[...]
    for k in range(TOP_K):
        f = f + gate[:, k:k + 1] * buf[k * tm:(k + 1) * tm, :]
    z = ALPHA * x_ref[...] + f
    mu = jnp.mean(z, axis=-1, keepdims=True)
    dlt = z - mu
    var = jnp.mean(dlt * dlt, axis=-1, keepdims=True)
    y = dlt * lax.rsqrt(var + LN_EPS) * g_ref[...] + b_ref[...]
    o_ref[...] = y
    ob_ref[...] = y.astype(BF16)


def moe_combine(ys, dest_flat, gates, x, shared, g, b, tm):
    t, d = x.shape
    row = pl.BlockSpec((tm, d), lambda i, dr: (i, 0))
    vec = pl.BlockSpec((1, d), lambda i, dr: (0, 0))
    return pl.pallas_call(
        functools.partial(_combine_kernel, tm=tm),
        grid_spec=pltpu.PrefetchScalarGridSpec(
            num_scalar_prefetch=1,
            grid=(t // tm,),
            in_specs=[pl.BlockSpec(memory_space=pl.ANY), pl.BlockSpec((tm, LANES), lambda i, dr: (i, 0)),
                      row, row, vec, vec],
            out_specs=[row, row],
            scratch_shapes=[pltpu.VMEM((TOP_K * tm, d), F32), pltpu.SemaphoreType.DMA((DMA_RING,))],
        ),
        out_shape=[jax.ShapeDtypeStruct((t, d), F32), jax.ShapeDtypeStruct((t, d), BF16)],
        compiler_params=_cparams(("arbitrary",)),
        name="moe_combine",
    )(dest_flat, ys, gates, x, shared, g.reshape(1, d), b.reshape(1, d))


def moe_params(layer, w):
    p = {}
    p["w_router"] = _pad_cols(w["moe_w_router"][layer], LANES)
    p["b_router"] = _pad_cols(w["moe_b_router"][layer][None, :], LANES)
    p["w_gate"], p["w_up"], p["w_down"] = w["moe_w_gate"][layer], w["moe_w_up"][layer], w["moe_w_down"][layer]
    p["ws_gate"] = w["moe_ws_gate"][layer].astype(BF16)
    p["ws_up"] = w["moe_ws_up"][layer].astype(BF16)
    p["ws_down"] = w["moe_ws_down"][layer].astype(BF16)
    return p


def moe_layer(y, yb, p, ln_g, ln_b, *, tm, tm_dma):
    t = y.shape[0]
    eidx, gates, rank, cnt = moe_router(y, p["w_router"], p["b_router"], tm)
    counts = cnt[0, :N_EXPERTS].astype(I32)
    padded = (counts + MOE_BM - 1) // MOE_BM * MOE_BM
    pad_end = jnp.cumsum(padded)
    pad_start = pad_end - padded
    e8 = eidx[:, :TOP_K]
    dest = (pad_start[e8] + rank[:, :TOP_K]).reshape(-1)
    n_blocks = -(-(t * TOP_K + N_EXPERTS * (MOE_BM - 1)) // MOE_BM)
    n_rows = n_blocks * MOE_BM
    blk_e = jnp.searchsorted(pad_end, jnp.arange(n_blocks, dtype=I32) * MOE_BM, side="right")
    blk_e = jnp.minimum(blk_e, N_EXPERTS - 1).astype(I32)
    n_used = (pad_end[-1:] // MOE_BM).astype(I32)
    zero_blk = jnp.concatenate([jnp.where(padded > 0, pad_end // MOE_BM - 1, -1).astype(I32), n_used])
    xs = moe_dispatch(y, dest, zero_blk, n_rows, tm_dma)
    ys = moe_experts(xs, blk_e, n_used, p["w_gate"], p["w_up"], p["w_down"])
    shared = shared_ffn(yb, p["ws_gate"], p["ws_up"], p["ws_down"], tm)
    return moe_combine(ys, dest, gates, y, shared, ln_g, ln_b, tm_dma)


C_QKV = (H_C + 2 * KVH_C) * DH_C
C_QI = H_IDX * D_IDX
IDX_SCALE = (H_IDX ** -0.5) * (D_IDX ** -0.5)


def _rot_half64(x, lane):
    return jnp.where(lane % D_IDX < D_IDX // 2, pltpu.roll(x, LANES - D_IDX // 2, 1), pltpu.roll(x, D_IDX // 2, 1))


def _c_post_kernel(qkv_ref, qi_ref, kw_ref, c128_ref, s128_ref, c64_ref, s64_ref, lng_ref, lnb_ref,
                   qb_o, kf_o, kb_o, vb_o, qib_o, kwo_o):
    tm = qkv_ref.shape[0]
    c128, s128 = c128_ref[...], s128_ref[...]
    c64, s64 = c64_ref[...], s64_ref[...]
    lane = lax.broadcasted_iota(I32, (tm, LANES), 1)
    for h in range(H_C + KVH_C):
        x = qkv_ref[:, h * DH_C:(h + 1) * DH_C]
        y = x * c128 + pltpu.roll(x, DH_C // 2, 1) * s128
        if h < H_C:
            qb_o[:, h * DH_C:(h + 1) * DH_C] = y.astype(BF16)
        else:
            j = h - H_C
            kf_o[:, j * DH_C:(j + 1) * DH_C] = y
            kb_o[:, j * DH_C:(j + 1) * DH_C] = y.astype(BF16)
    vb_o[...] = qkv_ref[:, (H_C + KVH_C) * DH_C:].astype(BF16)
    for j in range(C_QI // LANES):
        x = qi_ref[:, j * LANES:(j + 1) * LANES]
        qib_o[:, j * LANES:(j + 1) * LANES] = (x * c64 + _rot_half64(x, lane) * s64).astype(BF16)
    x = kw_ref[...]
    is_k = lane < D_IDX
    mu = jnp.sum(jnp.where(is_k, x, 0.0), axis=-1, keepdims=True) * (1.0 / D_IDX)
    d = jnp.where(is_k, x - mu, 0.0)
    var = jnp.sum(d * d, axis=-1, keepdims=True) * (1.0 / D_IDX)
    y = d * lax.rsqrt(var + LN_EPS) * lng_ref[...] + lnb_ref[...]
    y = y * c64 + _rot_half64(y, lane) * s64
    kwo_o[...] = jnp.where(is_k, y, x * IDX_SCALE)


def c_post(qkv, qi, kw, tabs, lng, lnb, tm):
    t = qkv.shape[0]
    row = lambda wd: pl.BlockSpec((tm, wd), lambda i: (i, 0))
    vec = pl.BlockSpec((1, LANES), lambda i: (0, 0))
    sh = lambda wd, dt: jax.ShapeDtypeStruct((t, wd), dt)
    return pl.pallas_call(
        _c_post_kernel,
        grid=(t // tm,),
        in_specs=[row(C_QKV), row(C_QI), row(LANES), row(LANES), row(LANES), row(LANES), row(LANES), vec, vec],
        out_specs=[row(H_C * DH_C), row(KVH_C * DH_C), row(KVH_C * DH_C), row(KVH_C * DH_C), row(C_QI), row(LANES)],
        out_shape=[sh(H_C * DH_C, BF16), sh(KVH_C * DH_C, F32), sh(KVH_C * DH_C, BF16), sh(KVH_C * DH_C, BF16),
                   sh(C_QI, BF16), sh(LANES, F32)],
        compiler_params=_cparams(("parallel",)),
        name="c_post",
    )(qkv, qi, kw, *tabs, lng, lnb)


def _rope_tables(pos):
    def tab(d):
        inv = ROPE_THETA ** (-jnp.arange(0, d, 2, dtype=F32) / d)
        ang = pos.astype(F32)[:, None] * inv[None, :]
        c, s = jnp.cos(ang), jnp.sin(ang)
        return jnp.concatenate([c, c], axis=1), jnp.concatenate([-s, s], axis=1)

    c128, s128 = tab(DH_C)
    c64, s64 = tab(D_IDX)
    return c128, s128, jnp.tile(c64, (1, 2)), jnp.tile(s64, (1, 2))


def _index_heads(qi, wi, kb):
    acc = None
    for h in range(H_IDX):
        d = lax.dot_general(qi[:, h * D_IDX:(h + 1) * D_IDX], kb, (((1,), (1,)), ((), ())),
                            preferred_element_type=F32)
        term = wi[:, D_IDX + h:D_IDX + h + 1] * jnp.maximum(d, 0.0)
        acc = term if acc is None else acc + term
    return acc + 0.0


def _scores_prompt_kernel(qi_ref, wi_ref, keys_ref, o_ref, *, qblk):
    qb = pl.program_id(1)
    s_len = keys_ref.shape[0]
    kb = keys_ref[:, :D_IDX].astype(BF16)
    sc = _index_heads(qi_ref[...], wi_ref[...], kb)
    row = lax.broadcasted_iota(I32, (qblk, s_len), 0) + qb * qblk
    col = lax.broadcasted_iota(I32, (qblk, s_len), 1)
    o_ref[...] = jnp.where(col <= row, sc, -jnp.inf)


def scores_prompt(qib, kwo, n_p, l_p, qblk):
    nq = l_p // qblk
    return pl.pallas_call(
        functools.partial(_scores_prompt_kernel, qblk=qblk),
        grid=(n_p, nq),
        in_specs=[pl.BlockSpec((qblk, C_QI), lambda b, q: (b * nq + q, 0)),
                  pl.BlockSpec((qblk, LANES), lambda b, q: (b * nq + q, 0)),
                  pl.BlockSpec((l_p, LANES), lambda b, q: (b, 0))],
        out_specs=pl.BlockSpec((qblk, l_p), lambda b, q: (b * nq + q, 0)),
        out_shape=jax.ShapeDtypeStruct((n_p * l_p, l_p), F32),
        compiler_params=_cparams(("parallel", "parallel")),
        name="scores_prompt",
    )(qib, kwo, kwo)


SROWS = 16


def _scores_sample_kernel(pt_ref, qi_ref, wi_ref, page_ref, new_ref, o_ref, *, n_pages, l_s):
    j = pl.program_id(1)

    @pl.when(j < n_pages)
    def _():
        kb = page_ref[0].astype(BF16)
        o_ref[0] = _index_heads(qi_ref[0], wi_ref[0], kb)

    @pl.when(j == n_pages)
    def _():
        page = page_ref.shape[1]
        kb = jnp.concatenate([new_ref[0][:, :D_IDX], jnp.zeros((page - SROWS, D_IDX), F32)], axis=0).astype(BF16)
        sc = _index_heads(qi_ref[0], wi_ref[0], kb)
        row = lax.broadcasted_iota(I32, (SROWS, page), 0)
        col = lax.broadcasted_iota(I32, (SROWS, page), 1)
        o_ref[0] = jnp.where((col <= row) & (col < l_s), sc, -jnp.inf)


def scores_sample(page_table, qi3, wi3, cache_kidx, new3, l_s):
    n_s, n_pages = page_table.shape
    page = cache_kidx.shape[1]
    return pl.pallas_call(
        functools.partial(_scores_sample_kernel, n_pages=n_pages, l_s=l_s),
        grid_spec=pltpu.PrefetchScalarGridSpec(
            num_scalar_prefetch=1,
            grid=(n_s, n_pages + 1),
            in_specs=[pl.BlockSpec((1, SROWS, C_QI), lambda b, j, pt: (b, 0, 0)),
                      pl.BlockSpec((1, SROWS, LANES), lambda b, j, pt: (b, 0, 0)),
                      pl.BlockSpec((1, page, D_IDX), lambda b, j, pt: (pt[b, jnp.minimum(j, n_pages - 1)], 0, 0)),
                      pl.BlockSpec((1, SROWS, LANES), lambda b, j, pt: (b, 0, 0))],
            out_specs=pl.BlockSpec((1, SROWS, page), lambda b, j, pt: (b, 0, j)),
        ),
        out_shape=jax.ShapeDtypeStruct((n_s, SROWS, (n_pages + 1) * page), F32),
        compiler_params=_cparams(("parallel", "arbitrary")),
        name="scores_sample",
    )(page_table, qi3, wi3, cache_kidx, new3)


def _select_kernel(sc_ref, utri_ref, o_ref, key_scr, *, topk, chunk):
    rows, s_len = sc_ref.shape
    bits = lax.bitcast_convert_type(sc_ref[...], I32)
    key_scr[...] = bits ^ ((bits >> 31) & jnp.int32(0x7FFFFFFF))
    sign = jnp.int32(-2 ** 31)
    kf = jnp.float32(topk)

    def bisect(it, t_u):
        cand = t_u | (jnp.int32(1) << (31 - it))
        cnt = jnp.sum(jnp.where(key_scr[...] >= (cand ^ sign), 1.0, 0.0), axis=-1, keepdims=True)
        return jnp.where(cnt >= kf, cand, t_u)

    t_u = lax.fori_loop(0, 32, bisect, jnp.zeros((rows, 1), I32))
    thr = t_u ^ sign
    need = kf - jnp.sum(jnp.where(key_scr[...] > thr, 1.0, 0.0), axis=-1, keepdims=True)
    utri = utri_ref[...]
    run = jnp.zeros((rows, 1), F32)
    for c in range(s_len // chunk):
        cs = slice(c * chunk, (c + 1) * chunk)
        keyc = key_scr[:, cs]
        eqc = jnp.where(keyc == thr, 1.0, 0.0)
        before = _dot(eqc.astype(BF16), utri) + run
        pick = (keyc > thr) | ((keyc == thr) & (before < need))
        o_ref[:, cs] = jnp.where(pick & (sc_ref[:, cs] > -jnp.inf), 1.0, 0.0).astype(o_ref.dtype)
        run = run + jnp.sum(eqc, axis=-1, keepdims=True)


def topk_select(scores, topk, rblk, chunk):
    r, s_len = scores.shape
    assert r % rblk == 0 and s_len % chunk == 0
    utri = (jnp.arange(chunk)[:, None] < jnp.arange(chunk)[None, :]).astype(BF16)
    return pl.pallas_call(
        functools.partial(_select_kernel, topk=topk, chunk=chunk),
        grid=(r // rblk,),
        in_specs=[pl.BlockSpec((rblk, s_len), lambda i: (i, 0)), pl.BlockSpec((chunk, chunk), lambda i: (0, 0))],
        out_specs=pl.BlockSpec((rblk, s_len), lambda i: (i, 0)),
        out_shape=jax.ShapeDtypeStruct((r, s_len), BF16),
        scratch_shapes=[pltpu.VMEM((rblk, s_len), I32)],
        compiler_params=_cparams(("parallel",)),
        name="topk_select",
    )(scores, utri)


NEG_BIG = -1e30


def _attn_init(m_scr, l_scr, acc_scr):
    m_scr[...] = jnp.full(m_scr.shape, NEG_BIG, F32)
    l_scr[...] = jnp.zeros(l_scr.shape, F32)
    acc_scr[...] = jnp.zeros(acc_scr.shape, F32)


def _attn_block(q, k, v, mask, m_scr, l_scr, acc_scr):
    on = jnp.concatenate([mask.astype(F32)] * GROUP_C, axis=0) > 0.0
    for kh in range(KVH_C):
        qh = jnp.concatenate([q[:, (kh * GROUP_C + g) * DH_C:(kh * GROUP_C + g + 1) * DH_C]
                              for g in range(GROUP_C)], axis=0)
        ks = k[:, kh * DH_C:(kh + 1) * DH_C]
        s = lax.dot_general(qh, ks, (((1,), (1,)), ((), ())), preferred_element_type=F32) * (DH_C ** -0.5)
        s = jnp.where(on, s, NEG_BIG)
        m_old = m_scr[kh]
        m_new = jnp.maximum(m_old, jnp.max(s, axis=-1, keepdims=True))
        alpha = jnp.exp(m_old - m_new)
        p = jnp.where(on, jnp.exp(s - m_new), 0.0)
        l_scr[kh] = alpha * l_scr[kh] + jnp.sum(p, axis=-1, keepdims=True)
        acc_scr[kh] = alpha * acc_scr[kh] + _dot(p.astype(BF16), v[:, kh * DH_C:(kh + 1) * DH_C])
        m_scr[kh] = m_new


def _attn_finish(o_ref, nq, l_scr, acc_scr, lead=None):
    for kh in range(KVH_C):
        out = acc_scr[kh] / l_scr[kh]
        for g in range(GROUP_C):
            hs = slice((kh * GROUP_C + g) * DH_C, (kh * GROUP_C + g + 1) * DH_C)
            val = out[g * nq:(g + 1) * nq, :].astype(o_ref.dtype)
            if lead is None:
                o_ref[:, hs] = val
            else:
                o_ref[lead, :, hs] = val


def _attn_prompt_kernel(q_ref, k_ref, v_ref, mask_ref, o_ref, m_scr, l_scr, acc_scr, *, qblk, kblk):
    qb = pl.program_id(1)
    kb = pl.program_id(2)

    @pl.when(kb == 0)
    def _():
        _attn_init(m_scr, l_scr, acc_scr)

    @pl.when(kb * kblk <= qb * qblk + qblk - 1)
    def _():
        _attn_block(q_ref[...], k_ref[...], v_ref[...], mask_ref[...], m_scr, l_scr, acc_scr)

    @pl.when(kb == pl.num_programs(2) - 1)
    def _():
        _attn_finish(o_ref, qblk, l_scr, acc_scr)


def attn_prompt(qb_, kb_, vb_, mask, n_p, l_p, qblk, kblk):
    nq, nk = l_p // qblk, l_p // kblk
    last = lambda q, k: jnp.minimum(k, (q * qblk + qblk - 1) // kblk)
    rows = GROUP_C * qblk
    return pl.pallas_call(
        functools.partial(_attn_prompt_kernel, qblk=qblk, kblk=kblk),
        grid=(n_p, nq, nk),
        in_specs=[pl.BlockSpec((qblk, H_C * DH_C), lambda b, q, k: (b * nq + q, 0)),
                  pl.BlockSpec((kblk, KVH_C * DH_C), lambda b, q, k: (b * nk + last(q, k), 0)),
                  pl.BlockSpec((kblk, KVH_C * DH_C), lambda b, q, k: (b * nk + last(q, k), 0)),
                  pl.BlockSpec((qblk, kblk), lambda b, q, k: (b * nq + q, last(q, k)))],
        out_specs=pl.BlockSpec((qblk, H_C * DH_C), lambda b, q, k: (b * nq + q, 0)),
        out_shape=jax.ShapeDtypeStruct((n_p * l_p, H_C * DH_C), BF16),
        scratch_shapes=[pltpu.VMEM((KVH_C, rows, 1), F32), pltpu.VMEM((KVH_C, rows, 1), F32),
                        pltpu.VMEM((KVH_C, rows, DH_C), F32)],
        compiler_params=_cparams(("parallel", "parallel", "arbitrary")),
        name="attn_prompt",
    )(qb_, kb_, vb_, mask)


def _attn_sample_kernel(pt_ref, q_ref, kp_ref, vp_ref, kn_ref, vn_ref, mask_ref, o_ref, m_scr, l_scr, acc_scr,
                        *, n_pages):
    j = pl.program_id(1)

    @pl.when(j == 0)
    def _():
        _attn_init(m_scr, l_scr, acc_scr)

    @pl.when(j < n_pages)
    def _():
        _attn_block(q_ref[0], kp_ref[0].astype(BF16), vp_ref[0].astype(BF16), mask_ref[...], m_scr, l_scr, acc_scr)

    @pl.when(j == n_pages)
    def _():
        page = kp_ref.shape[1]
        zpad = jnp.zeros((page - SROWS, KVH_C * DH_C), BF16)
        kn = jnp.concatenate([kn_ref[0], zpad], axis=0)
        vn = jnp.concatenate([vn_ref[0], zpad], axis=0)
        _attn_block(q_ref[0], kn, vn, mask_ref[...], m_scr, l_scr, acc_scr)
        _attn_finish(o_ref, SROWS, l_scr, acc_scr, lead=0)


def attn_sample(page_table, q3, cache_k2, cache_v2, knew3, vnew3, mask):
    n_s, n_pages = page_table.shape
    page = cache_k2.shape[1]
    kvw = KVH_C * DH_C
    rows = GROUP_C * SROWS
    pmap = lambda b, j, pt: (pt[b, jnp.minimum(j, n_pages - 1)], 0, 0)
    own = lambda b, j, pt: (b, 0, 0)
    return pl.pallas_call(
        functools.partial(_attn_sample_kernel, n_pages=n_pages),
        grid_spec=pltpu.PrefetchScalarGridSpec(
            num_scalar_prefetch=1,
            grid=(n_s, n_pages + 1),
            in_specs=[pl.BlockSpec((1, SROWS, H_C * DH_C), own),
                      pl.BlockSpec((1, page, kvw), pmap), pl.BlockSpec((1, page, kvw), pmap),
                      pl.BlockSpec((1, SROWS, kvw), own), pl.BlockSpec((1, SROWS, kvw), own),
                      pl.BlockSpec((SROWS, page), lambda b, j, pt: (b, j))],
            out_specs=pl.BlockSpec((1, SROWS, H_C * DH_C), own),
            scratch_shapes=[pltpu.VMEM((KVH_C, rows, 1), F32), pltpu.VMEM((KVH_C, rows, 1), F32),
                            pltpu.VMEM((KVH_C, rows, DH_C), F32)],
        ),
        out_shape=jax.ShapeDtypeStruct((n_s, SROWS, H_C * DH_C), BF16),
        compiler_params=_cparams(("parallel", "arbitrary")),
        name="attn_sample",
    )(page_table, q3, cache_k2, cache_v2, knew3, vnew3, mask)


def c_params(li, w):
    p = {}
    w_in = w["c_w_in"][li]
    p["w_qkv"] = w_in[:, :C_QKV].astype(BF16)
    p["w_qi"] = w_in[:, C_QKV:C_QKV + C_QI].astype(BF16)
    p["w_kw"] = _pad_cols(w_in[:, C_QKV + C_QI:], LANES).astype(BF16)
    p["ln_g"] = _pad_cols(w["c_kidx_ln_g"][li][None, :], LANES)
    p["ln_b"] = _pad_cols(w["c_kidx_ln_b"][li][None, :], LANES)
    p["w_out"] = w["c_w_out"][li].astype(BF16)
    return p


def _pad_sample_rows(a, n_s, l_s):
    a3 = a.reshape(n_s, l_s, a.shape[1])
    return jnp.pad(a3, ((0, 0), (0, SROWS - l_s), (0, 0)))


def c_layer(xb, geo, p, cache_k, cache_v, cache_kidx, page_table, *, tm, qblk, kblk):
    n_p, l_p, n_s, l_s = geo
    tp = n_p * l_p
    n_pages, page = page_table.shape[1], cache_k.shape[1]
    past = n_pages * page
    qkv = matmul([xb], [p["w_qkv"]], tm, 1024, name="c_in_qkv")
    qi = matmul([xb], [p["w_qi"]], tm, C_QI, name="c_in_qi")
    kw = matmul([xb], [p["w_kw"]], tm, LANES, name="c_in_kw")
    pos = jnp.concatenate([jnp.tile(jnp.arange(l_p, dtype=I32), n_p),
                           jnp.tile(past + jnp.arange(l_s, dtype=I32), n_s)])
    q_b, k_f, k_b, v_b, qi_b, kwo = c_post(qkv, qi, kw, _rope_tables(pos), p["ln_g"], p["ln_b"], tm)
    v_f = qkv[:, (H_C + KVH_C) * DH_C:]
    sc_p = scores_prompt(qi_b, kwo, n_p, l_p, qblk)
    mask_p = topk_select(sc_p, min(TOPK_MAX, l_p // 4), qblk, 256)
    o_p = attn_prompt(q_b, k_b, v_b, mask_p, n_p, l_p, qblk, kblk)
    qi3 = _pad_sample_rows(qi_b[tp:], n_s, l_s)
    kw3 = _pad_sample_rows(kwo[tp:], n_s, l_s)
    sc_s = scores_sample(page_table, qi3, kw3, cache_kidx, kw3, l_s)
    l_tot = past + l_s
    mask_s = topk_select(sc_s.reshape(n_s * SROWS, -1), min(TOPK_MAX, l_tot // 4), LANES, LANES)
    kvw = KVH_C * DH_C
    o_s3 = attn_sample(page_table, _pad_sample_rows(q_b[tp:], n_s, l_s),
                       cache_k.reshape(-1, page, kvw), cache_v.reshape(-1, page, kvw),
                       _pad_sample_rows(k_b[tp:], n_s, l_s), _pad_sample_rows(v_b[tp:], n_s, l_s), mask_s)
    o = jnp.concatenate([o_p, o_s3[:, :l_s].reshape(n_s * l_s, -1)], axis=0)
    h = matmul([o], [p["w_out"]], tm, 1024, name="c_out")
    heads = lambda a, n, l: a.reshape(n, l, KVH_C, DH_C)
    new_p = (heads(k_f[:tp], n_p, l_p), heads(v_f[:tp], n_p, l_p), kwo[:tp, :D_IDX].reshape(n_p, l_p, D_IDX))
    new_s = (heads(k_f[tp:], n_s, l_s), heads(v_f[tp:], n_s, l_s), kwo[tp:, :D_IDX].reshape(n_s, l_s, D_IDX))
    return h, new_p, new_s


TM = 256
BLK = 128
TM_DMA = 64
QBLK = 128
KBLK = 512


def kernel(x_prompt, x_sample, state_shift, state_wkv, state_ssm_re, state_ssm_im, cache_k, cache_v, cache_kidx, page_table, ab_w_in, ab_mu, rwkv_w0, rwkv_w2, rwkv_a0, rwkv_a2, rwkv_g2, rwkv_k_k, rwkv_k_a, rwkv_r_k, rwkv_ln_g, rwkv_ln_b, s5_lam_re, s5_lam_im, s5_log_dt, s5_b_re, s5_b_im, s5_c_re, s5_c_im, s5_d, s5_w_glu, s5_b_glu, ab_w_out, c_w_in, c_kidx_ln_g, c_kidx_ln_b, c_w_out, ln_mix_g, ln_mix_b, ln_ffn_g, ln_ffn_b, moe_w_router, moe_b_router, moe_w_gate, moe_w_up, moe_w_down, moe_ws_gate, moe_ws_up, moe_ws_down):
    w = dict(locals())
    n_p, l_p, _ = x_prompt.shape
    n_s, l_s, _ = x_sample.shape
    geo = (n_p, l_p, n_s, l_s)
    tp = n_p * l_p
    y = jnp.concatenate([x_prompt.reshape(-1, D_MODEL), x_sample.reshape(-1, D_MODEL)], axis=0)
    yb = y.astype(BF16)
    outs = {k: [] for k in ("kp", "vp", "kip", "ks", "vs", "kis", "shp", "shs", "wkvp", "wkvs",
                            "srp", "sip", "srs", "sis")}
    for layer in range(DEPTH):
        li = layer // 2
        if layer % 2 == 0:
            h, shift, wkv, ssm_p, ssm_s = ab_layer(yb, geo, ab_params(li, w, l_s), state_shift[li], state_wkv[li],
                                                   state_ssm_re[li], state_ssm_im[li], tm=TM, blk=BLK)
            for k, v in zip(("shp", "shs", "wkvp", "wkvs", "srp", "sip", "srs", "sis"),
                            (*shift, *wkv, *ssm_p, *ssm_s)):
                outs[k].append(v)
        else:
            h, new_p, new_s = c_layer(yb, geo, c_params(li, w), cache_k[li], cache_v[li], cache_kidx[li],
                                      page_table, tm=TM, qblk=QBLK, kblk=KBLK)
            for k, v in zip(("kp", "vp", "kip", "ks", "vs", "kis"), (*new_p, *new_s)):
                outs[k].append(v)
        y, yb = add_ln(y, h, ln_mix_g[layer], ln_mix_b[layer], TM)
        y, yb = moe_layer(y, yb, moe_params(layer, w), ln_ffn_g[layer], ln_ffn_b[layer], tm=TM, tm_dma=TM_DMA)
    st = lambda k: jnp.stack(outs[k])
    return (y[:tp].reshape(n_p, l_p, D_MODEL), y[tp:].reshape(n_s, l_s, D_MODEL),
            st("kp"), st("vp"), st("kip"), st("ks"), st("vs"), st("kis"),
            st("shp"), st("shs"), st("wkvp"), st("wkvs"), st("srp"), st("sip"), st("srs"), st("sis"))
```

```python
import functools
import math

import jax
import jax.numpy as jnp
from jax import lax
from jax.experimental import pallas as pl
from jax.experimental.pallas import tpu as pltpu

F32, BF16, I32 = jnp.float32, jnp.bfloat16, jnp.int32

D_MODEL = 2048
DEPTH = 2
ALPHA = (2.0 * DEPTH) ** 0.25
LN_EPS = 1e-5
C_A = 1024
HS_A = 64
H_A = 16
LORA_W, LORA_A, LORA_G = 64, 64, 160
A_PROJ = 3 * C_A + LORA_W + LORA_A + LORA_G
GN_EPS_A = 64e-5
C_B = 1024
S5_P = 16
S5_GROUPS = 64
S5_N = 64
S5_STATE = S5_GROUPS * S5_N
DH_C = 128
H_C = 16
KVH_C = 4
GROUP_C = 4
H_IDX = 16
D_IDX = 64
TOPK_MAX = 256
ROPE_THETA = 10000.0
N_EXPERTS = 64
TOP_K = 8
N_EXPERT_GROUPS = 8
TOPK_GROUPS = 4
D_EXPERT = 512
ROUTED_SCALE = 2.5

LANES = 128
SUBLANES = 8
VMEM_LIMIT = 56 * 1024 * 1024

LORA_PAD = 512
EXP_M05 = math.exp(-0.5)


def _cparams(sem):
    return pltpu.CompilerParams(dimension_semantics=sem, vmem_limit_bytes=VMEM_LIMIT)


def _dot(a, b):
    return jnp.dot(a, b, preferred_element_type=F32)


def _split3(a):
    a1 = a.astype(BF16)
    r1 = a - a1.astype(F32)
    a2 = r1.astype(BF16)
    a3 = (r1 - a2.astype(F32)).astype(BF16)
    return a1, a2, a3


def _dot_sel(a, sel_bf16):
    a1, a2, a3 = _split3(a)
    return _dot(a1, sel_bf16) + _dot(a2, sel_bf16) + _dot(a3, sel_bf16)


def _dot3(a, b):
    a1 = a.astype(BF16)
    a2 = (a - a1.astype(F32)).astype(BF16)
    b1 = b.astype(BF16)
    b2 = (b - b1.astype(F32)).astype(BF16)
    return _dot(a1, b1) + _dot(a1, b2) + _dot(a2, b1)


def _mm_kernel(*refs, n_lhs):
    o_ref = refs[-1]
    acc = None
    for i in range(n_lhs):
        d = _dot(refs[i][...], refs[n_lhs + i][...])
        acc = d if acc is None else acc + d
    o_ref[...] = acc.astype(o_ref.dtype)


def matmul(xs, ws, tm, tn, out_dtype=F32, name="mm"):
    m = xs[0].shape[0]
    n = ws[0].shape[1]
    assert m % tm == 0 and n % tn == 0, (m, tm, n, tn)
    in_specs = ([pl.BlockSpec((tm, x.shape[1]), lambda j, i: (i, 0)) for x in xs]
                + [pl.BlockSpec((w.shape[0], tn), lambda j, i: (0, j)) for w in ws])
    return pl.pallas_call(
        functools.partial(_mm_kernel, n_lhs=len(xs)),
        grid=(n // tn, m // tm),
        in_specs=in_specs,
        out_specs=pl.BlockSpec((tm, tn), lambda j, i: (i, j)),
        out_shape=jax.ShapeDtypeStruct((m, n), out_dtype),
        compiler_params=_cparams(("parallel", "parallel")),
        name=name,
    )(*xs, *ws)


def _add_ln_kernel(x_ref, h_ref, g_ref, b_ref, o_ref, ob_ref):
    z = ALPHA * x_ref[...] + h_ref[...]
    mu = jnp.mean(z, axis=-1, keepdims=True)
    d = z - mu
    var = jnp.mean(d * d, axis=-1, keepdims=True)
    y = d * lax.rsqrt(var + LN_EPS) * g_ref[...] + b_ref[...]
    o_ref[...] = y
    ob_ref[...] = y.astype(BF16)


def add_ln(x, h, g, b, tm):
    t, d = x.shape
    row = pl.BlockSpec((tm, d), lambda i: (i, 0))
    vec = pl.BlockSpec((1, d), lambda i: (0, 0))
    return pl.pallas_call(
        _add_ln_kernel,
        grid=(t // tm,),
        in_specs=[row, row, vec, vec],
        out_specs=[row, row],
        out_shape=[jax.ShapeDtypeStruct((t, d), F32), jax.ShapeDtypeStruct((t, d), BF16)],
        compiler_params=_cparams(("parallel",)),
        name="add_ln",
    )(x, h, g.reshape(1, d), b.reshape(1, d))


def _rwkv_prep_kernel(rkv_ref, rkvp_ref, lo_ref, lop_ref, mu1_ref, mu2_ref, w0_ref, w2_ref, a0_ref, a2_ref,
                      g2_ref, kk_ref, ka_ref, rk_ref, gsel_ref, gselt_ref,
                      kk_o, wr_o, w_o, b_o, k_o, v_o, br_o, kr_o, g_o, bonus_o):
    x = rkv_ref[...]
    m = x + (rkvp_ref[...] - x) * mu1_ref[...]
    r = m[:, :C_A]
    k = m[:, C_A:2 * C_A]
    v = m[:, 2 * C_A:]
    l = lo_ref[...]
    lm = l + (lop_ref[...] - l) * mu2_ref[...]
    lw = lm[:, :LANES]
    la = lm[:, LANES:2 * LANES]
    lg = lm[:, 2 * LANES:]
    w_raw = w0_ref[...] + _dot3(jnp.tanh(lw), w2_ref[...])
    decay = jnp.exp(-EXP_M05 * jax.nn.sigmoid(w_raw))
    a = jax.nn.sigmoid(a0_ref[...] + _dot3(la, a2_ref[...]))
    g = _dot3(jax.nn.sigmoid(lg), g2_ref[...])
    gsel = gsel_ref[...]
    gselt = gselt_ref[...]
    kk = k * kk_ref[...]
    nrm = jnp.sqrt(_dot_sel(kk * kk, gsel))
    inv = 1.0 / jnp.maximum(nrm, 1e-12)
    kk = kk * _dot_sel(inv, gselt)
    k_mod = k * (1.0 + (a - 1.0) * ka_ref[...])
    sb = _dot_sel(r * k_mod * rk_ref[...], gsel)
    bonus_o[...] = _dot_sel(sb, gselt) * v
    g_o[...] = g
    b = kk * a
    kk_o[...] = kk
    wr_o[...] = decay * r
    w_o[...] = decay
    b_o[...] = b
    k_o[...] = k_mod
    v_o[...] = v
    br_o[...] = _dot_sel(_dot_sel(b * r, gsel), gselt)
    kr_o[...] = _dot_sel(_dot_sel(k_mod * r, gsel), gselt)


RWKV_ROWS = 8


def rwkv_prep(rkv, rkv_prev, lora, lora_prev, p, tm):
    t = rkv.shape[0]
    row = lambda w: pl.BlockSpec((tm, w), lambda i: (i, 0))
    full = lambda a: pl.BlockSpec(a.shape, lambda i: (0,) * a.ndim)
    consts = [p["mu_rkv"], p["mu_lora"], p["w0"], p["w2"], p["a0"], p["a2"], p["g2"], p["k_k"], p["k_a"], p["r_k"],
              p["gsel"], p["gselt"]]
    n_out = RWKV_ROWS + 2
    return pl.pallas_call(
        _rwkv_prep_kernel,
        grid=(t // tm,),
        in_specs=[row(3 * C_A), row(3 * C_A), row(LORA_PAD), row(LORA_PAD)] + [full(c) for c in consts],
        out_specs=[row(C_A)] * n_out,
        out_shape=[jax.ShapeDtypeStruct((t, C_A), F32)] * n_out,
        compiler_params=_cparams(("parallel",)),
        name="rwkv_prep",
    )(rkv, rkv_prev, lora, lora_prev, *consts)


PAIRS = H_A // 2
LHS_PER_PAIR = 4


def _wkv_steps(rows, row_base, n_steps, npar, s_scr, bd_ref, store_o):
    sub = lax.broadcasted_iota(I32, (HS_A, LANES), 0)
    lane = lax.broadcasted_iota(I32, (HS_A, LANES), 1)
    diag = (lane % HS_A) == sub
    bd = bd_ref[...]
    blk_rows = LHS_PER_PAIR * HS_A

    def step(t, carry):
        vec = [[ref[pl.ds(row_base[s] + t, 1), :] for ref in rows[s]] for s in range(npar)]
        lhs = []
        for s in range(npar):
            kk, wr, _, _, _, v, _, _ = vec[s]
            for p in range(PAIRS):
                ls = slice(p * LANES, (p + 1) * LANES)
                st = s_scr[s * PAIRS + p]
                pk = st * kk[:, ls]
                hi = pk.astype(BF16)
                lo = (pk - hi.astype(F32)).astype(BF16)
                q2 = (st * wr[:, ls]).astype(BF16)
                ve = jnp.where(diag, v[:, ls], 0.0).astype(BF16)
                lhs += [hi, lo, q2, ve]
        res = _dot(jnp.concatenate(lhs, axis=0), bd)
        for s in range(npar):
            _, _, w, b, k, v, br, kr = vec[s]
            o_parts = []
            for p in range(PAIRS):
                ls = slice(p * LANES, (p + 1) * LANES)
                r0 = (s * PAIRS + p) * blk_rows
                z1 = res[r0:r0 + HS_A] + res[r0 + HS_A:r0 + 2 * HS_A]
                z2 = res[r0 + 2 * HS_A:r0 + 3 * HS_A]
                vb = res[r0 + 3 * HS_A:r0 + 4 * HS_A]
                st = s_scr[s * PAIRS + p]
                s_scr[s * PAIRS + p] = st * w[:, ls] - z1 * b[:, ls] + vb * k[:, ls]
                z1r = jnp.sum(jnp.where(diag, z1, 0.0), axis=0, keepdims=True)
                z2r = jnp.sum(jnp.where(diag, z2, 0.0), axis=0, keepdims=True)
                o_parts.append(z2r - z1r * br[:, ls] + v[:, ls] * kr[:, ls])
            store_o(s, t, jnp.concatenate(o_parts, axis=1))
        return carry

    lax.fori_loop(0, n_steps, step, 0)


def _pack_state(s_ref, q, s_scr, slot):
    for p in range(PAIRS):
        s_scr[slot * PAIRS + p] = jnp.concatenate([s_ref[q, 2 * p], s_ref[q, 2 * p + 1]], axis=1)


def _unpack_state(s_scr, slot, s_ref, q):
    for p in range(PAIRS):
        st = s_scr[slot * PAIRS + p]
        s_ref[q, 2 * p] = st[:, :HS_A]
        s_ref[q, 2 * p + 1] = st[:, HS_A:]


def _wkv_long_kernel(*refs, npar, blk):
    n_in = npar * RWKV_ROWS
    rows = [refs[s * RWKV_ROWS:(s + 1) * RWKV_ROWS] for s in range(npar)]
    bd_ref, o_ref, sout_ref, s_scr = refs[n_in:n_in + 4]
    c = pl.program_id(0)

    @pl.when(c == 0)
    def _():
        s_scr[...] = jnp.zeros(s_scr.shape, F32)

    def store_o(s, t, o_row):
        o_ref[s, pl.ds(t, 1), :] = o_row

    _wkv_steps(rows, [0] * npar, blk, npar, s_scr, bd_ref, store_o)

    @pl.when(c == pl.num_programs(0) - 1)
    def _():
        for s in range(npar):
            _unpack_state(s_scr, s, sout_ref, s)


WKV_NPAR_SHORT = 2
WKV_SPB = 8


def _wkv_short_kernel(*refs, seq_len, spb):
    rows_refs = refs[:RWKV_ROWS]
    s0_ref, bd_ref, o_ref, sout_ref, s_scr = refs[RWKV_ROWS:RWKV_ROWS + 5]
    c = pl.program_id(1)
    npar = WKV_NPAR_SHORT

    def group(gq, carry):
        qs = [gq * npar + s for s in range(npar)]
        for s in range(npar):
            _pack_state(s0_ref, qs[s], s_scr, s)

        def store_o(s, t, o_row):
            o_ref[pl.ds((c * spb + qs[s]) * seq_len + t, 1), :] = o_row

        _wkv_steps([rows_refs] * npar, [(c * spb + q) * seq_len for q in qs], seq_len, npar, s_scr, bd_ref, store_o)
        for s in range(npar):
            _unpack_state(s_scr, s, sout_ref, qs[s])
        return carry

    lax.fori_loop(0, spb // npar, group, 0)


def _pair_ones():
    half = jnp.arange(LANES) // HS_A
    return (half[:, None] == half[None, :]).astype(BF16)


def wkv_long(rows, *, n_seq, seq_len, blk):
    cps = seq_len // blk
    in_specs = [pl.BlockSpec((blk, C_A), functools.partial(lambda c, s: (s * cps + c, 0), s=s))
                for s in range(n_seq) for _ in range(RWKV_ROWS)]
    return pl.pallas_call(
        functools.partial(_wkv_long_kernel, npar=n_seq, blk=blk),
        grid=(cps,),
        in_specs=in_specs + [pl.BlockSpec((LANES, LANES), lambda c: (0, 0))],
        out_specs=[pl.BlockSpec((n_seq, blk, C_A), lambda c: (0, c, 0)),
                   pl.BlockSpec((n_seq, H_A, HS_A, HS_A), lambda c: (0, 0, 0, 0))],
        out_shape=[jax.ShapeDtypeStruct((n_seq, seq_len, C_A), F32),
                   jax.ShapeDtypeStruct((n_seq, H_A, HS_A, HS_A), F32)],
        scratch_shapes=[pltpu.VMEM((n_seq * PAIRS, HS_A, LANES), F32)],
        compiler_params=_cparams(("arbitrary",)),
        name="wkv_long",
    )(*(list(rows) * n_seq), _pair_ones())


def wkv_short(rows, s0, *, row0, n_seq, seq_len, blk, s0_seq0):
    assert row0 % blk == 0 and blk % seq_len == 0
    b0 = row0 // blk
    nsb = blk // seq_len
    spb = min(WKV_SPB, nsb)
    assert n_seq % nsb == 0 and nsb % spb == 0 and spb % WKV_NPAR_SHORT == 0 and s0_seq0 % spb == 0
    sub = nsb // spb
    sspec = lambda off: pl.BlockSpec((spb, H_A, HS_A, HS_A), lambda s, c: (off + s * sub + c, 0, 0, 0))
    return pl.pallas_call(
        functools.partial(_wkv_short_kernel, seq_len=seq_len, spb=spb),
        grid=(n_seq // nsb, sub),
        in_specs=[pl.BlockSpec((blk, C_A), lambda s, c: (b0 + s, 0))] * RWKV_ROWS
        + [sspec(s0_seq0 // spb), pl.BlockSpec((LANES, LANES), lambda s, c: (0, 0))],
        out_specs=[pl.BlockSpec((blk, C_A), lambda s, c: (s, 0)), sspec(0)],
        out_shape=[jax.ShapeDtypeStruct((n_seq * seq_len, C_A), F32),
                   jax.ShapeDtypeStruct((n_seq, H_A, HS_A, HS_A), F32)],
        scratch_shapes=[pltpu.VMEM((WKV_NPAR_SHORT * PAIRS, HS_A, LANES), F32)],
        compiler_params=_cparams(("parallel", "arbitrary")),
        name="wkv_short",
    )(*rows, s0, _pair_ones())


def _rwkv_post_kernel(o_in_ref, bonus_ref, g_ref, lng_ref, lnb_ref, gsel_ref, gselt_ref, o_ref):
    o = o_in_ref[...]
    gsel = gsel_ref[...]
    gselt = gselt_ref[...]
    mu = _dot_sel(o, gsel) * (1.0 / HS_A)
    d = o - _dot_sel(mu, gselt)
    var = _dot_sel(d * d, gsel) * (1.0 / HS_A)
    rstd = lax.rsqrt(var + GN_EPS_A)
    y = d * _dot_sel(rstd, gselt) * lng_ref[...] + lnb_ref[...]
    o_ref[...] = ((y + bonus_ref[...]) * g_ref[...]).astype(o_ref.dtype)


def rwkv_post(o_wkv, bonus, g, p, tm):
    t = bonus.shape[0]
    row = pl.BlockSpec((tm, C_A), lambda i: (i, 0))
    full = lambda a: pl.BlockSpec(a.shape, lambda i: (0,) * a.ndim)
    consts = [p["ln_g"], p["ln_b"], p["gsel"], p["gselt"]]
    return pl.pallas_call(
        _rwkv_post_kernel,
        grid=(t // tm,),
        in_specs=[row, row, row] + [full(c) for c in consts],
        out_specs=row,
        out_shape=jax.ShapeDtypeStruct((t, C_A), BF16),
        compiler_params=_cparams(("parallel",)),
        name="rwkv_post",
    )(o_wkv, bonus, g, *consts)


S5_KT = 4
S5_KU = C_B // S5_KT
S5_KH = S5_STATE // S5_KT
S5_LC = 512


def _gelu_tanh(x):
    return 0.5 * x * (1.0 + jnp.tanh(math.sqrt(2.0 / math.pi) * (x + 0.044715 * (x * x * x))))


def _s5_kernel(u_ref, wbr_ref, wbi_ref, lamk_ref, pre_ref, pim_ref, h0r_ref, h0i_ref, wcr_ref, wci_ref,
               dsk_ref, wglu_ref, bglu_ref, ob_ref, hr_out, hi_out, hre, him, car_re, car_im, y_scr,
               *, seq_len, blk):
    long_mode = seq_len >= blk
    period = SUBLANES if long_mode else seq_len
    u = u_ref[...]
    ub = u.astype(BF16)
    for kt in range(S5_KT):
        uk = ub[:, kt * S5_KU:(kt + 1) * S5_KU]
        hre[:, kt * S5_KH:(kt + 1) * S5_KH] = _dot(uk, wbr_ref[kt])
        him[:, kt * S5_KH:(kt + 1) * S5_KH] = _dot(uk, wbi_ref[kt])

    if long_mode:
        @pl.when(pl.program_id(1) == 0)
        def _():
            car_re[...] = jnp.zeros(car_re.shape, F32)
            car_im[...] = jnp.zeros(car_im.shape, F32)

    rowi = lax.broadcasted_iota(I32, (SUBLANES, 1), 0) % period
    for lc in range(S5_STATE // S5_LC):
        ls = slice(lc * S5_LC, (lc + 1) * S5_LC)
        lam = [(lamk_ref[2 * i:2 * i + 1, ls], lamk_ref[2 * i + 1:2 * i + 2, ls]) for i in range(3)]
        pre = pre_ref[:, ls]
        pim = pim_ref[:, ls]

        def tile(i, carry, ls=ls, lam=lam, pre=pre, pim=pim):
            r0 = pl.multiple_of(i * SUBLANES, SUBLANES)
            xr = hre[pl.ds(r0, SUBLANES), ls]
            xi = him[pl.ds(r0, SUBLANES), ls]
            for lvl, sh in enumerate((1, 2, 4)):
                if sh >= period:
                    break
                lr, li = lam[lvl]
                keep = rowi >= sh
                sr = jnp.where(keep, pltpu.roll(xr, sh, 0), 0.0)
                si = jnp.where(keep, pltpu.roll(xi, sh, 0), 0.0)
                xr, xi = xr + lr * sr - li * si, xi + lr * si + li * sr
            if long_mode:
                hr, hi = carry
            else:
                hr = h0r_ref[pl.ds(r0, SUBLANES), ls]
                hi = h0i_ref[pl.ds(r0, SUBLANES), ls]
            xr, xi = xr + pre * hr - pim * hi, xi + pre * hi + pim * hr
            hre[pl.ds(r0, SUBLANES), ls] = xr
            him[pl.ds(r0, SUBLANES), ls] = xi
            if long_mode:
                last_r = jnp.broadcast_to(xr[SUBLANES - 1:SUBLANES, :], xr.shape)
                last_i = jnp.broadcast_to(xi[SUBLANES - 1:SUBLANES, :], xi.shape)
                return (last_r, last_i)
            return carry

        if long_mode:
            init = (car_re[:, ls], car_im[:, ls])
            fin = lax.fori_loop(0, blk // SUBLANES, tile, init)
            car_re[:, ls] = fin[0]
            car_im[:, ls] = fin[1]
        else:
            lax.fori_loop(0, blk // SUBLANES, tile, 0)

    if long_mode:
        hr_out[0] = car_re[...]
        hi_out[0] = car_im[...]
    else:
        hr_out[...] = hre[...]
        hi_out[...] = him[...]

    for kt in range(S5_KT):
        hs = slice(kt * S5_KH, (kt + 1) * S5_KH)
        y_scr[:, kt * S5_KU:(kt + 1) * S5_KU] = (_dot(hre[:, hs].astype(BF16), wcr_ref[kt])
                                                 - _dot(him[:, hs].astype(BF16), wci_ref[kt]))
    y = y_scr[...] + dsk_ref[...] * u
    z = _gelu_tanh(y)
    gate = jax.nn.sigmoid(_dot(z.astype(BF16), wglu_ref[...]) + bglu_ref[...])
    ob_ref[...] = (z * gate).astype(ob_ref.dtype)


def s5_mix(u, p, h0r_rows, h0i_rows, *, row0, n_seq, seq_len, blk):
    long_mode = seq_len >= blk
    assert row0 % blk == 0
    b0 = row0 // blk
    t_out = n_seq * seq_len
    if long_mode:
        cps = seq_len // blk
        grid = (n_seq, cps)
        umap = lambda s, c: (b0 + s * cps + c, 0)
        omap = lambda s, c: (s * cps + c, 0)
        hspec = pl.BlockSpec((1, SUBLANES, S5_STATE), lambda s, c: (s, 0, 0))
        hshape = jax.ShapeDtypeStruct((n_seq, SUBLANES, S5_STATE), F32)
        h0spec = pl.BlockSpec((SUBLANES, S5_STATE), lambda s, c: (0, 0))
    else:
        assert t_out % blk == 0
        grid = (t_out // blk, 1)
        umap = lambda s, c: (b0 + s, 0)
        omap = lambda s, c: (s, 0)
        hspec = pl.BlockSpec((blk, S5_STATE), omap)
        hshape = jax.ShapeDtypeStruct((t_out, S5_STATE), F32)
        h0spec = pl.BlockSpec((blk, S5_STATE), omap)
    full = lambda a: pl.BlockSpec(a.shape, lambda s, c: (0,) * a.ndim)
    pre, pim = (p["pow_re8"], p["pow_im8"]) if long_mode else (p["pow_re_s"], p["pow_im_s"])
    args = [u, p["wb_re"], p["wb_im"], p["lamk"], pre, pim, h0r_rows, h0i_rows, p["wc_re"], p["wc_im"],
            p["d_skip"], p["w_glu"], p["b_glu"]]
    in_specs = [pl.BlockSpec((blk, C_B), umap)] + [full(a) for a in args[1:6]] + [h0spec, h0spec] \
        + [full(a) for a in args[8:]]
    return pl.pallas_call(
        functools.partial(_s5_kernel, seq_len=seq_len, blk=blk),
        grid=grid,
        in_specs=in_specs,
        out_specs=[pl.BlockSpec((blk, C_B), omap), hspec, hspec],
        out_shape=[jax.ShapeDtypeStruct((t_out, C_B), BF16), hshape, hshape],
        scratch_shapes=[pltpu.VMEM((blk, S5_STATE), F32), pltpu.VMEM((blk, S5_STATE), F32),
                        pltpu.VMEM((SUBLANES, S5_STATE), F32), pltpu.VMEM((SUBLANES, S5_STATE), F32),
                        pltpu.VMEM((blk, C_B), F32)],
        compiler_params=_cparams(("parallel", "arbitrary")),
        name="s5_long" if long_mode else "s5_short",
    )(*args)


def _pad_cols(a, n):
    return jnp.pad(a, ((0, 0), (0, n - a.shape[1])))


def _pad_rows(a, n):
    return jnp.pad(a, ((0, n - a.shape[0]), (0, 0)))


def _lora_layout(a):
    o = 3 * C_A
    return jnp.concatenate([_pad_cols(a[:, o:o + LORA_W], LANES),
                            _pad_cols(a[:, o + LORA_W:o + LORA_W + LORA_A], LANES),
                            _pad_cols(a[:, o + LORA_W + LORA_A:A_PROJ], 2 * LANES)], axis=1)


def _lora_unlayout(a):
    return jnp.concatenate([a[:, :LORA_W], a[:, LANES:LANES + LORA_A], a[:, 2 * LANES:2 * LANES + LORA_G]], axis=1)


def _cpow_table(lbr, lbi, n):
    res_r, res_i = [lbr], [lbi]
    for _ in range(n - 1):
        pr, pi = res_r[-1], res_i[-1]
        res_r.append(pr * lbr - pi * lbi)
        res_i.append(pr * lbi + pi * lbr)
    return (jnp.stack([x.reshape(-1) for x in res_r]), jnp.stack([x.reshape(-1) for x in res_i]))


def ab_params(li, w, sample_len):
    p = {}
    w_in = w["ab_w_in"][li]
    p["w_rkv"] = w_in[:, :3 * C_A].astype(BF16)
    p["w_lora"] = _lora_layout(w_in).astype(BF16)
    p["w_u"] = w_in[:, A_PROJ:].astype(BF16)
    mu = w["ab_mu"][li][None, :]
    p["mu_rkv"] = mu[:, :3 * C_A]
    p["mu_lora"] = _lora_layout(mu)
    row = lambda a: a.reshape(1, -1)
    p["w0"] = row(w["rwkv_w0"][li])
    p["w2"] = _pad_rows(w["rwkv_w2"][li], LANES)
    p["a0"] = row(w["rwkv_a0"][li])
    p["a2"] = _pad_rows(w["rwkv_a2"][li], LANES)
    p["g2"] = _pad_rows(w["rwkv_g2"][li], 2 * LANES)
    p["k_k"] = row(w["rwkv_k_k"][li])
    p["k_a"] = row(w["rwkv_k_a"][li])
    p["r_k"] = row(w["rwkv_r_k"][li])
    p["ln_g"] = row(w["rwkv_ln_g"][li])
    p["ln_b"] = row(w["rwkv_ln_b"][li])
    head_of_col = jnp.arange(C_A) // HS_A
    gsel = (head_of_col[:, None] == jnp.arange(LANES)[None, :])
    p["gsel"] = gsel.astype(BF16)
    p["gselt"] = gsel.T.astype(BF16)
    lr = jnp.minimum(w["s5_lam_re"][li], -1e-4)
    lim = w["s5_lam_im"][li]
    dt = jnp.exp(w["s5_log_dt"][li])[:, None]
    mag = jnp.exp(lr * dt)
    lbr, lbi = mag * jnp.cos(lim * dt), mag * jnp.sin(lim * dt)
    den = lr * lr + lim * lim
    pr, pim = lbr - 1.0, lbi
    fr = (pr * lr + pim * lim) / den
    fi = (pim * lr - pr * lim) / den
    br, bi = w["s5_b_re"][li], w["s5_b_im"][li]
    bbr = fr[..., None] * br - fi[..., None] * bi
    bbi = fr[..., None] * bi + fi[..., None] * br
    gpc = S5_GROUPS // S5_KT
    eye = jnp.eye(gpc, dtype=F32)

    def blockdiag_in(bb):
        b4 = bb.reshape(S5_KT, gpc, S5_N, S5_P)
        return jnp.einsum("kgnp,gh->kgphn", b4, eye).reshape(S5_KT, S5_KU, S5_KH).astype(BF16)

    def blockdiag_out(cc):
        c4 = cc.reshape(S5_KT, gpc, S5_P, S5_N)
        return jnp.einsum("kgpn,gh->kgnhp", c4, eye).reshape(S5_KT, S5_KH, S5_KU).astype(BF16)

    p["wb_re"], p["wb_im"] = blockdiag_in(bbr), blockdiag_in(bbi)
    p["wc_re"], p["wc_im"] = blockdiag_out(w["s5_c_re"][li]), blockdiag_out(w["s5_c_im"][li])
    pw_r, pw_i = _cpow_table(lbr, lbi, SUBLANES)
    p["pow_re8"], p["pow_im8"] = pw_r, pw_i
    reps = SUBLANES // sample_len
    p["pow_re_s"] = jnp.tile(pw_r[:sample_len], (reps, 1))
    p["pow_im_s"] = jnp.tile(pw_i[:sample_len], (reps, 1))
    p["lamk"] = jnp.stack([pw_r[0], pw_i[0], pw_r[1], pw_i[1], pw_r[3], pw_i[3]])
    p["d_skip"] = row(w["s5_d"][li])
    p["w_glu"] = w["s5_w_glu"][li].astype(BF16)
    p["b_glu"] = row(w["s5_b_glu"][li])
    w_out = w["ab_w_out"][li]
    p["w_out_a"] = w_out[:C_A].astype(BF16)
    p["w_out_b"] = w_out[C_A:].astype(BF16)
    return p


def _shift_rows(a, first_sample, geo):
    n_p, l_p, n_s, l_s = geo
    wd = a.shape[1]
    ap = a[:n_p * l_p].reshape(n_p, l_p, wd)
    prev_p = jnp.concatenate([jnp.zeros((n_p, 1, wd), a.dtype), ap[:, :-1]], axis=1)
    as_ = a[n_p * l_p:].reshape(n_s, l_s, wd)
    prev_s = jnp.concatenate([first_sample[:, None, :], as_[:, :-1]], axis=1)
    return jnp.concatenate([prev_p.reshape(-1, wd), prev_s.reshape(-1, wd)], axis=0)


def _last_rows(a, geo):
    n_p, l_p, n_s, l_s = geo
    wd = a.shape[1]
    return (a[:n_p * l_p].reshape(n_p, l_p, wd)[:, -1], a[n_p * l_p:].reshape(n_s, l_s, wd)[:, -1])


def ab_layer(xb, geo, p, state_shift, state_wkv, wkv_seq0, state_re, state_im, *, tm, blk):
    n_p, l_p, n_s, l_s = geo
    tp, ts = n_p * l_p, n_s * l_s
    rkv = matmul([xb], [p["w_rkv"]], tm, 1024, name="ab_in_rkv")
    lora = matmul([xb], [p["w_lora"]], tm, LORA_PAD, name="ab_in_lora")
    u = matmul([xb], [p["w_u"]], tm, C_B, name="ab_in_u")
    rkv_prev = _shift_rows(rkv, state_shift[:, :3 * C_A], geo)
    lora_prev = _shift_rows(lora, _lora_layout(state_shift), geo)
    prep = rwkv_prep(rkv, rkv_prev, lora, lora_prev, p, tm)
    rows, g, bonus = prep[:RWKV_ROWS], prep[RWKV_ROWS], prep[RWKV_ROWS + 1]
    o_p, wkv_p = wkv_long(rows, n_seq=n_p, seq_len=l_p, blk=blk)
    o_s, wkv_s = wkv_short(rows, state_wkv, row0=tp, n_seq=n_s, seq_len=l_s, blk=blk, s0_seq0=wkv_seq0)
    o_wkv = jnp.concatenate([o_p.reshape(tp, C_A), o_s], axis=0)
    o_a = rwkv_post(o_wkv, bonus, g, p, tm)
    dummy = jnp.zeros((SUBLANES, S5_STATE), F32)
    ob_p, hr_p, hi_p = s5_mix(u, p, dummy, dummy, row0=0, n_seq=n_p, seq_len=l_p, blk=blk)
    h0r = jnp.repeat(state_re.reshape(n_s, S5_STATE), l_s, axis=0)
    h0i = jnp.repeat(state_im.reshape(n_s, S5_STATE), l_s, axis=0)
    ob_s, hr_s, hi_s = s5_mix(u, p, h0r, h0i, row0=tp, n_seq=n_s, seq_len=l_s, blk=blk)
    o_b = jnp.concatenate([ob_p, ob_s], axis=0)
    h = matmul([o_a, o_b], [p["w_out_a"], p["w_out_b"]], tm, 1024, name="ab_out")
    rkv_lp, rkv_ls = _last_rows(rkv, geo)
    lo_lp, lo_ls = _last_rows(lora, geo)
    shift_p = jnp.concatenate([rkv_lp, _lora_unlayout(lo_lp)], axis=1)
    shift_s = jnp.concatenate([rkv_ls, _lora_unlayout(lo_ls)], axis=1)
    grp = lambda a, n: a.reshape(n, S5_GROUPS, S5_N)
    ssm_p = (grp(hr_p[:, SUBLANES - 1], n_p), grp(hi_p[:, SUBLANES - 1], n_p))
    ssm_s = (grp(hr_s[l_s - 1::l_s], n_s), grp(hi_s[l_s - 1::l_s], n_s))
    return h, (shift_p, shift_s), (wkv_p, wkv_s), ssm_p, ssm_s


def _router_kernel(x_ref, wr_ref, br_ref, ltri_ref, eidx_o, gate_o, rank_o, cnt_o, carry):
    i = pl.program_id(0)

    @pl.when(i == 0)
    def _():
        carry[...] = jnp.zeros(carry.shape, F32)

    tm = x_ref.shape[0]
    neg = -jnp.inf
    lane = lax.broadcasted_iota(I32, (tm, LANES), 1)
    real = lane < N_EXPERTS
    per_group = N_EXPERTS // N_EXPERT_GROUPS
    grp = lane // per_group
    scores = jax.nn.sigmoid(_dot3(x_ref[...], wr_ref[...]))
    biased = jnp.where(real, scores + br_ref[...], neg)

    gs = []
    for g in range(N_EXPERT_GROUPS):
        xg = jnp.where(grp == g, biased, neg)
        m1 = jnp.max(xg, axis=-1, keepdims=True)
        i1 = jnp.min(jnp.where(xg == m1, lane, LANES), axis=-1, keepdims=True)
        m2 = jnp.max(jnp.where(lane == i1, neg, xg), axis=-1, keepdims=True)
        gs.append(m1 + m2)
    keep = jnp.zeros((tm, LANES), jnp.bool_)
    for g in range(N_EXPERT_GROUPS):
        better = jnp.zeros((tm, 1), I32)
        for g2 in range(N_EXPERT_GROUPS):
            if g2 == g:
                continue
            wins = (gs[g2] > gs[g]) | ((gs[g2] == gs[g]) & (g2 < g))
            better = better + wins.astype(I32)
        keep = keep | ((grp == g) & (better < TOPK_GROUPS))
    masked = jnp.where(keep, biased, neg)

    cur = masked
    sel = jnp.zeros((tm, LANES), F32)
    es, ss = [], []
    for _ in range(TOP_K):
        m = jnp.max(cur, axis=-1, keepdims=True)
        ik = jnp.min(jnp.where(cur == m, lane, LANES), axis=-1, keepdims=True)
        hit = lane == ik
        ss.append(jnp.sum(jnp.where(hit, scores, 0.0), axis=-1, keepdims=True))
        es.append(ik)
        cur = jnp.where(hit, neg, cur)
        sel = jnp.where(hit, 1.0, sel)
    tot = ss[0]
    for k in range(1, TOP_K):
        tot = tot + ss[k]

    prefix = _dot(ltri_ref[...], sel.astype(BF16)) + carry[0:1, :]
    eidx = jnp.zeros((tm, LANES), I32)
    gate = jnp.zeros((tm, LANES), F32)
    rank = jnp.zeros((tm, LANES), I32)
    for k in range(TOP_K):
        slot = lane == k
        rk = jnp.sum(jnp.where(lane == es[k], prefix, 0.0), axis=-1, keepdims=True)
        eidx = jnp.where(slot, es[k], eidx)
        gate = jnp.where(slot, ss[k] / tot * ROUTED_SCALE, gate)
        rank = jnp.where(slot, rk.astype(I32), rank)
    eidx_o[...] = eidx
    gate_o[...] = gate
    rank_o[...] = rank
    carry[...] = carry[...] + jnp.sum(sel, axis=0, keepdims=True)
    cnt_o[...] = carry[...]


def moe_router(x, w_router_pad, b_router_pad, tm):
    t, d = x.shape
    ltri = (jnp.arange(tm)[:, None] > jnp.arange(tm)[None, :]).astype(BF16)
    row = pl.BlockSpec((tm, LANES), lambda i: (i, 0))
    out = jax.ShapeDtypeStruct((t, LANES), I32)
    return pl.pallas_call(
        _router_kernel,
        grid=(t // tm,),
        in_specs=[pl.BlockSpec((tm, d), lambda i: (i, 0)), pl.BlockSpec((d, LANES), lambda i: (0, 0)),
                  pl.BlockSpec((1, LANES), lambda i: (0, 0)), pl.BlockSpec((tm, tm), lambda i: (0, 0))],
        out_specs=[row, row, row, pl.BlockSpec((SUBLANES, LANES), lambda i: (0, 0))],
        out_shape=[out, jax.ShapeDtypeStruct((t, LANES), F32), out,
                   jax.ShapeDtypeStruct((SUBLANES, LANES), F32)],
        scratch_shapes=[pltpu.VMEM((SUBLANES, LANES), F32)],
        compiler_params=_cparams(("arbitrary",)),
        name="moe_router",
    )(x, w_router_pad, b_router_pad, ltri)


MOE_BM = 256
GATHER_UNROLL = 8


def _gather_rows_start(src_hbm, idx_ref, idx0, dst_buf, n_rows, sem):
    def body(j, c):
        pltpu.make_async_copy(src_hbm.at[pl.ds(idx_ref[idx0 + j], 1), :], dst_buf.at[pl.ds(j, 1), :], sem).start()
        return c

    lax.fori_loop(0, n_rows, body, 0, unroll=GATHER_UNROLL)


def _gather_rows_wait(src_hbm, dst_buf, n_rows, sem):
    pltpu.make_async_copy(src_hbm.at[pl.ds(0, n_rows), :], dst_buf, sem).wait()


def _expert_kernel(blk_e_ref, nused_ref, tok_ref, x_hbm, wg_ref, wu_ref, wd_ref, ys_ref, xbuf, sems, wgb, wub, wdb):
    i = pl.program_id(0)
    n_used = nused_ref[0]

    @pl.when(i == 0)
    def _():
        _gather_rows_start(x_hbm, tok_ref, 0, xbuf.at[0], MOE_BM, sems.at[0])

    @pl.when(i + 1 < n_used)
    def _():
        nxt = (i + 1) % 2
        _gather_rows_start(x_hbm, tok_ref, (i + 1) * MOE_BM, xbuf.at[nxt], MOE_BM, sems.at[nxt])

    @pl.when(i < n_used)
    def _():
        e = blk_e_ref[i]
        prev = blk_e_ref[jnp.maximum(i - 1, 0)]

        @pl.when((i == 0) | (e != prev))
        def _():
            wgb[...] = wg_ref[0].astype(BF16)
            wub[...] = wu_ref[0].astype(BF16)
            wdb[...] = wd_ref[0].astype(BF16)

        cur = i % 2
        _gather_rows_wait(x_hbm, xbuf.at[cur], MOE_BM, sems.at[cur])
        x = xbuf[cur].astype(BF16)
        hg = _dot(x, wgb[...])
        hu = _dot(x, wub[...])
        h = hg * jax.nn.sigmoid(hg) * hu
        ys_ref[...] = _dot(h.astype(BF16), wdb[...])

    @pl.when(i >= nused_ref[0])
    def _():
        ys_ref[...] = jnp.zeros(ys_ref.shape, F32)


def moe_experts(x, blk_e, n_used, tok_of_row, w_gate, w_up, w_down):
    d = x.shape[1]
    n_rows = tok_of_row.shape[0]
    n_blocks = n_rows // MOE_BM
    de = w_gate.shape[2]
    wmap = lambda i, be, nu, tk: (be[jnp.minimum(i, nu[0] - 1)], 0, 0)
    return pl.pallas_call(
        _expert_kernel,
        grid_spec=pltpu.PrefetchScalarGridSpec(
            num_scalar_prefetch=3,
            grid=(n_blocks,),
            in_specs=[pl.BlockSpec(memory_space=pl.ANY),
                      pl.BlockSpec((1, d, de), wmap), pl.BlockSpec((1, d, de), wmap), pl.BlockSpec((1, de, d), wmap)],
            out_specs=pl.BlockSpec((MOE_BM, d), lambda i, be, nu, tk: (i, 0)),
            scratch_shapes=[pltpu.VMEM((2, MOE_BM, d), F32), pltpu.SemaphoreType.DMA((2,)),
                            pltpu.VMEM((d, de), BF16), pltpu.VMEM((d, de), BF16), pltpu.VMEM((de, d), BF16)],
        ),
        out_shape=jax.ShapeDtypeStruct((n_rows, d), F32),
        compiler_params=_cparams(("arbitrary",)),
        name="moe_experts",
    )(blk_e, n_used, tok_of_row, x, w_gate, w_up, w_down)


def _shared_ffn_kernel(x_ref, wg_ref, wu_ref, wd_ref, o_ref):
    x = x_ref[...]
    hg = _dot(x, wg_ref[...])
    hu = _dot(x, wu_ref[...])
    h = hg * jax.nn.sigmoid(hg) * hu
    o_ref[...] = _dot(h.astype(BF16), wd_ref[...])


def shared_ffn(xb, wg, wu, wd, tm):
    t, d = xb.shape
    full = lambda a: pl.BlockSpec(a.shape, lambda i: (0, 0))
    row = pl.BlockSpec((tm, d), lambda i: (i, 0))
    return pl.pallas_call(
        _shared_ffn_kernel,
        grid=(t // tm,),
        in_specs=[row, full(wg), full(wu), full(wd)],
        out_specs=row,
        out_shape=jax.ShapeDtypeStruct((t, d), F32),
        compiler_params=_cparams(("parallel",)),
        name="shared_ffn",
    )(xb, wg, wu, wd)


def _combine_kernel(dest_ref, ys_hbm, gate_ref, x_ref, sh_ref, g_ref, b_ref, o_ref, ob_ref, buf, sems, *, tm):
    i = pl.program_id(0)
    n_asg = tm * TOP_K

    @pl.when(i == 0)
    def _():
        _gather_rows_start(ys_hbm, dest_ref, 0, buf.at[0], n_asg, sems.at[0])

    @pl.when(i + 1 < pl.num_programs(0))
    def _():
        nxt = (i + 1) % 2
        _gather_rows_start(ys_hbm, dest_ref, (i + 1) * n_asg, buf.at[nxt], n_asg, sems.at[nxt])

    cur = i % 2
    _gather_rows_wait(ys_hbm, buf.at[cur], n_asg, sems.at[cur])
    gate = gate_ref[...]
    f = sh_ref[...]
    for k in range(TOP_K):
        f = f + gate[:, k:k + 1] * buf[cur, k * tm:(k + 1) * tm, :]
    z = ALPHA * x_ref[...] + f
    mu = jnp.mean(z, axis=-1, keepdims=True)
    dlt = z - mu
    var = jnp.mean(dlt * dlt, axis=-1, keepdims=True)
    y = dlt * lax.rsqrt(var + LN_EPS) * g_ref[...] + b_ref[...]
    o_ref[...] = y
    ob_ref[...] = y.astype(BF16)


def moe_combine(ys, dest_flat, gates, x, shared, g, b, tm):
    t, d = x.shape
    row = pl.BlockSpec((tm, d), lambda i, dr: (i, 0))
    vec = pl.BlockSpec((1, d), lambda i, dr: (0, 0))
    return pl.pallas_call(
        functools.partial(_combine_kernel, tm=tm),
        grid_spec=pltpu.PrefetchScalarGridSpec(
            num_scalar_prefetch=1,
            grid=(t // tm,),
            in_specs=[pl.BlockSpec(memory_space=pl.ANY), pl.BlockSpec((tm, LANES), lambda i, dr: (i, 0)),
                      row, row, vec, vec],
            out_specs=[row, row],
            scratch_shapes=[pltpu.VMEM((2, TOP_K * tm, d), F32), pltpu.SemaphoreType.DMA((2,))],
        ),
        out_shape=[jax.ShapeDtypeStruct((t, d), F32), jax.ShapeDtypeStruct((t, d), BF16)],
        compiler_params=_cparams(("arbitrary",)),
        name="moe_combine",
    )(dest_flat, ys, gates, x, shared, g.reshape(1, d), b.reshape(1, d))


def moe_params(layer, w):
    p = {}
    p["w_router"] = _pad_cols(w["moe_w_router"][layer], LANES)
    p["b_router"] = _pad_cols(w["moe_b_router"][layer][None, :], LANES)
    stack = lambda a: a.reshape((-1,) + a.shape[2:])
    p["w_gate"], p["w_up"], p["w_down"] = stack(w["moe_w_gate"]), stack(w["moe_w_up"]), stack(w["moe_w_down"])
    p["e0"] = layer * N_EXPERTS
    p["ws_gate"] = w["moe_ws_gate"][layer].astype(BF16)
    p["ws_up"] = w["moe_ws_up"][layer].astype(BF16)
    p["ws_down"] = w["moe_ws_down"][layer].astype(BF16)
    return p


def moe_layer(y, yb, p, ln_g, ln_b, *, tm, tm_dma):
    t = y.shape[0]
    eidx, gates, rank, cnt = moe_router(y, p["w_router"], p["b_router"], tm)
    counts = cnt[0, :N_EXPERTS].astype(I32)
    padded = (counts + MOE_BM - 1) // MOE_BM * MOE_BM
    pad_end = jnp.cumsum(padded)
    pad_start = pad_end - padded
    e8 = eidx[:, :TOP_K]
    dest = pad_start[e8] + rank[:, :TOP_K]
    n_blocks = -(-(t * TOP_K + N_EXPERTS * (MOE_BM - 1)) // MOE_BM)
    n_rows = n_blocks * MOE_BM
    blk_start = jnp.arange(n_blocks, dtype=I32) * MOE_BM
    blk_e = jnp.sum((pad_end[None, :] <= blk_start[:, None]).astype(I32), axis=1)
    blk_e = jnp.minimum(blk_e, N_EXPERTS - 1) + p["e0"]
    n_used = (pad_end[-1:] // MOE_BM).astype(I32)
    tok = jnp.broadcast_to(jnp.arange(t, dtype=I32)[:, None], (t, TOP_K))
    tok_of_row = jnp.zeros((n_rows,), I32).at[dest.reshape(-1)].set(tok.reshape(-1))
    ys = moe_experts(y, blk_e, n_used, tok_of_row, p["w_gate"], p["w_up"], p["w_down"])
    shared = shared_ffn(yb, p["ws_gate"], p["ws_up"], p["ws_down"], tm)
    dest_tiles = dest.reshape(t // tm_dma, tm_dma, TOP_K).transpose(0, 2, 1).reshape(-1)
    return moe_combine(ys, dest_tiles, gates, y, shared, ln_g, ln_b, tm_dma)


C_QKV = (H_C + 2 * KVH_C) * DH_C
C_QI = H_IDX * D_IDX
IDX_SCALE = (H_IDX ** -0.5) * (D_IDX ** -0.5)


def _rot_half64(x, lane):
    return jnp.where(lane % D_IDX < D_IDX // 2, pltpu.roll(x, LANES - D_IDX // 2, 1), pltpu.roll(x, D_IDX // 2, 1))


def _c_post_kernel(qkv_ref, qi_ref, kw_ref, c128_ref, s128_ref, c64_ref, s64_ref, lng_ref, lnb_ref,
                   qb_o, kf_o, kb_o, vb_o, qib_o, kwo_o):
    tm = qkv_ref.shape[0]
    c128, s128 = c128_ref[...], s128_ref[...]
    c64, s64 = c64_ref[...], s64_ref[...]
    lane = lax.broadcasted_iota(I32, (tm, LANES), 1)
    for h in range(H_C + KVH_C):
        x = qkv_ref[:, h * DH_C:(h + 1) * DH_C]
        y = x * c128 + pltpu.roll(x, DH_C // 2, 1) * s128
        if h < H_C:
            qb_o[:, h * DH_C:(h + 1) * DH_C] = y.astype(BF16)
        else:
            j = h - H_C
            kf_o[:, j * DH_C:(j + 1) * DH_C] = y
            kb_o[:, j * DH_C:(j + 1) * DH_C] = y.astype(BF16)
    vb_o[...] = qkv_ref[:, (H_C + KVH_C) * DH_C:].astype(BF16)
    for j in range(C_QI // LANES):
        x = qi_ref[:, j * LANES:(j + 1) * LANES]
        qib_o[:, j * LANES:(j + 1) * LANES] = (x * c64 + _rot_half64(x, lane) * s64).astype(BF16)
    x = kw_ref[...]
    is_k = lane < D_IDX
    mu = jnp.sum(jnp.where(is_k, x, 0.0), axis=-1, keepdims=True) * (1.0 / D_IDX)
    d = jnp.where(is_k, x - mu, 0.0)
    var = jnp.sum(d * d, axis=-1, keepdims=True) * (1.0 / D_IDX)
    y = d * lax.rsqrt(var + LN_EPS) * lng_ref[...] + lnb_ref[...]
    y = y * c64 + _rot_half64(y, lane) * s64
    kwo_o[...] = jnp.where(is_k, y, x * IDX_SCALE)


def c_post(qkv, qi, kw, tabs, lng, lnb, tm):
    t = qkv.shape[0]
    row = lambda wd: pl.BlockSpec((tm, wd), lambda i: (i, 0))
    vec = pl.BlockSpec((1, LANES), lambda i: (0, 0))
    sh = lambda wd, dt: jax.ShapeDtypeStruct((t, wd), dt)
    return pl.pallas_call(
        _c_post_kernel,
        grid=(t // tm,),
        in_specs=[row(C_QKV), row(C_QI), row(LANES), row(LANES), row(LANES), row(LANES), row(LANES), vec, vec],
        out_specs=[row(H_C * DH_C), row(KVH_C * DH_C), row(KVH_C * DH_C), row(KVH_C * DH_C), row(C_QI), row(LANES)],
        out_shape=[sh(H_C * DH_C, BF16), sh(KVH_C * DH_C, F32), sh(KVH_C * DH_C, BF16), sh(KVH_C * DH_C, BF16),
                   sh(C_QI, BF16), sh(LANES, F32)],
        compiler_params=_cparams(("parallel",)),
        name="c_post",
    )(qkv, qi, kw, *tabs, lng, lnb)


def _rope_tables(pos):
    def tab(d):
        inv = ROPE_THETA ** (-jnp.arange(0, d, 2, dtype=F32) / d)
        ang = pos.astype(F32)[:, None] * inv[None, :]
        c, s = jnp.cos(ang), jnp.sin(ang)
        return jnp.concatenate([c, c], axis=1), jnp.concatenate([-s, s], axis=1)

    c128, s128 = tab(DH_C)
    c64, s64 = tab(D_IDX)
    return c128, s128, jnp.tile(c64, (1, 2)), jnp.tile(s64, (1, 2))


def _index_heads(qi, wi, kb):
    acc = None
    for h in range(H_IDX):
        d = lax.dot_general(qi[:, h * D_IDX:(h + 1) * D_IDX], kb, (((1,), (1,)), ((), ())),
                            preferred_element_type=F32)
        term = wi[:, D_IDX + h:D_IDX + h + 1] * jnp.maximum(d, 0.0)
        acc = term if acc is None else acc + term
    return acc + 0.0


def _scores_prompt_kernel(qi_ref, wi_ref, keys_ref, o_ref, *, qblk):
    qb = pl.program_id(1)
    s_len = keys_ref.shape[0]
    kb = keys_ref[:, :D_IDX].astype(BF16)
    sc = _index_heads(qi_ref[...], wi_ref[...], kb)
    row = lax.broadcasted_iota(I32, (qblk, s_len), 0) + qb * qblk
    col = lax.broadcasted_iota(I32, (qblk, s_len), 1)
    o_ref[...] = jnp.where(col <= row, sc, -jnp.inf)


def scores_prompt(qib, kwo, n_p, l_p, qblk):
    nq = l_p // qblk
    return pl.pallas_call(
        functools.partial(_scores_prompt_kernel, qblk=qblk),
        grid=(n_p, nq),
        in_specs=[pl.BlockSpec((qblk, C_QI), lambda b, q: (b * nq + q, 0)),
                  pl.BlockSpec((qblk, LANES), lambda b, q: (b * nq + q, 0)),
                  pl.BlockSpec((l_p, LANES), lambda b, q: (b, 0))],
        out_specs=pl.BlockSpec((qblk, l_p), lambda b, q: (b * nq + q, 0)),
        out_shape=jax.ShapeDtypeStruct((n_p * l_p, l_p), F32),
        compiler_params=_cparams(("parallel", "parallel")),
        name="scores_prompt",
    )(qib, kwo, kwo)


SROWS = 16


def _topk_mask(sc_ref, utri_ref, o_ref, key_scr, *, topk, chunk):
    rows, s_len = sc_ref.shape
    bits = lax.bitcast_convert_type(sc_ref[...], I32)
    key_scr[...] = bits ^ ((bits >> 31) & jnp.int32(0x7FFFFFFF))
    sign = jnp.int32(-2 ** 31)
    kf = jnp.float32(topk)

    def bisect(it, t_u):
        cand = t_u | (jnp.int32(1) << (31 - it))
        cnt = jnp.sum(jnp.where(key_scr[...] >= (cand ^ sign), 1.0, 0.0), axis=-1, keepdims=True)
        return jnp.where(cnt >= kf, cand, t_u)

    t_u = lax.fori_loop(0, 32, bisect, jnp.zeros((rows, 1), I32))
    thr = t_u ^ sign
    need = kf - jnp.sum(jnp.where(key_scr[...] > thr, 1.0, 0.0), axis=-1, keepdims=True)
    utri = utri_ref[...]
    run = jnp.zeros((rows, 1), F32)
    for c in range(s_len // chunk):
        cs = slice(c * chunk, (c + 1) * chunk)
        keyc = key_scr[:, cs]
        eqc = jnp.where(keyc == thr, 1.0, 0.0)
        before = _dot(eqc.astype(BF16), utri) + run
        pick = (keyc > thr) | ((keyc == thr) & (before < need))
        o_ref[:, cs] = jnp.where(pick & (sc_ref[:, cs] > -jnp.inf), 1.0, 0.0).astype(o_ref.dtype)
        run = run + jnp.sum(eqc, axis=-1, keepdims=True)


def _strict_upper(chunk):
    return (jnp.arange(chunk)[:, None] < jnp.arange(chunk)[None, :]).astype(BF16)


def topk_select(scores, topk, rblk, chunk):
    r, s_len = scores.shape
    assert r % rblk == 0 and s_len % chunk == 0
    utri = _strict_upper(chunk)
    return pl.pallas_call(
        functools.partial(_topk_mask, topk=topk, chunk=chunk),
        grid=(r // rblk,),
        in_specs=[pl.BlockSpec((rblk, s_len), lambda i: (i, 0)), pl.BlockSpec((chunk, chunk), lambda i: (0, 0))],
        out_specs=pl.BlockSpec((rblk, s_len), lambda i: (i, 0)),
        out_shape=jax.ShapeDtypeStruct((r, s_len), BF16),
        scratch_shapes=[pltpu.VMEM((rblk, s_len), I32)],
        compiler_params=_cparams(("parallel",)),
        name="topk_select",
    )(scores, utri)


NEG_BIG = -1e30


def _attn_init(m_scr, l_scr, acc_scr):
    m_scr[...] = jnp.full(m_scr.shape, NEG_BIG, F32)
    l_scr[...] = jnp.zeros(l_scr.shape, F32)
    acc_scr[...] = jnp.zeros(acc_scr.shape, F32)


def _attn_block(q, k, v, mask, m_scr, l_scr, acc_scr):
    on = jnp.concatenate([mask.astype(F32)] * GROUP_C, axis=0) > 0.0
    for kh in range(KVH_C):
        qh = jnp.concatenate([q[:, (kh * GROUP_C + g) * DH_C:(kh * GROUP_C + g + 1) * DH_C]
                              for g in range(GROUP_C)], axis=0)
        ks = k[:, kh * DH_C:(kh + 1) * DH_C]
        s = lax.dot_general(qh, ks, (((1,), (1,)), ((), ())), preferred_element_type=F32) * (DH_C ** -0.5)
        s = jnp.where(on, s, NEG_BIG)
        m_old = m_scr[kh]
        m_new = jnp.maximum(m_old, jnp.max(s, axis=-1, keepdims=True))
        alpha = jnp.exp(m_old - m_new)
        p = jnp.where(on, jnp.exp(s - m_new), 0.0)
        l_scr[kh] = alpha * l_scr[kh] + jnp.sum(p, axis=-1, keepdims=True)
        acc_scr[kh] = alpha * acc_scr[kh] + _dot(p.astype(BF16), v[:, kh * DH_C:(kh + 1) * DH_C])
        m_scr[kh] = m_new


def _attn_finish(o_ref, nq, l_scr, acc_scr, lead=None):
    for kh in range(KVH_C):
        out = acc_scr[kh] / l_scr[kh]
        for g in range(GROUP_C):
            hs = slice((kh * GROUP_C + g) * DH_C, (kh * GROUP_C + g + 1) * DH_C)
            val = out[g * nq:(g + 1) * nq, :].astype(o_ref.dtype)
            if lead is None:
                o_ref[:, hs] = val
            else:
                o_ref[lead, :, hs] = val


def _attn_prompt_kernel(q_ref, k_ref, v_ref, mask_ref, o_ref, m_scr, l_scr, acc_scr, *, qblk, kblk):
    qb = pl.program_id(1)
    kb = pl.program_id(2)

    @pl.when(kb == 0)
    def _():
        _attn_init(m_scr, l_scr, acc_scr)

    @pl.when(kb * kblk <= qb * qblk + qblk - 1)
    def _():
        _attn_block(q_ref[...], k_ref[...], v_ref[...], mask_ref[...], m_scr, l_scr, acc_scr)

    @pl.when(kb == pl.num_programs(2) - 1)
    def _():
        _attn_finish(o_ref, qblk, l_scr, acc_scr)


def attn_prompt(qb_, kb_, vb_, mask, n_p, l_p, qblk, kblk):
    nq, nk = l_p // qblk, l_p // kblk
    last = lambda q, k: jnp.minimum(k, (q * qblk + qblk - 1) // kblk)
    rows = GROUP_C * qblk
    return pl.pallas_call(
        functools.partial(_attn_prompt_kernel, qblk=qblk, kblk=kblk),
        grid=(n_p, nq, nk),
        in_specs=[pl.BlockSpec((qblk, H_C * DH_C), lambda b, q, k: (b * nq + q, 0)),
                  pl.BlockSpec((kblk, KVH_C * DH_C), lambda b, q, k: (b * nk + last(q, k), 0)),
                  pl.BlockSpec((kblk, KVH_C * DH_C), lambda b, q, k: (b * nk + last(q, k), 0)),
                  pl.BlockSpec((qblk, kblk), lambda b, q, k: (b * nq + q, last(q, k)))],
        out_specs=pl.BlockSpec((qblk, H_C * DH_C), lambda b, q, k: (b * nq + q, 0)),
        out_shape=jax.ShapeDtypeStruct((n_p * l_p, H_C * DH_C), BF16),
        scratch_shapes=[pltpu.VMEM((KVH_C, rows, 1), F32), pltpu.VMEM((KVH_C, rows, 1), F32),
                        pltpu.VMEM((KVH_C, rows, DH_C), F32)],
        compiler_params=_cparams(("parallel", "parallel", "arbitrary")),
        name="attn_prompt",
    )(qb_, kb_, vb_, mask)


def _dsa_sample_kernel(pt_ref, qi_ref, kw_ref, q_ref, kn_ref, vn_ref, utri_ref, kidx_hbm, k_hbm, v_hbm, o_ref,
                       kibuf, kbuf, vbuf, sems, sc_scr, key_scr, mask_scr, *, n_pages, page, l_s, topk, pool0):
    b = pl.program_id(0)
    past = n_pages * page
    s_len = past + page
    pools = ((kidx_hbm, kibuf), (k_hbm, kbuf), (v_hbm, vbuf))

    def page_copy(which, bb, slot, pg):
        src, dst = pools[which]
        phys = pt_ref[bb * n_pages + pg] + pool0
        return pltpu.make_async_copy(src.at[phys], dst.at[slot, pl.ds(pg * page, page), :], sems.at[slot, which])

    def fetch(bb, slot):
        for pg in range(n_pages):
            for which in range(3):
                page_copy(which, bb, slot, pg).start()

    @pl.when(b == 0)
    def _():
        for _, buf in pools:
            buf[:, past:, :] = jnp.zeros((2, page) + buf.shape[2:], F32)
        fetch(0, 0)

    @pl.when(b + 1 < pl.num_programs(0))
    def _():
        fetch(b + 1, (b + 1) % 2)

    cur = b % 2
    for pg in range(n_pages):
        for which in range(3):
            page_copy(which, b, cur, pg).wait()
    kw = kw_ref[0]
    kibuf[cur, past:past + SROWS, :] = kw[:, :D_IDX]
    kbuf[cur, past:past + SROWS, :] = kn_ref[0]
    vbuf[cur, past:past + SROWS, :] = vn_ref[0]

    sc = _index_heads(qi_ref[0], kw, kibuf[cur].astype(BF16))
    row = lax.broadcasted_iota(I32, (SROWS, s_len), 0)
    col = lax.broadcasted_iota(I32, (SROWS, s_len), 1)
    visible = (col < past) | ((col - past <= row) & (col - past < l_s))
    sc_scr[...] = jnp.where(visible, sc, -jnp.inf)
    _topk_mask(sc_scr, utri_ref, mask_scr, key_scr, topk=topk, chunk=page)

    on = jnp.concatenate([mask_scr[...]] * GROUP_C, axis=0) > 0.0
    q = q_ref[0]
    for kh in range(KVH_C):
        hs = slice(kh * DH_C, (kh + 1) * DH_C)
        qh = jnp.concatenate([q[:, (kh * GROUP_C + g) * DH_C:(kh * GROUP_C + g + 1) * DH_C]
                              for g in range(GROUP_C)], axis=0)
        s = lax.dot_general(qh, kbuf[cur, :, hs].astype(BF16), (((1,), (1,)), ((), ())),
                            preferred_element_type=F32) * (DH_C ** -0.5)
        s = jnp.where(on, s, NEG_BIG)
        p = jnp.where(on, jnp.exp(s - jnp.max(s, axis=-1, keepdims=True)), 0.0)
        out = _dot(p.astype(BF16), vbuf[cur, :, hs].astype(BF16)) / jnp.sum(p, axis=-1, keepdims=True)
        for g in range(GROUP_C):
            o_ref[0, :, (kh * GROUP_C + g) * DH_C:(kh * GROUP_C + g + 1) * DH_C] = (
                out[g * SROWS:(g + 1) * SROWS, :].astype(o_ref.dtype))


def dsa_sample(page_table, qi3, kw3, q3, knew3, vnew3, cache_kidx2, cache_k2, cache_v2, *, pool0, l_s, topk):
    n_s, n_pages = page_table.shape
    page = cache_k2.shape[1]
    kvw = KVH_C * DH_C
    s_len = (n_pages + 1) * page
    own = lambda wd: pl.BlockSpec((1, SROWS, wd), lambda b, pt: (b, 0, 0))
    hbm = pl.BlockSpec(memory_space=pl.ANY)
    return pl.pallas_call(
        functools.partial(_dsa_sample_kernel, n_pages=n_pages, page=page, l_s=l_s, topk=topk, pool0=pool0),
        grid_spec=pltpu.PrefetchScalarGridSpec(
            num_scalar_prefetch=1,
            grid=(n_s,),
            in_specs=[own(C_QI), own(LANES), own(H_C * DH_C), own(kvw), own(kvw),
                      pl.BlockSpec((page, page), lambda b, pt: (0, 0)), hbm, hbm, hbm],
            out_specs=own(H_C * DH_C),
            scratch_shapes=[pltpu.VMEM((2, s_len, D_IDX), F32), pltpu.VMEM((2, s_len, kvw), F32),
                            pltpu.VMEM((2, s_len, kvw), F32), pltpu.SemaphoreType.DMA((2, 3)),
                            pltpu.VMEM((SROWS, s_len), F32), pltpu.VMEM((SROWS, s_len), I32),
                            pltpu.VMEM((SROWS, s_len), F32)],
        ),
        out_shape=jax.ShapeDtypeStruct((n_s, SROWS, H_C * DH_C), BF16),
        compiler_params=_cparams(("arbitrary",)),
        name="dsa_sample",
    )(page_table.reshape(-1), qi3, kw3, q3, knew3, vnew3, _strict_upper(page), cache_kidx2, cache_k2, cache_v2)


def c_params(li, w):
    p = {}
    w_in = w["c_w_in"][li]
    p["w_qkv"] = w_in[:, :C_QKV].astype(BF16)
    p["w_qi"] = w_in[:, C_QKV:C_QKV + C_QI].astype(BF16)
    p["w_kw"] = _pad_cols(w_in[:, C_QKV + C_QI:], LANES).astype(BF16)
    p["ln_g"] = _pad_cols(w["c_kidx_ln_g"][li][None, :], LANES)
    p["ln_b"] = _pad_cols(w["c_kidx_ln_b"][li][None, :], LANES)
    p["w_out"] = w["c_w_out"][li].astype(BF16)
    return p


def _pad_sample_rows(a, n_s, l_s):
    a3 = a.reshape(n_s, l_s, a.shape[1])
    return jnp.pad(a3, ((0, 0), (0, SROWS - l_s), (0, 0)))


def c_layer(xb, geo, p, li, cache_k, cache_v, cache_kidx, page_table, *, tm, qblk, kblk):
    n_p, l_p, n_s, l_s = geo
    tp = n_p * l_p
    n_pool, page = cache_k.shape[1], cache_k.shape[2]
    n_pages = page_table.shape[1]
    past = n_pages * page
    qkv = matmul([xb], [p["w_qkv"]], tm, 1024, name="c_in_qkv")
    qi = matmul([xb], [p["w_qi"]], tm, C_QI, name="c_in_qi")
    kw = matmul([xb], [p["w_kw"]], tm, LANES, name="c_in_kw")
    pos = jnp.concatenate([jnp.tile(jnp.arange(l_p, dtype=I32), n_p),
                           jnp.tile(past + jnp.arange(l_s, dtype=I32), n_s)])
    q_b, k_f, k_b, v_b, qi_b, kwo = c_post(qkv, qi, kw, _rope_tables(pos), p["ln_g"], p["ln_b"], tm)
    v_f = qkv[:, (H_C + KVH_C) * DH_C:]
    sc_p = scores_prompt(qi_b, kwo, n_p, l_p, qblk)
    mask_p = topk_select(sc_p, min(TOPK_MAX, l_p // 4), qblk, 256)
    o_p = attn_prompt(q_b, k_b, v_b, mask_p, n_p, l_p, qblk, kblk)
    kvw = KVH_C * DH_C
    pad = lambda a: _pad_sample_rows(a[tp:], n_s, l_s)
    o_s3 = dsa_sample(page_table, pad(qi_b), pad(kwo), pad(q_b), pad(k_f), pad(v_f),
                      cache_kidx.reshape(-1, page, D_IDX), cache_k.reshape(-1, page, kvw),
                      cache_v.reshape(-1, page, kvw), pool0=li * n_pool, l_s=l_s,
                      topk=min(TOPK_MAX, (past + l_s) // 4))
    o = jnp.concatenate([o_p, o_s3[:, :l_s].reshape(n_s * l_s, -1)], axis=0)
    h = matmul([o], [p["w_out"]], tm, 1024, name="c_out")
    heads = lambda a, n, l: a.reshape(n, l, KVH_C, DH_C)
    new_p = (heads(k_f[:tp], n_p, l_p), heads(v_f[:tp], n_p, l_p), kwo[:tp, :D_IDX].reshape(n_p, l_p, D_IDX))
    new_s = (heads(k_f[tp:], n_s, l_s), heads(v_f[tp:], n_s, l_s), kwo[tp:, :D_IDX].reshape(n_s, l_s, D_IDX))
    return h, new_p, new_s


TM = 256
BLK = 128
TM_DMA = 64
QBLK = 128
KBLK = 512


def kernel(x_prompt, x_sample, state_shift, state_wkv, state_ssm_re, state_ssm_im, cache_k, cache_v, cache_kidx, page_table, ab_w_in, ab_mu, rwkv_w0, rwkv_w2, rwkv_a0, rwkv_a2, rwkv_g2, rwkv_k_k, rwkv_k_a, rwkv_r_k, rwkv_ln_g, rwkv_ln_b, s5_lam_re, s5_lam_im, s5_log_dt, s5_b_re, s5_b_im, s5_c_re, s5_c_im, s5_d, s5_w_glu, s5_b_glu, ab_w_out, c_w_in, c_kidx_ln_g, c_kidx_ln_b, c_w_out, ln_mix_g, ln_mix_b, ln_ffn_g, ln_ffn_b, moe_w_router, moe_b_router, moe_w_gate, moe_w_up, moe_w_down, moe_ws_gate, moe_ws_up, moe_ws_down):
    w = dict(locals())
    n_p, l_p, _ = x_prompt.shape
    n_s, l_s, _ = x_sample.shape
    geo = (n_p, l_p, n_s, l_s)
    tp = n_p * l_p
    y = jnp.concatenate([x_prompt.reshape(-1, D_MODEL), x_sample.reshape(-1, D_MODEL)], axis=0)
    yb = y.astype(BF16)
    outs = {k: [] for k in ("kp", "vp", "kip", "ks", "vs", "kis", "shp", "shs", "wkvp", "wkvs",
                            "srp", "sip", "srs", "sis")}
    for layer in range(DEPTH):
        li = layer // 2
        if layer % 2 == 0:
            wkv_all = state_wkv.reshape((-1,) + state_wkv.shape[2:])
            h, shift, wkv, ssm_p, ssm_s = ab_layer(yb, geo, ab_params(li, w, l_s), state_shift[li], wkv_all, li * n_s,
                                                   state_ssm_re[li], state_ssm_im[li], tm=TM, blk=BLK)
            for k, v in zip(("shp", "shs", "wkvp", "wkvs", "srp", "sip", "srs", "sis"),
                            (*shift, *wkv, *ssm_p, *ssm_s)):
                outs[k].append(v)
        else:
            h, new_p, new_s = c_layer(yb, geo, c_params(li, w), li, cache_k, cache_v, cache_kidx,
                                      page_table, tm=TM, qblk=QBLK, kblk=KBLK)
            for k, v in zip(("kp", "vp", "kip", "ks", "vs", "kis"), (*new_p, *new_s)):
                outs[k].append(v)
        y, yb = add_ln(y, h, ln_mix_g[layer], ln_mix_b[layer], TM)
        y, yb = moe_layer(y, yb, moe_params(layer, w), ln_ffn_g[layer], ln_ffn_b[layer], tm=TM, tm_dma=TM_DMA)
    st = lambda k: jnp.stack(outs[k])
    return (y[:tp].reshape(n_p, l_p, D_MODEL), y[tp:].reshape(n_s, l_s, D_MODEL),
            st("kp"), st("vp"), st("kip"), st("ks"), st("vs"), st("kis"),
            st("shp"), st("shs"), st("wkvp"), st("wkvs"), st("srp"), st("sip"), st("srs"), st("sis"))
```

```python
import functools
import math

import jax
import jax.numpy as jnp
from jax import lax
from jax.experimental import pallas as pl
from jax.experimental.pallas import tpu as pltpu

F32, BF16, I32 = jnp.float32, jnp.bfloat16, jnp.int32

D_MODEL = 2048
DEPTH = 2
ALPHA = (2.0 * DEPTH) ** 0.25
LN_EPS = 1e-5
C_A = 1024
HS_A = 64
H_A = 16
LORA_W, LORA_A, LORA_G = 64, 64, 160
A_PROJ = 3 * C_A + LORA_W + LORA_A + LORA_G
GN_EPS_A = 64e-5
C_B = 1024
S5_P = 16
S5_GROUPS = 64
S5_N = 64
S5_STATE = S5_GROUPS * S5_N
DH_C = 128
H_C = 16
KVH_C = 4
GROUP_C = 4
H_IDX = 16
D_IDX = 64
TOPK_MAX = 256
ROPE_THETA = 10000.0
N_EXPERTS = 64
TOP_K = 8
N_EXPERT_GROUPS = 8
TOPK_GROUPS = 4
D_EXPERT = 512
ROUTED_SCALE = 2.5

LANES = 128
SUBLANES = 8
VMEM_LIMIT = 56 * 1024 * 1024

LORA_PAD = 512
EXP_M05 = math.exp(-0.5)


def _cparams(sem):
    return pltpu.CompilerParams(dimension_semantics=sem, vmem_limit_bytes=VMEM_LIMIT)


def _dot(a, b):
    return jnp.dot(a, b, preferred_element_type=F32)


def _lane_fold(x, op):
    tiles = [x[:, i * LANES:(i + 1) * LANES] for i in range(x.shape[1] // LANES)]
    return functools.reduce(op, tiles)


def _row_sum(x):
    return jnp.sum(_lane_fold(x, jnp.add), axis=-1, keepdims=True)


def _row_max(x):
    return jnp.max(_lane_fold(x, jnp.maximum), axis=-1, keepdims=True)


def _split3(a):
    a1 = a.astype(BF16)
    r1 = a - a1.astype(F32)
    a2 = r1.astype(BF16)
    a3 = (r1 - a2.astype(F32)).astype(BF16)
    return a1, a2, a3


def _dot_sel(a, sel_bf16):
    a1, a2, a3 = _split3(a)
    return _dot(a1, sel_bf16) + _dot(a2, sel_bf16) + _dot(a3, sel_bf16)


def _dot3(a, b):
    a1 = a.astype(BF16)
    a2 = (a - a1.astype(F32)).astype(BF16)
    b1 = b.astype(BF16)
    b2 = (b - b1.astype(F32)).astype(BF16)
    return _dot(a1, b1) + _dot(a1, b2) + _dot(a2, b1)


def _mm_kernel(*refs, n_lhs):
    o_ref = refs[-1]
    acc = None
    for i in range(n_lhs):
        d = _dot(refs[i][...], refs[n_lhs + i][...])
        acc = d if acc is None else acc + d
    o_ref[...] = acc.astype(o_ref.dtype)


def matmul(xs, ws, tm, tn, out_dtype=F32, name="mm"):
    m = xs[0].shape[0]
    n = ws[0].shape[1]
    assert m % tm == 0 and n % tn == 0, (m, tm, n, tn)
    in_specs = ([pl.BlockSpec((tm, x.shape[1]), lambda j, i: (i, 0)) for x in xs]
                + [pl.BlockSpec((w.shape[0], tn), lambda j, i: (0, j)) for w in ws])
    return pl.pallas_call(
        functools.partial(_mm_kernel, n_lhs=len(xs)),
        grid=(n // tn, m // tm),
        in_specs=in_specs,
        out_specs=pl.BlockSpec((tm, tn), lambda j, i: (i, j)),
        out_shape=jax.ShapeDtypeStruct((m, n), out_dtype),
        compiler_params=_cparams(("parallel", "parallel")),
        name=name,
    )(*xs, *ws)


def _add_ln_kernel(x_ref, h_ref, g_ref, b_ref, o_ref, ob_ref):
    z = ALPHA * x_ref[...] + h_ref[...]
    inv_d = 1.0 / z.shape[1]
    mu = _row_sum(z) * inv_d
    d = z - mu
    var = _row_sum(d * d) * inv_d
    y = d * lax.rsqrt(var + LN_EPS) * g_ref[...] + b_ref[...]
    o_ref[...] = y
    ob_ref[...] = y.astype(BF16)


def add_ln(x, h, g, b, tm):
    t, d = x.shape
    row = pl.BlockSpec((tm, d), lambda i: (i, 0))
    vec = pl.BlockSpec((1, d), lambda i: (0, 0))
    return pl.pallas_call(
        _add_ln_kernel,
        grid=(t // tm,),
        in_specs=[row, row, vec, vec],
        out_specs=[row, row],
        out_shape=[jax.ShapeDtypeStruct((t, d), F32), jax.ShapeDtypeStruct((t, d), BF16)],
        compiler_params=_cparams(("parallel",)),
        name="add_ln",
    )(x, h, g.reshape(1, d), b.reshape(1, d))


def _rwkv_prep_kernel(rkv_ref, rkvp_ref, lo_ref, lop_ref, mu1_ref, mu2_ref, w0_ref, w2_ref, a0_ref, a2_ref,
                      g2_ref, kk_ref, ka_ref, rk_ref, gsel_ref, gselt_ref,
                      kk_o, wr_o, w_o, b_o, k_o, v_o, kr_o, g_o, bonus_o):
    x = rkv_ref[...]
    m = x + (rkvp_ref[...] - x) * mu1_ref[...]
    r = m[:, :C_A]
    k = m[:, C_A:2 * C_A]
    v = m[:, 2 * C_A:]
    l = lo_ref[...]
    lm = l + (lop_ref[...] - l) * mu2_ref[...]
    lw = lm[:, :LANES]
    la = lm[:, LANES:2 * LANES]
    lg = lm[:, 2 * LANES:]
    w_raw = w0_ref[...] + _dot3(jnp.tanh(lw), w2_ref[...])
    decay = jnp.exp(-EXP_M05 * jax.nn.sigmoid(w_raw))
    a = jax.nn.sigmoid(a0_ref[...] + _dot3(la, a2_ref[...]))
    g = _dot3(jax.nn.sigmoid(lg), g2_ref[...])
    gsel = gsel_ref[...]
    gselt = gselt_ref[...]
    kk = k * kk_ref[...]
    nrm = jnp.sqrt(_dot_sel(kk * kk, gsel))
    inv = 1.0 / jnp.maximum(nrm, 1e-12)
    kk = kk * _dot_sel(inv, gselt)
    k_mod = k * (1.0 + (a - 1.0) * ka_ref[...])
    sb = _dot_sel(r * k_mod * rk_ref[...], gsel)
    bonus_o[...] = _dot_sel(sb, gselt) * v
    g_o[...] = g
    b = kk * a
    kk_o[...] = kk
    br = _dot_sel(_dot_sel(b * r, gsel), gselt)
    wr_o[...] = decay * r - kk * br
    w_o[...] = decay
    b_o[...] = b
    k_o[...] = k_mod
    v_o[...] = v
    kr_o[...] = _dot_sel(_dot_sel(k_mod * r, gsel), gselt)


RWKV_ROWS = 7


def rwkv_prep(rkv, rkv_prev, lora, lora_prev, p, tm):
    t = rkv.shape[0]
    row = lambda w: pl.BlockSpec((tm, w), lambda i: (i, 0))
    full = lambda a: pl.BlockSpec(a.shape, lambda i: (0,) * a.ndim)
    consts = [p["mu_rkv"], p["mu_lora"], p["w0"], p["w2"], p["a0"], p["a2"], p["g2"], p["k_k"], p["k_a"], p["r_k"],
              p["gsel"], p["gselt"]]
    n_out = RWKV_ROWS + 2
    return pl.pallas_call(
        _rwkv_prep_kernel,
        grid=(t // tm,),
        in_specs=[row(3 * C_A), row(3 * C_A), row(LORA_PAD), row(LORA_PAD)] + [full(c) for c in consts],
        out_specs=[row(C_A)] * n_out,
        out_shape=[jax.ShapeDtypeStruct((t, C_A), F32)] * n_out,
        compiler_params=_cparams(("parallel",)),
        name="rwkv_prep",
    )(rkv, rkv_prev, lora, lora_prev, *consts)


TILE_HEADS = 4
TILE_W = TILE_HEADS * HS_A
N_TILES = H_A // TILE_HEADS
LHS_PER_TILE = 4


def _wkv_steps(rows, row_base, n_steps, npar, s_scr, bd_ref, store_o):
    sub = lax.broadcasted_iota(I32, (HS_A, TILE_W), 0)
    lane = lax.broadcasted_iota(I32, (HS_A, TILE_W), 1)
    diag = (lane % HS_A) == sub
    bd = bd_ref[...]
    blk_rows = LHS_PER_TILE * HS_A

    def step(t, carry):
        vec = [[ref[pl.ds(row_base[s] + t, 1), :] for ref in rows[s]] for s in range(npar)]
        results = []
        for s in range(npar):
            kk, wr2, _, _, _, v, _ = vec[s]
            lhs = []
            for p in range(N_TILES):
                ls = slice(p * TILE_W, (p + 1) * TILE_W)
                st = s_scr[s * N_TILES + p]
                pk = st * kk[:, ls]
                hi = pk.astype(BF16)
                lo = (pk - hi.astype(F32)).astype(BF16)
                q2 = (st * wr2[:, ls]).astype(BF16)
                ve = jnp.where(diag, v[:, ls], 0.0).astype(BF16)
                lhs += [hi, lo, q2, ve]
            results.append(_dot(jnp.concatenate(lhs, axis=0), bd))
        for s in range(npar):
            _, _, w, b, k, v, kr = vec[s]
            res = results[s]
            o_parts = []
            for p in range(N_TILES):
                ls = slice(p * TILE_W, (p + 1) * TILE_W)
                r0 = p * blk_rows
                z1 = res[r0:r0 + HS_A] + res[r0 + HS_A:r0 + 2 * HS_A]
                z2 = res[r0 + 2 * HS_A:r0 + 3 * HS_A]
                vb = res[r0 + 3 * HS_A:r0 + 4 * HS_A]
                st = s_scr[s * N_TILES + p]
                s_scr[s * N_TILES + p] = st * w[:, ls] - z1 * b[:, ls] + vb * k[:, ls]
                z2r = jnp.sum(jnp.where(diag, z2, 0.0), axis=0, keepdims=True)
                o_parts.append(z2r + v[:, ls] * kr[:, ls])
            store_o(s, t, jnp.concatenate(o_parts, axis=1))
        return carry

    lax.fori_loop(0, n_steps, step, 0)


def _pack_state(s_ref, q, s_scr, slot):
    for p in range(N_TILES):
        s_scr[slot * N_TILES + p] = jnp.concatenate([s_ref[q, TILE_HEADS * p + h] for h in range(TILE_HEADS)], axis=1)


def _unpack_state(s_scr, slot, s_ref, q):
    for p in range(N_TILES):
        st = s_scr[slot * N_TILES + p]
        for h in range(TILE_HEADS):
            s_ref[q, TILE_HEADS * p + h] = st[:, h * HS_A:(h + 1) * HS_A]


def _wkv_long_kernel(*refs, npar, blk):
    n_in = npar * RWKV_ROWS
    rows = [refs[s * RWKV_ROWS:(s + 1) * RWKV_ROWS] for s in range(npar)]
    bd_ref, o_ref, sout_ref, s_scr = refs[n_in:n_in + 4]
    c = pl.program_id(0)

    @pl.when(c == 0)
    def _():
        s_scr[...] = jnp.zeros(s_scr.shape, F32)

    def store_o(s, t, o_row):
        o_ref[s, pl.ds(t, 1), :] = o_row

    _wkv_steps(rows, [0] * npar, blk, npar, s_scr, bd_ref, store_o)

    @pl.when(c == pl.num_programs(0) - 1)
    def _():
        for s in range(npar):
            _unpack_state(s_scr, s, sout_ref, s)


WKV_NPAR_SHORT = 2
WKV_SPB = 8


def _wkv_short_kernel(*refs, seq_len, spb):
    rows_refs = refs[:RWKV_ROWS]
    s0_ref, bd_ref, o_ref, sout_ref, s_scr = refs[RWKV_ROWS:RWKV_ROWS + 5]
    c = pl.program_id(1)
    npar = WKV_NPAR_SHORT

    def group(gq, carry):
        qs = [gq * npar + s for s in range(npar)]
        for s in range(npar):
            _pack_state(s0_ref, qs[s], s_scr, s)

        def store_o(s, t, o_row):
            o_ref[pl.ds((c * spb + qs[s]) * seq_len + t, 1), :] = o_row

        _wkv_steps([rows_refs] * npar, [(c * spb + q) * seq_len for q in qs], seq_len, npar, s_scr, bd_ref, store_o)
        for s in range(npar):
            _unpack_state(s_scr, s, sout_ref, qs[s])
        return carry

    lax.fori_loop(0, spb // npar, group, 0)


def _head_ones():
    head = jnp.arange(TILE_W) // HS_A
    return (head[:, None] == head[None, :]).astype(BF16)


def wkv_long(rows, *, n_seq, seq_len, blk):
    cps = seq_len // blk
    in_specs = [pl.BlockSpec((blk, C_A), functools.partial(lambda c, s: (s * cps + c, 0), s=s))
                for s in range(n_seq) for _ in range(RWKV_ROWS)]
    return pl.pallas_call(
        functools.partial(_wkv_long_kernel, npar=n_seq, blk=blk),
        grid=(cps,),
        in_specs=in_specs + [pl.BlockSpec((TILE_W, TILE_W), lambda c: (0, 0))],
        out_specs=[pl.BlockSpec((n_seq, blk, C_A), lambda c: (0, c, 0)),
                   pl.BlockSpec((n_seq, H_A, HS_A, HS_A), lambda c: (0, 0, 0, 0))],
        out_shape=[jax.ShapeDtypeStruct((n_seq, seq_len, C_A), F32),
                   jax.ShapeDtypeStruct((n_seq, H_A, HS_A, HS_A), F32)],
        scratch_shapes=[pltpu.VMEM((n_seq * N_TILES, HS_A, TILE_W), F32)],
        compiler_params=_cparams(("arbitrary",)),
        name="wkv_long",
    )(*(list(rows) * n_seq), _head_ones())


def wkv_short(rows, s0, *, row0, n_seq, seq_len, blk, s0_seq0):
    assert row0 % blk == 0 and blk % seq_len == 0
    b0 = row0 // blk
    nsb = blk // seq_len
    spb = min(WKV_SPB, nsb)
    assert n_seq % nsb == 0 and nsb % spb == 0 and spb % WKV_NPAR_SHORT == 0 and s0_seq0 % spb == 0
    sub = nsb // spb
    sspec = lambda off: pl.BlockSpec((spb, H_A, HS_A, HS_A), lambda s, c: (off + s * sub + c, 0, 0, 0))
    return pl.pallas_call(
        functools.partial(_wkv_short_kernel, seq_len=seq_len, spb=spb),
        grid=(n_seq // nsb, sub),
        in_specs=[pl.BlockSpec((blk, C_A), lambda s, c: (b0 + s, 0))] * RWKV_ROWS
        + [sspec(s0_seq0 // spb), pl.BlockSpec((TILE_W, TILE_W), lambda s, c: (0, 0))],
        out_specs=[pl.BlockSpec((blk, C_A), lambda s, c: (s, 0)), sspec(0)],
        out_shape=[jax.ShapeDtypeStruct((n_seq * seq_len, C_A), F32),
                   jax.ShapeDtypeStruct((n_seq, H_A, HS_A, HS_A), F32)],
        scratch_shapes=[pltpu.VMEM((WKV_NPAR_SHORT * N_TILES, HS_A, TILE_W), F32)],
        compiler_params=_cparams(("parallel", "arbitrary")),
        name="wkv_short",
    )(*rows, s0, _head_ones())


def _rwkv_post_kernel(o_in_ref, bonus_ref, g_ref, lng_ref, lnb_ref, gsel_ref, gselt_ref, o_ref):
    o = o_in_ref[...]
    gsel = gsel_ref[...]
    gselt = gselt_ref[...]
    mu = _dot_sel(o, gsel) * (1.0 / HS_A)
    d = o - _dot_sel(mu, gselt)
    var = _dot_sel(d * d, gsel) * (1.0 / HS_A)
    rstd = lax.rsqrt(var + GN_EPS_A)
    y = d * _dot_sel(rstd, gselt) * lng_ref[...] + lnb_ref[...]
    o_ref[...] = ((y + bonus_ref[...]) * g_ref[...]).astype(o_ref.dtype)


def rwkv_post(o_wkv, bonus, g, p, tm):
    t = bonus.shape[0]
    row = pl.BlockSpec((tm, C_A), lambda i: (i, 0))
    full = lambda a: pl.BlockSpec(a.shape, lambda i: (0,) * a.ndim)
    consts = [p["ln_g"], p["ln_b"], p["gsel"], p["gselt"]]
    return pl.pallas_call(
        _rwkv_post_kernel,
        grid=(t // tm,),
        in_specs=[row, row, row] + [full(c) for c in consts],
        out_specs=row,
        out_shape=jax.ShapeDtypeStruct((t, C_A), BF16),
        compiler_params=_cparams(("parallel",)),
        name="rwkv_post",
    )(o_wkv, bonus, g, *consts)


S5_KT = 4
S5_KU = C_B // S5_KT
S5_KH = S5_STATE // S5_KT
S5_LC = 512


def _gelu_tanh(x):
    return 0.5 * x * (1.0 + jnp.tanh(math.sqrt(2.0 / math.pi) * (x + 0.044715 * (x * x * x))))


def _s5_kernel(u_ref, wbr_ref, wbi_ref, lamk_ref, pre_ref, pim_ref, h0r_ref, h0i_ref, wcr_ref, wci_ref,
               dsk_ref, wglu_ref, bglu_ref, ob_ref, hr_out, hi_out, hre, him, car_re, car_im, y_scr,
               *, seq_len, blk):
    long_mode = seq_len >= blk
    period = SUBLANES if long_mode else seq_len
    u = u_ref[...]
    ub = u.astype(BF16)
    for kt in range(S5_KT):
        uk = ub[:, kt * S5_KU:(kt + 1) * S5_KU]
        hre[:, kt * S5_KH:(kt + 1) * S5_KH] = _dot(uk, wbr_ref[kt])
        him[:, kt * S5_KH:(kt + 1) * S5_KH] = _dot(uk, wbi_ref[kt])

    if long_mode:
        @pl.when(pl.program_id(1) == 0)
        def _():
            car_re[...] = jnp.zeros(car_re.shape, F32)
            car_im[...] = jnp.zeros(car_im.shape, F32)

    rowi = lax.broadcasted_iota(I32, (SUBLANES, 1), 0) % period
    for lc in range(S5_STATE // S5_LC):
        ls = slice(lc * S5_LC, (lc + 1) * S5_LC)
        lam = [(lamk_ref[2 * i:2 * i + 1, ls], lamk_ref[2 * i + 1:2 * i + 2, ls]) for i in range(3)]
        pre = pre_ref[:, ls]
        pim = pim_ref[:, ls]

        def tile(i, carry, ls=ls, lam=lam, pre=pre, pim=pim):
            r0 = pl.multiple_of(i * SUBLANES, SUBLANES)
            xr = hre[pl.ds(r0, SUBLANES), ls]
            xi = him[pl.ds(r0, SUBLANES), ls]
            for lvl, sh in enumerate((1, 2, 4)):
                if sh >= period:
                    break
                lr, li = lam[lvl]
                keep = rowi >= sh
                sr = jnp.where(keep, pltpu.roll(xr, sh, 0), 0.0)
                si = jnp.where(keep, pltpu.roll(xi, sh, 0), 0.0)
                xr, xi = xr + lr * sr - li * si, xi + lr * si + li * sr
            if long_mode:
                hr, hi = carry
            else:
                hr = h0r_ref[pl.ds(r0, SUBLANES), ls]
                hi = h0i_ref[pl.ds(r0, SUBLANES), ls]
            xr, xi = xr + pre * hr - pim * hi, xi + pre * hi + pim * hr
            hre[pl.ds(r0, SUBLANES), ls] = xr
            him[pl.ds(r0, SUBLANES), ls] = xi
            if long_mode:
                last_r = jnp.broadcast_to(xr[SUBLANES - 1:SUBLANES, :], xr.shape)
                last_i = jnp.broadcast_to(xi[SUBLANES - 1:SUBLANES, :], xi.shape)
                return (last_r, last_i)
            return carry

        if long_mode:
            init = (car_re[:, ls], car_im[:, ls])
            fin = lax.fori_loop(0, blk // SUBLANES, tile, init)
            car_re[:, ls] = fin[0]
            car_im[:, ls] = fin[1]
        else:
            lax.fori_loop(0, blk // SUBLANES, tile, 0)

    if long_mode:
        hr_out[0] = car_re[...]
        hi_out[0] = car_im[...]
    else:
        hr_out[...] = hre[...]
        hi_out[...] = him[...]

    for kt in range(S5_KT):
        hs = slice(kt * S5_KH, (kt + 1) * S5_KH)
        y_scr[:, kt * S5_KU:(kt + 1) * S5_KU] = (_dot(hre[:, hs].astype(BF16), wcr_ref[kt])
                                                 - _dot(him[:, hs].astype(BF16), wci_ref[kt]))
    y = y_scr[...] + dsk_ref[...] * u
    z = _gelu_tanh(y)
    gate = jax.nn.sigmoid(_dot(z.astype(BF16), wglu_ref[...]) + bglu_ref[...])
    ob_ref[...] = (z * gate).astype(ob_ref.dtype)


def s5_mix(u, p, h0r_rows, h0i_rows, *, row0, n_seq, seq_len, blk):
    long_mode = seq_len >= blk
    assert row0 % blk == 0
    b0 = row0 // blk
    t_out = n_seq * seq_len
    if long_mode:
        cps = seq_len // blk
        grid = (n_seq, cps)
        umap = lambda s, c: (b0 + s * cps + c, 0)
        omap = lambda s, c: (s * cps + c, 0)
        hspec = pl.BlockSpec((1, SUBLANES, S5_STATE), lambda s, c: (s, 0, 0))
        hshape = jax.ShapeDtypeStruct((n_seq, SUBLANES, S5_STATE), F32)
        h0spec = pl.BlockSpec((SUBLANES, S5_STATE), lambda s, c: (0, 0))
    else:
        assert t_out % blk == 0
        grid = (t_out // blk, 1)
        umap = lambda s, c: (b0 + s, 0)
        omap = lambda s, c: (s, 0)
        hspec = pl.BlockSpec((blk, S5_STATE), omap)
        hshape = jax.ShapeDtypeStruct((t_out, S5_STATE), F32)
        h0spec = pl.BlockSpec((blk, S5_STATE), omap)
    full = lambda a: pl.BlockSpec(a.shape, lambda s, c: (0,) * a.ndim)
    pre, pim = (p["pow_re8"], p["pow_im8"]) if long_mode else (p["pow_re_s"], p["pow_im_s"])
    args = [u, p["wb_re"], p["wb_im"], p["lamk"], pre, pim, h0r_rows, h0i_rows, p["wc_re"], p["wc_im"],
            p["d_skip"], p["w_glu"], p["b_glu"]]
    in_specs = [pl.BlockSpec((blk, C_B), umap)] + [full(a) for a in args[1:6]] + [h0spec, h0spec] \
        + [full(a) for a in args[8:]]
    return pl.pallas_call(
        functools.partial(_s5_kernel, seq_len=seq_len, blk=blk),
        grid=grid,
        in_specs=in_specs,
        out_specs=[pl.BlockSpec((blk, C_B), omap), hspec, hspec],
        out_shape=[jax.ShapeDtypeStruct((t_out, C_B), BF16), hshape, hshape],
        scratch_shapes=[pltpu.VMEM((blk, S5_STATE), F32), pltpu.VMEM((blk, S5_STATE), F32),
                        pltpu.VMEM((SUBLANES, S5_STATE), F32), pltpu.VMEM((SUBLANES, S5_STATE), F32),
                        pltpu.VMEM((blk, C_B), F32)],
        compiler_params=_cparams(("parallel", "arbitrary")),
        name="s5_long" if long_mode else "s5_short",
    )(*args)


def _pad_cols(a, n):
    return jnp.pad(a, ((0, 0), (0, n - a.shape[1])))


def _pad_rows(a, n):
    return jnp.pad(a, ((0, n - a.shape[0]), (0, 0)))


def _lora_layout(a):
    o = 3 * C_A
    return jnp.concatenate([_pad_cols(a[:, o:o + LORA_W], LANES),
                            _pad_cols(a[:, o + LORA_W:o + LORA_W + LORA_A], LANES),
                            _pad_cols(a[:, o + LORA_W + LORA_A:A_PROJ], 2 * LANES)], axis=1)


def _lora_unlayout(a):
    return jnp.concatenate([a[:, :LORA_W], a[:, LANES:LANES + LORA_A], a[:, 2 * LANES:2 * LANES + LORA_G]], axis=1)


def _cpow_table(lbr, lbi, n):
    res_r, res_i = [lbr], [lbi]
    for _ in range(n - 1):
        pr, pi = res_r[-1], res_i[-1]
        res_r.append(pr * lbr - pi * lbi)
        res_i.append(pr * lbi + pi * lbr)
    return (jnp.stack([x.reshape(-1) for x in res_r]), jnp.stack([x.reshape(-1) for x in res_i]))


def ab_params(li, w, sample_len):
    p = {}
    w_in = w["ab_w_in"][li]
    p["w_rkv"] = w_in[:, :3 * C_A].astype(BF16)
    p["w_lora"] = _lora_layout(w_in).astype(BF16)
    p["w_u"] = w_in[:, A_PROJ:].astype(BF16)
    mu = w["ab_mu"][li][None, :]
    p["mu_rkv"] = mu[:, :3 * C_A]
    p["mu_lora"] = _lora_layout(mu)
    row = lambda a: a.reshape(1, -1)
    p["w0"] = row(w["rwkv_w0"][li])
    p["w2"] = _pad_rows(w["rwkv_w2"][li], LANES)
    p["a0"] = row(w["rwkv_a0"][li])
    p["a2"] = _pad_rows(w["rwkv_a2"][li], LANES)
    p["g2"] = _pad_rows(w["rwkv_g2"][li], 2 * LANES)
    p["k_k"] = row(w["rwkv_k_k"][li])
    p["k_a"] = row(w["rwkv_k_a"][li])
    p["r_k"] = row(w["rwkv_r_k"][li])
    p["ln_g"] = row(w["rwkv_ln_g"][li])
    p["ln_b"] = row(w["rwkv_ln_b"][li])
    head_of_col = jnp.arange(C_A) // HS_A
    gsel = (head_of_col[:, None] == jnp.arange(LANES)[None, :])
    p["gsel"] = gsel.astype(BF16)
    p["gselt"] = gsel.T.astype(BF16)
    lr = jnp.minimum(w["s5_lam_re"][li], -1e-4)
    lim = w["s5_lam_im"][li]
    dt = jnp.exp(w["s5_log_dt"][li])[:, None]
    mag = jnp.exp(lr * dt)
    lbr, lbi = mag * jnp.cos(lim * dt), mag * jnp.sin(lim * dt)
    den = lr * lr + lim * lim
    pr, pim = lbr - 1.0, lbi
    fr = (pr * lr + pim * lim) / den
    fi = (pim * lr - pr * lim) / den
    br, bi = w["s5_b_re"][li], w["s5_b_im"][li]
    bbr = fr[..., None] * br - fi[..., None] * bi
    bbi = fr[..., None] * bi + fi[..., None] * br
    gpc = S5_GROUPS // S5_KT
    eye = jnp.eye(gpc, dtype=F32)

    def blockdiag_in(bb):
        b4 = bb.reshape(S5_KT, gpc, S5_N, S5_P)
        return jnp.einsum("kgnp,gh->kgphn", b4, eye).reshape(S5_KT, S5_KU, S5_KH).astype(BF16)

    def blockdiag_out(cc):
        c4 = cc.reshape(S5_KT, gpc, S5_P, S5_N)
        return jnp.einsum("kgpn,gh->kgnhp", c4, eye).reshape(S5_KT, S5_KH, S5_KU).astype(BF16)

    p["wb_re"], p["wb_im"] = blockdiag_in(bbr), blockdiag_in(bbi)
    p["wc_re"], p["wc_im"] = blockdiag_out(w["s5_c_re"][li]), blockdiag_out(w["s5_c_im"][li])
    pw_r, pw_i = _cpow_table(lbr, lbi, SUBLANES)
    p["pow_re8"], p["pow_im8"] = pw_r, pw_i
    reps = SUBLANES // sample_len
    p["pow_re_s"] = jnp.tile(pw_r[:sample_len], (reps, 1))
    p["pow_im_s"] = jnp.tile(pw_i[:sample_len], (reps, 1))
    p["lamk"] = jnp.stack([pw_r[0], pw_i[0], pw_r[1], pw_i[1], pw_r[3], pw_i[3]])
    p["d_skip"] = row(w["s5_d"][li])
    p["w_glu"] = w["s5_w_glu"][li].astype(BF16)
    p["b_glu"] = row(w["s5_b_glu"][li])
    w_out = w["ab_w_out"][li]
    p["w_out_a"] = w_out[:C_A].astype(BF16)
    p["w_out_b"] = w_out[C_A:].astype(BF16)
    return p


def _shift_rows(a, first_sample, geo):
    n_p, l_p, n_s, l_s = geo
    wd = a.shape[1]
    ap = a[:n_p * l_p].reshape(n_p, l_p, wd)
    prev_p = jnp.concatenate([jnp.zeros((n_p, 1, wd), a.dtype), ap[:, :-1]], axis=1)
    as_ = a[n_p * l_p:].reshape(n_s, l_s, wd)
    prev_s = jnp.concatenate([first_sample[:, None, :], as_[:, :-1]], axis=1)
    return jnp.concatenate([prev_p.reshape(-1, wd), prev_s.reshape(-1, wd)], axis=0)


def _last_rows(a, geo):
    n_p, l_p, n_s, l_s = geo
    wd = a.shape[1]
    return (a[:n_p * l_p].reshape(n_p, l_p, wd)[:, -1], a[n_p * l_p:].reshape(n_s, l_s, wd)[:, -1])


def ab_layer(xb, geo, p, state_shift, state_wkv, wkv_seq0, state_re, state_im, *, tm, blk):
    n_p, l_p, n_s, l_s = geo
    tp, ts = n_p * l_p, n_s * l_s
    rkv = matmul([xb], [p["w_rkv"]], tm, 1024, name="ab_in_rkv")
    lora = matmul([xb], [p["w_lora"]], tm, LORA_PAD, name="ab_in_lora")
    u = matmul([xb], [p["w_u"]], tm, C_B, name="ab_in_u")
    rkv_prev = _shift_rows(rkv, state_shift[:, :3 * C_A], geo)
    lora_prev = _shift_rows(lora, _lora_layout(state_shift), geo)
    prep = rwkv_prep(rkv, rkv_prev, lora, lora_prev, p, tm)
    rows, g, bonus = prep[:RWKV_ROWS], prep[RWKV_ROWS], prep[RWKV_ROWS + 1]
    o_p, wkv_p = wkv_long(rows, n_seq=n_p, seq_len=l_p, blk=blk)
    o_s, wkv_s = wkv_short(rows, state_wkv, row0=tp, n_seq=n_s, seq_len=l_s, blk=blk, s0_seq0=wkv_seq0)
    o_wkv = jnp.concatenate([o_p.reshape(tp, C_A), o_s], axis=0)
    o_a = rwkv_post(o_wkv, bonus, g, p, tm)
    dummy = jnp.zeros((SUBLANES, S5_STATE), F32)
    ob_p, hr_p, hi_p = s5_mix(u, p, dummy, dummy, row0=0, n_seq=n_p, seq_len=l_p, blk=blk)
    h0r = jnp.repeat(state_re.reshape(n_s, S5_STATE), l_s, axis=0)
    h0i = jnp.repeat(state_im.reshape(n_s, S5_STATE), l_s, axis=0)
    ob_s, hr_s, hi_s = s5_mix(u, p, h0r, h0i, row0=tp, n_seq=n_s, seq_len=l_s, blk=blk)
    o_b = jnp.concatenate([ob_p, ob_s], axis=0)
    h = matmul([o_a, o_b], [p["w_out_a"], p["w_out_b"]], tm, 1024, name="ab_out")
    rkv_lp, rkv_ls = _last_rows(rkv, geo)
    lo_lp, lo_ls = _last_rows(lora, geo)
    shift_p = jnp.concatenate([rkv_lp, _lora_unlayout(lo_lp)], axis=1)
    shift_s = jnp.concatenate([rkv_ls, _lora_unlayout(lo_ls)], axis=1)
    grp = lambda a, n: a.reshape(n, S5_GROUPS, S5_N)
    ssm_p = (grp(hr_p[:, SUBLANES - 1], n_p), grp(hi_p[:, SUBLANES - 1], n_p))
    ssm_s = (grp(hr_s[l_s - 1::l_s], n_s), grp(hi_s[l_s - 1::l_s], n_s))
    return h, (shift_p, shift_s), (wkv_p, wkv_s), ssm_p, ssm_s


def _router_kernel(x_ref, wr_ref, br_ref, ltri_ref, eidx_o, gate_o, rank_o, cnt_o, carry):
    i = pl.program_id(0)

    @pl.when(i == 0)
    def _():
        carry[...] = jnp.zeros(carry.shape, F32)

    tm = x_ref.shape[0]
    neg = -jnp.inf
    lane = lax.broadcasted_iota(I32, (tm, LANES), 1)
    real = lane < N_EXPERTS
    per_group = N_EXPERTS // N_EXPERT_GROUPS
    grp = lane // per_group
    scores = jax.nn.sigmoid(_dot3(x_ref[...], wr_ref[...]))
    biased = jnp.where(real, scores + br_ref[...], neg)

    gs = []
    for g in range(N_EXPERT_GROUPS):
        xg = jnp.where(grp == g, biased, neg)
        m1 = jnp.max(xg, axis=-1, keepdims=True)
        i1 = jnp.min(jnp.where(xg == m1, lane, LANES), axis=-1, keepdims=True)
        m2 = jnp.max(jnp.where(lane == i1, neg, xg), axis=-1, keepdims=True)
        gs.append(m1 + m2)
    keep = jnp.zeros((tm, LANES), jnp.bool_)
    for g in range(N_EXPERT_GROUPS):
        better = jnp.zeros((tm, 1), I32)
        for g2 in range(N_EXPERT_GROUPS):
            if g2 == g:
                continue
            wins = (gs[g2] > gs[g]) | ((gs[g2] == gs[g]) & (g2 < g))
            better = better + wins.astype(I32)
        keep = keep | ((grp == g) & (better < TOPK_GROUPS))
    masked = jnp.where(keep, biased, neg)

    cur = masked
    sel = jnp.zeros((tm, LANES), F32)
    es, ss = [], []
    for _ in range(TOP_K):
        m = jnp.max(cur, axis=-1, keepdims=True)
        ik = jnp.min(jnp.where(cur == m, lane, LANES), axis=-1, keepdims=True)
        hit = lane == ik
        ss.append(jnp.sum(jnp.where(hit, scores, 0.0), axis=-1, keepdims=True))
        es.append(ik)
        cur = jnp.where(hit, neg, cur)
        sel = jnp.where(hit, 1.0, sel)
    tot = ss[0]
    for k in range(1, TOP_K):
        tot = tot + ss[k]

    prefix = _dot(ltri_ref[...], sel.astype(BF16)) + carry[0:1, :]
    eidx = jnp.zeros((tm, LANES), I32)
    gate = jnp.zeros((tm, LANES), F32)
    rank = jnp.zeros((tm, LANES), I32)
    for k in range(TOP_K):
        slot = lane == k
        rk = jnp.sum(jnp.where(lane == es[k], prefix, 0.0), axis=-1, keepdims=True)
        eidx = jnp.where(slot, es[k], eidx)
        gate = jnp.where(slot, ss[k] / tot * ROUTED_SCALE, gate)
        rank = jnp.where(slot, rk.astype(I32), rank)
    eidx_o[...] = eidx
    gate_o[...] = gate
    rank_o[...] = rank
    carry[...] = carry[...] + jnp.sum(sel, axis=0, keepdims=True)
    cnt_o[...] = carry[...]


def moe_router(x, w_router_pad, b_router_pad, tm):
    t, d = x.shape
    ltri = (jnp.arange(tm)[:, None] > jnp.arange(tm)[None, :]).astype(BF16)
    row = pl.BlockSpec((tm, LANES), lambda i: (i, 0))
    out = jax.ShapeDtypeStruct((t, LANES), I32)
    return pl.pallas_call(
        _router_kernel,
        grid=(t // tm,),
        in_specs=[pl.BlockSpec((tm, d), lambda i: (i, 0)), pl.BlockSpec((d, LANES), lambda i: (0, 0)),
                  pl.BlockSpec((1, LANES), lambda i: (0, 0)), pl.BlockSpec((tm, tm), lambda i: (0, 0))],
        out_specs=[row, row, row, pl.BlockSpec((SUBLANES, LANES), lambda i: (0, 0))],
        out_shape=[out, jax.ShapeDtypeStruct((t, LANES), F32), out,
                   jax.ShapeDtypeStruct((SUBLANES, LANES), F32)],
        scratch_shapes=[pltpu.VMEM((SUBLANES, LANES), F32)],
        compiler_params=_cparams(("arbitrary",)),
        name="moe_router",
    )(x, w_router_pad, b_router_pad, ltri)


MOE_BM = 256
GATHER_UNROLL = 8


def _gather_rows_start(src_hbm, idx_ref, idx0, dst_buf, n_rows, sem):
    def body(j, c):
        pltpu.make_async_copy(src_hbm.at[pl.ds(idx_ref[idx0 + j], 1), :], dst_buf.at[pl.ds(j, 1), :], sem).start()
        return c

    lax.fori_loop(0, n_rows, body, 0, unroll=GATHER_UNROLL)


def _gather_rows_wait(src_hbm, dst_buf, n_rows, sem):
    pltpu.make_async_copy(src_hbm.at[pl.ds(0, n_rows), :], dst_buf, sem).wait()


def _expert_kernel(blk_e_ref, nused_ref, tok_ref, x_hbm, wg_ref, wu_ref, wd_ref, ys_ref, xbuf, sems, wgb, wub, wdb):
    i = pl.program_id(0)
    n_used = nused_ref[0]

    @pl.when(i == 0)
    def _():
        _gather_rows_start(x_hbm, tok_ref, 0, xbuf.at[0], MOE_BM, sems.at[0])

    @pl.when(i < n_used)
    def _():
        e = blk_e_ref[i]
        prev = blk_e_ref[jnp.maximum(i - 1, 0)]

        @pl.when((i == 0) | (e != prev))
        def _():
            wgb[...] = wg_ref[0].astype(BF16)
            wub[...] = wu_ref[0].astype(BF16)
            wdb[...] = wd_ref[0].astype(BF16)

        cur = i % 2
        nxt = 1 - cur
        _gather_rows_wait(x_hbm, xbuf.at[cur], MOE_BM, sems.at[cur])
        nxt_row0 = jnp.minimum(i + 1, n_used - 1) * MOE_BM
        for j in range(MOE_BM):
            pltpu.make_async_copy(x_hbm.at[pl.ds(tok_ref[nxt_row0 + j], 1), :], xbuf.at[nxt, pl.ds(j, 1), :],
                                  sems.at[nxt]).start()
        x = xbuf[cur].astype(BF16)
        hg = _dot(x, wgb[...])
        hu = _dot(x, wub[...])
        h = hg * jax.nn.sigmoid(hg) * hu
        ys_ref[...] = _dot(h.astype(BF16), wdb[...])

        @pl.when(i == n_used - 1)
        def _():
            _gather_rows_wait(x_hbm, xbuf.at[nxt], MOE_BM, sems.at[nxt])

    @pl.when(i >= nused_ref[0])
    def _():
        ys_ref[...] = jnp.zeros(ys_ref.shape, F32)


def moe_experts(x, blk_e, n_used, tok_of_row, w_gate, w_up, w_down):
    d = x.shape[1]
    n_rows = tok_of_row.shape[0]
    n_blocks = n_rows // MOE_BM
    de = w_gate.shape[2]
    wmap = lambda i, be, nu, tk: (be[jnp.minimum(i, nu[0] - 1)], 0, 0)
    return pl.pallas_call(
        _expert_kernel,
        grid_spec=pltpu.PrefetchScalarGridSpec(
            num_scalar_prefetch=3,
            grid=(n_blocks,),
            in_specs=[pl.BlockSpec(memory_space=pl.ANY),
                      pl.BlockSpec((1, d, de), wmap), pl.BlockSpec((1, d, de), wmap), pl.BlockSpec((1, de, d), wmap)],
            out_specs=pl.BlockSpec((MOE_BM, d), lambda i, be, nu, tk: (i, 0)),
            scratch_shapes=[pltpu.VMEM((2, MOE_BM, d), F32), pltpu.SemaphoreType.DMA((2,)),
                            pltpu.VMEM((d, de), BF16), pltpu.VMEM((d, de), BF16), pltpu.VMEM((de, d), BF16)],
        ),
        out_shape=jax.ShapeDtypeStruct((n_rows, d), F32),
        compiler_params=_cparams(("arbitrary",)),
        name="moe_experts",
    )(blk_e, n_used, tok_of_row, x, w_gate, w_up, w_down)


def _shared_ffn_kernel(x_ref, wg_ref, wu_ref, wd_ref, o_ref):
    x = x_ref[...]
    hg = _dot(x, wg_ref[...])
    hu = _dot(x, wu_ref[...])
    h = hg * jax.nn.sigmoid(hg) * hu
    o_ref[...] = _dot(h.astype(BF16), wd_ref[...])


def shared_ffn(xb, wg, wu, wd, tm):
    t, d = xb.shape
    full = lambda a: pl.BlockSpec(a.shape, lambda i: (0, 0))
    row = pl.BlockSpec((tm, d), lambda i: (i, 0))
    return pl.pallas_call(
        _shared_ffn_kernel,
        grid=(t // tm,),
        in_specs=[row, full(wg), full(wu), full(wd)],
        out_specs=row,
        out_shape=jax.ShapeDtypeStruct((t, d), F32),
        compiler_params=_cparams(("parallel",)),
        name="shared_ffn",
    )(xb, wg, wu, wd)


def _combine_kernel(dest_ref, ys_hbm, gate_ref, x_ref, sh_ref, g_ref, b_ref, o_ref, ob_ref, buf, sems, *, tm):
    i = pl.program_id(0)
    n_asg = tm * TOP_K

    @pl.when(i == 0)
    def _():
        _gather_rows_start(ys_hbm, dest_ref, 0, buf.at[0], n_asg, sems.at[0])

    @pl.when(i + 1 < pl.num_programs(0))
    def _():
        nxt = (i + 1) % 2
        _gather_rows_start(ys_hbm, dest_ref, (i + 1) * n_asg, buf.at[nxt], n_asg, sems.at[nxt])

    cur = i % 2
    _gather_rows_wait(ys_hbm, buf.at[cur], n_asg, sems.at[cur])
    gate = gate_ref[...]
    f = sh_ref[...]
    for k in range(TOP_K):
        f = f + gate[:, k:k + 1] * buf[cur, k * tm:(k + 1) * tm, :]
    z = ALPHA * x_ref[...] + f
    inv_d = 1.0 / z.shape[1]
    mu = _row_sum(z) * inv_d
    dlt = z - mu
    var = _row_sum(dlt * dlt) * inv_d
    y = dlt * lax.rsqrt(var + LN_EPS) * g_ref[...] + b_ref[...]
    o_ref[...] = y
    ob_ref[...] = y.astype(BF16)


def moe_combine(ys, dest_flat, gates, x, shared, g, b, tm):
    t, d = x.shape
    row = pl.BlockSpec((tm, d), lambda i, dr: (i, 0))
    vec = pl.BlockSpec((1, d), lambda i, dr: (0, 0))
    return pl.pallas_call(
        functools.partial(_combine_kernel, tm=tm),
        grid_spec=pltpu.PrefetchScalarGridSpec(
            num_scalar_prefetch=1,
            grid=(t // tm,),
            in_specs=[pl.BlockSpec(memory_space=pl.ANY), pl.BlockSpec((tm, LANES), lambda i, dr: (i, 0)),
                      row, row, vec, vec],
            out_specs=[row, row],
            scratch_shapes=[pltpu.VMEM((2, TOP_K * tm, d), F32), pltpu.SemaphoreType.DMA((2,))],
        ),
        out_shape=[jax.ShapeDtypeStruct((t, d), F32), jax.ShapeDtypeStruct((t, d), BF16)],
        compiler_params=_cparams(("arbitrary",)),
        name="moe_combine",
    )(dest_flat, ys, gates, x, shared, g.reshape(1, d), b.reshape(1, d))


def moe_params(layer, w):
    p = {}
    p["w_router"] = _pad_cols(w["moe_w_router"][layer], LANES)
    p["b_router"] = _pad_cols(w["moe_b_router"][layer][None, :], LANES)
    stack = lambda a: a.reshape((-1,) + a.shape[2:])
    p["w_gate"], p["w_up"], p["w_down"] = stack(w["moe_w_gate"]), stack(w["moe_w_up"]), stack(w["moe_w_down"])
    p["e0"] = layer * N_EXPERTS
    p["ws_gate"] = w["moe_ws_gate"][layer].astype(BF16)
    p["ws_up"] = w["moe_ws_up"][layer].astype(BF16)
    p["ws_down"] = w["moe_ws_down"][layer].astype(BF16)
    return p


def moe_layer(y, yb, p, ln_g, ln_b, *, tm, tm_dma):
    t = y.shape[0]
    eidx, gates, rank, cnt = moe_router(y, p["w_router"], p["b_router"], tm)
    counts = cnt[0, :N_EXPERTS].astype(I32)
    padded = (counts + MOE_BM - 1) // MOE_BM * MOE_BM
    pad_end = jnp.cumsum(padded)
    pad_start = pad_end - padded
    e8 = eidx[:, :TOP_K]
    dest = pad_start[e8] + rank[:, :TOP_K]
    n_blocks = -(-(t * TOP_K + N_EXPERTS * (MOE_BM - 1)) // MOE_BM)
    n_rows = n_blocks * MOE_BM
    blk_start = jnp.arange(n_blocks, dtype=I32) * MOE_BM
    blk_e = jnp.sum((pad_end[None, :] <= blk_start[:, None]).astype(I32), axis=1)
    blk_e = jnp.minimum(blk_e, N_EXPERTS - 1) + p["e0"]
    n_used = (pad_end[-1:] // MOE_BM).astype(I32)
    tok = jnp.broadcast_to(jnp.arange(t, dtype=I32)[:, None], (t, TOP_K))
    tok_of_row = jnp.zeros((n_rows,), I32).at[dest.reshape(-1)].set(tok.reshape(-1))
    ys = moe_experts(y, blk_e, n_used, tok_of_row, p["w_gate"], p["w_up"], p["w_down"])
    shared = shared_ffn(yb, p["ws_gate"], p["ws_up"], p["ws_down"], tm)
    dest_tiles = dest.reshape(t // tm_dma, tm_dma, TOP_K).transpose(0, 2, 1).reshape(-1)
    return moe_combine(ys, dest_tiles, gates, y, shared, ln_g, ln_b, tm_dma)


C_QKV = (H_C + 2 * KVH_C) * DH_C
C_QI = H_IDX * D_IDX
IDX_SCALE = (H_IDX ** -0.5) * (D_IDX ** -0.5)


def _rot_half64(x, lane):
    return jnp.where(lane % D_IDX < D_IDX // 2, pltpu.roll(x, LANES - D_IDX // 2, 1), pltpu.roll(x, D_IDX // 2, 1))


def _c_post_kernel(qkv_ref, qi_ref, kw_ref, c128_ref, s128_ref, c64_ref, s64_ref, lng_ref, lnb_ref,
                   qb_o, kf_o, kb_o, vb_o, qib_o, kwo_o):
    tm = qkv_ref.shape[0]
    c128, s128 = c128_ref[...], s128_ref[...]
    c64, s64 = c64_ref[...], s64_ref[...]
    lane = lax.broadcasted_iota(I32, (tm, LANES), 1)
    for h in range(H_C + KVH_C):
        x = qkv_ref[:, h * DH_C:(h + 1) * DH_C]
        y = x * c128 + pltpu.roll(x, DH_C // 2, 1) * s128
        if h < H_C:
            qb_o[:, h * DH_C:(h + 1) * DH_C] = (y * (DH_C ** -0.5)).astype(BF16)
        else:
            j = h - H_C
            kf_o[:, j * DH_C:(j + 1) * DH_C] = y
            kb_o[:, j * DH_C:(j + 1) * DH_C] = y.astype(BF16)
    vb_o[...] = qkv_ref[:, (H_C + KVH_C) * DH_C:].astype(BF16)
    for j in range(C_QI // LANES):
        x = qi_ref[:, j * LANES:(j + 1) * LANES]
        qib_o[:, j * LANES:(j + 1) * LANES] = (x * c64 + _rot_half64(x, lane) * s64).astype(BF16)
    x = kw_ref[...]
    is_k = lane < D_IDX
    mu = jnp.sum(jnp.where(is_k, x, 0.0), axis=-1, keepdims=True) * (1.0 / D_IDX)
    d = jnp.where(is_k, x - mu, 0.0)
    var = jnp.sum(d * d, axis=-1, keepdims=True) * (1.0 / D_IDX)
    y = d * lax.rsqrt(var + LN_EPS) * lng_ref[...] + lnb_ref[...]
    y = y * c64 + _rot_half64(y, lane) * s64
    kwo_o[...] = jnp.where(is_k, y, x * IDX_SCALE)


def c_post(qkv, qi, kw, tabs, lng, lnb, tm):
    t = qkv.shape[0]
    row = lambda wd: pl.BlockSpec((tm, wd), lambda i: (i, 0))
    vec = pl.BlockSpec((1, LANES), lambda i: (0, 0))
    sh = lambda wd, dt: jax.ShapeDtypeStruct((t, wd), dt)
    return pl.pallas_call(
        _c_post_kernel,
        grid=(t // tm,),
        in_specs=[row(C_QKV), row(C_QI), row(LANES), row(LANES), row(LANES), row(LANES), row(LANES), vec, vec],
        out_specs=[row(H_C * DH_C), row(KVH_C * DH_C), row(KVH_C * DH_C), row(KVH_C * DH_C), row(C_QI), row(LANES)],
        out_shape=[sh(H_C * DH_C, BF16), sh(KVH_C * DH_C, F32), sh(KVH_C * DH_C, BF16), sh(KVH_C * DH_C, BF16),
                   sh(C_QI, BF16), sh(LANES, F32)],
        compiler_params=_cparams(("parallel",)),
        name="c_post",
    )(qkv, qi, kw, *tabs, lng, lnb)


def _rope_tables(pos):
    def tab(d):
        inv = ROPE_THETA ** (-jnp.arange(0, d, 2, dtype=F32) / d)
        ang = pos.astype(F32)[:, None] * inv[None, :]
        c, s = jnp.cos(ang), jnp.sin(ang)
        return jnp.concatenate([c, c], axis=1), jnp.concatenate([-s, s], axis=1)

    c128, s128 = tab(DH_C)
    c64, s64 = tab(D_IDX)
    return c128, s128, jnp.tile(c64, (1, 2)), jnp.tile(s64, (1, 2))


def _index_heads(qi, wi, kb):
    acc = None
    for h in range(H_IDX):
        d = lax.dot_general(qi[:, h * D_IDX:(h + 1) * D_IDX], kb, (((1,), (1,)), ((), ())),
                            preferred_element_type=F32)
        term = wi[:, D_IDX + h:D_IDX + h + 1] * jnp.maximum(d, 0.0)
        acc = term if acc is None else acc + term
    return acc + 0.0


def _scores_prompt_kernel(qi_ref, wi_ref, keys_ref, o_ref, *, qblk):
    qb = pl.program_id(1)
    s_len = keys_ref.shape[0]
    kb = keys_ref[:, :D_IDX].astype(BF16)
    sc = _index_heads(qi_ref[...], wi_ref[...], kb)
    row = lax.broadcasted_iota(I32, (qblk, s_len), 0) + qb * qblk
    col = lax.broadcasted_iota(I32, (qblk, s_len), 1)
    o_ref[...] = jnp.where(col <= row, sc, -jnp.inf)


def scores_prompt(qib, kwo, n_p, l_p, qblk):
    nq = l_p // qblk
    return pl.pallas_call(
        functools.partial(_scores_prompt_kernel, qblk=qblk),
        grid=(n_p, nq),
        in_specs=[pl.BlockSpec((qblk, C_QI), lambda b, q: (b * nq + q, 0)),
                  pl.BlockSpec((qblk, LANES), lambda b, q: (b * nq + q, 0)),
                  pl.BlockSpec((l_p, LANES), lambda b, q: (b, 0))],
        out_specs=pl.BlockSpec((qblk, l_p), lambda b, q: (b * nq + q, 0)),
        out_shape=jax.ShapeDtypeStruct((n_p * l_p, l_p), F32),
        compiler_params=_cparams(("parallel", "parallel")),
        name="scores_prompt",
    )(qib, kwo, kwo)


SROWS = 16


def _topk_mask(sc_ref, utri_ref, o_ref, key_scr, *, topk, chunk):
    rows, s_len = sc_ref.shape
    bits = lax.bitcast_convert_type(sc_ref[...], I32)
    key_scr[...] = bits ^ ((bits >> 31) & jnp.int32(0x7FFFFFFF))
    sign = jnp.int32(-2 ** 31)
    kf = jnp.float32(topk)

    def bisect(it, t_u):
        cand = t_u | (jnp.int32(1) << (31 - it))
        cnt = _row_sum(jnp.where(key_scr[...] >= (cand ^ sign), 1.0, 0.0))
        return jnp.where(cnt >= kf, cand, t_u)

    t_u = lax.fori_loop(0, 32, bisect, jnp.zeros((rows, 1), I32))
    thr = t_u ^ sign
    need = kf - _row_sum(jnp.where(key_scr[...] > thr, 1.0, 0.0))
    utri = utri_ref[...]
    run = jnp.zeros((rows, 1), F32)
    for c in range(s_len // chunk):
        cs = slice(c * chunk, (c + 1) * chunk)
        keyc = key_scr[:, cs]
        eqc = jnp.where(keyc == thr, 1.0, 0.0)
        before = _dot(eqc.astype(BF16), utri) + run
        pick = (keyc > thr) | ((keyc == thr) & (before < need))
        o_ref[:, cs] = jnp.where(pick & (sc_ref[:, cs] > -jnp.inf), 1.0, 0.0).astype(o_ref.dtype)
        run = run + _row_sum(eqc)
    if o_ref.shape[1] > s_len:
        o_ref[:, s_len:] = jnp.zeros((rows, o_ref.shape[1] - s_len), o_ref.dtype)


def _strict_upper(chunk):
    return (jnp.arange(chunk)[:, None] < jnp.arange(chunk)[None, :]).astype(BF16)


SELECT_CLASSES = 4


def topk_select_causal(scores, topk, n_seq, seq_len, rblk, chunk):
    n_cls = SELECT_CLASSES if seq_len % (SELECT_CLASSES * max(chunk, rblk)) == 0 else 1
    cw = seq_len // n_cls
    utri = _strict_upper(chunk)
    parts = []
    for c in range(n_cls):
        width = (c + 1) * cw
        rmap = functools.partial(lambda b, i, c: (b * (seq_len // rblk) + c * (cw // rblk) + i, 0), c=c)
        parts.append(pl.pallas_call(
            functools.partial(_topk_mask, topk=topk, chunk=chunk),
            grid=(n_seq, cw // rblk),
            in_specs=[pl.BlockSpec((rblk, width), rmap), pl.BlockSpec((chunk, chunk), lambda b, i: (0, 0))],
            out_specs=pl.BlockSpec((rblk, seq_len), lambda b, i: (b * (cw // rblk) + i, 0)),
            out_shape=jax.ShapeDtypeStruct((n_seq * cw, seq_len), BF16),
            scratch_shapes=[pltpu.VMEM((rblk, width), I32)],
            compiler_params=_cparams(("parallel", "parallel")),
            name="topk_select_%d" % c,
        )(scores, utri).reshape(n_seq, cw, seq_len))
    return jnp.concatenate(parts, axis=1).reshape(n_seq * seq_len, seq_len)


NEG_BIG = -1e30


def _attn_init(m_scr, l_scr, acc_scr):
    m_scr[...] = jnp.full(m_scr.shape, NEG_BIG, F32)
    l_scr[...] = jnp.zeros(l_scr.shape, F32)
    acc_scr[...] = jnp.zeros(acc_scr.shape, F32)


def _attn_block(q, k, v, mask, m_scr, l_scr, acc_scr):
    bias = jnp.concatenate([(mask.astype(F32) - 1.0) * (-NEG_BIG)] * GROUP_C, axis=0)
    heads = range(KVH_C)
    ss = []
    for kh in heads:
        qh = jnp.concatenate([q[:, (kh * GROUP_C + g) * DH_C:(kh * GROUP_C + g + 1) * DH_C]
                              for g in range(GROUP_C)], axis=0)
        ks = k[:, kh * DH_C:(kh + 1) * DH_C]
        ss.append(lax.dot_general(qh, ks, (((1,), (1,)), ((), ())), preferred_element_type=F32) + bias)
    n_tiles = ss[0].shape[1] // LANES
    m_old = [m_scr[kh] for kh in heads]
    m_new = [jnp.maximum(m_old[kh], jnp.broadcast_to(_row_max(ss[kh]), m_old[kh].shape)) for kh in heads]
    ps = [[jnp.exp(ss[kh][:, i * LANES:(i + 1) * LANES] - m_new[kh]) for i in range(n_tiles)] for kh in heads]
    for kh in heads:
        alpha = jnp.exp(m_old[kh] - m_new[kh])
        l_scr[kh] = alpha * l_scr[kh] + functools.reduce(jnp.add, ps[kh])
        p = jnp.concatenate(ps[kh], axis=1).astype(BF16)
        acc_scr[kh] = alpha * acc_scr[kh] + _dot(p, v[:, kh * DH_C:(kh + 1) * DH_C])
        m_scr[kh] = m_new[kh]


def _attn_finish(o_ref, nq, l_scr, acc_scr, lead=None):
    for kh in range(KVH_C):
        out = acc_scr[kh] / jnp.sum(l_scr[kh], axis=-1, keepdims=True)
        for g in range(GROUP_C):
            hs = slice((kh * GROUP_C + g) * DH_C, (kh * GROUP_C + g + 1) * DH_C)
            val = out[g * nq:(g + 1) * nq, :].astype(o_ref.dtype)
            if lead is None:
                o_ref[:, hs] = val
            else:
                o_ref[lead, :, hs] = val


def _attn_prompt_kernel(q_ref, k_ref, v_ref, mask_ref, o_ref, m_scr, l_scr, acc_scr, *, qblk, kblk):
    qb = pl.program_id(1)
    kb = pl.program_id(2)

    @pl.when(kb == 0)
    def _():
        _attn_init(m_scr, l_scr, acc_scr)

    @pl.when(kb * kblk <= qb * qblk + qblk - 1)
    def _():
        _attn_block(q_ref[...], k_ref[...], v_ref[...], mask_ref[...], m_scr, l_scr, acc_scr)

    @pl.when(kb == pl.num_programs(2) - 1)
    def _():
        _attn_finish(o_ref, qblk, l_scr, acc_scr)


def attn_prompt(qb_, kb_, vb_, mask, n_p, l_p, qblk, kblk):
    nq, nk = l_p // qblk, l_p // kblk
    last = lambda q, k: jnp.minimum(k, (q * qblk + qblk - 1) // kblk)
    rows = GROUP_C * qblk
    return pl.pallas_call(
        functools.partial(_attn_prompt_kernel, qblk=qblk, kblk=kblk),
        grid=(n_p, nq, nk),
        in_specs=[pl.BlockSpec((qblk, H_C * DH_C), lambda b, q, k: (b * nq + q, 0)),
                  pl.BlockSpec((kblk, KVH_C * DH_C), lambda b, q, k: (b * nk + last(q, k), 0)),
                  pl.BlockSpec((kblk, KVH_C * DH_C), lambda b, q, k: (b * nk + last(q, k), 0)),
                  pl.BlockSpec((qblk, kblk), lambda b, q, k: (b * nq + q, last(q, k)))],
        out_specs=pl.BlockSpec((qblk, H_C * DH_C), lambda b, q, k: (b * nq + q, 0)),
        out_shape=jax.ShapeDtypeStruct((n_p * l_p, H_C * DH_C), BF16),
        scratch_shapes=[pltpu.VMEM((KVH_C, rows, LANES), F32), pltpu.VMEM((KVH_C, rows, LANES), F32),
                        pltpu.VMEM((KVH_C, rows, DH_C), F32)],
        compiler_params=_cparams(("parallel", "parallel", "arbitrary")),
        name="attn_prompt",
    )(qb_, kb_, vb_, mask)


def _dsa_sample_kernel(pt_ref, qi_ref, kw_ref, q_ref, kn_ref, vn_ref, utri_ref, kidx_hbm, k_hbm, v_hbm, o_ref,
                       kibuf, kbuf, vbuf, sems, sc_scr, key_scr, mask_scr, *, n_pages, page, l_s, topk, pool0):
    b = pl.program_id(0)
    past = n_pages * page
    s_len = past + page

    def page_copies(bb, slot, pg):
        phys = pt_ref[bb * n_pages + pg] + pool0
        rows = pl.ds(pg * page, page)
        cps = [pltpu.make_async_copy(kidx_hbm.at[phys], kibuf.at[slot, rows, :], sems.at[slot, 0])]
        for kh in range(KVH_C):
            cps.append(pltpu.make_async_copy(k_hbm.at[phys, :, kh, :], kbuf.at[slot, kh, rows, :], sems.at[slot, 1]))
            cps.append(pltpu.make_async_copy(v_hbm.at[phys, :, kh, :], vbuf.at[slot, kh, rows, :], sems.at[slot, 2]))
        return cps

    def fetch(bb, slot):
        for pg in range(n_pages):
            for cp in page_copies(bb, slot, pg):
                cp.start()

    @pl.when(b == 0)
    def _():
        kibuf[:, past:, :] = jnp.zeros((2, page, D_IDX), F32)
        kbuf[:, :, past:, :] = jnp.zeros((2, KVH_C, page, DH_C), F32)
        vbuf[:, :, past:, :] = jnp.zeros((2, KVH_C, page, DH_C), F32)
        fetch(0, 0)

    @pl.when(b + 1 < pl.num_programs(0))
    def _():
        fetch(b + 1, (b + 1) % 2)

    cur = b % 2
    for pg in range(n_pages):
        for cp in page_copies(b, cur, pg):
            cp.wait()
    kw = kw_ref[0]
    kibuf[cur, past:past + SROWS, :] = kw[:, :D_IDX]
    for kh in range(KVH_C):
        hs = slice(kh * DH_C, (kh + 1) * DH_C)
        kbuf[cur, kh, past:past + SROWS, :] = kn_ref[0][:, hs]
        vbuf[cur, kh, past:past + SROWS, :] = vn_ref[0][:, hs]

    dots = lax.dot_general(qi_ref[0], kibuf[cur].astype(BF16), (((1,), (1,)), ((), ())),
                           preferred_element_type=F32)
    sc = None
    for h in range(H_IDX):
        term = kw[:, D_IDX + h:D_IDX + h + 1] * jnp.maximum(dots[h * SROWS:(h + 1) * SROWS, :], 0.0)
        sc = term if sc is None else sc + term
    row = lax.broadcasted_iota(I32, (SROWS, s_len), 0)
    col = lax.broadcasted_iota(I32, (SROWS, s_len), 1)
    visible = (col < past) | ((col - past <= row) & (col - past < l_s))
    sc_scr[...] = jnp.where(visible, sc + 0.0, -jnp.inf)
    _topk_mask(sc_scr, utri_ref, mask_scr, key_scr, topk=topk, chunk=page)

    bias = jnp.concatenate([(mask_scr[...] - 1.0) * (-NEG_BIG)] * GROUP_C, axis=0)
    q = q_ref[0]
    for kh in range(KVH_C):
        qh = jnp.concatenate([q[:, (kh * GROUP_C + g) * DH_C:(kh * GROUP_C + g + 1) * DH_C]
                              for g in range(GROUP_C)], axis=0)
        s = lax.dot_general(qh, kbuf[cur, kh].astype(BF16), (((1,), (1,)), ((), ())),
                            preferred_element_type=F32) + bias
        p = jnp.exp(s - _row_max(s))
        out = _dot(p.astype(BF16), vbuf[cur, kh].astype(BF16)) / _row_sum(p)
        for g in range(GROUP_C):
            o_ref[0, :, (kh * GROUP_C + g) * DH_C:(kh * GROUP_C + g + 1) * DH_C] = (
                out[g * SROWS:(g + 1) * SROWS, :].astype(o_ref.dtype))


def dsa_sample(page_table, qi_hm, kw3, q3, knew3, vnew3, cache_kidx3, cache_k4, cache_v4, *, pool0, l_s, topk):
    n_s, n_pages = page_table.shape
    page = cache_k4.shape[1]
    kvw = KVH_C * DH_C
    s_len = (n_pages + 1) * page
    own = lambda wd: pl.BlockSpec((1, SROWS, wd), lambda b, pt: (b, 0, 0))
    hbm = pl.BlockSpec(memory_space=pl.ANY)
    return pl.pallas_call(
        functools.partial(_dsa_sample_kernel, n_pages=n_pages, page=page, l_s=l_s, topk=topk, pool0=pool0),
        grid_spec=pltpu.PrefetchScalarGridSpec(
            num_scalar_prefetch=1,
            grid=(n_s,),
            in_specs=[pl.BlockSpec((1, H_IDX * SROWS, D_IDX), lambda b, pt: (b, 0, 0)),
                      own(LANES), own(H_C * DH_C), own(kvw), own(kvw),
                      pl.BlockSpec((page, page), lambda b, pt: (0, 0)), hbm, hbm, hbm],
            out_specs=own(H_C * DH_C),
            scratch_shapes=[pltpu.VMEM((2, s_len, D_IDX), F32), pltpu.VMEM((2, KVH_C, s_len, DH_C), F32),
                            pltpu.VMEM((2, KVH_C, s_len, DH_C), F32), pltpu.SemaphoreType.DMA((2, 3)),
                            pltpu.VMEM((SROWS, s_len), F32), pltpu.VMEM((SROWS, s_len), I32),
                            pltpu.VMEM((SROWS, s_len), F32)],
        ),
        out_shape=jax.ShapeDtypeStruct((n_s, SROWS, H_C * DH_C), BF16),
        compiler_params=_cparams(("arbitrary",)),
        name="dsa_sample",
    )(page_table.reshape(-1), qi_hm, kw3, q3, knew3, vnew3, _strict_upper(page), cache_kidx3, cache_k4, cache_v4)


def c_params(li, w):
    p = {}
    w_in = w["c_w_in"][li]
    p["w_qkv"] = w_in[:, :C_QKV].astype(BF16)
    p["w_qi"] = w_in[:, C_QKV:C_QKV + C_QI].astype(BF16)
    p["w_kw"] = _pad_cols(w_in[:, C_QKV + C_QI:], LANES).astype(BF16)
    p["ln_g"] = _pad_cols(w["c_kidx_ln_g"][li][None, :], LANES)
    p["ln_b"] = _pad_cols(w["c_kidx_ln_b"][li][None, :], LANES)
    p["w_out"] = w["c_w_out"][li].astype(BF16)
    return p


def _pad_sample_rows(a, n_s, l_s):
    a3 = a.reshape(n_s, l_s, a.shape[1])
    return jnp.pad(a3, ((0, 0), (0, SROWS - l_s), (0, 0)))


def c_layer(xb, geo, p, li, cache_k, cache_v, cache_kidx, page_table, *, tm, qblk, kblk):
    n_p, l_p, n_s, l_s = geo
    tp = n_p * l_p
    n_pool, page = cache_k.shape[1], cache_k.shape[2]
    n_pages = page_table.shape[1]
    past = n_pages * page
    qkv = matmul([xb], [p["w_qkv"]], tm, 1024, name="c_in_qkv")
    qi = matmul([xb], [p["w_qi"]], tm, C_QI, name="c_in_qi")
    kw = matmul([xb], [p["w_kw"]], tm, LANES, name="c_in_kw")
    pos = jnp.concatenate([jnp.tile(jnp.arange(l_p, dtype=I32), n_p),
                           jnp.tile(past + jnp.arange(l_s, dtype=I32), n_s)])
    q_b, k_f, k_b, v_b, qi_b, kwo = c_post(qkv, qi, kw, _rope_tables(pos), p["ln_g"], p["ln_b"], tm)
    v_f = qkv[:, (H_C + KVH_C) * DH_C:]
    sc_p = scores_prompt(qi_b, kwo, n_p, l_p, qblk)
    mask_p = topk_select_causal(sc_p, min(TOPK_MAX, l_p // 4), n_p, l_p, qblk, 256)
    o_p = attn_prompt(q_b, k_b, v_b, mask_p, n_p, l_p, qblk, kblk)
    pad = lambda a: _pad_sample_rows(a[tp:], n_s, l_s)
    qi_hm = pad(qi_b).reshape(n_s, SROWS, H_IDX, D_IDX).transpose(0, 2, 1, 3).reshape(n_s, H_IDX * SROWS, D_IDX)
    pool = lambda a: a.reshape((-1,) + a.shape[2:])
    o_s3 = dsa_sample(page_table, qi_hm, pad(kwo), pad(q_b), pad(k_f), pad(v_f),
                      pool(cache_kidx), pool(cache_k), pool(cache_v), pool0=li * n_pool, l_s=l_s,
                      topk=min(TOPK_MAX, (past + l_s) // 4))
    o = jnp.concatenate([o_p, o_s3[:, :l_s].reshape(n_s * l_s, -1)], axis=0)
    h = matmul([o], [p["w_out"]], tm, 1024, name="c_out")
    heads = lambda a, n, l: a.reshape(n, l, KVH_C, DH_C)
    new_p = (heads(k_f[:tp], n_p, l_p), heads(v_f[:tp], n_p, l_p), kwo[:tp, :D_IDX].reshape(n_p, l_p, D_IDX))
    new_s = (heads(k_f[tp:], n_s, l_s), heads(v_f[tp:], n_s, l_s), kwo[tp:, :D_IDX].reshape(n_s, l_s, D_IDX))
    return h, new_p, new_s


TM = 256
BLK = 128
TM_DMA = 64
QBLK = 128
KBLK = 512


def kernel(x_prompt, x_sample, state_shift, state_wkv, state_ssm_re, state_ssm_im, cache_k, cache_v, cache_kidx, page_table, ab_w_in, ab_mu, rwkv_w0, rwkv_w2, rwkv_a0, rwkv_a2, rwkv_g2, rwkv_k_k, rwkv_k_a, rwkv_r_k, rwkv_ln_g, rwkv_ln_b, s5_lam_re, s5_lam_im, s5_log_dt, s5_b_re, s5_b_im, s5_c_re, s5_c_im, s5_d, s5_w_glu, s5_b_glu, ab_w_out, c_w_in, c_kidx_ln_g, c_kidx_ln_b, c_w_out, ln_mix_g, ln_mix_b, ln_ffn_g, ln_ffn_b, moe_w_router, moe_b_router, moe_w_gate, moe_w_up, moe_w_down, moe_ws_gate, moe_ws_up, moe_ws_down):
    w = dict(locals())
    n_p, l_p, _ = x_prompt.shape
    n_s, l_s, _ = x_sample.shape
    geo = (n_p, l_p, n_s, l_s)
    tp = n_p * l_p
    y = jnp.concatenate([x_prompt.reshape(-1, D_MODEL), x_sample.reshape(-1, D_MODEL)], axis=0)
    yb = y.astype(BF16)
    outs = {k: [] for k in ("kp", "vp", "kip", "ks", "vs", "kis", "shp", "shs", "wkvp", "wkvs",
                            "srp", "sip", "srs", "sis")}
    for layer in range(DEPTH):
        li = layer // 2
        if layer % 2 == 0:
            wkv_all = state_wkv.reshape((-1,) + state_wkv.shape[2:])
            h, shift, wkv, ssm_p, ssm_s = ab_layer(yb, geo, ab_params(li, w, l_s), state_shift[li], wkv_all, li * n_s,
                                                   state_ssm_re[li], state_ssm_im[li], tm=TM, blk=BLK)
            for k, v in zip(("shp", "shs", "wkvp", "wkvs", "srp", "sip", "srs", "sis"),
                            (*shift, *wkv, *ssm_p, *ssm_s)):
                outs[k].append(v)
        else:
            h, new_p, new_s = c_layer(yb, geo, c_params(li, w), li, cache_k, cache_v, cache_kidx,
                                      page_table, tm=TM, qblk=QBLK, kblk=KBLK)
            for k, v in zip(("kp", "vp", "kip", "ks", "vs", "kis"), (*new_p, *new_s)):
                outs[k].append(v)
        y, yb = add_ln(y, h, ln_mix_g[layer], ln_mix_b[layer], TM)
        y, yb = moe_layer(y, yb, moe_params(layer, w), ln_ffn_g[layer], ln_ffn_b[layer], tm=TM, tm_dma=TM_DMA)
    st = lambda k: jnp.stack(outs[k])
    return (y[:tp].reshape(n_p, l_p, D_MODEL), y[tp:].reshape(n_s, l_s, D_MODEL),
            st("kp"), st("vp"), st("kip"), st("ks"), st("vs"), st("kis"),
            st("shp"), st("shs"), st("wkvp"), st("wkvs"), st("srp"), st("sip"), st("srs"), st("sis"))
```

```python
import functools
import math

import jax
import jax.numpy as jnp
from jax import lax
from jax.experimental import pallas as pl
from jax.experimental.pallas import tpu as pltpu

F32, BF16, I32 = jnp.float32, jnp.bfloat16, jnp.int32

D_MODEL = 2048
DEPTH = 2
ALPHA = (2.0 * DEPTH) ** 0.25
LN_EPS = 1e-5
C_A = 1024
HS_A = 64
H_A = 16
LORA_W, LORA_A, LORA_G = 64, 64, 160
A_PROJ = 3 * C_A + LORA_W + LORA_A + LORA_G
GN_EPS_A = 64e-5
C_B = 1024
S5_P = 16
S5_GROUPS = 64
S5_N = 64
S5_STATE = S5_GROUPS * S5_N
DH_C = 128
H_C = 16
KVH_C = 4
GROUP_C = 4
H_IDX = 16
D_IDX = 64
TOPK_MAX = 256
ROPE_THETA = 10000.0
N_EXPERTS = 64
TOP_K = 8
N_EXPERT_GROUPS = 8
TOPK_GROUPS = 4
D_EXPERT = 512
ROUTED_SCALE = 2.5

LANES = 128
SUBLANES = 8
VMEM_LIMIT = 56 * 1024 * 1024

LORA_PAD = 512
EXP_M05 = math.exp(-0.5)


def _cparams(sem):
    return pltpu.CompilerParams(dimension_semantics=sem, vmem_limit_bytes=VMEM_LIMIT)


def _dot(a, b):
    return jnp.dot(a, b, preferred_element_type=F32)


def _lane_fold(x, op):
    tiles = [x[:, i * LANES:(i + 1) * LANES] for i in range(x.shape[1] // LANES)]
    return functools.reduce(op, tiles)


def _row_sum(x):
    return jnp.sum(_lane_fold(x, jnp.add), axis=-1, keepdims=True)


def _row_max(x):
    return jnp.max(_lane_fold(x, jnp.maximum), axis=-1, keepdims=True)


def _split3(a):
    a1 = a.astype(BF16)
    r1 = a - a1.astype(F32)
    a2 = r1.astype(BF16)
    a3 = (r1 - a2.astype(F32)).astype(BF16)
    return a1, a2, a3


def _dot_sel(a, sel_bf16):
    a1, a2, a3 = _split3(a)
    return _dot(a1, sel_bf16) + _dot(a2, sel_bf16) + _dot(a3, sel_bf16)


def _dot3(a, b):
    a1 = a.astype(BF16)
    a2 = (a - a1.astype(F32)).astype(BF16)
    b1 = b.astype(BF16)
    b2 = (b - b1.astype(F32)).astype(BF16)
    return _dot(a1, b1) + _dot(a1, b2) + _dot(a2, b1)


def _mm_kernel(*refs, n_lhs):
    o_ref = refs[-1]
    acc = None
    for i in range(n_lhs):
        d = _dot(refs[i][...], refs[n_lhs + i][...])
        acc = d if acc is None else acc + d
    o_ref[...] = acc.astype(o_ref.dtype)


MM_TM = 512


def matmul(xs, ws, tm, tn, out_dtype=F32, name="mm"):
    m = xs[0].shape[0]
    n = ws[0].shape[1]
    tm = MM_TM if m % MM_TM == 0 else tm
    assert m % tm == 0 and n % tn == 0, (m, tm, n, tn)
    in_specs = ([pl.BlockSpec((tm, x.shape[1]), lambda j, i: (i, 0)) for x in xs]
                + [pl.BlockSpec((w.shape[0], tn), lambda j, i: (0, j)) for w in ws])
    return pl.pallas_call(
        functools.partial(_mm_kernel, n_lhs=len(xs)),
        grid=(n // tn, m // tm),
        in_specs=in_specs,
        out_specs=pl.BlockSpec((tm, tn), lambda j, i: (i, j)),
        out_shape=jax.ShapeDtypeStruct((m, n), out_dtype),
        compiler_params=_cparams(("parallel", "parallel")),
        name=name,
    )(*xs, *ws)


def _add_ln_kernel(x_ref, h_ref, g_ref, b_ref, o_ref, ob_ref):
    z = ALPHA * x_ref[...] + h_ref[...]
    inv_d = 1.0 / z.shape[1]
    mu = _row_sum(z) * inv_d
    d = z - mu
    var = _row_sum(d * d) * inv_d
    y = d * lax.rsqrt(var + LN_EPS) * g_ref[...] + b_ref[...]
    o_ref[...] = y
    ob_ref[...] = y.astype(BF16)


def add_ln(x, h, g, b, tm):
    t, d = x.shape
    row = pl.BlockSpec((tm, d), lambda i: (i, 0))
    vec = pl.BlockSpec((1, d), lambda i: (0, 0))
    return pl.pallas_call(
        _add_ln_kernel,
        grid=(t // tm,),
        in_specs=[row, row, vec, vec],
        out_specs=[row, row],
        out_shape=[jax.ShapeDtypeStruct((t, d), F32), jax.ShapeDtypeStruct((t, d), BF16)],
        compiler_params=_cparams(("parallel",)),
        name="add_ln",
    )(x, h, g.reshape(1, d), b.reshape(1, d))


def _rwkv_prep_kernel(rkv_ref, rkvp_ref, lo_ref, lop_ref, mu1_ref, mu2_ref, w0_ref, w2_ref, a0_ref, a2_ref,
                      g2_ref, kk_ref, ka_ref, rk_ref, gsel_ref, gselt_ref,
                      kk_o, wr_o, w_o, b_o, k_o, v_o, kr_o, g_o, bonus_o):
    x = rkv_ref[...]
    m = x + (rkvp_ref[...] - x) * mu1_ref[...]
    r = m[:, :C_A]
    k = m[:, C_A:2 * C_A]
    v = m[:, 2 * C_A:]
    l = lo_ref[...]
    lm = l + (lop_ref[...] - l) * mu2_ref[...]
    lw = lm[:, :LANES]
    la = lm[:, LANES:2 * LANES]
    lg = lm[:, 2 * LANES:]
    w_raw = w0_ref[...] + _dot3(jnp.tanh(lw), w2_ref[...])
    decay = jnp.exp(-EXP_M05 * jax.nn.sigmoid(w_raw))
    a = jax.nn.sigmoid(a0_ref[...] + _dot3(la, a2_ref[...]))
    g = _dot3(jax.nn.sigmoid(lg), g2_ref[...])
    gsel = gsel_ref[...]
    gselt = gselt_ref[...]
    kk = k * kk_ref[...]
    nrm = jnp.sqrt(_dot_sel(kk * kk, gsel))
    inv = 1.0 / jnp.maximum(nrm, 1e-12)
    kk = kk * _dot_sel(inv, gselt)
    k_mod = k * (1.0 + (a - 1.0) * ka_ref[...])
    sb = _dot_sel(r * k_mod * rk_ref[...], gsel)
    bonus_o[...] = _dot_sel(sb, gselt) * v
    g_o[...] = g
    b = kk * a
    kk_o[...] = kk
    br = _dot_sel(_dot_sel(b * r, gsel), gselt)
    wr_o[...] = decay * r - kk * br
    w_o[...] = decay
    b_o[...] = b
    k_o[...] = k_mod
    v_o[...] = v
    kr_o[...] = _dot_sel(_dot_sel(k_mod * r, gsel), gselt)


RWKV_ROWS = 7


def rwkv_prep(rkv, rkv_prev, lora, lora_prev, p, tm):
    t = rkv.shape[0]
    row = lambda w: pl.BlockSpec((tm, w), lambda i: (i, 0))
    full = lambda a: pl.BlockSpec(a.shape, lambda i: (0,) * a.ndim)
    consts = [p["mu_rkv"], p["mu_lora"], p["w0"], p["w2"], p["a0"], p["a2"], p["g2"], p["k_k"], p["k_a"], p["r_k"],
              p["gsel"], p["gselt"]]
    n_out = RWKV_ROWS + 2
    return pl.pallas_call(
        _rwkv_prep_kernel,
        grid=(t // tm,),
        in_specs=[row(3 * C_A), row(3 * C_A), row(LORA_PAD), row(LORA_PAD)] + [full(c) for c in consts],
        out_specs=[row(C_A)] * n_out,
        out_shape=[jax.ShapeDtypeStruct((t, C_A), F32)] * n_out,
        compiler_params=_cparams(("parallel",)),
        name="rwkv_prep",
    )(rkv, rkv_prev, lora, lora_prev, *consts)


TILE_HEADS = 4
TILE_W = TILE_HEADS * HS_A
N_TILES = H_A // TILE_HEADS
LHS_PER_TILE = 3


def _wkv_steps(rows, row_base, n_steps, npar, s_scr, bd_ref, store_o):
    sub = lax.broadcasted_iota(I32, (HS_A, TILE_W), 0)
    lane = lax.broadcasted_iota(I32, (HS_A, TILE_W), 1)
    diag = (lane % HS_A) == sub
    bd = bd_ref[...]
    blk_rows = LHS_PER_TILE * HS_A

    def step(t, carry):
        vec = [[ref[pl.ds(row_base[s] + t, 1), :] for ref in rows[s]] for s in range(npar)]
        results = []
        for s in range(npar):
            kk, wr2, _, _, _, v, _ = vec[s]
            lhs = []
            for p in range(N_TILES):
                ls = slice(p * TILE_W, (p + 1) * TILE_W)
                st = s_scr[s * N_TILES + p]
                q1 = (st * kk[:, ls]).astype(BF16)
                q2 = (st * wr2[:, ls]).astype(BF16)
                ve = jnp.where(diag, v[:, ls], 0.0).astype(BF16)
                lhs += [q1, q2, ve]
            results.append(_dot(jnp.concatenate(lhs, axis=0), bd))
        for s in range(npar):
            _, _, w, b, k, v, kr = vec[s]
            res = results[s]
            o_parts = []
            for p in range(N_TILES):
                ls = slice(p * TILE_W, (p + 1) * TILE_W)
                r0 = p * blk_rows
                z1 = res[r0:r0 + HS_A]
                z2 = res[r0 + HS_A:r0 + 2 * HS_A]
                vb = res[r0 + 2 * HS_A:r0 + 3 * HS_A]
                st = s_scr[s * N_TILES + p]
                s_scr[s * N_TILES + p] = st * w[:, ls] - z1 * b[:, ls] + vb * k[:, ls]
                z2r = jnp.sum(jnp.where(diag, z2, 0.0), axis=0, keepdims=True)
                o_parts.append(z2r + v[:, ls] * kr[:, ls])
            store_o(s, t, jnp.concatenate(o_parts, axis=1))
        return carry

    lax.fori_loop(0, n_steps, step, 0)


def _pack_state(s_ref, q, s_scr, slot):
    for p in range(N_TILES):
        s_scr[slot * N_TILES + p] = jnp.concatenate([s_ref[q, TILE_HEADS * p + h] for h in range(TILE_HEADS)], axis=1)


def _unpack_state(s_scr, slot, s_ref, q):
    for p in range(N_TILES):
        st = s_scr[slot * N_TILES + p]
        for h in range(TILE_HEADS):
            s_ref[q, TILE_HEADS * p + h] = st[:, h * HS_A:(h + 1) * HS_A]


def _wkv_long_kernel(*refs, npar, blk):
    n_in = npar * RWKV_ROWS
    rows = [refs[s * RWKV_ROWS:(s + 1) * RWKV_ROWS] for s in range(npar)]
    bd_ref, o_ref, sout_ref, s_scr = refs[n_in:n_in + 4]
    c = pl.program_id(0)

    @pl.when(c == 0)
    def _():
        s_scr[...] = jnp.zeros(s_scr.shape, F32)

    def store_o(s, t, o_row):
        o_ref[s, pl.ds(t, 1), :] = o_row

    _wkv_steps(rows, [0] * npar, blk, npar, s_scr, bd_ref, store_o)

    @pl.when(c == pl.num_programs(0) - 1)
    def _():
        for s in range(npar):
            _unpack_state(s_scr, s, sout_ref, s)


WKV_NPAR_SHORT = 2
WKV_SPB = 8


def _wkv_short_kernel(*refs, seq_len, spb):
    rows_refs = refs[:RWKV_ROWS]
    s0_ref, bd_ref, o_ref, sout_ref, s_scr = refs[RWKV_ROWS:RWKV_ROWS + 5]
    c = pl.program_id(1)
    npar = WKV_NPAR_SHORT

    def group(gq, carry):
        qs = [gq * npar + s for s in range(npar)]
        for s in range(npar):
            _pack_state(s0_ref, qs[s], s_scr, s)

        def store_o(s, t, o_row):
            o_ref[pl.ds((c * spb + qs[s]) * seq_len + t, 1), :] = o_row

        _wkv_steps([rows_refs] * npar, [(c * spb + q) * seq_len for q in qs], seq_len, npar, s_scr, bd_ref, store_o)
        for s in range(npar):
            _unpack_state(s_scr, s, sout_ref, qs[s])
        return carry

    lax.fori_loop(0, spb // npar, group, 0)


def _head_ones():
    head = jnp.arange(TILE_W) // HS_A
    return (head[:, None] == head[None, :]).astype(BF16)


def wkv_long(rows, *, n_seq, seq_len, blk):
    cps = seq_len // blk
    in_specs = [pl.BlockSpec((blk, C_A), functools.partial(lambda c, s: (s * cps + c, 0), s=s))
                for s in range(n_seq) for _ in range(RWKV_ROWS)]
    return pl.pallas_call(
        functools.partial(_wkv_long_kernel, npar=n_seq, blk=blk),
        grid=(cps,),
        in_specs=in_specs + [pl.BlockSpec((TILE_W, TILE_W), lambda c: (0, 0))],
        out_specs=[pl.BlockSpec((n_seq, blk, C_A), lambda c: (0, c, 0)),
                   pl.BlockSpec((n_seq, H_A, HS_A, HS_A), lambda c: (0, 0, 0, 0))],
        out_shape=[jax.ShapeDtypeStruct((n_seq, seq_len, C_A), F32),
                   jax.ShapeDtypeStruct((n_seq, H_A, HS_A, HS_A), F32)],
        scratch_shapes=[pltpu.VMEM((n_seq * N_TILES, HS_A, TILE_W), F32)],
        compiler_params=_cparams(("arbitrary",)),
        name="wkv_long",
    )(*(list(rows) * n_seq), _head_ones())


def wkv_short(rows, s0, *, row0, n_seq, seq_len, blk, s0_seq0):
    assert row0 % blk == 0 and blk % seq_len == 0
    b0 = row0 // blk
    nsb = blk // seq_len
    spb = min(WKV_SPB, nsb)
    assert n_seq % nsb == 0 and nsb % spb == 0 and spb % WKV_NPAR_SHORT == 0 and s0_seq0 % spb == 0
    sub = nsb // spb
    sspec = lambda off: pl.BlockSpec((spb, H_A, HS_A, HS_A), lambda s, c: (off + s * sub + c, 0, 0, 0))
    return pl.pallas_call(
        functools.partial(_wkv_short_kernel, seq_len=seq_len, spb=spb),
        grid=(n_seq // nsb, sub),
        in_specs=[pl.BlockSpec((blk, C_A), lambda s, c: (b0 + s, 0))] * RWKV_ROWS
        + [sspec(s0_seq0 // spb), pl.BlockSpec((TILE_W, TILE_W), lambda s, c: (0, 0))],
        out_specs=[pl.BlockSpec((blk, C_A), lambda s, c: (s, 0)), sspec(0)],
        out_shape=[jax.ShapeDtypeStruct((n_seq * seq_len, C_A), F32),
                   jax.ShapeDtypeStruct((n_seq, H_A, HS_A, HS_A), F32)],
        scratch_shapes=[pltpu.VMEM((WKV_NPAR_SHORT * N_TILES, HS_A, TILE_W), F32)],
        compiler_params=_cparams(("parallel", "arbitrary")),
        name="wkv_short",
    )(*rows, s0, _head_ones())


def _rwkv_post_kernel(o_in_ref, bonus_ref, g_ref, lng_ref, lnb_ref, gsel_ref, gselt_ref, o_ref):
    o = o_in_ref[...]
    gsel = gsel_ref[...]
    gselt = gselt_ref[...]
    mu = _dot_sel(o, gsel) * (1.0 / HS_A)
    d = o - _dot_sel(mu, gselt)
    var = _dot_sel(d * d, gsel) * (1.0 / HS_A)
    rstd = lax.rsqrt(var + GN_EPS_A)
    y = d * _dot_sel(rstd, gselt) * lng_ref[...] + lnb_ref[...]
    o_ref[...] = ((y + bonus_ref[...]) * g_ref[...]).astype(o_ref.dtype)


def rwkv_post(o_wkv, bonus, g, p, tm):
    t = bonus.shape[0]
    row = pl.BlockSpec((tm, C_A), lambda i: (i, 0))
    full = lambda a: pl.BlockSpec(a.shape, lambda i: (0,) * a.ndim)
    consts = [p["ln_g"], p["ln_b"], p["gsel"], p["gselt"]]
    return pl.pallas_call(
        _rwkv_post_kernel,
        grid=(t // tm,),
        in_specs=[row, row, row] + [full(c) for c in consts],
        out_specs=row,
        out_shape=jax.ShapeDtypeStruct((t, C_A), BF16),
        compiler_params=_cparams(("parallel",)),
        name="rwkv_post",
    )(o_wkv, bonus, g, *consts)


S5_KT = 4
S5_KU = C_B // S5_KT
S5_KH = S5_STATE // S5_KT
S5_LC = 512


def _gelu_tanh(x):
    return 0.5 * x * (1.0 + jnp.tanh(math.sqrt(2.0 / math.pi) * (x + 0.044715 * (x * x * x))))


def _s5_kernel(u_ref, wbr_ref, wbi_ref, lamk_ref, pre_ref, pim_ref, h0r_ref, h0i_ref, wcr_ref, wci_ref,
               dsk_ref, wglu_ref, bglu_ref, ob_ref, hr_out, hi_out, hre, him, car_re, car_im, y_scr,
               *, seq_len, blk):
    long_mode = seq_len >= blk
    period = SUBLANES if long_mode else seq_len
    u = u_ref[...]
    ub = u.astype(BF16)
    for kt in range(S5_KT):
        uk = ub[:, kt * S5_KU:(kt + 1) * S5_KU]
        hre[:, kt * S5_KH:(kt + 1) * S5_KH] = _dot(uk, wbr_ref[kt])
        him[:, kt * S5_KH:(kt + 1) * S5_KH] = _dot(uk, wbi_ref[kt])

    if long_mode:
        @pl.when(pl.program_id(1) == 0)
        def _():
            car_re[...] = jnp.zeros(car_re.shape, F32)
            car_im[...] = jnp.zeros(car_im.shape, F32)

    rowi = lax.broadcasted_iota(I32, (SUBLANES, 1), 0) % period
    for lc in range(S5_STATE // S5_LC):
        ls = slice(lc * S5_LC, (lc + 1) * S5_LC)
        lam = [(lamk_ref[2 * i:2 * i + 1, ls], lamk_ref[2 * i + 1:2 * i + 2, ls]) for i in range(3)]
        pre = pre_ref[:, ls]
        pim = pim_ref[:, ls]

        def tile(i, carry, ls=ls, lam=lam, pre=pre, pim=pim):
            r0 = pl.multiple_of(i * SUBLANES, SUBLANES)
            xr = hre[pl.ds(r0, SUBLANES), ls]
            xi = him[pl.ds(r0, SUBLANES), ls]
            for lvl, sh in enumerate((1, 2, 4)):
                if sh >= period:
                    break
                lr, li = lam[lvl]
                keep = rowi >= sh
                sr = jnp.where(keep, pltpu.roll(xr, sh, 0), 0.0)
                si = jnp.where(keep, pltpu.roll(xi, sh, 0), 0.0)
                xr, xi = xr + lr * sr - li * si, xi + lr * si + li * sr
            if long_mode:
                hr, hi = carry
            else:
                hr = h0r_ref[pl.ds(r0, SUBLANES), ls]
                hi = h0i_ref[pl.ds(r0, SUBLANES), ls]
            xr, xi = xr + pre * hr - pim * hi, xi + pre * hi + pim * hr
            hre[pl.ds(r0, SUBLANES), ls] = xr
            him[pl.ds(r0, SUBLANES), ls] = xi
            if long_mode:
                last_r = jnp.broadcast_to(xr[SUBLANES - 1:SUBLANES, :], xr.shape)
                last_i = jnp.broadcast_to(xi[SUBLANES - 1:SUBLANES, :], xi.shape)
                return (last_r, last_i)
            return carry

        if long_mode:
            init = (car_re[:, ls], car_im[:, ls])
            fin = lax.fori_loop(0, blk // SUBLANES, tile, init)
            car_re[:, ls] = fin[0]
            car_im[:, ls] = fin[1]
        else:
            lax.fori_loop(0, blk // SUBLANES, tile, 0)

    if long_mode:
        hr_out[0] = car_re[...]
        hi_out[0] = car_im[...]
    else:
        hr_out[...] = hre[...]
        hi_out[...] = him[...]

    for kt in range(S5_KT):
        hs = slice(kt * S5_KH, (kt + 1) * S5_KH)
        y_scr[:, kt * S5_KU:(kt + 1) * S5_KU] = (_dot(hre[:, hs].astype(BF16), wcr_ref[kt])
                                                 - _dot(him[:, hs].astype(BF16), wci_ref[kt]))
    y = y_scr[...] + dsk_ref[...] * u
    z = _gelu_tanh(y)
    gate = jax.nn.sigmoid(_dot(z.astype(BF16), wglu_ref[...]) + bglu_ref[...])
    ob_ref[...] = (z * gate).astype(ob_ref.dtype)


def s5_mix(u, p, h0r_rows, h0i_rows, *, row0, n_seq, seq_len, blk):
    long_mode = seq_len >= blk
    assert row0 % blk == 0
    b0 = row0 // blk
    t_out = n_seq * seq_len
    if long_mode:
        cps = seq_len // blk
        grid = (n_seq, cps)
        umap = lambda s, c: (b0 + s * cps + c, 0)
        omap = lambda s, c: (s * cps + c, 0)
        hspec = pl.BlockSpec((1, SUBLANES, S5_STATE), lambda s, c: (s, 0, 0))
        hshape = jax.ShapeDtypeStruct((n_seq, SUBLANES, S5_STATE), F32)
        h0spec = pl.BlockSpec((SUBLANES, S5_STATE), lambda s, c: (0, 0))
    else:
        assert t_out % blk == 0
        grid = (t_out // blk, 1)
        umap = lambda s, c: (b0 + s, 0)
        omap = lambda s, c: (s, 0)
        hspec = pl.BlockSpec((blk, S5_STATE), omap)
        hshape = jax.ShapeDtypeStruct((t_out, S5_STATE), F32)
        h0spec = pl.BlockSpec((blk, S5_STATE), omap)
    full = lambda a: pl.BlockSpec(a.shape, lambda s, c: (0,) * a.ndim)
    pre, pim = (p["pow_re8"], p["pow_im8"]) if long_mode else (p["pow_re_s"], p["pow_im_s"])
    args = [u, p["wb_re"], p["wb_im"], p["lamk"], pre, pim, h0r_rows, h0i_rows, p["wc_re"], p["wc_im"],
            p["d_skip"], p["w_glu"], p["b_glu"]]
    in_specs = [pl.BlockSpec((blk, C_B), umap)] + [full(a) for a in args[1:6]] + [h0spec, h0spec] \
        + [full(a) for a in args[8:]]
    return pl.pallas_call(
        functools.partial(_s5_kernel, seq_len=seq_len, blk=blk),
        grid=grid,
        in_specs=in_specs,
        out_specs=[pl.BlockSpec((blk, C_B), omap), hspec, hspec],
        out_shape=[jax.ShapeDtypeStruct((t_out, C_B), BF16), hshape, hshape],
        scratch_shapes=[pltpu.VMEM((blk, S5_STATE), F32), pltpu.VMEM((blk, S5_STATE), F32),
                        pltpu.VMEM((SUBLANES, S5_STATE), F32), pltpu.VMEM((SUBLANES, S5_STATE), F32),
                        pltpu.VMEM((blk, C_B), F32)],
        compiler_params=_cparams(("parallel", "arbitrary")),
        name="s5_long" if long_mode else "s5_short",
    )(*args)


def _pad_cols(a, n):
    return jnp.pad(a, ((0, 0), (0, n - a.shape[1])))


def _pad_rows(a, n):
    return jnp.pad(a, ((0, n - a.shape[0]), (0, 0)))


def _lora_layout(a):
    o = 3 * C_A
    return jnp.concatenate([_pad_cols(a[:, o:o + LORA_W], LANES),
                            _pad_cols(a[:, o + LORA_W:o + LORA_W + LORA_A], LANES),
                            _pad_cols(a[:, o + LORA_W + LORA_A:A_PROJ], 2 * LANES)], axis=1)


def _lora_unlayout(a):
    return jnp.concatenate([a[:, :LORA_W], a[:, LANES:LANES + LORA_A], a[:, 2 * LANES:2 * LANES + LORA_G]], axis=1)


def _cpow_table(lbr, lbi, n):
    res_r, res_i = [lbr], [lbi]
    for _ in range(n - 1):
        pr, pi = res_r[-1], res_i[-1]
        res_r.append(pr * lbr - pi * lbi)
        res_i.append(pr * lbi + pi * lbr)
    return (jnp.stack([x.reshape(-1) for x in res_r]), jnp.stack([x.reshape(-1) for x in res_i]))


def ab_params(li, w, sample_len):
    p = {}
    w_in = w["ab_w_in"][li]
    p["w_rkv"] = w_in[:, :3 * C_A].astype(BF16)
    p["w_lora"] = _lora_layout(w_in).astype(BF16)
    p["w_u"] = w_in[:, A_PROJ:].astype(BF16)
    mu = w["ab_mu"][li][None, :]
    p["mu_rkv"] = mu[:, :3 * C_A]
    p["mu_lora"] = _lora_layout(mu)
    row = lambda a: a.reshape(1, -1)
    p["w0"] = row(w["rwkv_w0"][li])
    p["w2"] = _pad_rows(w["rwkv_w2"][li], LANES)
    p["a0"] = row(w["rwkv_a0"][li])
    p["a2"] = _pad_rows(w["rwkv_a2"][li], LANES)
    p["g2"] = _pad_rows(w["rwkv_g2"][li], 2 * LANES)
    p["k_k"] = row(w["rwkv_k_k"][li])
    p["k_a"] = row(w["rwkv_k_a"][li])
    p["r_k"] = row(w["rwkv_r_k"][li])
    p["ln_g"] = row(w["rwkv_ln_g"][li])
    p["ln_b"] = row(w["rwkv_ln_b"][li])
    head_of_col = jnp.arange(C_A) // HS_A
    gsel = (head_of_col[:, None] == jnp.arange(LANES)[None, :])
    p["gsel"] = gsel.astype(BF16)
    p["gselt"] = gsel.T.astype(BF16)
    lr = jnp.minimum(w["s5_lam_re"][li], -1e-4)
    lim = w["s5_lam_im"][li]
    dt = jnp.exp(w["s5_log_dt"][li])[:, None]
    mag = jnp.exp(lr * dt)
    lbr, lbi = mag * jnp.cos(lim * dt), mag * jnp.sin(lim * dt)
    den = lr * lr + lim * lim
    pr, pim = lbr - 1.0, lbi
    fr = (pr * lr + pim * lim) / den
    fi = (pim * lr - pr * lim) / den
    br, bi = w["s5_b_re"][li], w["s5_b_im"][li]
    bbr = fr[..., None] * br - fi[..., None] * bi
    bbi = fr[..., None] * bi + fi[..., None] * br
    gpc = S5_GROUPS // S5_KT
    eye = jnp.eye(gpc, dtype=F32)

    def blockdiag_in(bb):
        b4 = bb.reshape(S5_KT, gpc, S5_N, S5_P)
        return jnp.einsum("kgnp,gh->kgphn", b4, eye).reshape(S5_KT, S5_KU, S5_KH).astype(BF16)

    def blockdiag_out(cc):
        c4 = cc.reshape(S5_KT, gpc, S5_P, S5_N)
        return jnp.einsum("kgpn,gh->kgnhp", c4, eye).reshape(S5_KT, S5_KH, S5_KU).astype(BF16)

    p["wb_re"], p["wb_im"] = blockdiag_in(bbr), blockdiag_in(bbi)
    p["wc_re"], p["wc_im"] = blockdiag_out(w["s5_c_re"][li]), blockdiag_out(w["s5_c_im"][li])
    pw_r, pw_i = _cpow_table(lbr, lbi, SUBLANES)
    p["pow_re8"], p["pow_im8"] = pw_r, pw_i
    reps = SUBLANES // sample_len
    p["pow_re_s"] = jnp.tile(pw_r[:sample_len], (reps, 1))
    p["pow_im_s"] = jnp.tile(pw_i[:sample_len], (reps, 1))
    p["lamk"] = jnp.stack([pw_r[0], pw_i[0], pw_r[1], pw_i[1], pw_r[3], pw_i[3]])
    p["d_skip"] = row(w["s5_d"][li])
    p["w_glu"] = w["s5_w_glu"][li].astype(BF16)
    p["b_glu"] = row(w["s5_b_glu"][li])
    w_out = w["ab_w_out"][li]
    p["w_out_a"] = w_out[:C_A].astype(BF16)
    p["w_out_b"] = w_out[C_A:].astype(BF16)
    return p


def _shift_rows(a, first_sample, geo):
    n_p, l_p, n_s, l_s = geo
    wd = a.shape[1]
    ap = a[:n_p * l_p].reshape(n_p, l_p, wd)
    prev_p = jnp.concatenate([jnp.zeros((n_p, 1, wd), a.dtype), ap[:, :-1]], axis=1)
    as_ = a[n_p * l_p:].reshape(n_s, l_s, wd)
    prev_s = jnp.concatenate([first_sample[:, None, :], as_[:, :-1]], axis=1)
    return jnp.concatenate([prev_p.reshape(-1, wd), prev_s.reshape(-1, wd)], axis=0)


def _last_rows(a, geo):
    n_p, l_p, n_s, l_s = geo
    wd = a.shape[1]
    return (a[:n_p * l_p].reshape(n_p, l_p, wd)[:, -1], a[n_p * l_p:].reshape(n_s, l_s, wd)[:, -1])


def ab_layer(xb, geo, p, state_shift, state_wkv, wkv_seq0, state_re, state_im, *, tm, blk):
    n_p, l_p, n_s, l_s = geo
    tp, ts = n_p * l_p, n_s * l_s
    rkv = matmul([xb], [p["w_rkv"]], tm, 1024, name="ab_in_rkv")
    lora = matmul([xb], [p["w_lora"]], tm, LORA_PAD, name="ab_in_lora")
    u = matmul([xb], [p["w_u"]], tm, C_B, name="ab_in_u")
    rkv_prev = _shift_rows(rkv, state_shift[:, :3 * C_A], geo)
    lora_prev = _shift_rows(lora, _lora_layout(state_shift), geo)
    prep = rwkv_prep(rkv, rkv_prev, lora, lora_prev, p, tm)
    rows, g, bonus = prep[:RWKV_ROWS], prep[RWKV_ROWS], prep[RWKV_ROWS + 1]
    o_p, wkv_p = wkv_long(rows, n_seq=n_p, seq_len=l_p, blk=blk)
    o_s, wkv_s = wkv_short(rows, state_wkv, row0=tp, n_seq=n_s, seq_len=l_s, blk=blk, s0_seq0=wkv_seq0)
    o_wkv = jnp.concatenate([o_p.reshape(tp, C_A), o_s], axis=0)
    o_a = rwkv_post(o_wkv, bonus, g, p, tm)
    dummy = jnp.zeros((SUBLANES, S5_STATE), F32)
    ob_p, hr_p, hi_p = s5_mix(u, p, dummy, dummy, row0=0, n_seq=n_p, seq_len=l_p, blk=blk)
    h0r = jnp.repeat(state_re.reshape(n_s, S5_STATE), l_s, axis=0)
    h0i = jnp.repeat(state_im.reshape(n_s, S5_STATE), l_s, axis=0)
    ob_s, hr_s, hi_s = s5_mix(u, p, h0r, h0i, row0=tp, n_seq=n_s, seq_len=l_s, blk=blk)
    o_b = jnp.concatenate([ob_p, ob_s], axis=0)
    h = matmul([o_a, o_b], [p["w_out_a"], p["w_out_b"]], tm, 1024, name="ab_out")
    rkv_lp, rkv_ls = _last_rows(rkv, geo)
    lo_lp, lo_ls = _last_rows(lora, geo)
    shift_p = jnp.concatenate([rkv_lp, _lora_unlayout(lo_lp)], axis=1)
    shift_s = jnp.concatenate([rkv_ls, _lora_unlayout(lo_ls)], axis=1)
    grp = lambda a, n: a.reshape(n, S5_GROUPS, S5_N)
    ssm_p = (grp(hr_p[:, SUBLANES - 1], n_p), grp(hi_p[:, SUBLANES - 1], n_p))
    ssm_s = (grp(hr_s[l_s - 1::l_s], n_s), grp(hi_s[l_s - 1::l_s], n_s))
    return h, (shift_p, shift_s), (wkv_p, wkv_s), ssm_p, ssm_s


def _router_kernel(x_ref, wr_ref, br_ref, ltri_ref, eidx_o, gate_o, rank_o, cnt_o, carry):
    i = pl.program_id(0)

    @pl.when(i == 0)
    def _():
        carry[...] = jnp.zeros(carry.shape, F32)

    tm = x_ref.shape[0]
    neg = -jnp.inf
    lane_i = lax.broadcasted_iota(I32, (tm, LANES), 1)
    lane = lane_i.astype(F32)
    per_group = N_EXPERTS // N_EXPERT_GROUPS
    grp = (lane_i // per_group).astype(F32)
    scores = jax.nn.sigmoid(_dot3(x_ref[...], wr_ref[...]))
    biased = jnp.where(lane_i < N_EXPERTS, scores + br_ref[...], neg)
    rmax = lambda a: jnp.max(a, axis=-1, keepdims=True)
    rmin = lambda a: jnp.min(a, axis=-1, keepdims=True)
    rsum = lambda a: jnp.sum(a, axis=-1, keepdims=True)

    gsc = jnp.full((tm, LANES), neg, F32)
    for g in range(N_EXPERT_GROUPS):
        xg = jnp.where(grp == g, biased, neg)
        m1 = rmax(xg)
        twice = rsum(jnp.where(xg == m1, 1.0, 0.0)) >= 2.0
        m2 = jnp.where(twice, m1, rmax(jnp.where(xg < m1, xg, neg)))
        gsc = jnp.where(lane == g, m1 + m2, gsc)
    keep = jnp.zeros((tm, LANES), jnp.bool_)
    for _ in range(TOPK_GROUPS):
        m = rmax(gsc)
        gi = rmin(jnp.where(gsc == m, lane, float(LANES)))
        keep = keep | (grp == gi)
        gsc = jnp.where(lane == gi, neg, gsc)
    cur = jnp.where(keep, biased, neg)

    sel = jnp.zeros((tm, LANES), F32)
    es, ss = [], []
    for _ in range(TOP_K):
        m = rmax(cur)
        ik = rmin(jnp.where(cur == m, lane, float(LANES)))
        hit = lane == ik
        ss.append(rsum(jnp.where(hit, scores, 0.0)))
        es.append(ik)
        cur = jnp.where(hit, neg, cur)
        sel = jnp.where(hit, 1.0, sel)
    tot = ss[0]
    for k in range(1, TOP_K):
        tot = tot + ss[k]

    prefix = _dot(ltri_ref[...], sel.astype(BF16)) + carry[0:1, :]
    eidx = jnp.zeros((tm, LANES), F32)
    gate = jnp.zeros((tm, LANES), F32)
    rank = jnp.zeros((tm, LANES), F32)
    for k in range(TOP_K):
        slot = lane == k
        eidx = jnp.where(slot, es[k], eidx)
        gate = jnp.where(slot, ss[k] / tot * ROUTED_SCALE, gate)
        rank = jnp.where(slot, rsum(jnp.where(lane == es[k], prefix, 0.0)), rank)
    eidx_o[...] = eidx.astype(I32)
    gate_o[...] = gate
    rank_o[...] = rank.astype(I32)
    carry[...] = carry[...] + jnp.sum(sel, axis=0, keepdims=True)
    cnt_o[...] = carry[...]


def moe_router(x, w_router_pad, b_router_pad, tm):
    t, d = x.shape
    ltri = (jnp.arange(tm)[:, None] > jnp.arange(tm)[None, :]).astype(BF16)
    row = pl.BlockSpec((tm, LANES), lambda i: (i, 0))
    out = jax.ShapeDtypeStruct((t, LANES), I32)
    return pl.pallas_call(
        _router_kernel,
        grid=(t // tm,),
        in_specs=[pl.BlockSpec((tm, d), lambda i: (i, 0)), pl.BlockSpec((d, LANES), lambda i: (0, 0)),
                  pl.BlockSpec((1, LANES), lambda i: (0, 0)), pl.BlockSpec((tm, tm), lambda i: (0, 0))],
        out_specs=[row, row, row, pl.BlockSpec((SUBLANES, LANES), lambda i: (0, 0))],
        out_shape=[out, jax.ShapeDtypeStruct((t, LANES), F32), out,
                   jax.ShapeDtypeStruct((SUBLANES, LANES), F32)],
        scratch_shapes=[pltpu.VMEM((SUBLANES, LANES), F32)],
        compiler_params=_cparams(("arbitrary",)),
        name="moe_router",
    )(x, w_router_pad, b_router_pad, ltri)


MOE_BM = 256


def _gather_rows_start(src_hbm, idx_ref, idx0, dst_buf, n_rows, sem):
    for j in range(n_rows):
        pltpu.make_async_copy(src_hbm.at[pl.ds(idx_ref[idx0 + j], 1), :], dst_buf.at[pl.ds(j, 1), :], sem).start()


def _gather_rows_wait(src_hbm, dst_buf, n_rows, sem):
    pltpu.make_async_copy(src_hbm.at[pl.ds(0, n_rows), :], dst_buf, sem).wait()


def _expert_kernel(blk_e_ref, nused_ref, tok_ref, x_hbm, wg_ref, wu_ref, wd_ref, ys_ref, xbuf, sems, wgb, wub, wdb):
    i = pl.program_id(0)
    n_used = nused_ref[0]

    @pl.when(i == 0)
    def _():
        _gather_rows_start(x_hbm, tok_ref, 0, xbuf.at[0], MOE_BM, sems.at[0])

    @pl.when(i < n_used)
    def _():
        e = blk_e_ref[i]
        prev = blk_e_ref[jnp.maximum(i - 1, 0)]

        @pl.when((i == 0) | (e != prev))
        def _():
            wgb[...] = wg_ref[0].astype(BF16)
            wub[...] = wu_ref[0].astype(BF16)
            wdb[...] = wd_ref[0].astype(BF16)

        cur = i % 2
        nxt = 1 - cur
        _gather_rows_wait(x_hbm, xbuf.at[cur], MOE_BM, sems.at[cur])
        nxt_row0 = jnp.minimum(i + 1, n_used - 1) * MOE_BM
        _gather_rows_start(x_hbm, tok_ref, nxt_row0, xbuf.at[nxt], MOE_BM, sems.at[nxt])
        x = xbuf[cur].astype(BF16)
        hg = _dot(x, wgb[...])
        hu = _dot(x, wub[...])
        h = hg * jax.nn.sigmoid(hg) * hu
        ys_ref[...] = _dot(h.astype(BF16), wdb[...])

        @pl.when(i == n_used - 1)
        def _():
            _gather_rows_wait(x_hbm, xbuf.at[nxt], MOE_BM, sems.at[nxt])

    @pl.when(i >= nused_ref[0])
    def _():
        ys_ref[...] = jnp.zeros(ys_ref.shape, F32)


def moe_experts(x, blk_e, n_used, tok_of_row, w_gate, w_up, w_down):
    d = x.shape[1]
    n_rows = tok_of_row.shape[0]
    n_blocks = n_rows // MOE_BM
    de = w_gate.shape[2]
    wmap = lambda i, be, nu, tk: (be[jnp.minimum(i, nu[0] - 1)], 0, 0)
    return pl.pallas_call(
        _expert_kernel,
        grid_spec=pltpu.PrefetchScalarGridSpec(
            num_scalar_prefetch=3,
            grid=(n_blocks,),
            in_specs=[pl.BlockSpec(memory_space=pl.ANY),
                      pl.BlockSpec((1, d, de), wmap), pl.BlockSpec((1, d, de), wmap), pl.BlockSpec((1, de, d), wmap)],
            out_specs=pl.BlockSpec((MOE_BM, d), lambda i, be, nu, tk: (i, 0)),
            scratch_shapes=[pltpu.VMEM((2, MOE_BM, d), F32), pltpu.SemaphoreType.DMA((2,)),
                            pltpu.VMEM((d, de), BF16), pltpu.VMEM((d, de), BF16), pltpu.VMEM((de, d), BF16)],
        ),
        out_shape=jax.ShapeDtypeStruct((n_rows, d), F32),
        compiler_params=_cparams(("arbitrary",)),
        name="moe_experts",
    )(blk_e, n_used, tok_of_row, x, w_gate, w_up, w_down)


def _shared_ffn_kernel(x_ref, wg_ref, wu_ref, wd_ref, o_ref):
    x = x_ref[...]
    hg = _dot(x, wg_ref[...])
    hu = _dot(x, wu_ref[...])
    h = hg * jax.nn.sigmoid(hg) * hu
    o_ref[...] = _dot(h.astype(BF16), wd_ref[...])


def shared_ffn(xb, wg, wu, wd, tm):
    t, d = xb.shape
    full = lambda a: pl.BlockSpec(a.shape, lambda i: (0, 0))
    row = pl.BlockSpec((tm, d), lambda i: (i, 0))
    return pl.pallas_call(
        _shared_ffn_kernel,
        grid=(t // tm,),
        in_specs=[row, full(wg), full(wu), full(wd)],
        out_specs=row,
        out_shape=jax.ShapeDtypeStruct((t, d), F32),
        compiler_params=_cparams(("parallel",)),
        name="shared_ffn",
    )(xb, wg, wu, wd)


def _combine_kernel(dest_ref, ys_hbm, gate_ref, x_ref, sh_ref, g_ref, b_ref, o_ref, ob_ref, buf, sems, *, tm):
    i = pl.program_id(0)
    n_asg = tm * TOP_K

    @pl.when(i == 0)
    def _():
        _gather_rows_start(ys_hbm, dest_ref, 0, buf.at[0], n_asg, sems.at[0])

    @pl.when(i + 1 < pl.num_programs(0))
    def _():
        nxt = (i + 1) % 2
        _gather_rows_start(ys_hbm, dest_ref, (i + 1) * n_asg, buf.at[nxt], n_asg, sems.at[nxt])

    cur = i % 2
    _gather_rows_wait(ys_hbm, buf.at[cur], n_asg, sems.at[cur])
    gate = gate_ref[...]
    f = sh_ref[...]
    for k in range(TOP_K):
        f = f + gate[:, k:k + 1] * buf[cur, k * tm:(k + 1) * tm, :]
    z = ALPHA * x_ref[...] + f
    inv_d = 1.0 / z.shape[1]
    mu = _row_sum(z) * inv_d
    dlt = z - mu
    var = _row_sum(dlt * dlt) * inv_d
    y = dlt * lax.rsqrt(var + LN_EPS) * g_ref[...] + b_ref[...]
    o_ref[...] = y
    ob_ref[...] = y.astype(BF16)


def moe_combine(ys, dest_flat, gates, x, shared, g, b, tm):
    t, d = x.shape
    row = pl.BlockSpec((tm, d), lambda i, dr: (i, 0))
    vec = pl.BlockSpec((1, d), lambda i, dr: (0, 0))
    return pl.pallas_call(
        functools.partial(_combine_kernel, tm=tm),
        grid_spec=pltpu.PrefetchScalarGridSpec(
            num_scalar_prefetch=1,
            grid=(t // tm,),
            in_specs=[pl.BlockSpec(memory_space=pl.ANY), pl.BlockSpec((tm, LANES), lambda i, dr: (i, 0)),
                      row, row, vec, vec],
            out_specs=[row, row],
            scratch_shapes=[pltpu.VMEM((2, TOP_K * tm, d), F32), pltpu.SemaphoreType.DMA((2,))],
        ),
        out_shape=[jax.ShapeDtypeStruct((t, d), F32), jax.ShapeDtypeStruct((t, d), BF16)],
        compiler_params=_cparams(("arbitrary",)),
        name="moe_combine",
    )(dest_flat, ys, gates, x, shared, g.reshape(1, d), b.reshape(1, d))


def moe_params(layer, w):
    p = {}
    p["w_router"] = _pad_cols(w["moe_w_router"][layer], LANES)
    p["b_router"] = _pad_cols(w["moe_b_router"][layer][None, :], LANES)
    stack = lambda a: a.reshape((-1,) + a.shape[2:])
    p["w_gate"], p["w_up"], p["w_down"] = stack(w["moe_w_gate"]), stack(w["moe_w_up"]), stack(w["moe_w_down"])
    p["e0"] = layer * N_EXPERTS
    p["ws_gate"] = w["moe_ws_gate"][layer].astype(BF16)
    p["ws_up"] = w["moe_ws_up"][layer].astype(BF16)
    p["ws_down"] = w["moe_ws_down"][layer].astype(BF16)
    return p


def moe_layer(y, yb, p, ln_g, ln_b, *, tm, tm_dma):
    t = y.shape[0]
    eidx, gates, rank, cnt = moe_router(y, p["w_router"], p["b_router"], tm)
    counts = cnt[0, :N_EXPERTS].astype(I32)
    padded = (counts + MOE_BM - 1) // MOE_BM * MOE_BM
    pad_end = jnp.cumsum(padded)
    pad_start = pad_end - padded
    e8 = eidx[:, :TOP_K]
    dest = pad_start[e8] + rank[:, :TOP_K]
    n_blocks = -(-(t * TOP_K + N_EXPERTS * (MOE_BM - 1)) // MOE_BM)
    n_rows = n_blocks * MOE_BM
    blk_start = jnp.arange(n_blocks, dtype=I32) * MOE_BM
    blk_e = jnp.sum((pad_end[None, :] <= blk_start[:, None]).astype(I32), axis=1)
    blk_e = jnp.minimum(blk_e, N_EXPERTS - 1) + p["e0"]
    n_used = (pad_end[-1:] // MOE_BM).astype(I32)
    tok = jnp.broadcast_to(jnp.arange(t, dtype=I32)[:, None], (t, TOP_K))
    tok_of_row = jnp.zeros((n_rows,), I32).at[dest.reshape(-1)].set(tok.reshape(-1))
    ys = moe_experts(y, blk_e, n_used, tok_of_row, p["w_gate"], p["w_up"], p["w_down"])
    shared = shared_ffn(yb, p["ws_gate"], p["ws_up"], p["ws_down"], tm)
    dest_tiles = dest.reshape(t // tm_dma, tm_dma, TOP_K).transpose(0, 2, 1).reshape(-1)
    return moe_combine(ys, dest_tiles, gates, y, shared, ln_g, ln_b, tm_dma)


C_QKV = (H_C + 2 * KVH_C) * DH_C
C_QI = H_IDX * D_IDX
IDX_SCALE = (H_IDX ** -0.5) * (D_IDX ** -0.5)


def _rot_half64(x, lane):
    return jnp.where(lane % D_IDX < D_IDX // 2, pltpu.roll(x, LANES - D_IDX // 2, 1), pltpu.roll(x, D_IDX // 2, 1))


def _c_post_kernel(qkv_ref, qi_ref, kw_ref, c128_ref, s128_ref, c64_ref, s64_ref, lng_ref, lnb_ref,
                   qb_o, kf_o, kb_o, vb_o, qib_o, kwo_o):
    tm = qkv_ref.shape[0]
    c128, s128 = c128_ref[...], s128_ref[...]
    c64, s64 = c64_ref[...], s64_ref[...]
    lane = lax.broadcasted_iota(I32, (tm, LANES), 1)
    for h in range(H_C + KVH_C):
        x = qkv_ref[:, h * DH_C:(h + 1) * DH_C]
        y = x * c128 + pltpu.roll(x, DH_C // 2, 1) * s128
        if h < H_C:
            qb_o[:, h * DH_C:(h + 1) * DH_C] = (y * (DH_C ** -0.5)).astype(BF16)
        else:
            j = h - H_C
            kf_o[:, j * DH_C:(j + 1) * DH_C] = y
            kb_o[:, j * DH_C:(j + 1) * DH_C] = y.astype(BF16)
    vb_o[...] = qkv_ref[:, (H_C + KVH_C) * DH_C:].astype(BF16)
    for j in range(C_QI // LANES):
        x = qi_ref[:, j * LANES:(j + 1) * LANES]
        qib_o[:, j * LANES:(j + 1) * LANES] = (x * c64 + _rot_half64(x, lane) * s64).astype(BF16)
    x = kw_ref[...]
    is_k = lane < D_IDX
    mu = jnp.sum(jnp.where(is_k, x, 0.0), axis=-1, keepdims=True) * (1.0 / D_IDX)
    d = jnp.where(is_k, x - mu, 0.0)
    var = jnp.sum(d * d, axis=-1, keepdims=True) * (1.0 / D_IDX)
    y = d * lax.rsqrt(var + LN_EPS) * lng_ref[...] + lnb_ref[...]
    y = y * c64 + _rot_half64(y, lane) * s64
    kwo_o[...] = jnp.where(is_k, y, x * IDX_SCALE)


def c_post(qkv, qi, kw, tabs, lng, lnb, tm):
    t = qkv.shape[0]
    row = lambda wd: pl.BlockSpec((tm, wd), lambda i: (i, 0))
    vec = pl.BlockSpec((1, LANES), lambda i: (0, 0))
    sh = lambda wd, dt: jax.ShapeDtypeStruct((t, wd), dt)
    return pl.pallas_call(
        _c_post_kernel,
        grid=(t // tm,),
        in_specs=[row(C_QKV), row(C_QI), row(LANES), row(LANES), row(LANES), row(LANES), row(LANES), vec, vec],
        out_specs=[row(H_C * DH_C), row(KVH_C * DH_C), row(KVH_C * DH_C), row(KVH_C * DH_C), row(C_QI), row(LANES)],
        out_shape=[sh(H_C * DH_C, BF16), sh(KVH_C * DH_C, F32), sh(KVH_C * DH_C, BF16), sh(KVH_C * DH_C, BF16),
                   sh(C_QI, BF16), sh(LANES, F32)],
        compiler_params=_cparams(("parallel",)),
        name="c_post",
    )(qkv, qi, kw, *tabs, lng, lnb)


def _rope_tables(pos):
    def tab(d):
        inv = ROPE_THETA ** (-jnp.arange(0, d, 2, dtype=F32) / d)
        ang = pos.astype(F32)[:, None] * inv[None, :]
        c, s = jnp.cos(ang), jnp.sin(ang)
        return jnp.concatenate([c, c], axis=1), jnp.concatenate([-s, s], axis=1)

    c128, s128 = tab(DH_C)
    c64, s64 = tab(D_IDX)
    return c128, s128, jnp.tile(c64, (1, 2)), jnp.tile(s64, (1, 2))


def _index_heads(qi, wi, kb):
    acc = None
    for h in range(H_IDX):
        d = lax.dot_general(qi[:, h * D_IDX:(h + 1) * D_IDX], kb, (((1,), (1,)), ((), ())),
                            preferred_element_type=F32)
        term = wi[:, D_IDX + h:D_IDX + h + 1] * jnp.maximum(d, 0.0)
        acc = term if acc is None else acc + term
    return acc + 0.0


SROWS = 16


def _topk_mask(sc_ref, utri_ref, o_ref, key_scr, *, topk, chunk):
    rows, s_len = sc_ref.shape
    bits = lax.bitcast_convert_type(sc_ref[...], I32)
    key_scr[...] = bits ^ ((bits >> 31) & jnp.int32(0x7FFFFFFF))
    sign = jnp.int32(-2 ** 31)
    kf = jnp.float32(topk)

    def bisect(it, t_u):
        cand = t_u | (jnp.int32(1) << (31 - it))
        cnt = _row_sum(jnp.where(key_scr[...] >= (cand ^ sign), 1.0, 0.0))
        return jnp.where(cnt >= kf, cand, t_u)

    t_u = lax.fori_loop(0, 32, bisect, jnp.zeros((rows, 1), I32))
    thr = t_u ^ sign
    need = kf - _row_sum(jnp.where(key_scr[...] > thr, 1.0, 0.0))
    utri = utri_ref[...]
    run = jnp.zeros((rows, 1), F32)
    for c in range(s_len // chunk):
        cs = slice(c * chunk, (c + 1) * chunk)
        keyc = key_scr[:, cs]
        eqc = jnp.where(keyc == thr, 1.0, 0.0)
        before = _dot(eqc.astype(BF16), utri) + run
        pick = (keyc > thr) | ((keyc == thr) & (before < need))
        o_ref[:, cs] = jnp.where(pick & (sc_ref[:, cs] > -jnp.inf), 1.0, 0.0).astype(o_ref.dtype)
        run = run + _row_sum(eqc)
    if o_ref.shape[1] > s_len:
        o_ref[:, s_len:] = jnp.zeros((rows, o_ref.shape[1] - s_len), o_ref.dtype)


def _strict_upper(chunk):
    return (jnp.arange(chunk)[:, None] < jnp.arange(chunk)[None, :]).astype(BF16)


SELECT_CLASSES = 4


def _index_select_kernel(qi_ref, wi_ref, keys_ref, utri_ref, o_ref, sc_scr, key_scr, *, q0, topk, chunk):
    qblk, width = sc_scr.shape
    kb = keys_ref[:width, :D_IDX].astype(BF16)
    sc = _index_heads(qi_ref[...], wi_ref[...], kb)
    row = lax.broadcasted_iota(I32, (qblk, width), 0) + (q0 + pl.program_id(1) * qblk)
    col = lax.broadcasted_iota(I32, (qblk, width), 1)
    sc_scr[...] = jnp.where(col <= row, sc, -jnp.inf)
    _topk_mask(sc_scr, utri_ref, o_ref, key_scr, topk=topk, chunk=chunk)


def index_select_causal(qib, kwo, topk, n_seq, seq_len, qblk, chunk):
    n_cls = SELECT_CLASSES if seq_len % (SELECT_CLASSES * max(chunk, qblk)) == 0 else 1
    cw = seq_len // n_cls
    utri = _strict_upper(chunk)
    parts = []
    for c in range(n_cls):
        width = (c + 1) * cw
        qmap = functools.partial(lambda b, i, c: (b * (seq_len // qblk) + c * (cw // qblk) + i, 0), c=c)
        parts.append(pl.pallas_call(
            functools.partial(_index_select_kernel, q0=c * cw, topk=topk, chunk=chunk),
            grid=(n_seq, cw // qblk),
            in_specs=[pl.BlockSpec((qblk, C_QI), qmap), pl.BlockSpec((qblk, LANES), qmap),
                      pl.BlockSpec((seq_len, LANES), lambda b, i: (b, 0)),
                      pl.BlockSpec((chunk, chunk), lambda b, i: (0, 0))],
            out_specs=pl.BlockSpec((qblk, seq_len), lambda b, i: (b * (cw // qblk) + i, 0)),
            out_shape=jax.ShapeDtypeStruct((n_seq * cw, seq_len), BF16),
            scratch_shapes=[pltpu.VMEM((qblk, width), F32), pltpu.VMEM((qblk, width), I32)],
            compiler_params=_cparams(("parallel", "parallel")),
            name="index_select_%d" % c,
        )(qib, kwo, kwo, utri).reshape(n_seq, cw, seq_len))
    return jnp.concatenate(parts, axis=1).reshape(n_seq * seq_len, seq_len)


NEG_BIG = -1e30


def _attn_init(m_scr, l_scr, acc_scr):
    m_scr[...] = jnp.full(m_scr.shape, NEG_BIG, F32)
    l_scr[...] = jnp.zeros(l_scr.shape, F32)
    acc_scr[...] = jnp.zeros(acc_scr.shape, F32)


def _attn_block(q, k, v, mask, m_scr, l_scr, acc_scr):
    bias = jnp.concatenate([(mask.astype(F32) - 1.0) * (-NEG_BIG)] * GROUP_C, axis=0)
    heads = range(KVH_C)
    ss = []
    for kh in heads:
        qh = jnp.concatenate([q[:, (kh * GROUP_C + g) * DH_C:(kh * GROUP_C + g + 1) * DH_C]
                              for g in range(GROUP_C)], axis=0)
        ks = k[:, kh * DH_C:(kh + 1) * DH_C]
        ss.append(lax.dot_general(qh, ks, (((1,), (1,)), ((), ())), preferred_element_type=F32) + bias)
    n_tiles = ss[0].shape[1] // LANES
    m_old = [m_scr[kh] for kh in heads]
    m_new = [jnp.maximum(m_old[kh], jnp.broadcast_to(_row_max(ss[kh]), m_old[kh].shape)) for kh in heads]
    ps = [[jnp.exp(ss[kh][:, i * LANES:(i + 1) * LANES] - m_new[kh]) for i in range(n_tiles)] for kh in heads]
    for kh in heads:
        alpha = jnp.exp(m_old[kh] - m_new[kh])
        l_scr[kh] = alpha * l_scr[kh] + functools.reduce(jnp.add, ps[kh])
        p = jnp.concatenate(ps[kh], axis=1).astype(BF16)
        acc_scr[kh] = alpha * acc_scr[kh] + _dot(p, v[:, kh * DH_C:(kh + 1) * DH_C])
        m_scr[kh] = m_new[kh]


def _attn_finish(o_ref, nq, l_scr, acc_scr, lead=None):
    for kh in range(KVH_C):
        out = acc_scr[kh] / jnp.sum(l_scr[kh], axis=-1, keepdims=True)
        for g in range(GROUP_C):
            hs = slice((kh * GROUP_C + g) * DH_C, (kh * GROUP_C + g + 1) * DH_C)
            val = out[g * nq:(g + 1) * nq, :].astype(o_ref.dtype)
            if lead is None:
                o_ref[:, hs] = val
            else:
                o_ref[lead, :, hs] = val


def _attn_prompt_kernel(q_ref, k_ref, v_ref, mask_ref, o_ref, m_scr, l_scr, acc_scr, *, qblk, kblk):
    qb = pl.program_id(1)
    kb = pl.program_id(2)

    @pl.when(kb == 0)
    def _():
        _attn_init(m_scr, l_scr, acc_scr)

    @pl.when(kb * kblk <= qb * qblk + qblk - 1)
    def _():
        _attn_block(q_ref[...], k_ref[...], v_ref[...], mask_ref[...], m_scr, l_scr, acc_scr)

    @pl.when(kb == pl.num_programs(2) - 1)
    def _():
        _attn_finish(o_ref, qblk, l_scr, acc_scr)


def attn_prompt(qb_, kb_, vb_, mask, n_p, l_p, qblk, kblk):
    nq, nk = l_p // qblk, l_p // kblk
    last = lambda q, k: jnp.minimum(k, (q * qblk + qblk - 1) // kblk)
    rows = GROUP_C * qblk
    return pl.pallas_call(
        functools.partial(_attn_prompt_kernel, qblk=qblk, kblk=kblk),
        grid=(n_p, nq, nk),
        in_specs=[pl.BlockSpec((qblk, H_C * DH_C), lambda b, q, k: (b * nq + q, 0)),
                  pl.BlockSpec((kblk, KVH_C * DH_C), lambda b, q, k: (b * nk + last(q, k), 0)),
                  pl.BlockSpec((kblk, KVH_C * DH_C), lambda b, q, k: (b * nk + last(q, k), 0)),
                  pl.BlockSpec((qblk, kblk), lambda b, q, k: (b * nq + q, last(q, k)))],
        out_specs=pl.BlockSpec((qblk, H_C * DH_C), lambda b, q, k: (b * nq + q, 0)),
        out_shape=jax.ShapeDtypeStruct((n_p * l_p, H_C * DH_C), BF16),
        scratch_shapes=[pltpu.VMEM((KVH_C, rows, LANES), F32), pltpu.VMEM((KVH_C, rows, LANES), F32),
                        pltpu.VMEM((KVH_C, rows, DH_C), F32)],
        compiler_params=_cparams(("parallel", "parallel", "arbitrary")),
        name="attn_prompt",
    )(qb_, kb_, vb_, mask)


def _dsa_sample_kernel(pt_ref, qi_ref, kw_ref, q_ref, kn_ref, vn_ref, utri_ref, kidx_hbm, k_hbm, v_hbm, o_ref,
                       kibuf, kbuf, vbuf, sems, sc_scr, key_scr, mask_scr, *, n_pages, page, l_s, topk, pool0):
    b = pl.program_id(0)
    past = n_pages * page
    s_len = past + page

    def page_copies(bb, slot, pg):
        phys = pt_ref[bb * n_pages + pg] + pool0
        rows = pl.ds(pg * page, page)
        cps = [pltpu.make_async_copy(kidx_hbm.at[phys], kibuf.at[slot, rows, :], sems.at[slot, 0])]
        for kh in range(KVH_C):
            cps.append(pltpu.make_async_copy(k_hbm.at[phys, :, kh, :], kbuf.at[slot, kh, rows, :], sems.at[slot, 1]))
            cps.append(pltpu.make_async_copy(v_hbm.at[phys, :, kh, :], vbuf.at[slot, kh, rows, :], sems.at[slot, 2]))
        return cps

    def fetch(bb, slot):
        for pg in range(n_pages):
            for cp in page_copies(bb, slot, pg):
                cp.start()

    @pl.when(b == 0)
    def _():
        kibuf[:, past:, :] = jnp.zeros((2, page, D_IDX), F32)
        kbuf[:, :, past:, :] = jnp.zeros((2, KVH_C, page, DH_C), F32)
        vbuf[:, :, past:, :] = jnp.zeros((2, KVH_C, page, DH_C), F32)
        fetch(0, 0)

    @pl.when(b + 1 < pl.num_programs(0))
    def _():
        fetch(b + 1, (b + 1) % 2)

    cur = b % 2
    for pg in range(n_pages):
        for cp in page_copies(b, cur, pg):
            cp.wait()
    kw = kw_ref[0]
    kibuf[cur, past:past + SROWS, :] = kw[:, :D_IDX]
    for kh in range(KVH_C):
        hs = slice(kh * DH_C, (kh + 1) * DH_C)
        kbuf[cur, kh, past:past + SROWS, :] = kn_ref[0][:, hs]
        vbuf[cur, kh, past:past + SROWS, :] = vn_ref[0][:, hs]

    dots = lax.dot_general(qi_ref[0], kibuf[cur].astype(BF16), (((1,), (1,)), ((), ())),
                           preferred_element_type=F32)
    sc = None
    for h in range(H_IDX):
        term = kw[:, D_IDX + h:D_IDX + h + 1] * jnp.maximum(dots[h * SROWS:(h + 1) * SROWS, :], 0.0)
        sc = term if sc is None else sc + term
    row = lax.broadcasted_iota(I32, (SROWS, s_len), 0)
    col = lax.broadcasted_iota(I32, (SROWS, s_len), 1)
    visible = (col < past) | ((col - past <= row) & (col - past < l_s))
    sc_scr[...] = jnp.where(visible, sc + 0.0, -jnp.inf)
    _topk_mask(sc_scr, utri_ref, mask_scr, key_scr, topk=topk, chunk=page)

    bias = jnp.concatenate([(mask_scr[...] - 1.0) * (-NEG_BIG)] * GROUP_C, axis=0)
    q = q_ref[0]
    for kh in range(KVH_C):
        qh = jnp.concatenate([q[:, (kh * GROUP_C + g) * DH_C:(kh * GROUP_C + g + 1) * DH_C]
                              for g in range(GROUP_C)], axis=0)
        s = lax.dot_general(qh, kbuf[cur, kh].astype(BF16), (((1,), (1,)), ((), ())),
                            preferred_element_type=F32) + bias
        p = jnp.exp(s - _row_max(s))
        out = _dot(p.astype(BF16), vbuf[cur, kh].astype(BF16)) / _row_sum(p)
        for g in range(GROUP_C):
            o_ref[0, :, (kh * GROUP_C + g) * DH_C:(kh * GROUP_C + g + 1) * DH_C] = (
                out[g * SROWS:(g + 1) * SROWS, :].astype(o_ref.dtype))


def dsa_sample(page_table, qi_hm, kw3, q3, knew3, vnew3, cache_kidx3, cache_k4, cache_v4, *, pool0, l_s, topk):
    n_s, n_pages = page_table.shape
    page = cache_k4.shape[1]
    kvw = KVH_C * DH_C
    s_len = (n_pages + 1) * page
    own = lambda wd: pl.BlockSpec((1, SROWS, wd), lambda b, pt: (b, 0, 0))
    hbm = pl.BlockSpec(memory_space=pl.ANY)
    return pl.pallas_call(
        functools.partial(_dsa_sample_kernel, n_pages=n_pages, page=page, l_s=l_s, topk=topk, pool0=pool0),
        grid_spec=pltpu.PrefetchScalarGridSpec(
            num_scalar_prefetch=1,
            grid=(n_s,),
            in_specs=[pl.BlockSpec((1, H_IDX * SROWS, D_IDX), lambda b, pt: (b, 0, 0)),
                      own(LANES), own(H_C * DH_C), own(kvw), own(kvw),
                      pl.BlockSpec((page, page), lambda b, pt: (0, 0)), hbm, hbm, hbm],
            out_specs=own(H_C * DH_C),
            scratch_shapes=[pltpu.VMEM((2, s_len, D_IDX), F32), pltpu.VMEM((2, KVH_C, s_len, DH_C), F32),
                            pltpu.VMEM((2, KVH_C, s_len, DH_C), F32), pltpu.SemaphoreType.DMA((2, 3)),
                            pltpu.VMEM((SROWS, s_len), F32), pltpu.VMEM((SROWS, s_len), I32),
                            pltpu.VMEM((SROWS, s_len), F32)],
        ),
        out_shape=jax.ShapeDtypeStruct((n_s, SROWS, H_C * DH_C), BF16),
        compiler_params=_cparams(("arbitrary",)),
        name="dsa_sample",
    )(page_table.reshape(-1), qi_hm, kw3, q3, knew3, vnew3, _strict_upper(page), cache_kidx3, cache_k4, cache_v4)


def c_params(li, w):
    p = {}
    w_in = w["c_w_in"][li]
    p["w_qkv"] = w_in[:, :C_QKV].astype(BF16)
    p["w_qi"] = w_in[:, C_QKV:C_QKV + C_QI].astype(BF16)
    p["w_kw"] = _pad_cols(w_in[:, C_QKV + C_QI:], LANES).astype(BF16)
    p["ln_g"] = _pad_cols(w["c_kidx_ln_g"][li][None, :], LANES)
    p["ln_b"] = _pad_cols(w["c_kidx_ln_b"][li][None, :], LANES)
    p["w_out"] = w["c_w_out"][li].astype(BF16)
    return p


def _pad_sample_rows(a, n_s, l_s):
    a3 = a.reshape(n_s, l_s, a.shape[1])
    return jnp.pad(a3, ((0, 0), (0, SROWS - l_s), (0, 0)))


def c_layer(xb, geo, p, li, cache_k, cache_v, cache_kidx, page_table, *, tm, qblk, kblk):
    n_p, l_p, n_s, l_s = geo
    tp = n_p * l_p
    n_pool, page = cache_k.shape[1], cache_k.shape[2]
    n_pages = page_table.shape[1]
    past = n_pages * page
    qkv = matmul([xb], [p["w_qkv"]], tm, 1024, name="c_in_qkv")
    qi = matmul([xb], [p["w_qi"]], tm, C_QI, name="c_in_qi")
    kw = matmul([xb], [p["w_kw"]], tm, LANES, name="c_in_kw")
    pos = jnp.concatenate([jnp.tile(jnp.arange(l_p, dtype=I32), n_p),
                           jnp.tile(past + jnp.arange(l_s, dtype=I32), n_s)])
    q_b, k_f, k_b, v_b, qi_b, kwo = c_post(qkv, qi, kw, _rope_tables(pos), p["ln_g"], p["ln_b"], tm)
    v_f = qkv[:, (H_C + KVH_C) * DH_C:]
    mask_p = index_select_causal(qi_b, kwo, min(TOPK_MAX, l_p // 4), n_p, l_p, qblk, 256)
    o_p = attn_prompt(q_b, k_b, v_b, mask_p, n_p, l_p, qblk, kblk)
    pad = lambda a: _pad_sample_rows(a[tp:], n_s, l_s)
    qi_hm = pad(qi_b).reshape(n_s, SROWS, H_IDX, D_IDX).transpose(0, 2, 1, 3).reshape(n_s, H_IDX * SROWS, D_IDX)
    pool = lambda a: a.reshape((-1,) + a.shape[2:])
    o_s3 = dsa_sample(page_table, qi_hm, pad(kwo), pad(q_b), pad(k_f), pad(v_f),
                      pool(cache_kidx), pool(cache_k), pool(cache_v), pool0=li * n_pool, l_s=l_s,
                      topk=min(TOPK_MAX, (past + l_s) // 4))
    o = jnp.concatenate([o_p, o_s3[:, :l_s].reshape(n_s * l_s, -1)], axis=0)
    h = matmul([o], [p["w_out"]], tm, 1024, name="c_out")
    heads = lambda a, n, l: a.reshape(n, l, KVH_C, DH_C)
    new_p = (heads(k_f[:tp], n_p, l_p), heads(v_f[:tp], n_p, l_p), kwo[:tp, :D_IDX].reshape(n_p, l_p, D_IDX))
    new_s = (heads(k_f[tp:], n_s, l_s), heads(v_f[tp:], n_s, l_s), kwo[tp:, :D_IDX].reshape(n_s, l_s, D_IDX))
    return h, new_p, new_s


TM = 256
BLK = 128
TM_DMA = 64
QBLK = 128
KBLK = 512


def kernel(x_prompt, x_sample, state_shift, state_wkv, state_ssm_re, state_ssm_im, cache_k, cache_v, cache_kidx, page_table, ab_w_in, ab_mu, rwkv_w0, rwkv_w2, rwkv_a0, rwkv_a2, rwkv_g2, rwkv_k_k, rwkv_k_a, rwkv_r_k, rwkv_ln_g, rwkv_ln_b, s5_lam_re, s5_lam_im, s5_log_dt, s5_b_re, s5_b_im, s5_c_re, s5_c_im, s5_d, s5_w_glu, s5_b_glu, ab_w_out, c_w_in, c_kidx_ln_g, c_kidx_ln_b, c_w_out, ln_mix_g, ln_mix_b, ln_ffn_g, ln_ffn_b, moe_w_router, moe_b_router, moe_w_gate, moe_w_up, moe_w_down, moe_ws_gate, moe_ws_up, moe_ws_down):
    w = dict(locals())
    n_p, l_p, _ = x_prompt.shape
    n_s, l_s, _ = x_sample.shape
    geo = (n_p, l_p, n_s, l_s)
    tp = n_p * l_p
    y = jnp.concatenate([x_prompt.reshape(-1, D_MODEL), x_sample.reshape(-1, D_MODEL)], axis=0)
    yb = y.astype(BF16)
    outs = {k: [] for k in ("kp", "vp", "kip", "ks", "vs", "kis", "shp", "shs", "wkvp", "wkvs",
                            "srp", "sip", "srs", "sis")}
    for layer in range(DEPTH):
        li = layer // 2
        if layer % 2 == 0:
            wkv_all = state_wkv.reshape((-1,) + state_wkv.shape[2:])
            h, shift, wkv, ssm_p, ssm_s = ab_layer(yb, geo, ab_params(li, w, l_s), state_shift[li], wkv_all, li * n_s,
                                                   state_ssm_re[li], state_ssm_im[li], tm=TM, blk=BLK)
            for k, v in zip(("shp", "shs", "wkvp", "wkvs", "srp", "sip", "srs", "sis"),
                            (*shift, *wkv, *ssm_p, *ssm_s)):
                outs[k].append(v)
        else:
            h, new_p, new_s = c_layer(yb, geo, c_params(li, w), li, cache_k, cache_v, cache_kidx,
                                      page_table, tm=TM, qblk=QBLK, kblk=KBLK)
            for k, v in zip(("kp", "vp", "kip", "ks", "vs", "kis"), (*new_p, *new_s)):
                outs[k].append(v)
        y, yb = add_ln(y, h, ln_mix_g[layer], ln_mix_b[layer], TM)
        y, yb = moe_layer(y, yb, moe_params(layer, w), ln_ffn_g[layer], ln_ffn_b[layer], tm=TM, tm_dma=TM_DMA)
    st = lambda k: jnp.stack(outs[k])
    return (y[:tp].reshape(n_p, l_p, D_MODEL), y[tp:].reshape(n_s, l_s, D_MODEL),
            st("kp"), st("vp"), st("kip"), st("ks"), st("vs"), st("kis"),
            st("shp"), st("shs"), st("wkvp"), st("wkvs"), st("srp"), st("sip"), st("srs"), st("sis"))
```

```python
import functools
import math

import jax
import jax.numpy as jnp
from jax import lax
from jax.experimental import pallas as pl
from jax.experimental.pallas import tpu as pltpu

F32, BF16, I32 = jnp.float32, jnp.bfloat16, jnp.int32

D_MODEL = 2048
DEPTH = 2
ALPHA = (2.0 * DEPTH) ** 0.25
LN_EPS = 1e-5
C_A = 1024
HS_A = 64
H_A = 16
LORA_W, LORA_A, LORA_G = 64, 64, 160
A_PROJ = 3 * C_A + LORA_W + LORA_A + LORA_G
GN_EPS_A = 64e-5
C_B = 1024
S5_P = 16
S5_GROUPS = 64
S5_N = 64
S5_STATE = S5_GROUPS * S5_N
DH_C = 128
H_C = 16
KVH_C = 4
GROUP_C = 4
H_IDX = 16
D_IDX = 64
TOPK_MAX = 256
ROPE_THETA = 10000.0
N_EXPERTS = 64
TOP_K = 8
N_EXPERT_GROUPS = 8
TOPK_GROUPS = 4
D_EXPERT = 512
ROUTED_SCALE = 2.5

LANES = 128
SUBLANES = 8
VMEM_LIMIT = 56 * 1024 * 1024

LORA_PAD = 512
EXP_M05 = math.exp(-0.5)


def _cparams(sem):
    return pltpu.CompilerParams(dimension_semantics=sem, vmem_limit_bytes=VMEM_LIMIT)


def _dot(a, b):
    return jnp.dot(a, b, preferred_element_type=F32)


def _lane_fold(x, op):
    tiles = [x[:, i * LANES:(i + 1) * LANES] for i in range(x.shape[1] // LANES)]
    return functools.reduce(op, tiles)


def _row_sum(x):
    return jnp.sum(_lane_fold(x, jnp.add), axis=-1, keepdims=True)


def _row_max(x):
    return jnp.max(_lane_fold(x, jnp.maximum), axis=-1, keepdims=True)


def _split3(a):
    a1 = a.astype(BF16)
    r1 = a - a1.astype(F32)
    a2 = r1.astype(BF16)
    a3 = (r1 - a2.astype(F32)).astype(BF16)
    return a1, a2, a3


def _dot_sel(a, sel_bf16):
    a1, a2, a3 = _split3(a)
    return _dot(a1, sel_bf16) + _dot(a2, sel_bf16) + _dot(a3, sel_bf16)


def _dot3(a, b):
    a1 = a.astype(BF16)
    a2 = (a - a1.astype(F32)).astype(BF16)
    b1 = b.astype(BF16)
    b2 = (b - b1.astype(F32)).astype(BF16)
    return _dot(a1, b1) + _dot(a1, b2) + _dot(a2, b1)


def _mm_kernel(*refs, n_lhs):
    o_ref = refs[-1]
    acc = None
    for i in range(n_lhs):
        d = _dot(refs[i][...], refs[n_lhs + i][...])
        acc = d if acc is None else acc + d
    o_ref[...] = acc.astype(o_ref.dtype)


MM_TM = 512


def matmul(xs, ws, tm, tn, out_dtype=F32, name="mm"):
    m = xs[0].shape[0]
    n = ws[0].shape[1]
    tm = MM_TM if m % MM_TM == 0 else tm
    assert m % tm == 0 and n % tn == 0, (m, tm, n, tn)
    in_specs = ([pl.BlockSpec((tm, x.shape[1]), lambda j, i: (i, 0)) for x in xs]
                + [pl.BlockSpec((w.shape[0], tn), lambda j, i: (0, j)) for w in ws])
    return pl.pallas_call(
        functools.partial(_mm_kernel, n_lhs=len(xs)),
        grid=(n // tn, m // tm),
        in_specs=in_specs,
        out_specs=pl.BlockSpec((tm, tn), lambda j, i: (i, j)),
        out_shape=jax.ShapeDtypeStruct((m, n), out_dtype),
        compiler_params=_cparams(("parallel", "parallel")),
        name=name,
    )(*xs, *ws)


def _add_ln_kernel(x_ref, h_ref, g_ref, b_ref, o_ref, ob_ref):
    z = ALPHA * x_ref[...] + h_ref[...]
    inv_d = 1.0 / z.shape[1]
    mu = _row_sum(z) * inv_d
    d = z - mu
    var = _row_sum(d * d) * inv_d
    y = d * lax.rsqrt(var + LN_EPS) * g_ref[...] + b_ref[...]
    o_ref[...] = y
    ob_ref[...] = y.astype(BF16)


def add_ln(x, h, g, b, tm):
    t, d = x.shape
    row = pl.BlockSpec((tm, d), lambda i: (i, 0))
    vec = pl.BlockSpec((1, d), lambda i: (0, 0))
    return pl.pallas_call(
        _add_ln_kernel,
        grid=(t // tm,),
        in_specs=[row, row, vec, vec],
        out_specs=[row, row],
        out_shape=[jax.ShapeDtypeStruct((t, d), F32), jax.ShapeDtypeStruct((t, d), BF16)],
        compiler_params=_cparams(("parallel",)),
        name="add_ln",
    )(x, h, g.reshape(1, d), b.reshape(1, d))


def _rwkv_prep_kernel(rkv_ref, rkvp_ref, lo_ref, lop_ref, mu1_ref, mu2_ref, w0_ref, w2_ref, a0_ref, a2_ref,
                      g2_ref, kk_ref, ka_ref, rk_ref, gsel_ref, gselt_ref,
                      kk_o, wr_o, w_o, b_o, k_o, v_o, kr_o, g_o, bonus_o):
    x = rkv_ref[...]
    m = x + (rkvp_ref[...] - x) * mu1_ref[...]
    r = m[:, :C_A]
    k = m[:, C_A:2 * C_A]
    v = m[:, 2 * C_A:]
    l = lo_ref[...]
    lm = l + (lop_ref[...] - l) * mu2_ref[...]
    lw = lm[:, :LANES]
    la = lm[:, LANES:2 * LANES]
    lg = lm[:, 2 * LANES:]
    w_raw = w0_ref[...] + _dot3(jnp.tanh(lw), w2_ref[...])
    decay = jnp.exp(-EXP_M05 * jax.nn.sigmoid(w_raw))
    a = jax.nn.sigmoid(a0_ref[...] + _dot3(la, a2_ref[...]))
    g = _dot3(jax.nn.sigmoid(lg), g2_ref[...])
    gsel = gsel_ref[...]
    gselt = gselt_ref[...]
    kk = k * kk_ref[...]
    nrm = jnp.sqrt(_dot_sel(kk * kk, gsel))
    inv = 1.0 / jnp.maximum(nrm, 1e-12)
    kk = kk * _dot_sel(inv, gselt)
    k_mod = k * (1.0 + (a - 1.0) * ka_ref[...])
    sb = _dot_sel(r * k_mod * rk_ref[...], gsel)
    bonus_o[...] = _dot_sel(sb, gselt) * v
    g_o[...] = g
    b = kk * a
    kk_o[...] = kk
    br = _dot_sel(_dot_sel(b * r, gsel), gselt)
    wr_o[...] = decay * r - kk * br
    w_o[...] = decay
    b_o[...] = b
    k_o[...] = k_mod
    v_o[...] = v
    kr_o[...] = _dot_sel(_dot_sel(k_mod * r, gsel), gselt)


RWKV_ROWS = 7


def rwkv_prep(rkv, rkv_prev, lora, lora_prev, p, tm):
    t = rkv.shape[0]
    row = lambda w: pl.BlockSpec((tm, w), lambda i: (i, 0))
    full = lambda a: pl.BlockSpec(a.shape, lambda i: (0,) * a.ndim)
    consts = [p["mu_rkv"], p["mu_lora"], p["w0"], p["w2"], p["a0"], p["a2"], p["g2"], p["k_k"], p["k_a"], p["r_k"],
              p["gsel"], p["gselt"]]
    n_out = RWKV_ROWS + 2
    return pl.pallas_call(
        _rwkv_prep_kernel,
        grid=(t // tm,),
        in_specs=[row(3 * C_A), row(3 * C_A), row(LORA_PAD), row(LORA_PAD)] + [full(c) for c in consts],
        out_specs=[row(C_A)] * n_out,
        out_shape=[jax.ShapeDtypeStruct((t, C_A), F32)] * n_out,
        compiler_params=_cparams(("parallel",)),
        name="rwkv_prep",
    )(rkv, rkv_prev, lora, lora_prev, *consts)


TILE_HEADS = 4
TILE_W = TILE_HEADS * HS_A
N_TILES = H_A // TILE_HEADS
LHS_PER_TILE = 3
WKV_UNROLL = 4


def _wkv_steps(rows, row_base, n_steps, npar, s_scr, bd_ref, store_o):
    sub = lax.broadcasted_iota(I32, (HS_A, TILE_W), 0)
    lane = lax.broadcasted_iota(I32, (HS_A, TILE_W), 1)
    diag = (lane % HS_A) == sub
    bd = bd_ref[...]
    blk_rows = LHS_PER_TILE * HS_A

    def step(t, carry):
        vec = [[ref[pl.ds(row_base[s] + t, 1), :] for ref in rows[s]] for s in range(npar)]
        results = []
        for s in range(npar):
            kk, wr2, _, _, _, v, _ = vec[s]
            for p in range(N_TILES):
                ls = slice(p * TILE_W, (p + 1) * TILE_W)
                st = s_scr[s * N_TILES + p]
                q1 = (st * kk[:, ls]).astype(BF16)
                q2 = (st * wr2[:, ls]).astype(BF16)
                ve = jnp.where(diag, v[:, ls], 0.0).astype(BF16)
                results.append(_dot(jnp.concatenate([q1, q2, ve], axis=0), bd))
        for s in range(npar):
            _, _, w, b, k, v, kr = vec[s]
            o_parts = []
            for p in range(N_TILES):
                ls = slice(p * TILE_W, (p + 1) * TILE_W)
                res = results[s * N_TILES + p]
                z1 = res[:HS_A]
                z2 = res[HS_A:2 * HS_A]
                vb = res[2 * HS_A:]
                st = s_scr[s * N_TILES + p]
                s_scr[s * N_TILES + p] = st * w[:, ls] - z1 * b[:, ls] + vb * k[:, ls]
                z2r = jnp.sum(jnp.where(diag, z2, 0.0), axis=0, keepdims=True)
                o_parts.append(z2r + v[:, ls] * kr[:, ls])
            store_o(s, t, jnp.concatenate(o_parts, axis=1))
        return carry

    lax.fori_loop(0, n_steps, step, 0, unroll=WKV_UNROLL)


def _pack_state(s_ref, q, s_scr, slot):
    for p in range(N_TILES):
        s_scr[slot * N_TILES + p] = jnp.concatenate([s_ref[q, TILE_HEADS * p + h] for h in range(TILE_HEADS)], axis=1)


def _unpack_state(s_scr, slot, s_ref, q):
    for p in range(N_TILES):
        st = s_scr[slot * N_TILES + p]
        for h in range(TILE_HEADS):
            s_ref[q, TILE_HEADS * p + h] = st[:, h * HS_A:(h + 1) * HS_A]


def _wkv_long_kernel(*refs, npar, blk):
    n_in = npar * RWKV_ROWS
    rows = [refs[s * RWKV_ROWS:(s + 1) * RWKV_ROWS] for s in range(npar)]
    bd_ref, o_ref, sout_ref, s_scr = refs[n_in:n_in + 4]
    c = pl.program_id(0)

    @pl.when(c == 0)
    def _():
        s_scr[...] = jnp.zeros(s_scr.shape, F32)

    def store_o(s, t, o_row):
        o_ref[s, pl.ds(t, 1), :] = o_row

    _wkv_steps(rows, [0] * npar, blk, npar, s_scr, bd_ref, store_o)

    @pl.when(c == pl.num_programs(0) - 1)
    def _():
        for s in range(npar):
            _unpack_state(s_scr, s, sout_ref, s)


WKV_NPAR_SHORT = 2
WKV_SPB = 8


def _wkv_short_kernel(*refs, seq_len, spb):
    rows_refs = refs[:RWKV_ROWS]
    s0_ref, bd_ref, o_ref, sout_ref, s_scr = refs[RWKV_ROWS:RWKV_ROWS + 5]
    c = pl.program_id(1)
    npar = WKV_NPAR_SHORT

    def group(gq, carry):
        qs = [gq * npar + s for s in range(npar)]
        for s in range(npar):
            _pack_state(s0_ref, qs[s], s_scr, s)

        def store_o(s, t, o_row):
            o_ref[pl.ds((c * spb + qs[s]) * seq_len + t, 1), :] = o_row

        _wkv_steps([rows_refs] * npar, [(c * spb + q) * seq_len for q in qs], seq_len, npar, s_scr, bd_ref, store_o)
        for s in range(npar):
            _unpack_state(s_scr, s, sout_ref, qs[s])
        return carry

    lax.fori_loop(0, spb // npar, group, 0)


def _head_ones():
    head = jnp.arange(TILE_W) // HS_A
    return (head[:, None] == head[None, :]).astype(BF16)


def wkv_long(rows, *, n_seq, seq_len, blk):
    cps = seq_len // blk
    in_specs = [pl.BlockSpec((blk, C_A), functools.partial(lambda c, s: (s * cps + c, 0), s=s))
                for s in range(n_seq) for _ in range(RWKV_ROWS)]
    return pl.pallas_call(
        functools.partial(_wkv_long_kernel, npar=n_seq, blk=blk),
        grid=(cps,),
        in_specs=in_specs + [pl.BlockSpec((TILE_W, TILE_W), lambda c: (0, 0))],
        out_specs=[pl.BlockSpec((n_seq, blk, C_A), lambda c: (0, c, 0)),
                   pl.BlockSpec((n_seq, H_A, HS_A, HS_A), lambda c: (0, 0, 0, 0))],
        out_shape=[jax.ShapeDtypeStruct((n_seq, seq_len, C_A), F32),
                   jax.ShapeDtypeStruct((n_seq, H_A, HS_A, HS_A), F32)],
        scratch_shapes=[pltpu.VMEM((n_seq * N_TILES, HS_A, TILE_W), F32)],
        compiler_params=_cparams(("arbitrary",)),
        name="wkv_long",
    )(*(list(rows) * n_seq), _head_ones())


def wkv_short(rows, s0, *, row0, n_seq, seq_len, blk, s0_seq0):
    assert row0 % blk == 0 and blk % seq_len == 0
    b0 = row0 // blk
    nsb = blk // seq_len
    spb = min(WKV_SPB, nsb)
    assert n_seq % nsb == 0 and nsb % spb == 0 and spb % WKV_NPAR_SHORT == 0 and s0_seq0 % spb == 0
    sub = nsb // spb
    sspec = lambda off: pl.BlockSpec((spb, H_A, HS_A, HS_A), lambda s, c: (off + s * sub + c, 0, 0, 0))
    return pl.pallas_call(
        functools.partial(_wkv_short_kernel, seq_len=seq_len, spb=spb),
        grid=(n_seq // nsb, sub),
        in_specs=[pl.BlockSpec((blk, C_A), lambda s, c: (b0 + s, 0))] * RWKV_ROWS
        + [sspec(s0_seq0 // spb), pl.BlockSpec((TILE_W, TILE_W), lambda s, c: (0, 0))],
        out_specs=[pl.BlockSpec((blk, C_A), lambda s, c: (s, 0)), sspec(0)],
        out_shape=[jax.ShapeDtypeStruct((n_seq * seq_len, C_A), F32),
                   jax.ShapeDtypeStruct((n_seq, H_A, HS_A, HS_A), F32)],
        scratch_shapes=[pltpu.VMEM((WKV_NPAR_SHORT * N_TILES, HS_A, TILE_W), F32)],
        compiler_params=_cparams(("parallel", "arbitrary")),
        name="wkv_short",
    )(*rows, s0, _head_ones())


def _rwkv_post_kernel(o_in_ref, bonus_ref, g_ref, lng_ref, lnb_ref, gsel_ref, gselt_ref, o_ref):
    o = o_in_ref[...]
    gsel = gsel_ref[...]
    gselt = gselt_ref[...]
    mu = _dot_sel(o, gsel) * (1.0 / HS_A)
    d = o - _dot_sel(mu, gselt)
    var = _dot_sel(d * d, gsel) * (1.0 / HS_A)
    rstd = lax.rsqrt(var + GN_EPS_A)
    y = d * _dot_sel(rstd, gselt) * lng_ref[...] + lnb_ref[...]
    o_ref[...] = ((y + bonus_ref[...]) * g_ref[...]).astype(o_ref.dtype)


def rwkv_post(o_wkv, bonus, g, p, tm):
    t = bonus.shape[0]
    row = pl.BlockSpec((tm, C_A), lambda i: (i, 0))
    full = lambda a: pl.BlockSpec(a.shape, lambda i: (0,) * a.ndim)
    consts = [p["ln_g"], p["ln_b"], p["gsel"], p["gselt"]]
    return pl.pallas_call(
        _rwkv_post_kernel,
        grid=(t // tm,),
        in_specs=[row, row, row] + [full(c) for c in consts],
        out_specs=row,
        out_shape=jax.ShapeDtypeStruct((t, C_A), BF16),
        compiler_params=_cparams(("parallel",)),
        name="rwkv_post",
    )(o_wkv, bonus, g, *consts)


S5_KT = 4
S5_KU = C_B // S5_KT
S5_KH = S5_STATE // S5_KT
S5_LC = 512


def _gelu_tanh(x):
    return 0.5 * x * (1.0 + jnp.tanh(math.sqrt(2.0 / math.pi) * (x + 0.044715 * (x * x * x))))


def _s5_kernel(u_ref, wbr_ref, wbi_ref, lamk_ref, pre_ref, pim_ref, h0r_ref, h0i_ref, wcr_ref, wci_ref,
               dsk_ref, wglu_ref, bglu_ref, ob_ref, hr_out, hi_out, hre, him, car_re, car_im, y_scr,
               *, seq_len, blk):
    long_mode = seq_len >= blk
    period = SUBLANES if long_mode else seq_len
    u = u_ref[...]
    ub = u.astype(BF16)
    for kt in range(S5_KT):
        uk = ub[:, kt * S5_KU:(kt + 1) * S5_KU]
        hre[:, kt * S5_KH:(kt + 1) * S5_KH] = _dot(uk, wbr_ref[kt])
        him[:, kt * S5_KH:(kt + 1) * S5_KH] = _dot(uk, wbi_ref[kt])

    if long_mode:
        @pl.when(pl.program_id(1) == 0)
        def _():
            car_re[...] = jnp.zeros(car_re.shape, F32)
            car_im[...] = jnp.zeros(car_im.shape, F32)

    rowi = lax.broadcasted_iota(I32, (SUBLANES, 1), 0) % period
    for lc in range(S5_STATE // S5_LC):
        ls = slice(lc * S5_LC, (lc + 1) * S5_LC)
        lam = [(lamk_ref[2 * i:2 * i + 1, ls], lamk_ref[2 * i + 1:2 * i + 2, ls]) for i in range(3)]
        pre = pre_ref[:, ls]
        pim = pim_ref[:, ls]

        def tile(i, carry, ls=ls, lam=lam, pre=pre, pim=pim):
            r0 = pl.multiple_of(i * SUBLANES, SUBLANES)
            xr = hre[pl.ds(r0, SUBLANES), ls]
            xi = him[pl.ds(r0, SUBLANES), ls]
            for lvl, sh in enumerate((1, 2, 4)):
                if sh >= period:
                    break
                lr, li = lam[lvl]
                keep = rowi >= sh
                sr = jnp.where(keep, pltpu.roll(xr, sh, 0), 0.0)
                si = jnp.where(keep, pltpu.roll(xi, sh, 0), 0.0)
                xr, xi = xr + lr * sr - li * si, xi + lr * si + li * sr
            if long_mode:
                hr, hi = carry
            else:
                hr = h0r_ref[pl.ds(r0, SUBLANES), ls]
                hi = h0i_ref[pl.ds(r0, SUBLANES), ls]
            xr, xi = xr + pre * hr - pim * hi, xi + pre * hi + pim * hr
            hre[pl.ds(r0, SUBLANES), ls] = xr
            him[pl.ds(r0, SUBLANES), ls] = xi
            if long_mode:
                last_r = jnp.broadcast_to(xr[SUBLANES - 1:SUBLANES, :], xr.shape)
                last_i = jnp.broadcast_to(xi[SUBLANES - 1:SUBLANES, :], xi.shape)
                return (last_r, last_i)
            return carry

        if long_mode:
            init = (car_re[:, ls], car_im[:, ls])
            fin = lax.fori_loop(0, blk // SUBLANES, tile, init)
            car_re[:, ls] = fin[0]
            car_im[:, ls] = fin[1]
        else:
            lax.fori_loop(0, blk // SUBLANES, tile, 0)

    if long_mode:
        hr_out[0] = car_re[...]
        hi_out[0] = car_im[...]
    else:
        hr_out[...] = hre[...]
        hi_out[...] = him[...]

    for kt in range(S5_KT):
        hs = slice(kt * S5_KH, (kt + 1) * S5_KH)
        y_scr[:, kt * S5_KU:(kt + 1) * S5_KU] = (_dot(hre[:, hs].astype(BF16), wcr_ref[kt])
                                                 - _dot(him[:, hs].astype(BF16), wci_ref[kt]))
    y = y_scr[...] + dsk_ref[...] * u
    z = _gelu_tanh(y)
    gate = jax.nn.sigmoid(_dot(z.astype(BF16), wglu_ref[...]) + bglu_ref[...])
    ob_ref[...] = (z * gate).astype(ob_ref.dtype)


def s5_mix(u, p, h0r_rows, h0i_rows, *, row0, n_seq, seq_len, blk):
    long_mode = seq_len >= blk
    assert row0 % blk == 0
    b0 = row0 // blk
    t_out = n_seq * seq_len
    if long_mode:
        cps = seq_len // blk
        grid = (n_seq, cps)
        umap = lambda s, c: (b0 + s * cps + c, 0)
        omap = lambda s, c: (s * cps + c, 0)
        hspec = pl.BlockSpec((1, SUBLANES, S5_STATE), lambda s, c: (s, 0, 0))
        hshape = jax.ShapeDtypeStruct((n_seq, SUBLANES, S5_STATE), F32)
        h0spec = pl.BlockSpec((SUBLANES, S5_STATE), lambda s, c: (0, 0))
    else:
        assert t_out % blk == 0
        grid = (t_out // blk, 1)
        umap = lambda s, c: (b0 + s, 0)
        omap = lambda s, c: (s, 0)
        hspec = pl.BlockSpec((blk, S5_STATE), omap)
        hshape = jax.ShapeDtypeStruct((t_out, S5_STATE), F32)
        h0spec = pl.BlockSpec((blk, S5_STATE), omap)
    full = lambda a: pl.BlockSpec(a.shape, lambda s, c: (0,) * a.ndim)
    pre, pim = (p["pow_re8"], p["pow_im8"]) if long_mode else (p["pow_re_s"], p["pow_im_s"])
    args = [u, p["wb_re"], p["wb_im"], p["lamk"], pre, pim, h0r_rows, h0i_rows, p["wc_re"], p["wc_im"],
            p["d_skip"], p["w_glu"], p["b_glu"]]
    in_specs = [pl.BlockSpec((blk, C_B), umap)] + [full(a) for a in args[1:6]] + [h0spec, h0spec] \
        + [full(a) for a in args[8:]]
    return pl.pallas_call(
        functools.partial(_s5_kernel, seq_len=seq_len, blk=blk),
        grid=grid,
        in_specs=in_specs,
        out_specs=[pl.BlockSpec((blk, C_B), omap), hspec, hspec],
        out_shape=[jax.ShapeDtypeStruct((t_out, C_B), BF16), hshape, hshape],
        scratch_shapes=[pltpu.VMEM((blk, S5_STATE), F32), pltpu.VMEM((blk, S5_STATE), F32),
                        pltpu.VMEM((SUBLANES, S5_STATE), F32), pltpu.VMEM((SUBLANES, S5_STATE), F32),
                        pltpu.VMEM((blk, C_B), F32)],
        compiler_params=_cparams(("parallel", "arbitrary")),
        name="s5_long" if long_mode else "s5_short",
    )(*args)


def _pad_cols(a, n):
    return jnp.pad(a, ((0, 0), (0, n - a.shape[1])))


def _pad_rows(a, n):
    return jnp.pad(a, ((0, n - a.shape[0]), (0, 0)))


def _lora_layout(a):
    o = 3 * C_A
    return jnp.concatenate([_pad_cols(a[:, o:o + LORA_W], LANES),
                            _pad_cols(a[:, o + LORA_W:o + LORA_W + LORA_A], LANES),
                            _pad_cols(a[:, o + LORA_W + LORA_A:A_PROJ], 2 * LANES)], axis=1)


def _lora_unlayout(a):
    return jnp.concatenate([a[:, :LORA_W], a[:, LANES:LANES + LORA_A], a[:, 2 * LANES:2 * LANES + LORA_G]], axis=1)


def _cpow_table(lbr, lbi, n):
    res_r, res_i = [lbr], [lbi]
    for _ in range(n - 1):
        pr, pi = res_r[-1], res_i[-1]
        res_r.append(pr * lbr - pi * lbi)
        res_i.append(pr * lbi + pi * lbr)
    return (jnp.stack([x.reshape(-1) for x in res_r]), jnp.stack([x.reshape(-1) for x in res_i]))


def ab_params(li, w, sample_len):
    p = {}
    w_in = w["ab_w_in"][li]
    p["w_rkv"] = w_in[:, :3 * C_A].astype(BF16)
    p["w_lora"] = _lora_layout(w_in).astype(BF16)
    p["w_u"] = w_in[:, A_PROJ:].astype(BF16)
    mu = w["ab_mu"][li][None, :]
    p["mu_rkv"] = mu[:, :3 * C_A]
    p["mu_lora"] = _lora_layout(mu)
    row = lambda a: a.reshape(1, -1)
    p["w0"] = row(w["rwkv_w0"][li])
    p["w2"] = _pad_rows(w["rwkv_w2"][li], LANES)
    p["a0"] = row(w["rwkv_a0"][li])
    p["a2"] = _pad_rows(w["rwkv_a2"][li], LANES)
    p["g2"] = _pad_rows(w["rwkv_g2"][li], 2 * LANES)
    p["k_k"] = row(w["rwkv_k_k"][li])
    p["k_a"] = row(w["rwkv_k_a"][li])
    p["r_k"] = row(w["rwkv_r_k"][li])
    p["ln_g"] = row(w["rwkv_ln_g"][li])
    p["ln_b"] = row(w["rwkv_ln_b"][li])
    head_of_col = jnp.arange(C_A) // HS_A
    gsel = (head_of_col[:, None] == jnp.arange(LANES)[None, :])
    p["gsel"] = gsel.astype(BF16)
    p["gselt"] = gsel.T.astype(BF16)
    lr = jnp.minimum(w["s5_lam_re"][li], -1e-4)
    lim = w["s5_lam_im"][li]
    dt = jnp.exp(w["s5_log_dt"][li])[:, None]
    mag = jnp.exp(lr * dt)
    lbr, lbi = mag * jnp.cos(lim * dt), mag * jnp.sin(lim * dt)
    den = lr * lr + lim * lim
    pr, pim = lbr - 1.0, lbi
    fr = (pr * lr + pim * lim) / den
    fi = (pim * lr - pr * lim) / den
    br, bi = w["s5_b_re"][li], w["s5_b_im"][li]
    bbr = fr[..., None] * br - fi[..., None] * bi
    bbi = fr[..., None] * bi + fi[..., None] * br
    gpc = S5_GROUPS // S5_KT
    eye = jnp.eye(gpc, dtype=F32)

    def blockdiag_in(bb):
        b4 = bb.reshape(S5_KT, gpc, S5_N, S5_P)
        return jnp.einsum("kgnp,gh->kgphn", b4, eye).reshape(S5_KT, S5_KU, S5_KH).astype(BF16)

    def blockdiag_out(cc):
        c4 = cc.reshape(S5_KT, gpc, S5_P, S5_N)
        return jnp.einsum("kgpn,gh->kgnhp", c4, eye).reshape(S5_KT, S5_KH, S5_KU).astype(BF16)

    p["wb_re"], p["wb_im"] = blockdiag_in(bbr), blockdiag_in(bbi)
    p["wc_re"], p["wc_im"] = blockdiag_out(w["s5_c_re"][li]), blockdiag_out(w["s5_c_im"][li])
    pw_r, pw_i = _cpow_table(lbr, lbi, SUBLANES)
    p["pow_re8"], p["pow_im8"] = pw_r, pw_i
    reps = SUBLANES // sample_len
    p["pow_re_s"] = jnp.tile(pw_r[:sample_len], (reps, 1))
    p["pow_im_s"] = jnp.tile(pw_i[:sample_len], (reps, 1))
    p["lamk"] = jnp.stack([pw_r[0], pw_i[0], pw_r[1], pw_i[1], pw_r[3], pw_i[3]])
    p["d_skip"] = row(w["s5_d"][li])
    p["w_glu"] = w["s5_w_glu"][li].astype(BF16)
    p["b_glu"] = row(w["s5_b_glu"][li])
    w_out = w["ab_w_out"][li]
    p["w_out_a"] = w_out[:C_A].astype(BF16)
    p["w_out_b"] = w_out[C_A:].astype(BF16)
    return p


def _shift_rows(a, first_sample, geo):
    n_p, l_p, n_s, l_s = geo
    wd = a.shape[1]
    ap = a[:n_p * l_p].reshape(n_p, l_p, wd)
    prev_p = jnp.concatenate([jnp.zeros((n_p, 1, wd), a.dtype), ap[:, :-1]], axis=1)
    as_ = a[n_p * l_p:].reshape(n_s, l_s, wd)
    prev_s = jnp.concatenate([first_sample[:, None, :], as_[:, :-1]], axis=1)
    return jnp.concatenate([prev_p.reshape(-1, wd), prev_s.reshape(-1, wd)], axis=0)


def _last_rows(a, geo):
    n_p, l_p, n_s, l_s = geo
    wd = a.shape[1]
    return (a[:n_p * l_p].reshape(n_p, l_p, wd)[:, -1], a[n_p * l_p:].reshape(n_s, l_s, wd)[:, -1])


def ab_layer(xb, geo, p, state_shift, state_wkv, wkv_seq0, state_re, state_im, *, tm, blk):
    n_p, l_p, n_s, l_s = geo
    tp, ts = n_p * l_p, n_s * l_s
    rkv = matmul([xb], [p["w_rkv"]], tm, 1024, name="ab_in_rkv")
    lora = matmul([xb], [p["w_lora"]], tm, LORA_PAD, name="ab_in_lora")
    u = matmul([xb], [p["w_u"]], tm, C_B, name="ab_in_u")
    rkv_prev = _shift_rows(rkv, state_shift[:, :3 * C_A], geo)
    lora_prev = _shift_rows(lora, _lora_layout(state_shift), geo)
    prep = rwkv_prep(rkv, rkv_prev, lora, lora_prev, p, tm)
    rows, g, bonus = prep[:RWKV_ROWS], prep[RWKV_ROWS], prep[RWKV_ROWS + 1]
    o_p, wkv_p = wkv_long(rows, n_seq=n_p, seq_len=l_p, blk=blk)
    o_s, wkv_s = wkv_short(rows, state_wkv, row0=tp, n_seq=n_s, seq_len=l_s, blk=blk, s0_seq0=wkv_seq0)
    o_wkv = jnp.concatenate([o_p.reshape(tp, C_A), o_s], axis=0)
    o_a = rwkv_post(o_wkv, bonus, g, p, tm)
    dummy = jnp.zeros((SUBLANES, S5_STATE), F32)
    ob_p, hr_p, hi_p = s5_mix(u, p, dummy, dummy, row0=0, n_seq=n_p, seq_len=l_p, blk=blk)
    h0r = jnp.repeat(state_re.reshape(n_s, S5_STATE), l_s, axis=0)
    h0i = jnp.repeat(state_im.reshape(n_s, S5_STATE), l_s, axis=0)
    ob_s, hr_s, hi_s = s5_mix(u, p, h0r, h0i, row0=tp, n_seq=n_s, seq_len=l_s, blk=blk)
    o_b = jnp.concatenate([ob_p, ob_s], axis=0)
    h = matmul([o_a, o_b], [p["w_out_a"], p["w_out_b"]], tm, 1024, name="ab_out")
    rkv_lp, rkv_ls = _last_rows(rkv, geo)
    lo_lp, lo_ls = _last_rows(lora, geo)
    shift_p = jnp.concatenate([rkv_lp, _lora_unlayout(lo_lp)], axis=1)
    shift_s = jnp.concatenate([rkv_ls, _lora_unlayout(lo_ls)], axis=1)
    grp = lambda a, n: a.reshape(n, S5_GROUPS, S5_N)
    ssm_p = (grp(hr_p[:, SUBLANES - 1], n_p), grp(hi_p[:, SUBLANES - 1], n_p))
    ssm_s = (grp(hr_s[l_s - 1::l_s], n_s), grp(hi_s[l_s - 1::l_s], n_s))
    return h, (shift_p, shift_s), (wkv_p, wkv_s), ssm_p, ssm_s


def _router_kernel(x_ref, wr_ref, br_ref, ltri_ref, eidx_o, gate_o, rank_o, cnt_o, carry):
    i = pl.program_id(0)

    @pl.when(i == 0)
    def _():
        carry[...] = jnp.zeros(carry.shape, F32)

    tm = x_ref.shape[0]
    neg = -jnp.inf
    lane_i = lax.broadcasted_iota(I32, (tm, LANES), 1)
    lane = lane_i.astype(F32)
    per_group = N_EXPERTS // N_EXPERT_GROUPS
    grp = (lane_i // per_group).astype(F32)
    scores = jax.nn.sigmoid(_dot3(x_ref[...], wr_ref[...]))
    biased = jnp.where(lane_i < N_EXPERTS, scores + br_ref[...], neg)
    rmax = lambda a: jnp.max(a, axis=-1, keepdims=True)
    rmin = lambda a: jnp.min(a, axis=-1, keepdims=True)
    rsum = lambda a: jnp.sum(a, axis=-1, keepdims=True)

    gsc = jnp.full((tm, LANES), neg, F32)
    for g in range(N_EXPERT_GROUPS):
        xg = jnp.where(grp == g, biased, neg)
        m1 = rmax(xg)
        twice = rsum(jnp.where(xg == m1, 1.0, 0.0)) >= 2.0
        m2 = jnp.where(twice, m1, rmax(jnp.where(xg < m1, xg, neg)))
        gsc = jnp.where(lane == g, m1 + m2, gsc)
    keep = jnp.zeros((tm, LANES), jnp.bool_)
    for _ in range(TOPK_GROUPS):
        m = rmax(gsc)
        gi = rmin(jnp.where(gsc == m, lane, float(LANES)))
        keep = keep | (grp == gi)
        gsc = jnp.where(lane == gi, neg, gsc)
    cur = jnp.where(keep, biased, neg)

    sel = jnp.zeros((tm, LANES), F32)
    es, ss = [], []
    for _ in range(TOP_K):
        m = rmax(cur)
        ik = rmin(jnp.where(cur == m, lane, float(LANES)))
        hit = lane == ik
        ss.append(rsum(jnp.where(hit, scores, 0.0)))
        es.append(ik)
        cur = jnp.where(hit, neg, cur)
        sel = jnp.where(hit, 1.0, sel)
    tot = ss[0]
    for k in range(1, TOP_K):
        tot = tot + ss[k]

    prefix = _dot(ltri_ref[...], sel.astype(BF16)) + carry[0:1, :]
    eidx = jnp.zeros((tm, LANES), F32)
    gate = jnp.zeros((tm, LANES), F32)
    rank = jnp.zeros((tm, LANES), F32)
    for k in range(TOP_K):
        slot = lane == k
        eidx = jnp.where(slot, es[k], eidx)
        gate = jnp.where(slot, ss[k] / tot * ROUTED_SCALE, gate)
        rank = jnp.where(slot, rsum(jnp.where(lane == es[k], prefix, 0.0)), rank)
    eidx_o[...] = eidx.astype(I32)
    gate_o[...] = gate
    rank_o[...] = rank.astype(I32)
    carry[...] = carry[...] + jnp.sum(sel, axis=0, keepdims=True)
    cnt_o[...] = carry[...]


def moe_router(x, w_router_pad, b_router_pad, tm):
    t, d = x.shape
    ltri = (jnp.arange(tm)[:, None] > jnp.arange(tm)[None, :]).astype(BF16)
    row = pl.BlockSpec((tm, LANES), lambda i: (i, 0))
    out = jax.ShapeDtypeStruct((t, LANES), I32)
    return pl.pallas_call(
        _router_kernel,
        grid=(t // tm,),
        in_specs=[pl.BlockSpec((tm, d), lambda i: (i, 0)), pl.BlockSpec((d, LANES), lambda i: (0, 0)),
                  pl.BlockSpec((1, LANES), lambda i: (0, 0)), pl.BlockSpec((tm, tm), lambda i: (0, 0))],
        out_specs=[row, row, row, pl.BlockSpec((SUBLANES, LANES), lambda i: (0, 0))],
        out_shape=[out, jax.ShapeDtypeStruct((t, LANES), F32), out,
                   jax.ShapeDtypeStruct((SUBLANES, LANES), F32)],
        scratch_shapes=[pltpu.VMEM((SUBLANES, LANES), F32)],
        compiler_params=_cparams(("arbitrary",)),
        name="moe_router",
    )(x, w_router_pad, b_router_pad, ltri)


MOE_BM = 256


def _gather_rows_start(src_hbm, idx_ref, idx0, dst_buf, n_rows, sem):
    for j in range(n_rows):
        pltpu.make_async_copy(src_hbm.at[pl.ds(idx_ref[idx0 + j], 1), :], dst_buf.at[pl.ds(j, 1), :], sem).start()


def _gather_rows_wait(src_hbm, dst_buf, n_rows, sem):
    pltpu.make_async_copy(src_hbm.at[pl.ds(0, n_rows), :], dst_buf, sem).wait()


def _expert_kernel(blk_e_ref, nused_ref, tok_ref, x_hbm, wg_ref, wu_ref, wd_ref, ys_ref, xbuf, sems, wgb, wub, wdb):
    i = pl.program_id(0)
    n_used = nused_ref[0]

    @pl.when(i == 0)
    def _():
        _gather_rows_start(x_hbm, tok_ref, 0, xbuf.at[0], MOE_BM, sems.at[0])

    @pl.when(i < n_used)
    def _():
        e = blk_e_ref[i]
        prev = blk_e_ref[jnp.maximum(i - 1, 0)]

        @pl.when((i == 0) | (e != prev))
        def _():
            wgb[...] = wg_ref[0].astype(BF16)
            wub[...] = wu_ref[0].astype(BF16)
            wdb[...] = wd_ref[0].astype(BF16)

        cur = i % 2
        nxt = 1 - cur
        _gather_rows_wait(x_hbm, xbuf.at[cur], MOE_BM, sems.at[cur])
        nxt_row0 = jnp.minimum(i + 1, n_used - 1) * MOE_BM
        _gather_rows_start(x_hbm, tok_ref, nxt_row0, xbuf.at[nxt], MOE_BM, sems.at[nxt])
        x = xbuf[cur].astype(BF16)
        hg = _dot(x, wgb[...])
        hu = _dot(x, wub[...])
        h = hg * jax.nn.sigmoid(hg) * hu
        ys_ref[...] = _dot(h.astype(BF16), wdb[...])

        @pl.when(i == n_used - 1)
        def _():
            _gather_rows_wait(x_hbm, xbuf.at[nxt], MOE_BM, sems.at[nxt])

    @pl.when(i >= nused_ref[0])
    def _():
        ys_ref[...] = jnp.zeros(ys_ref.shape, F32)


def moe_experts(x, blk_e, n_used, tok_of_row, w_gate, w_up, w_down):
    d = x.shape[1]
    n_rows = tok_of_row.shape[0]
    n_blocks = n_rows // MOE_BM
    de = w_gate.shape[2]
    wmap = lambda i, be, nu, tk: (be[jnp.minimum(i, nu[0] - 1)], 0, 0)
    return pl.pallas_call(
        _expert_kernel,
        grid_spec=pltpu.PrefetchScalarGridSpec(
            num_scalar_prefetch=3,
            grid=(n_blocks,),
            in_specs=[pl.BlockSpec(memory_space=pl.ANY),
                      pl.BlockSpec((1, d, de), wmap), pl.BlockSpec((1, d, de), wmap), pl.BlockSpec((1, de, d), wmap)],
            out_specs=pl.BlockSpec((MOE_BM, d), lambda i, be, nu, tk: (i, 0)),
            scratch_shapes=[pltpu.VMEM((2, MOE_BM, d), F32), pltpu.SemaphoreType.DMA((2,)),
                            pltpu.VMEM((d, de), BF16), pltpu.VMEM((d, de), BF16), pltpu.VMEM((de, d), BF16)],
        ),
        out_shape=jax.ShapeDtypeStruct((n_rows, d), F32),
        compiler_params=_cparams(("arbitrary",)),
        name="moe_experts",
    )(blk_e, n_used, tok_of_row, x, w_gate, w_up, w_down)


def _shared_ffn_kernel(x_ref, wg_ref, wu_ref, wd_ref, o_ref):
    x = x_ref[...]
    hg = _dot(x, wg_ref[...])
    hu = _dot(x, wu_ref[...])
    h = hg * jax.nn.sigmoid(hg) * hu
    o_ref[...] = _dot(h.astype(BF16), wd_ref[...])


def shared_ffn(xb, wg, wu, wd, tm):
    t, d = xb.shape
    full = lambda a: pl.BlockSpec(a.shape, lambda i: (0, 0))
    row = pl.BlockSpec((tm, d), lambda i: (i, 0))
    return pl.pallas_call(
        _shared_ffn_kernel,
        grid=(t // tm,),
        in_specs=[row, full(wg), full(wu), full(wd)],
        out_specs=row,
        out_shape=jax.ShapeDtypeStruct((t, d), F32),
        compiler_params=_cparams(("parallel",)),
        name="shared_ffn",
    )(xb, wg, wu, wd)


def _combine_kernel(dest_ref, ys_hbm, gate_ref, x_ref, sh_ref, g_ref, b_ref, o_ref, ob_ref, buf, sems, *, tm):
    i = pl.program_id(0)
    n_asg = tm * TOP_K

    @pl.when(i == 0)
    def _():
        _gather_rows_start(ys_hbm, dest_ref, 0, buf.at[0], n_asg, sems.at[0])

    @pl.when(i + 1 < pl.num_programs(0))
    def _():
        nxt = (i + 1) % 2
        _gather_rows_start(ys_hbm, dest_ref, (i + 1) * n_asg, buf.at[nxt], n_asg, sems.at[nxt])

    cur = i % 2
    _gather_rows_wait(ys_hbm, buf.at[cur], n_asg, sems.at[cur])
    gate = gate_ref[...]
    f = sh_ref[...]
    for k in range(TOP_K):
        f = f + gate[:, k:k + 1] * buf[cur, k * tm:(k + 1) * tm, :]
    z = ALPHA * x_ref[...] + f
    inv_d = 1.0 / z.shape[1]
    mu = _row_sum(z) * inv_d
    dlt = z - mu
    var = _row_sum(dlt * dlt) * inv_d
    y = dlt * lax.rsqrt(var + LN_EPS) * g_ref[...] + b_ref[...]
    o_ref[...] = y
    ob_ref[...] = y.astype(BF16)


def moe_combine(ys, dest_flat, gates, x, shared, g, b, tm):
    t, d = x.shape
    row = pl.BlockSpec((tm, d), lambda i, dr: (i, 0))
    vec = pl.BlockSpec((1, d), lambda i, dr: (0, 0))
    return pl.pallas_call(
        functools.partial(_combine_kernel, tm=tm),
        grid_spec=pltpu.PrefetchScalarGridSpec(
            num_scalar_prefetch=1,
            grid=(t // tm,),
            in_specs=[pl.BlockSpec(memory_space=pl.ANY), pl.BlockSpec((tm, LANES), lambda i, dr: (i, 0)),
                      row, row, vec, vec],
            out_specs=[row, row],
            scratch_shapes=[pltpu.VMEM((2, TOP_K * tm, d), F32), pltpu.SemaphoreType.DMA((2,))],
        ),
        out_shape=[jax.ShapeDtypeStruct((t, d), F32), jax.ShapeDtypeStruct((t, d), BF16)],
        compiler_params=_cparams(("arbitrary",)),
        name="moe_combine",
    )(dest_flat, ys, gates, x, shared, g.reshape(1, d), b.reshape(1, d))


def moe_params(layer, w):
    p = {}
    p["w_router"] = _pad_cols(w["moe_w_router"][layer], LANES)
    p["b_router"] = _pad_cols(w["moe_b_router"][layer][None, :], LANES)
    stack = lambda a: a.reshape((-1,) + a.shape[2:])
    p["w_gate"], p["w_up"], p["w_down"] = stack(w["moe_w_gate"]), stack(w["moe_w_up"]), stack(w["moe_w_down"])
    p["e0"] = layer * N_EXPERTS
    p["ws_gate"] = w["moe_ws_gate"][layer].astype(BF16)
    p["ws_up"] = w["moe_ws_up"][layer].astype(BF16)
    p["ws_down"] = w["moe_ws_down"][layer].astype(BF16)
    return p


def moe_layer(y, yb, p, ln_g, ln_b, *, tm, tm_dma):
    t = y.shape[0]
    eidx, gates, rank, cnt = moe_router(y, p["w_router"], p["b_router"], tm)
    counts = cnt[0, :N_EXPERTS].astype(I32)
    padded = (counts + MOE_BM - 1) // MOE_BM * MOE_BM
    pad_end = jnp.cumsum(padded)
    pad_start = pad_end - padded
    e8 = eidx[:, :TOP_K]
    dest = pad_start[e8] + rank[:, :TOP_K]
    n_blocks = -(-(t * TOP_K + N_EXPERTS * (MOE_BM - 1)) // MOE_BM)
    n_rows = n_blocks * MOE_BM
    blk_start = jnp.arange(n_blocks, dtype=I32) * MOE_BM
    blk_e = jnp.sum((pad_end[None, :] <= blk_start[:, None]).astype(I32), axis=1)
    blk_e = jnp.minimum(blk_e, N_EXPERTS - 1) + p["e0"]
    n_used = (pad_end[-1:] // MOE_BM).astype(I32)
    tok = jnp.broadcast_to(jnp.arange(t, dtype=I32)[:, None], (t, TOP_K))
    tok_of_row = jnp.zeros((n_rows,), I32).at[dest.reshape(-1)].set(tok.reshape(-1))
    ys = moe_experts(y, blk_e, n_used, tok_of_row, p["w_gate"], p["w_up"], p["w_down"])
    shared = shared_ffn(yb, p["ws_gate"], p["ws_up"], p["ws_down"], tm)
    dest_tiles = dest.reshape(t // tm_dma, tm_dma, TOP_K).transpose(0, 2, 1).reshape(-1)
    return moe_combine(ys, dest_tiles, gates, y, shared, ln_g, ln_b, tm_dma)


C_QKV = (H_C + 2 * KVH_C) * DH_C
C_QI = H_IDX * D_IDX
IDX_SCALE = (H_IDX ** -0.5) * (D_IDX ** -0.5)


def _rot_half64(x, lane):
    return jnp.where(lane % D_IDX < D_IDX // 2, pltpu.roll(x, LANES - D_IDX // 2, 1), pltpu.roll(x, D_IDX // 2, 1))


def _c_post_kernel(qkv_ref, qi_ref, kw_ref, c128_ref, s128_ref, c64_ref, s64_ref, lng_ref, lnb_ref,
                   qb_o, kf_o, kb_o, vb_o, qib_o, kwo_o):
    tm = qkv_ref.shape[0]
    c128, s128 = c128_ref[...], s128_ref[...]
    c64, s64 = c64_ref[...], s64_ref[...]
    lane = lax.broadcasted_iota(I32, (tm, LANES), 1)
    for h in range(H_C + KVH_C):
        x = qkv_ref[:, h * DH_C:(h + 1) * DH_C]
        y = x * c128 + pltpu.roll(x, DH_C // 2, 1) * s128
        if h < H_C:
            qb_o[:, h * DH_C:(h + 1) * DH_C] = (y * (DH_C ** -0.5)).astype(BF16)
        else:
            j = h - H_C
            kf_o[:, j * DH_C:(j + 1) * DH_C] = y
            kb_o[:, j * DH_C:(j + 1) * DH_C] = y.astype(BF16)
    vb_o[...] = qkv_ref[:, (H_C + KVH_C) * DH_C:].astype(BF16)
    for j in range(C_QI // LANES):
        x = qi_ref[:, j * LANES:(j + 1) * LANES]
        qib_o[:, j * LANES:(j + 1) * LANES] = (x * c64 + _rot_half64(x, lane) * s64).astype(BF16)
    x = kw_ref[...]
    is_k = lane < D_IDX
    mu = jnp.sum(jnp.where(is_k, x, 0.0), axis=-1, keepdims=True) * (1.0 / D_IDX)
    d = jnp.where(is_k, x - mu, 0.0)
    var = jnp.sum(d * d, axis=-1, keepdims=True) * (1.0 / D_IDX)
    y = d * lax.rsqrt(var + LN_EPS) * lng_ref[...] + lnb_ref[...]
    y = y * c64 + _rot_half64(y, lane) * s64
    kwo_o[...] = jnp.where(is_k, y, x * IDX_SCALE)


def c_post(qkv, qi, kw, tabs, lng, lnb, tm):
    t = qkv.shape[0]
    row = lambda wd: pl.BlockSpec((tm, wd), lambda i: (i, 0))
    vec = pl.BlockSpec((1, LANES), lambda i: (0, 0))
    sh = lambda wd, dt: jax.ShapeDtypeStruct((t, wd), dt)
    return pl.pallas_call(
        _c_post_kernel,
        grid=(t // tm,),
        in_specs=[row(C_QKV), row(C_QI), row(LANES), row(LANES), row(LANES), row(LANES), row(LANES), vec, vec],
        out_specs=[row(H_C * DH_C), row(KVH_C * DH_C), row(KVH_C * DH_C), row(KVH_C * DH_C), row(C_QI), row(LANES)],
        out_shape=[sh(H_C * DH_C, BF16), sh(KVH_C * DH_C, F32), sh(KVH_C * DH_C, BF16), sh(KVH_C * DH_C, BF16),
                   sh(C_QI, BF16), sh(LANES, F32)],
        compiler_params=_cparams(("parallel",)),
        name="c_post",
    )(qkv, qi, kw, *tabs, lng, lnb)


def _rope_tables(pos):
    def tab(d):
        inv = ROPE_THETA ** (-jnp.arange(0, d, 2, dtype=F32) / d)
        ang = pos.astype(F32)[:, None] * inv[None, :]
        c, s = jnp.cos(ang), jnp.sin(ang)
        return jnp.concatenate([c, c], axis=1), jnp.concatenate([-s, s], axis=1)

    c128, s128 = tab(DH_C)
    c64, s64 = tab(D_IDX)
    return c128, s128, jnp.tile(c64, (1, 2)), jnp.tile(s64, (1, 2))


def _index_heads(qi, wi, kb):
    acc = None
    for h in range(H_IDX):
        d = lax.dot_general(qi[:, h * D_IDX:(h + 1) * D_IDX], kb, (((1,), (1,)), ((), ())),
                            preferred_element_type=F32)
        term = wi[:, D_IDX + h:D_IDX + h + 1] * jnp.maximum(d, 0.0)
        acc = term if acc is None else acc + term
    return acc + 0.0


SROWS = 16


def _topk_mask(sc_ref, utri_ref, o_ref, key_scr, *, topk, chunk, bits_per_step=1):
    rows, s_len = sc_ref.shape
    bits = lax.bitcast_convert_type(sc_ref[...], I32)
    key_scr[...] = bits ^ ((bits >> 31) & jnp.int32(0x7FFFFFFF))
    sign = jnp.int32(-2 ** 31)
    kf = jnp.float32(topk)

    def enough(cand):
        return _row_sum(jnp.where(key_scr[...] >= (cand ^ sign), 1.0, 0.0)) >= kf

    def bisect(it, t_u):
        low = (32 - bits_per_step) - bits_per_step * it
        for pattern in range(1, 2 ** bits_per_step):
            cand = t_u | (jnp.int32(pattern) << low)
            best = jnp.where(enough(cand), cand, t_u if pattern == 1 else best)
        return best

    t_u = lax.fori_loop(0, 32 // bits_per_step, bisect, jnp.zeros((rows, 1), I32))
    thr = t_u ^ sign
    need = kf - _row_sum(jnp.where(key_scr[...] > thr, 1.0, 0.0))
    utri = utri_ref[...]
    run = jnp.zeros((rows, 1), F32)
    for c in range(s_len // chunk):
        cs = slice(c * chunk, (c + 1) * chunk)
        keyc = key_scr[:, cs]
        eqc = jnp.where(keyc == thr, 1.0, 0.0)
        before = _dot(eqc.astype(BF16), utri) + run
        pick = (keyc > thr) | ((keyc == thr) & (before < need))
        o_ref[:, cs] = jnp.where(pick & (sc_ref[:, cs] > -jnp.inf), 1.0, 0.0).astype(o_ref.dtype)
        run = run + _row_sum(eqc)
    if o_ref.shape[1] > s_len:
        o_ref[:, s_len:] = jnp.zeros((rows, o_ref.shape[1] - s_len), o_ref.dtype)


def _strict_upper(chunk):
    return (jnp.arange(chunk)[:, None] < jnp.arange(chunk)[None, :]).astype(BF16)


SELECT_CLASSES = 4


def _index_select_kernel(qi_ref, wi_ref, keys_ref, utri_ref, o_ref, sc_scr, key_scr, *, q0, topk, chunk):
    qblk, width = sc_scr.shape
    kb = keys_ref[:width, :D_IDX].astype(BF16)
    sc = _index_heads(qi_ref[...], wi_ref[...], kb)
    row = lax.broadcasted_iota(I32, (qblk, width), 0) + (q0 + pl.program_id(1) * qblk)
    col = lax.broadcasted_iota(I32, (qblk, width), 1)
    sc_scr[...] = jnp.where(col <= row, sc, -jnp.inf)
    _topk_mask(sc_scr, utri_ref, o_ref, key_scr, topk=topk, chunk=chunk)


def index_select_causal(qib, kwo, topk, n_seq, seq_len, qblk, chunk):
    n_cls = SELECT_CLASSES if seq_len % (SELECT_CLASSES * max(chunk, qblk)) == 0 else 1
    cw = seq_len // n_cls
    utri = _strict_upper(chunk)
    parts = []
    for c in range(n_cls):
        width = (c + 1) * cw
        qmap = functools.partial(lambda b, i, c: (b * (seq_len // qblk) + c * (cw // qblk) + i, 0), c=c)
        parts.append(pl.pallas_call(
            functools.partial(_index_select_kernel, q0=c * cw, topk=topk, chunk=chunk),
            grid=(n_seq, cw // qblk),
            in_specs=[pl.BlockSpec((qblk, C_QI), qmap), pl.BlockSpec((qblk, LANES), qmap),
                      pl.BlockSpec((seq_len, LANES), lambda b, i: (b, 0)),
                      pl.BlockSpec((chunk, chunk), lambda b, i: (0, 0))],
            out_specs=pl.BlockSpec((qblk, seq_len), lambda b, i: (b * (cw // qblk) + i, 0)),
            out_shape=jax.ShapeDtypeStruct((n_seq * cw, seq_len), BF16),
            scratch_shapes=[pltpu.VMEM((qblk, width), F32), pltpu.VMEM((qblk, width), I32)],
            compiler_params=_cparams(("parallel", "parallel")),
            name="index_select_%d" % c,
        )(qib, kwo, kwo, utri).reshape(n_seq, cw, seq_len))
    return jnp.concatenate(parts, axis=1).reshape(n_seq * seq_len, seq_len)


NEG_BIG = -1e30


def _attn_init(m_scr, l_scr, acc_scr):
    m_scr[...] = jnp.full(m_scr.shape, NEG_BIG, F32)
    l_scr[...] = jnp.zeros(l_scr.shape, F32)
    acc_scr[...] = jnp.zeros(acc_scr.shape, F32)


def _attn_block(q, k, v, mask, m_scr, l_scr, acc_scr):
    bias = jnp.concatenate([(mask.astype(F32) - 1.0) * (-NEG_BIG)] * GROUP_C, axis=0)
    heads = range(KVH_C)
    ss = []
    for kh in heads:
        qh = jnp.concatenate([q[:, (kh * GROUP_C + g) * DH_C:(kh * GROUP_C + g + 1) * DH_C]
                              for g in range(GROUP_C)], axis=0)
        ks = k[:, kh * DH_C:(kh + 1) * DH_C]
        ss.append(lax.dot_general(qh, ks, (((1,), (1,)), ((), ())), preferred_element_type=F32) + bias)
    n_tiles = ss[0].shape[1] // LANES
    m_old = [m_scr[kh] for kh in heads]
    m_new = [jnp.maximum(m_old[kh], jnp.broadcast_to(_row_max(ss[kh]), m_old[kh].shape)) for kh in heads]
    ps = [[jnp.exp(ss[kh][:, i * LANES:(i + 1) * LANES] - m_new[kh]) for i in range(n_tiles)] for kh in heads]
    for kh in heads:
        alpha = jnp.exp(m_old[kh] - m_new[kh])
        l_scr[kh] = alpha * l_scr[kh] + functools.reduce(jnp.add, ps[kh])
        p = jnp.concatenate(ps[kh], axis=1).astype(BF16)
        acc_scr[kh] = alpha * acc_scr[kh] + _dot(p, v[:, kh * DH_C:(kh + 1) * DH_C])
        m_scr[kh] = m_new[kh]


def _attn_finish(o_ref, nq, l_scr, acc_scr, lead=None):
    for kh in range(KVH_C):
        out = acc_scr[kh] / jnp.sum(l_scr[kh], axis=-1, keepdims=True)
        for g in range(GROUP_C):
            hs = slice((kh * GROUP_C + g) * DH_C, (kh * GROUP_C + g + 1) * DH_C)
            val = out[g * nq:(g + 1) * nq, :].astype(o_ref.dtype)
            if lead is None:
                o_ref[:, hs] = val
            else:
                o_ref[lead, :, hs] = val


def _attn_prompt_kernel(q_ref, k_ref, v_ref, mask_ref, o_ref, m_scr, l_scr, acc_scr, *, qblk, kblk):
    qb = pl.program_id(1)
    kb = pl.program_id(2)

    @pl.when(kb == 0)
    def _():
        _attn_init(m_scr, l_scr, acc_scr)

    @pl.when(kb * kblk <= qb * qblk + qblk - 1)
    def _():
        _attn_block(q_ref[...], k_ref[...], v_ref[...], mask_ref[...], m_scr, l_scr, acc_scr)

    @pl.when(kb == pl.num_programs(2) - 1)
    def _():
        _attn_finish(o_ref, qblk, l_scr, acc_scr)


def attn_prompt(qb_, kb_, vb_, mask, n_p, l_p, qblk, kblk):
    nq, nk = l_p // qblk, l_p // kblk
    last = lambda q, k: jnp.minimum(k, (q * qblk + qblk - 1) // kblk)
    rows = GROUP_C * qblk
    return pl.pallas_call(
        functools.partial(_attn_prompt_kernel, qblk=qblk, kblk=kblk),
        grid=(n_p, nq, nk),
        in_specs=[pl.BlockSpec((qblk, H_C * DH_C), lambda b, q, k: (b * nq + q, 0)),
                  pl.BlockSpec((kblk, KVH_C * DH_C), lambda b, q, k: (b * nk + last(q, k), 0)),
                  pl.BlockSpec((kblk, KVH_C * DH_C), lambda b, q, k: (b * nk + last(q, k), 0)),
                  pl.BlockSpec((qblk, kblk), lambda b, q, k: (b * nq + q, last(q, k)))],
        out_specs=pl.BlockSpec((qblk, H_C * DH_C), lambda b, q, k: (b * nq + q, 0)),
        out_shape=jax.ShapeDtypeStruct((n_p * l_p, H_C * DH_C), BF16),
        scratch_shapes=[pltpu.VMEM((KVH_C, rows, LANES), F32), pltpu.VMEM((KVH_C, rows, LANES), F32),
                        pltpu.VMEM((KVH_C, rows, DH_C), F32)],
        compiler_params=_cparams(("parallel", "parallel", "arbitrary")),
        name="attn_prompt",
    )(qb_, kb_, vb_, mask)


def _dsa_sample_kernel(pt_ref, qi_ref, kw_ref, q_ref, kn_ref, vn_ref, utri_ref, kidx_hbm, k_hbm, v_hbm, o_ref,
                       kibuf, kbuf, vbuf, sems, sc_scr, key_scr, mask_scr, *, n_pages, page, l_s, topk, pool0):
    b = pl.program_id(0)
    past = n_pages * page
    s_len = past + page

    def page_copies(bb, slot, pg):
        phys = pt_ref[bb * n_pages + pg] + pool0
        rows = pl.ds(pg * page, page)
        cps = [pltpu.make_async_copy(kidx_hbm.at[phys], kibuf.at[slot, rows, :], sems.at[slot, 0])]
        for kh in range(KVH_C):
            cps.append(pltpu.make_async_copy(k_hbm.at[phys, :, kh, :], kbuf.at[slot, kh, rows, :], sems.at[slot, 1]))
            cps.append(pltpu.make_async_copy(v_hbm.at[phys, :, kh, :], vbuf.at[slot, kh, rows, :], sems.at[slot, 2]))
        return cps

    def fetch(bb, slot):
        for pg in range(n_pages):
            for cp in page_copies(bb, slot, pg):
                cp.start()

    @pl.when(b == 0)
    def _():
        kibuf[:, past:, :] = jnp.zeros((2, page, D_IDX), F32)
        kbuf[:, :, past:, :] = jnp.zeros((2, KVH_C, page, DH_C), F32)
        vbuf[:, :, past:, :] = jnp.zeros((2, KVH_C, page, DH_C), F32)
        fetch(0, 0)

    @pl.when(b + 1 < pl.num_programs(0))
    def _():
        fetch(b + 1, (b + 1) % 2)

    cur = b % 2
    for pg in range(n_pages):
        for cp in page_copies(b, cur, pg):
            cp.wait()
    kw = kw_ref[0]
    kibuf[cur, past:past + SROWS, :] = kw[:, :D_IDX]
    for kh in range(KVH_C):
        hs = slice(kh * DH_C, (kh + 1) * DH_C)
        kbuf[cur, kh, past:past + SROWS, :] = kn_ref[0][:, hs]
        vbuf[cur, kh, past:past + SROWS, :] = vn_ref[0][:, hs]

    dots = lax.dot_general(qi_ref[0], kibuf[cur].astype(BF16), (((1,), (1,)), ((), ())),
                           preferred_element_type=F32)
    sc = None
    for h in range(H_IDX):
        term = kw[:, D_IDX + h:D_IDX + h + 1] * jnp.maximum(dots[h * SROWS:(h + 1) * SROWS, :], 0.0)
        sc = term if sc is None else sc + term
    row = lax.broadcasted_iota(I32, (SROWS, s_len), 0)
    col = lax.broadcasted_iota(I32, (SROWS, s_len), 1)
    visible = (col < past) | ((col - past <= row) & (col - past < l_s))
    sc_scr[...] = jnp.where(visible, sc + 0.0, -jnp.inf)
    _topk_mask(sc_scr, utri_ref, mask_scr, key_scr, topk=topk, chunk=page, bits_per_step=2)

    bias = jnp.concatenate([(mask_scr[...] - 1.0) * (-NEG_BIG)] * GROUP_C, axis=0)
    q = q_ref[0]
    for kh in range(KVH_C):
        qh = jnp.concatenate([q[:, (kh * GROUP_C + g) * DH_C:(kh * GROUP_C + g + 1) * DH_C]
                              for g in range(GROUP_C)], axis=0)
        s = lax.dot_general(qh, kbuf[cur, kh].astype(BF16), (((1,), (1,)), ((), ())),
                            preferred_element_type=F32) + bias
        p = jnp.exp(s - _row_max(s))
        out = _dot(p.astype(BF16), vbuf[cur, kh].astype(BF16)) / _row_sum(p)
        for g in range(GROUP_C):
            o_ref[0, :, (kh * GROUP_C + g) * DH_C:(kh * GROUP_C + g + 1) * DH_C] = (
                out[g * SROWS:(g + 1) * SROWS, :].astype(o_ref.dtype))


def dsa_sample(page_table, qi_hm, kw3, q3, knew3, vnew3, cache_kidx3, cache_k4, cache_v4, *, pool0, l_s, topk):
    n_s, n_pages = page_table.shape
    page = cache_k4.shape[1]
    kvw = KVH_C * DH_C
    s_len = (n_pages + 1) * page
    own = lambda wd: pl.BlockSpec((1, SROWS, wd), lambda b, pt: (b, 0, 0))
    hbm = pl.BlockSpec(memory_space=pl.ANY)
    return pl.pallas_call(
        functools.partial(_dsa_sample_kernel, n_pages=n_pages, page=page, l_s=l_s, topk=topk, pool0=pool0),
        grid_spec=pltpu.PrefetchScalarGridSpec(
            num_scalar_prefetch=1,
            grid=(n_s,),
            in_specs=[pl.BlockSpec((1, H_IDX * SROWS, D_IDX), lambda b, pt: (b, 0, 0)),
                      own(LANES), own(H_C * DH_C), own(kvw), own(kvw),
                      pl.BlockSpec((page, page), lambda b, pt: (0, 0)), hbm, hbm, hbm],
            out_specs=own(H_C * DH_C),
            scratch_shapes=[pltpu.VMEM((2, s_len, D_IDX), F32), pltpu.VMEM((2, KVH_C, s_len, DH_C), F32),
                            pltpu.VMEM((2, KVH_C, s_len, DH_C), F32), pltpu.SemaphoreType.DMA((2, 3)),
                            pltpu.VMEM((SROWS, s_len), F32), pltpu.VMEM((SROWS, s_len), I32),
                            pltpu.VMEM((SROWS, s_len), F32)],
        ),
        out_shape=jax.ShapeDtypeStruct((n_s, SROWS, H_C * DH_C), BF16),
        compiler_params=_cparams(("arbitrary",)),
        name="dsa_sample",
    )(page_table.reshape(-1), qi_hm, kw3, q3, knew3, vnew3, _strict_upper(page), cache_kidx3, cache_k4, cache_v4)


def c_params(li, w):
    p = {}
    w_in = w["c_w_in"][li]
    p["w_qkv"] = w_in[:, :C_QKV].astype(BF16)
    p["w_qi"] = w_in[:, C_QKV:C_QKV + C_QI].astype(BF16)
    p["w_kw"] = _pad_cols(w_in[:, C_QKV + C_QI:], LANES).astype(BF16)
    p["ln_g"] = _pad_cols(w["c_kidx_ln_g"][li][None, :], LANES)
    p["ln_b"] = _pad_cols(w["c_kidx_ln_b"][li][None, :], LANES)
    p["w_out"] = w["c_w_out"][li].astype(BF16)
    return p


def _pad_sample_rows(a, n_s, l_s):
    a3 = a.reshape(n_s, l_s, a.shape[1])
    return jnp.pad(a3, ((0, 0), (0, SROWS - l_s), (0, 0)))


def c_layer(xb, geo, p, li, cache_k, cache_v, cache_kidx, page_table, *, tm, qblk, kblk):
    n_p, l_p, n_s, l_s = geo
    tp = n_p * l_p
    n_pool, page = cache_k.shape[1], cache_k.shape[2]
    n_pages = page_table.shape[1]
    past = n_pages * page
    qkv = matmul([xb], [p["w_qkv"]], tm, 1024, name="c_in_qkv")
    qi = matmul([xb], [p["w_qi"]], tm, C_QI, name="c_in_qi")
    kw = matmul([xb], [p["w_kw"]], tm, LANES, name="c_in_kw")
    pos = jnp.concatenate([jnp.tile(jnp.arange(l_p, dtype=I32), n_p),
                           jnp.tile(past + jnp.arange(l_s, dtype=I32), n_s)])
    q_b, k_f, k_b, v_b, qi_b, kwo = c_post(qkv, qi, kw, _rope_tables(pos), p["ln_g"], p["ln_b"], tm)
    v_f = qkv[:, (H_C + KVH_C) * DH_C:]
    mask_p = index_select_causal(qi_b, kwo, min(TOPK_MAX, l_p // 4), n_p, l_p, qblk, 256)
    o_p = attn_prompt(q_b, k_b, v_b, mask_p, n_p, l_p, qblk, kblk)
    pad = lambda a: _pad_sample_rows(a[tp:], n_s, l_s)
    qi_hm = pad(qi_b).reshape(n_s, SROWS, H_IDX, D_IDX).transpose(0, 2, 1, 3).reshape(n_s, H_IDX * SROWS, D_IDX)
    pool = lambda a: a.reshape((-1,) + a.shape[2:])
    o_s3 = dsa_sample(page_table, qi_hm, pad(kwo), pad(q_b), pad(k_f), pad(v_f),
                      pool(cache_kidx), pool(cache_k), pool(cache_v), pool0=li * n_pool, l_s=l_s,
                      topk=min(TOPK_MAX, (past + l_s) // 4))
    o = jnp.concatenate([o_p, o_s3[:, :l_s].reshape(n_s * l_s, -1)], axis=0)
    h = matmul([o], [p["w_out"]], tm, 1024, name="c_out")
    heads = lambda a, n, l: a.reshape(n, l, KVH_C, DH_C)
    new_p = (heads(k_f[:tp], n_p, l_p), heads(v_f[:tp], n_p, l_p), kwo[:tp, :D_IDX].reshape(n_p, l_p, D_IDX))
    new_s = (heads(k_f[tp:], n_s, l_s), heads(v_f[tp:], n_s, l_s), kwo[tp:, :D_IDX].reshape(n_s, l_s, D_IDX))
    return h, new_p, new_s


TM = 256
BLK = 128
TM_DMA = 64
QBLK = 128
KBLK = 512


def kernel(x_prompt, x_sample, state_shift, state_wkv, state_ssm_re, state_ssm_im, cache_k, cache_v, cache_kidx, page_table, ab_w_in, ab_mu, rwkv_w0, rwkv_w2, rwkv_a0, rwkv_a2, rwkv_g2, rwkv_k_k, rwkv_k_a, rwkv_r_k, rwkv_ln_g, rwkv_ln_b, s5_lam_re, s5_lam_im, s5_log_dt, s5_b_re, s5_b_im, s5_c_re, s5_c_im, s5_d, s5_w_glu, s5_b_glu, ab_w_out, c_w_in, c_kidx_ln_g, c_kidx_ln_b, c_w_out, ln_mix_g, ln_mix_b, ln_ffn_g, ln_ffn_b, moe_w_router, moe_b_router, moe_w_gate, moe_w_up, moe_w_down, moe_ws_gate, moe_ws_up, moe_ws_down):
    w = dict(locals())
    n_p, l_p, _ = x_prompt.shape
    n_s, l_s, _ = x_sample.shape
    geo = (n_p, l_p, n_s, l_s)
    tp = n_p * l_p
    y = jnp.concatenate([x_prompt.reshape(-1, D_MODEL), x_sample.reshape(-1, D_MODEL)], axis=0)
    yb = y.astype(BF16)
    outs = {k: [] for k in ("kp", "vp", "kip", "ks", "vs", "kis", "shp", "shs", "wkvp", "wkvs",
                            "srp", "sip", "srs", "sis")}
    for layer in range(DEPTH):
        li = layer // 2
        if layer % 2 == 0:
            wkv_all = state_wkv.reshape((-1,) + state_wkv.shape[2:])
            h, shift, wkv, ssm_p, ssm_s = ab_layer(yb, geo, ab_params(li, w, l_s), state_shift[li], wkv_all, li * n_s,
                                                   state_ssm_re[li], state_ssm_im[li], tm=TM, blk=BLK)
            for k, v in zip(("shp", "shs", "wkvp", "wkvs", "srp", "sip", "srs", "sis"),
                            (*shift, *wkv, *ssm_p, *ssm_s)):
                outs[k].append(v)
        else:
            h, new_p, new_s = c_layer(yb, geo, c_params(li, w), li, cache_k, cache_v, cache_kidx,
                                      page_table, tm=TM, qblk=QBLK, kblk=KBLK)
            for k, v in zip(("kp", "vp", "kip", "ks", "vs", "kis"), (*new_p, *new_s)):
                outs[k].append(v)
        y, yb = add_ln(y, h, ln_mix_g[layer], ln_mix_b[layer], TM)
        y, yb = moe_layer(y, yb, moe_params(layer, w), ln_ffn_g[layer], ln_ffn_b[layer], tm=TM, tm_dma=TM_DMA)
    st = lambda k: jnp.stack(outs[k])
    return (y[:tp].reshape(n_p, l_p, D_MODEL), y[tp:].reshape(n_s, l_s, D_MODEL),
            st("kp"), st("vp"), st("kip"), st("ks"), st("vs"), st("kis"),
            st("shp"), st("shs"), st("wkvp"), st("wkvs"), st("srp"), st("sip"), st("srs"), st("sis"))
```

```python
import functools
import math

import jax
import jax.numpy as jnp
from jax import lax
from jax.experimental import pallas as pl
from jax.experimental.pallas import tpu as pltpu

F32, BF16, I32 = jnp.float32, jnp.bfloat16, jnp.int32

D_MODEL = 2048
DEPTH = 2
ALPHA = (2.0 * DEPTH) ** 0.25
LN_EPS = 1e-5
C_A = 1024
HS_A = 64
H_A = 16
LORA_W, LORA_A, LORA_G = 64, 64, 160
A_PROJ = 3 * C_A + LORA_W + LORA_A + LORA_G
GN_EPS_A = 64e-5
C_B = 1024
S5_P = 16
S5_GROUPS = 64
S5_N = 64
S5_STATE = S5_GROUPS * S5_N
DH_C = 128
H_C = 16
KVH_C = 4
GROUP_C = 4
H_IDX = 16
D_IDX = 64
TOPK_MAX = 256
ROPE_THETA = 10000.0
N_EXPERTS = 64
TOP_K = 8
N_EXPERT_GROUPS = 8
TOPK_GROUPS = 4
D_EXPERT = 512
ROUTED_SCALE = 2.5

LANES = 128
SUBLANES = 8
VMEM_LIMIT = 56 * 1024 * 1024

LORA_PAD = 512
EXP_M05 = math.exp(-0.5)


def _cparams(sem):
    return pltpu.CompilerParams(dimension_semantics=sem, vmem_limit_bytes=VMEM_LIMIT)


def _dot(a, b):
    return jnp.dot(a, b, preferred_element_type=F32)


def _lane_fold(x, op):
    tiles = [x[:, i * LANES:(i + 1) * LANES] for i in range(x.shape[1] // LANES)]
    return functools.reduce(op, tiles)


def _row_sum(x):
    return jnp.sum(_lane_fold(x, jnp.add), axis=-1, keepdims=True)


def _row_max(x):
    return jnp.max(_lane_fold(x, jnp.maximum), axis=-1, keepdims=True)


def _split3(a):
    a1 = a.astype(BF16)
    r1 = a - a1.astype(F32)
    a2 = r1.astype(BF16)
    a3 = (r1 - a2.astype(F32)).astype(BF16)
    return a1, a2, a3


def _dot_sel(a, sel_bf16):
    a1, a2, a3 = _split3(a)
    return _dot(a1, sel_bf16) + _dot(a2, sel_bf16) + _dot(a3, sel_bf16)


def _dot3(a, b):
    a1 = a.astype(BF16)
    a2 = (a - a1.astype(F32)).astype(BF16)
    b1 = b.astype(BF16)
    b2 = (b - b1.astype(F32)).astype(BF16)
    return _dot(a1, b1) + _dot(a1, b2) + _dot(a2, b1)


def _mm_kernel(*refs, n_lhs):
    o_ref = refs[-1]
    acc = None
    for i in range(n_lhs):
        d = _dot(refs[i][...], refs[n_lhs + i][...])
        acc = d if acc is None else acc + d
    o_ref[...] = acc.astype(o_ref.dtype)


MM_TM = 512


def matmul(xs, ws, tm, tn, out_dtype=F32, name="mm"):
    m = xs[0].shape[0]
    n = ws[0].shape[1]
    tm = MM_TM if m % MM_TM == 0 else tm
    assert m % tm == 0 and n % tn == 0, (m, tm, n, tn)
    in_specs = ([pl.BlockSpec((tm, x.shape[1]), lambda j, i: (i, 0)) for x in xs]
                + [pl.BlockSpec((w.shape[0], tn), lambda j, i: (0, j)) for w in ws])
    return pl.pallas_call(
        functools.partial(_mm_kernel, n_lhs=len(xs)),
        grid=(n // tn, m // tm),
        in_specs=in_specs,
        out_specs=pl.BlockSpec((tm, tn), lambda j, i: (i, j)),
        out_shape=jax.ShapeDtypeStruct((m, n), out_dtype),
        compiler_params=_cparams(("parallel", "parallel")),
        name=name,
    )(*xs, *ws)


def _add_ln_kernel(x_ref, h_ref, g_ref, b_ref, o_ref, ob_ref):
    z = ALPHA * x_ref[...] + h_ref[...]
    inv_d = 1.0 / z.shape[1]
    mu = _row_sum(z) * inv_d
    d = z - mu
    var = _row_sum(d * d) * inv_d
    y = d * lax.rsqrt(var + LN_EPS) * g_ref[...] + b_ref[...]
    o_ref[...] = y
    ob_ref[...] = y.astype(BF16)


def add_ln(x, h, g, b, tm):
    t, d = x.shape
    row = pl.BlockSpec((tm, d), lambda i: (i, 0))
    vec = pl.BlockSpec((1, d), lambda i: (0, 0))
    return pl.pallas_call(
        _add_ln_kernel,
        grid=(t // tm,),
        in_specs=[row, row, vec, vec],
        out_specs=[row, row],
        out_shape=[jax.ShapeDtypeStruct((t, d), F32), jax.ShapeDtypeStruct((t, d), BF16)],
        compiler_params=_cparams(("parallel",)),
        name="add_ln",
    )(x, h, g.reshape(1, d), b.reshape(1, d))


def _rwkv_prep_kernel(rkv_ref, rkvp_ref, lo_ref, lop_ref, mu1_ref, mu2_ref, w0_ref, w2_ref, a0_ref, a2_ref,
                      g2_ref, kk_ref, ka_ref, rk_ref, gsel_ref, gselt_ref,
                      kk_o, wr_o, w_o, b_o, k_o, v_o, kr_o, g_o, bonus_o):
    x = rkv_ref[...]
    m = x + (rkvp_ref[...] - x) * mu1_ref[...]
    r = m[:, :C_A]
    k = m[:, C_A:2 * C_A]
    v = m[:, 2 * C_A:]
    l = lo_ref[...]
    lm = l + (lop_ref[...] - l) * mu2_ref[...]
    lw = lm[:, :LANES]
    la = lm[:, LANES:2 * LANES]
    lg = lm[:, 2 * LANES:]
    w_raw = w0_ref[...] + _dot3(jnp.tanh(lw), w2_ref[...])
    decay = jnp.exp(-EXP_M05 * jax.nn.sigmoid(w_raw))
    a = jax.nn.sigmoid(a0_ref[...] + _dot3(la, a2_ref[...]))
    g = _dot3(jax.nn.sigmoid(lg), g2_ref[...])
    gsel = gsel_ref[...]
    gselt = gselt_ref[...]
    kk = k * kk_ref[...]
    nrm = jnp.sqrt(_dot_sel(kk * kk, gsel))
    inv = 1.0 / jnp.maximum(nrm, 1e-12)
    kk = kk * _dot_sel(inv, gselt)
    k_mod = k * (1.0 + (a - 1.0) * ka_ref[...])
    sb = _dot_sel(r * k_mod * rk_ref[...], gsel)
    bonus_o[...] = _dot_sel(sb, gselt) * v
    g_o[...] = g
    b = kk * a
    kk_o[...] = kk
    br = _dot_sel(_dot_sel(b * r, gsel), gselt)
    wr_o[...] = decay * r - kk * br
    w_o[...] = decay
    b_o[...] = b
    k_o[...] = k_mod
    v_o[...] = v
    kr_o[...] = _dot_sel(_dot_sel(k_mod * r, gsel), gselt)


RWKV_ROWS = 7


def rwkv_prep(rkv, rkv_prev, lora, lora_prev, p, tm):
    t = rkv.shape[0]
    row = lambda w: pl.BlockSpec((tm, w), lambda i: (i, 0))
    full = lambda a: pl.BlockSpec(a.shape, lambda i: (0,) * a.ndim)
    consts = [p["mu_rkv"], p["mu_lora"], p["w0"], p["w2"], p["a0"], p["a2"], p["g2"], p["k_k"], p["k_a"], p["r_k"],
              p["gsel"], p["gselt"]]
    n_out = RWKV_ROWS + 2
    return pl.pallas_call(
        _rwkv_prep_kernel,
        grid=(t // tm,),
        in_specs=[row(3 * C_A), row(3 * C_A), row(LORA_PAD), row(LORA_PAD)] + [full(c) for c in consts],
        out_specs=[row(C_A)] * n_out,
        out_shape=[jax.ShapeDtypeStruct((t, C_A), F32)] * n_out,
        compiler_params=_cparams(("parallel",)),
        name="rwkv_prep",
    )(rkv, rkv_prev, lora, lora_prev, *consts)


TILE_HEADS = 4
TILE_W = TILE_HEADS * HS_A
N_TILES = H_A // TILE_HEADS
LHS_PER_TILE = 3
WKV_UNROLL = 4


def _wkv_steps(rows, row_base, n_steps, npar, s_scr, bd_ref, store_o):
    sub = lax.broadcasted_iota(I32, (HS_A, TILE_W), 0)
    lane = lax.broadcasted_iota(I32, (HS_A, TILE_W), 1)
    diag = (lane % HS_A) == sub
    bd = bd_ref[...]
    blk_rows = LHS_PER_TILE * HS_A

    def step(t, carry):
        vec = [[ref[pl.ds(row_base[s] + t, 1), :] for ref in rows[s]] for s in range(npar)]
        results = []
        for s in range(npar):
            kk, wr2, _, _, _, v, _ = vec[s]
            for p in range(N_TILES):
                ls = slice(p * TILE_W, (p + 1) * TILE_W)
                st = s_scr[s * N_TILES + p]
                q1 = (st * kk[:, ls]).astype(BF16)
                q2 = (st * wr2[:, ls]).astype(BF16)
                ve = jnp.where(diag, v[:, ls], 0.0).astype(BF16)
                results.append(_dot(jnp.concatenate([q1, q2, ve], axis=0), bd))
        for s in range(npar):
            _, _, w, b, k, v, kr = vec[s]
            o_parts = []
            for p in range(N_TILES):
                ls = slice(p * TILE_W, (p + 1) * TILE_W)
                res = results[s * N_TILES + p]
                z1 = res[:HS_A]
                z2 = res[HS_A:2 * HS_A]
                vb = res[2 * HS_A:]
                st = s_scr[s * N_TILES + p]
                s_scr[s * N_TILES + p] = st * w[:, ls] - z1 * b[:, ls] + vb * k[:, ls]
                z2r = jnp.sum(jnp.where(diag, z2, 0.0), axis=0, keepdims=True)
                o_parts.append(z2r + v[:, ls] * kr[:, ls])
            store_o(s, t, jnp.concatenate(o_parts, axis=1))
        return carry

    lax.fori_loop(0, n_steps, step, 0, unroll=WKV_UNROLL)


def _pack_state(s_ref, q, s_scr, slot):
    for p in range(N_TILES):
        s_scr[slot * N_TILES + p] = jnp.concatenate([s_ref[q, TILE_HEADS * p + h] for h in range(TILE_HEADS)], axis=1)


def _unpack_state(s_scr, slot, s_ref, q):
    for p in range(N_TILES):
        st = s_scr[slot * N_TILES + p]
        for h in range(TILE_HEADS):
            s_ref[q, TILE_HEADS * p + h] = st[:, h * HS_A:(h + 1) * HS_A]


def _wkv_long_kernel(*refs, npar, blk):
    n_in = npar * RWKV_ROWS
    rows = [refs[s * RWKV_ROWS:(s + 1) * RWKV_ROWS] for s in range(npar)]
    bd_ref, o_ref, sout_ref, s_scr = refs[n_in:n_in + 4]
    c = pl.program_id(0)

    @pl.when(c == 0)
    def _():
        s_scr[...] = jnp.zeros(s_scr.shape, F32)

    def store_o(s, t, o_row):
        o_ref[s, pl.ds(t, 1), :] = o_row

    _wkv_steps(rows, [0] * npar, blk, npar, s_scr, bd_ref, store_o)

    @pl.when(c == pl.num_programs(0) - 1)
    def _():
        for s in range(npar):
            _unpack_state(s_scr, s, sout_ref, s)


WKV_NPAR_SHORT = 2
WKV_SPB = 8


def _wkv_short_kernel(*refs, seq_len, spb):
    rows_refs = refs[:RWKV_ROWS]
    s0_ref, bd_ref, o_ref, sout_ref, s_scr = refs[RWKV_ROWS:RWKV_ROWS + 5]
    c = pl.program_id(1)
    npar = WKV_NPAR_SHORT

    def group(gq, carry):
        qs = [gq * npar + s for s in range(npar)]
        for s in range(npar):
            _pack_state(s0_ref, qs[s], s_scr, s)

        def store_o(s, t, o_row):
            o_ref[pl.ds((c * spb + qs[s]) * seq_len + t, 1), :] = o_row

        _wkv_steps([rows_refs] * npar, [(c * spb + q) * seq_len for q in qs], seq_len, npar, s_scr, bd_ref, store_o)
        for s in range(npar):
            _unpack_state(s_scr, s, sout_ref, qs[s])
        return carry

    lax.fori_loop(0, spb // npar, group, 0)


def _head_ones():
    head = jnp.arange(TILE_W) // HS_A
    return (head[:, None] == head[None, :]).astype(BF16)


def wkv_long(rows, *, n_seq, seq_len, blk):
    cps = seq_len // blk
    in_specs = [pl.BlockSpec((blk, C_A), functools.partial(lambda c, s: (s * cps + c, 0), s=s))
                for s in range(n_seq) for _ in range(RWKV_ROWS)]
    return pl.pallas_call(
        functools.partial(_wkv_long_kernel, npar=n_seq, blk=blk),
        grid=(cps,),
        in_specs=in_specs + [pl.BlockSpec((TILE_W, TILE_W), lambda c: (0, 0))],
        out_specs=[pl.BlockSpec((n_seq, blk, C_A), lambda c: (0, c, 0)),
                   pl.BlockSpec((n_seq, H_A, HS_A, HS_A), lambda c: (0, 0, 0, 0))],
        out_shape=[jax.ShapeDtypeStruct((n_seq, seq_len, C_A), F32),
                   jax.ShapeDtypeStruct((n_seq, H_A, HS_A, HS_A), F32)],
        scratch_shapes=[pltpu.VMEM((n_seq * N_TILES, HS_A, TILE_W), F32)],
        compiler_params=_cparams(("arbitrary",)),
        name="wkv_long",
    )(*(list(rows) * n_seq), _head_ones())


def wkv_short(rows, s0, *, row0, n_seq, seq_len, blk, s0_seq0):
    assert row0 % blk == 0 and blk % seq_len == 0
    b0 = row0 // blk
    nsb = blk // seq_len
    spb = min(WKV_SPB, nsb)
    assert n_seq % nsb == 0 and nsb % spb == 0 and spb % WKV_NPAR_SHORT == 0 and s0_seq0 % spb == 0
    sub = nsb // spb
    sspec = lambda off: pl.BlockSpec((spb, H_A, HS_A, HS_A), lambda s, c: (off + s * sub + c, 0, 0, 0))
    return pl.pallas_call(
        functools.partial(_wkv_short_kernel, seq_len=seq_len, spb=spb),
        grid=(n_seq // nsb, sub),
        in_specs=[pl.BlockSpec((blk, C_A), lambda s, c: (b0 + s, 0))] * RWKV_ROWS
        + [sspec(s0_seq0 // spb), pl.BlockSpec((TILE_W, TILE_W), lambda s, c: (0, 0))],
        out_specs=[pl.BlockSpec((blk, C_A), lambda s, c: (s, 0)), sspec(0)],
        out_shape=[jax.ShapeDtypeStruct((n_seq * seq_len, C_A), F32),
                   jax.ShapeDtypeStruct((n_seq, H_A, HS_A, HS_A), F32)],
        scratch_shapes=[pltpu.VMEM((WKV_NPAR_SHORT * N_TILES, HS_A, TILE_W), F32)],
        compiler_params=_cparams(("parallel", "arbitrary")),
        name="wkv_short",
    )(*rows, s0, _head_ones())


def _rwkv_post_kernel(o_in_ref, bonus_ref, g_ref, lng_ref, lnb_ref, gsel_ref, gselt_ref, o_ref):
    o = o_in_ref[...]
    gsel = gsel_ref[...]
    gselt = gselt_ref[...]
    mu = _dot_sel(o, gsel) * (1.0 / HS_A)
    d = o - _dot_sel(mu, gselt)
    var = _dot_sel(d * d, gsel) * (1.0 / HS_A)
    rstd = lax.rsqrt(var + GN_EPS_A)
    y = d * _dot_sel(rstd, gselt) * lng_ref[...] + lnb_ref[...]
    o_ref[...] = ((y + bonus_ref[...]) * g_ref[...]).astype(o_ref.dtype)


def rwkv_post(o_wkv, bonus, g, p, tm):
    t = bonus.shape[0]
    row = pl.BlockSpec((tm, C_A), lambda i: (i, 0))
    full = lambda a: pl.BlockSpec(a.shape, lambda i: (0,) * a.ndim)
    consts = [p["ln_g"], p["ln_b"], p["gsel"], p["gselt"]]
    return pl.pallas_call(
        _rwkv_post_kernel,
        grid=(t // tm,),
        in_specs=[row, row, row] + [full(c) for c in consts],
        out_specs=row,
        out_shape=jax.ShapeDtypeStruct((t, C_A), BF16),
        compiler_params=_cparams(("parallel",)),
        name="rwkv_post",
    )(o_wkv, bonus, g, *consts)


S5_KT = 4
S5_KU = C_B // S5_KT
S5_KH = S5_STATE // S5_KT
S5_LC = 512


def _gelu_tanh(x):
    return 0.5 * x * (1.0 + jnp.tanh(math.sqrt(2.0 / math.pi) * (x + 0.044715 * (x * x * x))))


def _s5_kernel(u_ref, wbr_ref, wbi_ref, lamk_ref, pre_ref, pim_ref, h0r_ref, h0i_ref, wcr_ref, wci_ref,
               dsk_ref, wglu_ref, bglu_ref, ob_ref, hr_out, hi_out, hre, him, car_re, car_im, y_scr,
               *, seq_len, blk):
    long_mode = seq_len >= blk
    period = SUBLANES if long_mode else seq_len
    u = u_ref[...]
    ub = u.astype(BF16)
    for kt in range(S5_KT):
        uk = ub[:, kt * S5_KU:(kt + 1) * S5_KU]
        hre[:, kt * S5_KH:(kt + 1) * S5_KH] = _dot(uk, wbr_ref[kt])
        him[:, kt * S5_KH:(kt + 1) * S5_KH] = _dot(uk, wbi_ref[kt])

    if long_mode:
        @pl.when(pl.program_id(1) == 0)
        def _():
            car_re[...] = jnp.zeros(car_re.shape, F32)
            car_im[...] = jnp.zeros(car_im.shape, F32)

    rowi = lax.broadcasted_iota(I32, (SUBLANES, 1), 0) % period
    for lc in range(S5_STATE // S5_LC):
        ls = slice(lc * S5_LC, (lc + 1) * S5_LC)
        lam = [(lamk_ref[2 * i:2 * i + 1, ls], lamk_ref[2 * i + 1:2 * i + 2, ls]) for i in range(3)]
        pre = pre_ref[:, ls]
        pim = pim_ref[:, ls]

        def tile(i, carry, ls=ls, lam=lam, pre=pre, pim=pim):
            r0 = pl.multiple_of(i * SUBLANES, SUBLANES)
            xr = hre[pl.ds(r0, SUBLANES), ls]
            xi = him[pl.ds(r0, SUBLANES), ls]
            for lvl, sh in enumerate((1, 2, 4)):
                if sh >= period:
                    break
                lr, li = lam[lvl]
                keep = rowi >= sh
                sr = jnp.where(keep, pltpu.roll(xr, sh, 0), 0.0)
                si = jnp.where(keep, pltpu.roll(xi, sh, 0), 0.0)
                xr, xi = xr + lr * sr - li * si, xi + lr * si + li * sr
            if long_mode:
                hr, hi = carry
            else:
                hr = h0r_ref[pl.ds(r0, SUBLANES), ls]
                hi = h0i_ref[pl.ds(r0, SUBLANES), ls]
            xr, xi = xr + pre * hr - pim * hi, xi + pre * hi + pim * hr
            hre[pl.ds(r0, SUBLANES), ls] = xr
            him[pl.ds(r0, SUBLANES), ls] = xi
            if long_mode:
                last_r = jnp.broadcast_to(xr[SUBLANES - 1:SUBLANES, :], xr.shape)
                last_i = jnp.broadcast_to(xi[SUBLANES - 1:SUBLANES, :], xi.shape)
                return (last_r, last_i)
            return carry

        if long_mode:
            init = (car_re[:, ls], car_im[:, ls])
            fin = lax.fori_loop(0, blk // SUBLANES, tile, init)
            car_re[:, ls] = fin[0]
            car_im[:, ls] = fin[1]
        else:
            lax.fori_loop(0, blk // SUBLANES, tile, 0)

    if long_mode:
        hr_out[0] = car_re[...]
        hi_out[0] = car_im[...]
    else:
        hr_out[...] = hre[...]
        hi_out[...] = him[...]

    for kt in range(S5_KT):
        hs = slice(kt * S5_KH, (kt + 1) * S5_KH)
        y_scr[:, kt * S5_KU:(kt + 1) * S5_KU] = (_dot(hre[:, hs].astype(BF16), wcr_ref[kt])
                                                 - _dot(him[:, hs].astype(BF16), wci_ref[kt]))
    y = y_scr[...] + dsk_ref[...] * u
    z = _gelu_tanh(y)
    gate = jax.nn.sigmoid(_dot(z.astype(BF16), wglu_ref[...]) + bglu_ref[...])
    ob_ref[...] = (z * gate).astype(ob_ref.dtype)


def s5_mix(u, p, h0r_rows, h0i_rows, *, row0, n_seq, seq_len, blk):
    long_mode = seq_len >= blk
    assert row0 % blk == 0
    b0 = row0 // blk
    t_out = n_seq * seq_len
    if long_mode:
        cps = seq_len // blk
        grid = (n_seq, cps)
        umap = lambda s, c: (b0 + s * cps + c, 0)
        omap = lambda s, c: (s * cps + c, 0)
        hspec = pl.BlockSpec((1, SUBLANES, S5_STATE), lambda s, c: (s, 0, 0))
        hshape = jax.ShapeDtypeStruct((n_seq, SUBLANES, S5_STATE), F32)
        h0spec = pl.BlockSpec((SUBLANES, S5_STATE), lambda s, c: (0, 0))
    else:
        assert t_out % blk == 0
        grid = (t_out // blk, 1)
        umap = lambda s, c: (b0 + s, 0)
        omap = lambda s, c: (s, 0)
        hspec = pl.BlockSpec((blk, S5_STATE), omap)
        hshape = jax.ShapeDtypeStruct((t_out, S5_STATE), F32)
        h0spec = pl.BlockSpec((blk, S5_STATE), omap)
    full = lambda a: pl.BlockSpec(a.shape, lambda s, c: (0,) * a.ndim)
    pre, pim = (p["pow_re8"], p["pow_im8"]) if long_mode else (p["pow_re_s"], p["pow_im_s"])
    args = [u, p["wb_re"], p["wb_im"], p["lamk"], pre, pim, h0r_rows, h0i_rows, p["wc_re"], p["wc_im"],
            p["d_skip"], p["w_glu"], p["b_glu"]]
    in_specs = [pl.BlockSpec((blk, C_B), umap)] + [full(a) for a in args[1:6]] + [h0spec, h0spec] \
        + [full(a) for a in args[8:]]
    return pl.pallas_call(
        functools.partial(_s5_kernel, seq_len=seq_len, blk=blk),
        grid=grid,
        in_specs=in_specs,
        out_specs=[pl.BlockSpec((blk, C_B), omap), hspec, hspec],
        out_shape=[jax.ShapeDtypeStruct((t_out, C_B), BF16), hshape, hshape],
        scratch_shapes=[pltpu.VMEM((blk, S5_STATE), F32), pltpu.VMEM((blk, S5_STATE), F32),
                        pltpu.VMEM((SUBLANES, S5_STATE), F32), pltpu.VMEM((SUBLANES, S5_STATE), F32),
                        pltpu.VMEM((blk, C_B), F32)],
        compiler_params=_cparams(("parallel", "arbitrary")),
        name="s5_long" if long_mode else "s5_short",
    )(*args)


def _pad_cols(a, n):
    return jnp.pad(a, ((0, 0), (0, n - a.shape[1])))


def _pad_rows(a, n):
    return jnp.pad(a, ((0, n - a.shape[0]), (0, 0)))


def _lora_layout(a):
    o = 3 * C_A
    return jnp.concatenate([_pad_cols(a[:, o:o + LORA_W], LANES),
                            _pad_cols(a[:, o + LORA_W:o + LORA_W + LORA_A], LANES),
                            _pad_cols(a[:, o + LORA_W + LORA_A:A_PROJ], 2 * LANES)], axis=1)


def _lora_unlayout(a):
    return jnp.concatenate([a[:, :LORA_W], a[:, LANES:LANES + LORA_A], a[:, 2 * LANES:2 * LANES + LORA_G]], axis=1)


def _cpow_table(lbr, lbi, n):
    res_r, res_i = [lbr], [lbi]
    for _ in range(n - 1):
        pr, pi = res_r[-1], res_i[-1]
        res_r.append(pr * lbr - pi * lbi)
        res_i.append(pr * lbi + pi * lbr)
    return (jnp.stack([x.reshape(-1) for x in res_r]), jnp.stack([x.reshape(-1) for x in res_i]))


def ab_params(li, w, sample_len):
    p = {}
    w_in = w["ab_w_in"][li]
    p["w_rkv"] = w_in[:, :3 * C_A].astype(BF16)
    p["w_lora"] = _lora_layout(w_in).astype(BF16)
    p["w_u"] = w_in[:, A_PROJ:].astype(BF16)
    mu = w["ab_mu"][li][None, :]
    p["mu_rkv"] = mu[:, :3 * C_A]
    p["mu_lora"] = _lora_layout(mu)
    row = lambda a: a.reshape(1, -1)
    p["w0"] = row(w["rwkv_w0"][li])
    p["w2"] = _pad_rows(w["rwkv_w2"][li], LANES)
    p["a0"] = row(w["rwkv_a0"][li])
    p["a2"] = _pad_rows(w["rwkv_a2"][li], LANES)
    p["g2"] = _pad_rows(w["rwkv_g2"][li], 2 * LANES)
    p["k_k"] = row(w["rwkv_k_k"][li])
    p["k_a"] = row(w["rwkv_k_a"][li])
    p["r_k"] = row(w["rwkv_r_k"][li])
    p["ln_g"] = row(w["rwkv_ln_g"][li])
    p["ln_b"] = row(w["rwkv_ln_b"][li])
    head_of_col = jnp.arange(C_A) // HS_A
    gsel = (head_of_col[:, None] == jnp.arange(LANES)[None, :])
    p["gsel"] = gsel.astype(BF16)
    p["gselt"] = gsel.T.astype(BF16)
    lr = jnp.minimum(w["s5_lam_re"][li], -1e-4)
    lim = w["s5_lam_im"][li]
    dt = jnp.exp(w["s5_log_dt"][li])[:, None]
    mag = jnp.exp(lr * dt)
    lbr, lbi = mag * jnp.cos(lim * dt), mag * jnp.sin(lim * dt)
    den = lr * lr + lim * lim
    pr, pim = lbr - 1.0, lbi
    fr = (pr * lr + pim * lim) / den
    fi = (pim * lr - pr * lim) / den
    br, bi = w["s5_b_re"][li], w["s5_b_im"][li]
    bbr = fr[..., None] * br - fi[..., None] * bi
    bbi = fr[..., None] * bi + fi[..., None] * br
    gpc = S5_GROUPS // S5_KT
    eye = jnp.eye(gpc, dtype=F32)

    def blockdiag_in(bb):
        b4 = bb.reshape(S5_KT, gpc, S5_N, S5_P)
        return jnp.einsum("kgnp,gh->kgphn", b4, eye).reshape(S5_KT, S5_KU, S5_KH).astype(BF16)

    def blockdiag_out(cc):
        c4 = cc.reshape(S5_KT, gpc, S5_P, S5_N)
        return jnp.einsum("kgpn,gh->kgnhp", c4, eye).reshape(S5_KT, S5_KH, S5_KU).astype(BF16)

    p["wb_re"], p["wb_im"] = blockdiag_in(bbr), blockdiag_in(bbi)
    p["wc_re"], p["wc_im"] = blockdiag_out(w["s5_c_re"][li]), blockdiag_out(w["s5_c_im"][li])
    pw_r, pw_i = _cpow_table(lbr, lbi, SUBLANES)
    p["pow_re8"], p["pow_im8"] = pw_r, pw_i
    reps = SUBLANES // sample_len
    p["pow_re_s"] = jnp.tile(pw_r[:sample_len], (reps, 1))
    p["pow_im_s"] = jnp.tile(pw_i[:sample_len], (reps, 1))
    p["lamk"] = jnp.stack([pw_r[0], pw_i[0], pw_r[1], pw_i[1], pw_r[3], pw_i[3]])
    p["d_skip"] = row(w["s5_d"][li])
    p["w_glu"] = w["s5_w_glu"][li].astype(BF16)
    p["b_glu"] = row(w["s5_b_glu"][li])
    w_out = w["ab_w_out"][li]
    p["w_out_a"] = w_out[:C_A].astype(BF16)
    p["w_out_b"] = w_out[C_A:].astype(BF16)
    return p


def _shift_rows(a, first_sample, geo):
    n_p, l_p, n_s, l_s = geo
    wd = a.shape[1]
    ap = a[:n_p * l_p].reshape(n_p, l_p, wd)
    prev_p = jnp.concatenate([jnp.zeros((n_p, 1, wd), a.dtype), ap[:, :-1]], axis=1)
    as_ = a[n_p * l_p:].reshape(n_s, l_s, wd)
    prev_s = jnp.concatenate([first_sample[:, None, :], as_[:, :-1]], axis=1)
    return jnp.concatenate([prev_p.reshape(-1, wd), prev_s.reshape(-1, wd)], axis=0)


def _last_rows(a, geo):
    n_p, l_p, n_s, l_s = geo
    wd = a.shape[1]
    return (a[:n_p * l_p].reshape(n_p, l_p, wd)[:, -1], a[n_p * l_p:].reshape(n_s, l_s, wd)[:, -1])


def ab_layer(xb, geo, p, state_shift, state_wkv, wkv_seq0, state_re, state_im, *, tm, blk):
    n_p, l_p, n_s, l_s = geo
    tp, ts = n_p * l_p, n_s * l_s
    rkv = matmul([xb], [p["w_rkv"]], tm, 1024, name="ab_in_rkv")
    lora = matmul([xb], [p["w_lora"]], tm, LORA_PAD, name="ab_in_lora")
    u = matmul([xb], [p["w_u"]], tm, C_B, name="ab_in_u")
    rkv_prev = _shift_rows(rkv, state_shift[:, :3 * C_A], geo)
    lora_prev = _shift_rows(lora, _lora_layout(state_shift), geo)
    prep = rwkv_prep(rkv, rkv_prev, lora, lora_prev, p, tm)
    rows, g, bonus = prep[:RWKV_ROWS], prep[RWKV_ROWS], prep[RWKV_ROWS + 1]
    o_p, wkv_p = wkv_long(rows, n_seq=n_p, seq_len=l_p, blk=blk)
    o_s, wkv_s = wkv_short(rows, state_wkv, row0=tp, n_seq=n_s, seq_len=l_s, blk=blk, s0_seq0=wkv_seq0)
    o_wkv = jnp.concatenate([o_p.reshape(tp, C_A), o_s], axis=0)
    o_a = rwkv_post(o_wkv, bonus, g, p, tm)
    dummy = jnp.zeros((SUBLANES, S5_STATE), F32)
    ob_p, hr_p, hi_p = s5_mix(u, p, dummy, dummy, row0=0, n_seq=n_p, seq_len=l_p, blk=blk)
    h0r = jnp.repeat(state_re.reshape(n_s, S5_STATE), l_s, axis=0)
    h0i = jnp.repeat(state_im.reshape(n_s, S5_STATE), l_s, axis=0)
    ob_s, hr_s, hi_s = s5_mix(u, p, h0r, h0i, row0=tp, n_seq=n_s, seq_len=l_s, blk=blk)
    o_b = jnp.concatenate([ob_p, ob_s], axis=0)
    h = matmul([o_a, o_b], [p["w_out_a"], p["w_out_b"]], tm, 1024, name="ab_out")
    rkv_lp, rkv_ls = _last_rows(rkv, geo)
    lo_lp, lo_ls = _last_rows(lora, geo)
    shift_p = jnp.concatenate([rkv_lp, _lora_unlayout(lo_lp)], axis=1)
    shift_s = jnp.concatenate([rkv_ls, _lora_unlayout(lo_ls)], axis=1)
    grp = lambda a, n: a.reshape(n, S5_GROUPS, S5_N)
    ssm_p = (grp(hr_p[:, SUBLANES - 1], n_p), grp(hi_p[:, SUBLANES - 1], n_p))
    ssm_s = (grp(hr_s[l_s - 1::l_s], n_s), grp(hi_s[l_s - 1::l_s], n_s))
    return h, (shift_p, shift_s), (wkv_p, wkv_s), ssm_p, ssm_s


def _router_kernel(x_ref, wr_ref, br_ref, ltri_ref, eidx_o, gate_o, rank_o, cnt_o, carry):
    i = pl.program_id(0)

    @pl.when(i == 0)
    def _():
        carry[...] = jnp.zeros(carry.shape, F32)

    tm = x_ref.shape[0]
    neg = -jnp.inf
    lane_i = lax.broadcasted_iota(I32, (tm, LANES), 1)
    lane = lane_i.astype(F32)
    per_group = N_EXPERTS // N_EXPERT_GROUPS
    grp = (lane_i // per_group).astype(F32)
    scores = jax.nn.sigmoid(_dot3(x_ref[...], wr_ref[...]))
    biased = jnp.where(lane_i < N_EXPERTS, scores + br_ref[...], neg)
    rmax = lambda a: jnp.max(a, axis=-1, keepdims=True)
    rmin = lambda a: jnp.min(a, axis=-1, keepdims=True)
    rsum = lambda a: jnp.sum(a, axis=-1, keepdims=True)

    gsc = jnp.full((tm, LANES), neg, F32)
    for g in range(N_EXPERT_GROUPS):
        xg = jnp.where(grp == g, biased, neg)
        m1 = rmax(xg)
        twice = rsum(jnp.where(xg == m1, 1.0, 0.0)) >= 2.0
        m2 = jnp.where(twice, m1, rmax(jnp.where(xg < m1, xg, neg)))
        gsc = jnp.where(lane == g, m1 + m2, gsc)
    keep = jnp.zeros((tm, LANES), jnp.bool_)
    for _ in range(TOPK_GROUPS):
        m = rmax(gsc)
        gi = rmin(jnp.where(gsc == m, lane, float(LANES)))
        keep = keep | (grp == gi)
        gsc = jnp.where(lane == gi, neg, gsc)
    cur = jnp.where(keep, biased, neg)

    sel = jnp.zeros((tm, LANES), F32)
    es, ss = [], []
    for _ in range(TOP_K):
        m = rmax(cur)
        ik = rmin(jnp.where(cur == m, lane, float(LANES)))
        hit = lane == ik
        ss.append(rsum(jnp.where(hit, scores, 0.0)))
        es.append(ik)
        cur = jnp.where(hit, neg, cur)
        sel = jnp.where(hit, 1.0, sel)
    tot = ss[0]
    for k in range(1, TOP_K):
        tot = tot + ss[k]

    prefix = _dot(ltri_ref[...], sel.astype(BF16)) + carry[0:1, :]
    eidx = jnp.zeros((tm, LANES), F32)
    gate = jnp.zeros((tm, LANES), F32)
    rank = jnp.zeros((tm, LANES), F32)
    for k in range(TOP_K):
        slot = lane == k
        eidx = jnp.where(slot, es[k], eidx)
        gate = jnp.where(slot, ss[k] / tot * ROUTED_SCALE, gate)
        rank = jnp.where(slot, rsum(jnp.where(lane == es[k], prefix, 0.0)), rank)
    eidx_o[...] = eidx.astype(I32)
    gate_o[...] = gate
    rank_o[...] = rank.astype(I32)
    carry[...] = carry[...] + jnp.sum(sel, axis=0, keepdims=True)
    cnt_o[...] = carry[...]


def moe_router(x, w_router_pad, b_router_pad, tm):
    t, d = x.shape
    ltri = (jnp.arange(tm)[:, None] > jnp.arange(tm)[None, :]).astype(BF16)
    row = pl.BlockSpec((tm, LANES), lambda i: (i, 0))
    out = jax.ShapeDtypeStruct((t, LANES), I32)
    return pl.pallas_call(
        _router_kernel,
        grid=(t // tm,),
        in_specs=[pl.BlockSpec((tm, d), lambda i: (i, 0)), pl.BlockSpec((d, LANES), lambda i: (0, 0)),
                  pl.BlockSpec((1, LANES), lambda i: (0, 0)), pl.BlockSpec((tm, tm), lambda i: (0, 0))],
        out_specs=[row, row, row, pl.BlockSpec((SUBLANES, LANES), lambda i: (0, 0))],
        out_shape=[out, jax.ShapeDtypeStruct((t, LANES), F32), out,
                   jax.ShapeDtypeStruct((SUBLANES, LANES), F32)],
        scratch_shapes=[pltpu.VMEM((SUBLANES, LANES), F32)],
        compiler_params=_cparams(("arbitrary",)),
        name="moe_router",
    )(x, w_router_pad, b_router_pad, ltri)


MOE_BM = 256


def _gather_rows_start(src_hbm, idx_ref, idx0, dst_buf, n_rows, sem):
    for j in range(n_rows):
        pltpu.make_async_copy(src_hbm.at[pl.ds(idx_ref[idx0 + j], 1), :], dst_buf.at[pl.ds(j, 1), :], sem).start()


def _gather_rows_wait(src_hbm, dst_buf, n_rows, sem):
    pltpu.make_async_copy(src_hbm.at[pl.ds(0, n_rows), :], dst_buf, sem).wait()


def _expert_kernel(blk_e_ref, nused_ref, tok_ref, x_hbm, wg_ref, wu_ref, wd_ref, ys_ref, xbuf_a, xbuf_b, sems,
                   wgb, wub, wdb):
    i = pl.program_id(0)
    n_used = nused_ref[0]
    active = i < n_used

    @pl.when(i == 0)
    def _():
        _gather_rows_start(x_hbm, tok_ref, 0, xbuf_a, MOE_BM, sems.at[0])

    @pl.when(active)
    def _():
        e = blk_e_ref[i]
        prev = blk_e_ref[jnp.maximum(i - 1, 0)]

        @pl.when((i == 0) | (e != prev))
        def _():
            wgb[...] = wg_ref[0].astype(BF16)
            wub[...] = wu_ref[0].astype(BF16)
            wdb[...] = wd_ref[0].astype(BF16)

    def block(cur_buf, cur_sem, nxt_buf, nxt_sem):
        _gather_rows_wait(x_hbm, cur_buf, MOE_BM, cur_sem)
        nxt_row0 = jnp.minimum(i + 1, n_used - 1) * MOE_BM
        _gather_rows_start(x_hbm, tok_ref, nxt_row0, nxt_buf, MOE_BM, nxt_sem)
        x = cur_buf[...].astype(BF16)
        hg = _dot(x, wgb[...])
        hu = _dot(x, wub[...])
        h = hg * jax.nn.sigmoid(hg) * hu
        ys_ref[...] = _dot(h.astype(BF16), wdb[...])

        @pl.when(i == n_used - 1)
        def _():
            _gather_rows_wait(x_hbm, nxt_buf, MOE_BM, nxt_sem)

    @pl.when(active & (i % 2 == 0))
    def _():
        block(xbuf_a, sems.at[0], xbuf_b, sems.at[1])

    @pl.when(active & (i % 2 == 1))
    def _():
        block(xbuf_b, sems.at[1], xbuf_a, sems.at[0])

    @pl.when(i >= nused_ref[0])
    def _():
        ys_ref[...] = jnp.zeros(ys_ref.shape, F32)


def moe_experts(x, blk_e, n_used, tok_of_row, w_gate, w_up, w_down):
    d = x.shape[1]
    n_rows = tok_of_row.shape[0]
    n_blocks = n_rows // MOE_BM
    de = w_gate.shape[2]
    wmap = lambda i, be, nu, tk: (be[jnp.minimum(i, nu[0] - 1)], 0, 0)
    return pl.pallas_call(
        _expert_kernel,
        grid_spec=pltpu.PrefetchScalarGridSpec(
            num_scalar_prefetch=3,
            grid=(n_blocks,),
            in_specs=[pl.BlockSpec(memory_space=pl.ANY),
                      pl.BlockSpec((1, d, de), wmap), pl.BlockSpec((1, d, de), wmap), pl.BlockSpec((1, de, d), wmap)],
            out_specs=pl.BlockSpec((MOE_BM, d), lambda i, be, nu, tk: (i, 0)),
            scratch_shapes=[pltpu.VMEM((MOE_BM, d), F32), pltpu.VMEM((MOE_BM, d), F32), pltpu.SemaphoreType.DMA((2,)),
                            pltpu.VMEM((d, de), BF16), pltpu.VMEM((d, de), BF16), pltpu.VMEM((de, d), BF16)],
        ),
        out_shape=jax.ShapeDtypeStruct((n_rows, d), F32),
        compiler_params=_cparams(("arbitrary",)),
        name="moe_experts",
    )(blk_e, n_used, tok_of_row, x, w_gate, w_up, w_down)


def _shared_ffn_kernel(x_ref, wg_ref, wu_ref, wd_ref, o_ref):
    x = x_ref[...]
    hg = _dot(x, wg_ref[...])
    hu = _dot(x, wu_ref[...])
    h = hg * jax.nn.sigmoid(hg) * hu
    o_ref[...] = _dot(h.astype(BF16), wd_ref[...])


def shared_ffn(xb, wg, wu, wd, tm):
    t, d = xb.shape
    full = lambda a: pl.BlockSpec(a.shape, lambda i: (0, 0))
    row = pl.BlockSpec((tm, d), lambda i: (i, 0))
    return pl.pallas_call(
        _shared_ffn_kernel,
        grid=(t // tm,),
        in_specs=[row, full(wg), full(wu), full(wd)],
        out_specs=row,
        out_shape=jax.ShapeDtypeStruct((t, d), F32),
        compiler_params=_cparams(("parallel",)),
        name="shared_ffn",
    )(xb, wg, wu, wd)


def _combine_kernel(dest_ref, ys_hbm, gate_ref, x_ref, sh_ref, g_ref, b_ref, o_ref, ob_ref, buf_a, buf_b, sems, *, tm):
    i = pl.program_id(0)
    n_tiles = pl.num_programs(0)
    n_asg = tm * TOP_K

    @pl.when(i == 0)
    def _():
        _gather_rows_start(ys_hbm, dest_ref, 0, buf_a, n_asg, sems.at[0])

    def tile(cur_buf, cur_sem, nxt_buf, nxt_sem):
        _gather_rows_wait(ys_hbm, cur_buf, n_asg, cur_sem)
        _gather_rows_start(ys_hbm, dest_ref, jnp.minimum(i + 1, n_tiles - 1) * n_asg, nxt_buf, n_asg, nxt_sem)
        gate = gate_ref[...]
        f = sh_ref[...]
        for k in range(TOP_K):
            f = f + gate[:, k:k + 1] * cur_buf[k * tm:(k + 1) * tm, :]
        z = ALPHA * x_ref[...] + f
        inv_d = 1.0 / z.shape[1]
        mu = _row_sum(z) * inv_d
        dlt = z - mu
        var = _row_sum(dlt * dlt) * inv_d
        y = dlt * lax.rsqrt(var + LN_EPS) * g_ref[...] + b_ref[...]
        o_ref[...] = y
        ob_ref[...] = y.astype(BF16)

        @pl.when(i == n_tiles - 1)
        def _():
            _gather_rows_wait(ys_hbm, nxt_buf, n_asg, nxt_sem)

    @pl.when(i % 2 == 0)
    def _():
        tile(buf_a, sems.at[0], buf_b, sems.at[1])

    @pl.when(i % 2 == 1)
    def _():
        tile(buf_b, sems.at[1], buf_a, sems.at[0])


def moe_combine(ys, dest_flat, gates, x, shared, g, b, tm):
    t, d = x.shape
    row = pl.BlockSpec((tm, d), lambda i, dr: (i, 0))
    vec = pl.BlockSpec((1, d), lambda i, dr: (0, 0))
    return pl.pallas_call(
        functools.partial(_combine_kernel, tm=tm),
        grid_spec=pltpu.PrefetchScalarGridSpec(
            num_scalar_prefetch=1,
            grid=(t // tm,),
            in_specs=[pl.BlockSpec(memory_space=pl.ANY), pl.BlockSpec((tm, LANES), lambda i, dr: (i, 0)),
                      row, row, vec, vec],
            out_specs=[row, row],
            scratch_shapes=[pltpu.VMEM((TOP_K * tm, d), F32), pltpu.VMEM((TOP_K * tm, d), F32),
                            pltpu.SemaphoreType.DMA((2,))],
        ),
        out_shape=[jax.ShapeDtypeStruct((t, d), F32), jax.ShapeDtypeStruct((t, d), BF16)],
        compiler_params=_cparams(("arbitrary",)),
        name="moe_combine",
    )(dest_flat, ys, gates, x, shared, g.reshape(1, d), b.reshape(1, d))


def moe_params(layer, w):
    p = {}
    p["w_router"] = _pad_cols(w["moe_w_router"][layer], LANES)
    p["b_router"] = _pad_cols(w["moe_b_router"][layer][None, :], LANES)
    stack = lambda a: a.reshape((-1,) + a.shape[2:])
    p["w_gate"], p["w_up"], p["w_down"] = stack(w["moe_w_gate"]), stack(w["moe_w_up"]), stack(w["moe_w_down"])
    p["e0"] = layer * N_EXPERTS
    p["ws_gate"] = w["moe_ws_gate"][layer].astype(BF16)
    p["ws_up"] = w["moe_ws_up"][layer].astype(BF16)
    p["ws_down"] = w["moe_ws_down"][layer].astype(BF16)
    return p


def moe_layer(y, yb, p, ln_g, ln_b, *, tm, tm_dma):
    t = y.shape[0]
    eidx, gates, rank, cnt = moe_router(y, p["w_router"], p["b_router"], tm)
    counts = cnt[0, :N_EXPERTS].astype(I32)
    padded = (counts + MOE_BM - 1) // MOE_BM * MOE_BM
    pad_end = jnp.cumsum(padded)
    pad_start = pad_end - padded
    e8 = eidx[:, :TOP_K]
    dest = pad_start[e8] + rank[:, :TOP_K]
    n_blocks = -(-(t * TOP_K + N_EXPERTS * (MOE_BM - 1)) // MOE_BM)
    n_rows = n_blocks * MOE_BM
    blk_start = jnp.arange(n_blocks, dtype=I32) * MOE_BM
    blk_e = jnp.sum((pad_end[None, :] <= blk_start[:, None]).astype(I32), axis=1)
    blk_e = jnp.minimum(blk_e, N_EXPERTS - 1) + p["e0"]
    n_used = (pad_end[-1:] // MOE_BM).astype(I32)
    tok = jnp.broadcast_to(jnp.arange(t, dtype=I32)[:, None], (t, TOP_K))
    tok_of_row = jnp.zeros((n_rows,), I32).at[dest.reshape(-1)].set(tok.reshape(-1))
    ys = moe_experts(y, blk_e, n_used, tok_of_row, p["w_gate"], p["w_up"], p["w_down"])
    shared = shared_ffn(yb, p["ws_gate"], p["ws_up"], p["ws_down"], tm)
    dest_tiles = dest.reshape(t // tm_dma, tm_dma, TOP_K).transpose(0, 2, 1).reshape(-1)
    return moe_combine(ys, dest_tiles, gates, y, shared, ln_g, ln_b, tm_dma)


C_QKV = (H_C + 2 * KVH_C) * DH_C
C_QI = H_IDX * D_IDX
IDX_SCALE = (H_IDX ** -0.5) * (D_IDX ** -0.5)


def _rot_half64(x, lane):
    return jnp.where(lane % D_IDX < D_IDX // 2, pltpu.roll(x, LANES - D_IDX // 2, 1), pltpu.roll(x, D_IDX // 2, 1))


def _c_post_kernel(qkv_ref, qi_ref, kw_ref, c128_ref, s128_ref, c64_ref, s64_ref, lng_ref, lnb_ref,
                   qb_o, kf_o, kb_o, vb_o, qib_o, kwo_o):
    tm = qkv_ref.shape[0]
    c128, s128 = c128_ref[...], s128_ref[...]
    c64, s64 = c64_ref[...], s64_ref[...]
    lane = lax.broadcasted_iota(I32, (tm, LANES), 1)
    for h in range(H_C + KVH_C):
        x = qkv_ref[:, h * DH_C:(h + 1) * DH_C]
        y = x * c128 + pltpu.roll(x, DH_C // 2, 1) * s128
        if h < H_C:
            qb_o[:, h * DH_C:(h + 1) * DH_C] = (y * (DH_C ** -0.5)).astype(BF16)
        else:
            j = h - H_C
            kf_o[:, j * DH_C:(j + 1) * DH_C] = y
            kb_o[:, j * DH_C:(j + 1) * DH_C] = y.astype(BF16)
    vb_o[...] = qkv_ref[:, (H_C + KVH_C) * DH_C:].astype(BF16)
    for j in range(C_QI // LANES):
        x = qi_ref[:, j * LANES:(j + 1) * LANES]
        qib_o[:, j * LANES:(j + 1) * LANES] = (x * c64 + _rot_half64(x, lane) * s64).astype(BF16)
    x = kw_ref[...]
    is_k = lane < D_IDX
    mu = jnp.sum(jnp.where(is_k, x, 0.0), axis=-1, keepdims=True) * (1.0 / D_IDX)
    d = jnp.where(is_k, x - mu, 0.0)
    var = jnp.sum(d * d, axis=-1, keepdims=True) * (1.0 / D_IDX)
    y = d * lax.rsqrt(var + LN_EPS) * lng_ref[...] + lnb_ref[...]
    y = y * c64 + _rot_half64(y, lane) * s64
    kwo_o[...] = jnp.where(is_k, y, x * IDX_SCALE)


def c_post(qkv, qi, kw, tabs, lng, lnb, tm):
    t = qkv.shape[0]
    row = lambda wd: pl.BlockSpec((tm, wd), lambda i: (i, 0))
    vec = pl.BlockSpec((1, LANES), lambda i: (0, 0))
    sh = lambda wd, dt: jax.ShapeDtypeStruct((t, wd), dt)
    return pl.pallas_call(
        _c_post_kernel,
        grid=(t // tm,),
        in_specs=[row(C_QKV), row(C_QI), row(LANES), row(LANES), row(LANES), row(LANES), row(LANES), vec, vec],
        out_specs=[row(H_C * DH_C), row(KVH_C * DH_C), row(KVH_C * DH_C), row(KVH_C * DH_C), row(C_QI), row(LANES)],
        out_shape=[sh(H_C * DH_C, BF16), sh(KVH_C * DH_C, F32), sh(KVH_C * DH_C, BF16), sh(KVH_C * DH_C, BF16),
                   sh(C_QI, BF16), sh(LANES, F32)],
        compiler_params=_cparams(("parallel",)),
        name="c_post",
    )(qkv, qi, kw, *tabs, lng, lnb)


def _rope_tables(pos):
    def tab(d):
        inv = ROPE_THETA ** (-jnp.arange(0, d, 2, dtype=F32) / d)
        ang = pos.astype(F32)[:, None] * inv[None, :]
        c, s = jnp.cos(ang), jnp.sin(ang)
        return jnp.concatenate([c, c], axis=1), jnp.concatenate([-s, s], axis=1)

    c128, s128 = tab(DH_C)
    c64, s64 = tab(D_IDX)
    return c128, s128, jnp.tile(c64, (1, 2)), jnp.tile(s64, (1, 2))


def _index_heads(qi, wi, kb):
    acc = None
    for h in range(H_IDX):
        d = lax.dot_general(qi[:, h * D_IDX:(h + 1) * D_IDX], kb, (((1,), (1,)), ((), ())),
                            preferred_element_type=F32)
        term = wi[:, D_IDX + h:D_IDX + h + 1] * jnp.maximum(d, 0.0)
        acc = term if acc is None else acc + term
    return acc + 0.0


SROWS = 16


def _topk_mask(sc_ref, utri_ref, o_ref, key_scr, *, topk, chunk, bits_per_step=1):
    rows, s_len = sc_ref.shape
    bits = lax.bitcast_convert_type(sc_ref[...], I32)
    key_scr[...] = bits ^ ((bits >> 31) & jnp.int32(0x7FFFFFFF))
    sign = jnp.int32(-2 ** 31)
    kf = jnp.float32(topk)

    def enough(cand):
        return _row_sum(jnp.where(key_scr[...] >= (cand ^ sign), 1.0, 0.0)) >= kf

    def bisect(it, t_u):
        low = (32 - bits_per_step) - bits_per_step * it
        for pattern in range(1, 2 ** bits_per_step):
            cand = t_u | (jnp.int32(pattern) << low)
            best = jnp.where(enough(cand), cand, t_u if pattern == 1 else best)
        return best

    t_u = lax.fori_loop(0, 32 // bits_per_step, bisect, jnp.zeros((rows, 1), I32))
    thr = t_u ^ sign
    need = kf - _row_sum(jnp.where(key_scr[...] > thr, 1.0, 0.0))
    utri = utri_ref[...]
    run = jnp.zeros((rows, 1), F32)
    for c in range(s_len // chunk):
        cs = slice(c * chunk, (c + 1) * chunk)
        keyc = key_scr[:, cs]
        eqc = jnp.where(keyc == thr, 1.0, 0.0)
        before = _dot(eqc.astype(BF16), utri) + run
        pick = (keyc > thr) | ((keyc == thr) & (before < need))
        o_ref[:, cs] = jnp.where(pick & (sc_ref[:, cs] > -jnp.inf), 1.0, 0.0).astype(o_ref.dtype)
        run = run + _row_sum(eqc)
    if o_ref.shape[1] > s_len:
        o_ref[:, s_len:] = jnp.zeros((rows, o_ref.shape[1] - s_len), o_ref.dtype)


def _strict_upper(chunk):
    return (jnp.arange(chunk)[:, None] < jnp.arange(chunk)[None, :]).astype(BF16)


SELECT_CLASSES = 4


def _index_select_kernel(qi_ref, wi_ref, keys_ref, utri_ref, o_ref, sc_scr, key_scr, *, q0, topk, chunk):
    qblk, width = sc_scr.shape
    kb = keys_ref[:width, :D_IDX].astype(BF16)
    sc = _index_heads(qi_ref[...], wi_ref[...], kb)
    row = lax.broadcasted_iota(I32, (qblk, width), 0) + (q0 + pl.program_id(1) * qblk)
    col = lax.broadcasted_iota(I32, (qblk, width), 1)
    sc_scr[...] = jnp.where(col <= row, sc, -jnp.inf)
    _topk_mask(sc_scr, utri_ref, o_ref, key_scr, topk=topk, chunk=chunk)


def index_select_causal(qib, kwo, topk, n_seq, seq_len, qblk, chunk):
    n_cls = SELECT_CLASSES if seq_len % (SELECT_CLASSES * max(chunk, qblk)) == 0 else 1
    cw = seq_len // n_cls
    utri = _strict_upper(chunk)
    parts = []
    for c in range(n_cls):
        width = (c + 1) * cw
        qmap = functools.partial(lambda b, i, c: (b * (seq_len // qblk) + c * (cw // qblk) + i, 0), c=c)
        parts.append(pl.pallas_call(
            functools.partial(_index_select_kernel, q0=c * cw, topk=topk, chunk=chunk),
            grid=(n_seq, cw // qblk),
            in_specs=[pl.BlockSpec((qblk, C_QI), qmap), pl.BlockSpec((qblk, LANES), qmap),
                      pl.BlockSpec((seq_len, LANES), lambda b, i: (b, 0)),
                      pl.BlockSpec((chunk, chunk), lambda b, i: (0, 0))],
            out_specs=pl.BlockSpec((qblk, seq_len), lambda b, i: (b * (cw // qblk) + i, 0)),
            out_shape=jax.ShapeDtypeStruct((n_seq * cw, seq_len), BF16),
            scratch_shapes=[pltpu.VMEM((qblk, width), F32), pltpu.VMEM((qblk, width), I32)],
            compiler_params=_cparams(("parallel", "parallel")),
            name="index_select_%d" % c,
        )(qib, kwo, kwo, utri).reshape(n_seq, cw, seq_len))
    return jnp.concatenate(parts, axis=1).reshape(n_seq * seq_len, seq_len)


NEG_BIG = -1e30


def _attn_init(m_scr, l_scr, acc_scr):
    m_scr[...] = jnp.full(m_scr.shape, NEG_BIG, F32)
    l_scr[...] = jnp.zeros(l_scr.shape, F32)
    acc_scr[...] = jnp.zeros(acc_scr.shape, F32)


def _attn_block(q, k, v, mask, m_scr, l_scr, acc_scr):
    bias = jnp.concatenate([(mask.astype(F32) - 1.0) * (-NEG_BIG)] * GROUP_C, axis=0)
    heads = range(KVH_C)
    ss = []
    for kh in heads:
        qh = jnp.concatenate([q[:, (kh * GROUP_C + g) * DH_C:(kh * GROUP_C + g + 1) * DH_C]
                              for g in range(GROUP_C)], axis=0)
        ks = k[:, kh * DH_C:(kh + 1) * DH_C]
        ss.append(lax.dot_general(qh, ks, (((1,), (1,)), ((), ())), preferred_element_type=F32) + bias)
    n_tiles = ss[0].shape[1] // LANES
    m_old = [m_scr[kh] for kh in heads]
    m_new = [jnp.maximum(m_old[kh], jnp.broadcast_to(_row_max(ss[kh]), m_old[kh].shape)) for kh in heads]
    ps = [[jnp.exp(ss[kh][:, i * LANES:(i + 1) * LANES] - m_new[kh]) for i in range(n_tiles)] for kh in heads]
    for kh in heads:
        alpha = jnp.exp(m_old[kh] - m_new[kh])
        l_scr[kh] = alpha * l_scr[kh] + functools.reduce(jnp.add, ps[kh])
        p = jnp.concatenate(ps[kh], axis=1).astype(BF16)
        acc_scr[kh] = alpha * acc_scr[kh] + _dot(p, v[:, kh * DH_C:(kh + 1) * DH_C])
        m_scr[kh] = m_new[kh]


def _attn_finish(o_ref, nq, l_scr, acc_scr, lead=None):
    for kh in range(KVH_C):
        out = acc_scr[kh] / jnp.sum(l_scr[kh], axis=-1, keepdims=True)
        for g in range(GROUP_C):
            hs = slice((kh * GROUP_C + g) * DH_C, (kh * GROUP_C + g + 1) * DH_C)
            val = out[g * nq:(g + 1) * nq, :].astype(o_ref.dtype)
            if lead is None:
                o_ref[:, hs] = val
            else:
                o_ref[lead, :, hs] = val


def _attn_prompt_kernel(q_ref, k_ref, v_ref, mask_ref, o_ref, m_scr, l_scr, acc_scr, *, qblk, kblk):
    qb = pl.program_id(1)
    kb = pl.program_id(2)

    @pl.when(kb == 0)
    def _():
        _attn_init(m_scr, l_scr, acc_scr)

    @pl.when(kb * kblk <= qb * qblk + qblk - 1)
    def _():
        _attn_block(q_ref[...], k_ref[...], v_ref[...], mask_ref[...], m_scr, l_scr, acc_scr)

    @pl.when(kb == pl.num_programs(2) - 1)
    def _():
        _attn_finish(o_ref, qblk, l_scr, acc_scr)


def attn_prompt(qb_, kb_, vb_, mask, n_p, l_p, qblk, kblk):
    nq, nk = l_p // qblk, l_p // kblk
    last = lambda q, k: jnp.minimum(k, (q * qblk + qblk - 1) // kblk)
    rows = GROUP_C * qblk
    return pl.pallas_call(
        functools.partial(_attn_prompt_kernel, qblk=qblk, kblk=kblk),
        grid=(n_p, nq, nk),
        in_specs=[pl.BlockSpec((qblk, H_C * DH_C), lambda b, q, k: (b * nq + q, 0)),
                  pl.BlockSpec((kblk, KVH_C * DH_C), lambda b, q, k: (b * nk + last(q, k), 0)),
                  pl.BlockSpec((kblk, KVH_C * DH_C), lambda b, q, k: (b * nk + last(q, k), 0)),
                  pl.BlockSpec((qblk, kblk), lambda b, q, k: (b * nq + q, last(q, k)))],
        out_specs=pl.BlockSpec((qblk, H_C * DH_C), lambda b, q, k: (b * nq + q, 0)),
        out_shape=jax.ShapeDtypeStruct((n_p * l_p, H_C * DH_C), BF16),
        scratch_shapes=[pltpu.VMEM((KVH_C, rows, LANES), F32), pltpu.VMEM((KVH_C, rows, LANES), F32),
                        pltpu.VMEM((KVH_C, rows, DH_C), F32)],
        compiler_params=_cparams(("parallel", "parallel", "arbitrary")),
        name="attn_prompt",
    )(qb_, kb_, vb_, mask)


def _dsa_sample_kernel(pt_ref, qi_ref, kw_ref, q_ref, kn_ref, vn_ref, utri_ref, kidx_hbm, k_hbm, v_hbm, o_ref,
                       kibuf, kbuf, vbuf, sems, sc_scr, key_scr, mask_scr, *, n_pages, page, l_s, topk, pool0):
    b = pl.program_id(0)
    past = n_pages * page
    s_len = past + page

    def page_copies(bb, slot, pg):
        phys = pt_ref[bb * n_pages + pg] + pool0
        rows = pl.ds(pg * page, page)
        cps = [pltpu.make_async_copy(kidx_hbm.at[phys], kibuf.at[slot, rows, :], sems.at[slot, 0])]
        for kh in range(KVH_C):
            cps.append(pltpu.make_async_copy(k_hbm.at[phys, :, kh, :], kbuf.at[slot, kh, rows, :], sems.at[slot, 1]))
            cps.append(pltpu.make_async_copy(v_hbm.at[phys, :, kh, :], vbuf.at[slot, kh, rows, :], sems.at[slot, 2]))
        return cps

    def fetch(bb, slot):
        for pg in range(n_pages):
            for cp in page_copies(bb, slot, pg):
                cp.start()

    @pl.when(b == 0)
    def _():
        kibuf[:, past:, :] = jnp.zeros((2, page, D_IDX), F32)
        kbuf[:, :, past:, :] = jnp.zeros((2, KVH_C, page, DH_C), F32)
        vbuf[:, :, past:, :] = jnp.zeros((2, KVH_C, page, DH_C), F32)
        fetch(0, 0)

    @pl.when(b + 1 < pl.num_programs(0))
    def _():
        fetch(b + 1, (b + 1) % 2)

    cur = b % 2
    for pg in range(n_pages):
        for cp in page_copies(b, cur, pg):
            cp.wait()
    kw = kw_ref[0]
    kibuf[cur, past:past + SROWS, :] = kw[:, :D_IDX]
    for kh in range(KVH_C):
        hs = slice(kh * DH_C, (kh + 1) * DH_C)
        kbuf[cur, kh, past:past + SROWS, :] = kn_ref[0][:, hs]
        vbuf[cur, kh, past:past + SROWS, :] = vn_ref[0][:, hs]

    dots = lax.dot_general(qi_ref[0], kibuf[cur].astype(BF16), (((1,), (1,)), ((), ())),
                           preferred_element_type=F32)
    sc = None
    for h in range(H_IDX):
        term = kw[:, D_IDX + h:D_IDX + h + 1] * jnp.maximum(dots[h * SROWS:(h + 1) * SROWS, :], 0.0)
        sc = term if sc is None else sc + term
    row = lax.broadcasted_iota(I32, (SROWS, s_len), 0)
    col = lax.broadcasted_iota(I32, (SROWS, s_len), 1)
    visible = (col < past) | ((col - past <= row) & (col - past < l_s))
    sc_scr[...] = jnp.where(visible, sc + 0.0, -jnp.inf)
    _topk_mask(sc_scr, utri_ref, mask_scr, key_scr, topk=topk, chunk=page, bits_per_step=2)

    bias = jnp.concatenate([(mask_scr[...] - 1.0) * (-NEG_BIG)] * GROUP_C, axis=0)
    q = q_ref[0]
    for kh in range(KVH_C):
        qh = jnp.concatenate([q[:, (kh * GROUP_C + g) * DH_C:(kh * GROUP_C + g + 1) * DH_C]
                              for g in range(GROUP_C)], axis=0)
        s = lax.dot_general(qh, kbuf[cur, kh].astype(BF16), (((1,), (1,)), ((), ())),
                            preferred_element_type=F32) + bias
        p = jnp.exp(s - _row_max(s))
        out = _dot(p.astype(BF16), vbuf[cur, kh].astype(BF16)) / _row_sum(p)
        for g in range(GROUP_C):
            o_ref[0, :, (kh * GROUP_C + g) * DH_C:(kh * GROUP_C + g + 1) * DH_C] = (
                out[g * SROWS:(g + 1) * SROWS, :].astype(o_ref.dtype))


def dsa_sample(page_table, qi_hm, kw3, q3, knew3, vnew3, cache_kidx3, cache_k4, cache_v4, *, pool0, l_s, topk):
    n_s, n_pages = page_table.shape
    page = cache_k4.shape[1]
    kvw = KVH_C * DH_C
    s_len = (n_pages + 1) * page
    own = lambda wd: pl.BlockSpec((1, SROWS, wd), lambda b, pt: (b, 0, 0))
    hbm = pl.BlockSpec(memory_space=pl.ANY)
    return pl.pallas_call(
        functools.partial(_dsa_sample_kernel, n_pages=n_pages, page=page, l_s=l_s, topk=topk, pool0=pool0),
        grid_spec=pltpu.PrefetchScalarGridSpec(
            num_scalar_prefetch=1,
            grid=(n_s,),
            in_specs=[pl.BlockSpec((1, H_IDX * SROWS, D_IDX), lambda b, pt: (b, 0, 0)),
                      own(LANES), own(H_C * DH_C), own(kvw), own(kvw),
                      pl.BlockSpec((page, page), lambda b, pt: (0, 0)), hbm, hbm, hbm],
            out_specs=own(H_C * DH_C),
            scratch_shapes=[pltpu.VMEM((2, s_len, D_IDX), F32), pltpu.VMEM((2, KVH_C, s_len, DH_C), F32),
                            pltpu.VMEM((2, KVH_C, s_len, DH_C), F32), pltpu.SemaphoreType.DMA((2, 3)),
                            pltpu.VMEM((SROWS, s_len), F32), pltpu.VMEM((SROWS, s_len), I32),
                            pltpu.VMEM((SROWS, s_len), F32)],
        ),
        out_shape=jax.ShapeDtypeStruct((n_s, SROWS, H_C * DH_C), BF16),
        compiler_params=_cparams(("arbitrary",)),
        name="dsa_sample",
    )(page_table.reshape(-1), qi_hm, kw3, q3, knew3, vnew3, _strict_upper(page), cache_kidx3, cache_k4, cache_v4)


def c_params(li, w):
    p = {}
    w_in = w["c_w_in"][li]
    p["w_qkv"] = w_in[:, :C_QKV].astype(BF16)
    p["w_qi"] = w_in[:, C_QKV:C_QKV + C_QI].astype(BF16)
    p["w_kw"] = _pad_cols(w_in[:, C_QKV + C_QI:], LANES).astype(BF16)
    p["ln_g"] = _pad_cols(w["c_kidx_ln_g"][li][None, :], LANES)
    p["ln_b"] = _pad_cols(w["c_kidx_ln_b"][li][None, :], LANES)
    p["w_out"] = w["c_w_out"][li].astype(BF16)
    return p


def _pad_sample_rows(a, n_s, l_s):
    a3 = a.reshape(n_s, l_s, a.shape[1])
    return jnp.pad(a3, ((0, 0), (0, SROWS - l_s), (0, 0)))


def c_layer(xb, geo, p, li, cache_k, cache_v, cache_kidx, page_table, *, tm, qblk, kblk):
    n_p, l_p, n_s, l_s = geo
    tp = n_p * l_p
    n_pool, page = cache_k.shape[1], cache_k.shape[2]
    n_pages = page_table.shape[1]
    past = n_pages * page
    qkv = matmul([xb], [p["w_qkv"]], tm, 1024, name="c_in_qkv")
    qi = matmul([xb], [p["w_qi"]], tm, C_QI, name="c_in_qi")
    kw = matmul([xb], [p["w_kw"]], tm, LANES, name="c_in_kw")
    pos = jnp.concatenate([jnp.tile(jnp.arange(l_p, dtype=I32), n_p),
                           jnp.tile(past + jnp.arange(l_s, dtype=I32), n_s)])
    q_b, k_f, k_b, v_b, qi_b, kwo = c_post(qkv, qi, kw, _rope_tables(pos), p["ln_g"], p["ln_b"], tm)
    v_f = qkv[:, (H_C + KVH_C) * DH_C:]
    mask_p = index_select_causal(qi_b, kwo, min(TOPK_MAX, l_p // 4), n_p, l_p, qblk, 256)
    o_p = attn_prompt(q_b, k_b, v_b, mask_p, n_p, l_p, qblk, kblk)
    pad = lambda a: _pad_sample_rows(a[tp:], n_s, l_s)
    qi_hm = pad(qi_b).reshape(n_s, SROWS, H_IDX, D_IDX).transpose(0, 2, 1, 3).reshape(n_s, H_IDX * SROWS, D_IDX)
    pool = lambda a: a.reshape((-1,) + a.shape[2:])
    o_s3 = dsa_sample(page_table, qi_hm, pad(kwo), pad(q_b), pad(k_f), pad(v_f),
                      pool(cache_kidx), pool(cache_k), pool(cache_v), pool0=li * n_pool, l_s=l_s,
                      topk=min(TOPK_MAX, (past + l_s) // 4))
    o = jnp.concatenate([o_p, o_s3[:, :l_s].reshape(n_s * l_s, -1)], axis=0)
    h = matmul([o], [p["w_out"]], tm, 1024, name="c_out")
    heads = lambda a, n, l: a.reshape(n, l, KVH_C, DH_C)
    new_p = (heads(k_f[:tp], n_p, l_p), heads(v_f[:tp], n_p, l_p), kwo[:tp, :D_IDX].reshape(n_p, l_p, D_IDX))
    new_s = (heads(k_f[tp:], n_s, l_s), heads(v_f[tp:], n_s, l_s), kwo[tp:, :D_IDX].reshape(n_s, l_s, D_IDX))
    return h, new_p, new_s


TM = 256
BLK = 128
TM_DMA = 64
QBLK = 128
KBLK = 512


def kernel(x_prompt, x_sample, state_shift, state_wkv, state_ssm_re, state_ssm_im, cache_k, cache_v, cache_kidx, page_table, ab_w_in, ab_mu, rwkv_w0, rwkv_w2, rwkv_a0, rwkv_a2, rwkv_g2, rwkv_k_k, rwkv_k_a, rwkv_r_k, rwkv_ln_g, rwkv_ln_b, s5_lam_re, s5_lam_im, s5_log_dt, s5_b_re, s5_b_im, s5_c_re, s5_c_im, s5_d, s5_w_glu, s5_b_glu, ab_w_out, c_w_in, c_kidx_ln_g, c_kidx_ln_b, c_w_out, ln_mix_g, ln_mix_b, ln_ffn_g, ln_ffn_b, moe_w_router, moe_b_router, moe_w_gate, moe_w_up, moe_w_down, moe_ws_gate, moe_ws_up, moe_ws_down):
    w = dict(locals())
    n_p, l_p, _ = x_prompt.shape
    n_s, l_s, _ = x_sample.shape
    geo = (n_p, l_p, n_s, l_s)
    tp = n_p * l_p
    y = jnp.concatenate([x_prompt.reshape(-1, D_MODEL), x_sample.reshape(-1, D_MODEL)], axis=0)
    yb = y.astype(BF16)
    outs = {k: [] for k in ("kp", "vp", "kip", "ks", "vs", "kis", "shp", "shs", "wkvp", "wkvs",
                            "srp", "sip", "srs", "sis")}
    for layer in range(DEPTH):
        li = layer // 2
        if layer % 2 == 0:
            wkv_all = state_wkv.reshape((-1,) + state_wkv.shape[2:])
            h, shift, wkv, ssm_p, ssm_s = ab_layer(yb, geo, ab_params(li, w, l_s), state_shift[li], wkv_all, li * n_s,
                                                   state_ssm_re[li], state_ssm_im[li], tm=TM, blk=BLK)
            for k, v in zip(("shp", "shs", "wkvp", "wkvs", "srp", "sip", "srs", "sis"),
                            (*shift, *wkv, *ssm_p, *ssm_s)):
                outs[k].append(v)
        else:
            h, new_p, new_s = c_layer(yb, geo, c_params(li, w), li, cache_k, cache_v, cache_kidx,
                                      page_table, tm=TM, qblk=QBLK, kblk=KBLK)
            for k, v in zip(("kp", "vp", "kip", "ks", "vs", "kis"), (*new_p, *new_s)):
                outs[k].append(v)
        y, yb = add_ln(y, h, ln_mix_g[layer], ln_mix_b[layer], TM)
        y, yb = moe_layer(y, yb, moe_params(layer, w), ln_ffn_g[layer], ln_ffn_b[layer], tm=TM, tm_dma=TM_DMA)
    st = lambda k: jnp.stack(outs[k])
    return (y[:tp].reshape(n_p, l_p, D_MODEL), y[tp:].reshape(n_s, l_s, D_MODEL),
            st("kp"), st("vp"), st("kip"), st("ks"), st("vs"), st("kis"),
            st("shp"), st("shs"), st("wkvp"), st("wkvs"), st("srp"), st("sip"), st("srs"), st("sis"))
```

```python
import functools
import math

import jax
import jax.numpy as jnp
from jax import lax
from jax.experimental import pallas as pl
from jax.experimental.pallas import tpu as pltpu

F32, BF16, I32 = jnp.float32, jnp.bfloat16, jnp.int32

D_MODEL = 2048
DEPTH = 2
ALPHA = (2.0 * DEPTH) ** 0.25
LN_EPS = 1e-5
C_A = 1024
HS_A = 64
H_A = 16
LORA_W, LORA_A, LORA_G = 64, 64, 160
A_PROJ = 3 * C_A + LORA_W + LORA_A + LORA_G
GN_EPS_A = 64e-5
C_B = 1024
S5_P = 16
S5_GROUPS = 64
S5_N = 64
S5_STATE = S5_GROUPS * S5_N
DH_C = 128
H_C = 16
KVH_C = 4
GROUP_C = 4
H_IDX = 16
D_IDX = 64
TOPK_MAX = 256
ROPE_THETA = 10000.0
N_EXPERTS = 64
TOP_K = 8
N_EXPERT_GROUPS = 8
TOPK_GROUPS = 4
D_EXPERT = 512
ROUTED_SCALE = 2.5

LANES = 128
SUBLANES = 8
VMEM_LIMIT = 56 * 1024 * 1024

LORA_PAD = 512
EXP_M05 = math.exp(-0.5)


def _cparams(sem):
    return pltpu.CompilerParams(dimension_semantics=sem, vmem_limit_bytes=VMEM_LIMIT)


def _dot(a, b):
    return jnp.dot(a, b, preferred_element_type=F32)


def _lane_fold(x, op):
    tiles = [x[:, i * LANES:(i + 1) * LANES] for i in range(x.shape[1] // LANES)]
    return functools.reduce(op, tiles)


def _row_sum(x):
    return jnp.sum(_lane_fold(x, jnp.add), axis=-1, keepdims=True)


def _row_max(x):
    return jnp.max(_lane_fold(x, jnp.maximum), axis=-1, keepdims=True)


def _split3(a):
    a1 = a.astype(BF16)
    r1 = a - a1.astype(F32)
    a2 = r1.astype(BF16)
    a3 = (r1 - a2.astype(F32)).astype(BF16)
    return a1, a2, a3


def _dot_sel(a, sel_bf16):
    a1, a2, a3 = _split3(a)
    return _dot(a1, sel_bf16) + _dot(a2, sel_bf16) + _dot(a3, sel_bf16)


def _dot3(a, b):
    a1 = a.astype(BF16)
    a2 = (a - a1.astype(F32)).astype(BF16)
    b1 = b.astype(BF16)
    b2 = (b - b1.astype(F32)).astype(BF16)
    return _dot(a1, b1) + _dot(a1, b2) + _dot(a2, b1)


def _mm_kernel(*refs, n_lhs):
    o_ref = refs[-1]
    acc = None
    for i in range(n_lhs):
        d = _dot(refs[i][...], refs[n_lhs + i][...])
        acc = d if acc is None else acc + d
    o_ref[...] = acc.astype(o_ref.dtype)


MM_TM = 512


def matmul(xs, ws, tm, tn, out_dtype=F32, name="mm"):
    m = xs[0].shape[0]
    n = ws[0].shape[1]
    tm = MM_TM if m % MM_TM == 0 else tm
    assert m % tm == 0 and n % tn == 0, (m, tm, n, tn)
    in_specs = ([pl.BlockSpec((tm, x.shape[1]), lambda j, i: (i, 0)) for x in xs]
                + [pl.BlockSpec((w.shape[0], tn), lambda j, i: (0, j)) for w in ws])
    return pl.pallas_call(
        functools.partial(_mm_kernel, n_lhs=len(xs)),
        grid=(n // tn, m // tm),
        in_specs=in_specs,
        out_specs=pl.BlockSpec((tm, tn), lambda j, i: (i, j)),
        out_shape=jax.ShapeDtypeStruct((m, n), out_dtype),
        compiler_params=_cparams(("parallel", "parallel")),
        name=name,
    )(*xs, *ws)


def _add_ln_kernel(x_ref, h_ref, g_ref, b_ref, o_ref, ob_ref):
    z = ALPHA * x_ref[...] + h_ref[...]
    inv_d = 1.0 / z.shape[1]
    mu = _row_sum(z) * inv_d
    d = z - mu
    var = _row_sum(d * d) * inv_d
    y = d * lax.rsqrt(var + LN_EPS) * g_ref[...] + b_ref[...]
    o_ref[...] = y
    ob_ref[...] = y.astype(BF16)


def add_ln(x, h, g, b, tm):
    t, d = x.shape
    row = pl.BlockSpec((tm, d), lambda i: (i, 0))
    vec = pl.BlockSpec((1, d), lambda i: (0, 0))
    return pl.pallas_call(
        _add_ln_kernel,
        grid=(t // tm,),
        in_specs=[row, row, vec, vec],
        out_specs=[row, row],
        out_shape=[jax.ShapeDtypeStruct((t, d), F32), jax.ShapeDtypeStruct((t, d), BF16)],
        compiler_params=_cparams(("parallel",)),
        name="add_ln",
    )(x, h, g.reshape(1, d), b.reshape(1, d))


def _rwkv_prep_kernel(rkv_ref, rkvp_ref, lo_ref, lop_ref, mu1_ref, mu2_ref, w0_ref, w2_ref, a0_ref, a2_ref,
                      g2_ref, kk_ref, ka_ref, rk_ref, gsel_ref, gselt_ref,
                      kk_o, wr_o, w_o, b_o, k_o, v_o, kr_o, g_o, bonus_o):
    x = rkv_ref[...]
    m = x + (rkvp_ref[...] - x) * mu1_ref[...]
    r = m[:, :C_A]
    k = m[:, C_A:2 * C_A]
    v = m[:, 2 * C_A:]
    l = lo_ref[...]
    lm = l + (lop_ref[...] - l) * mu2_ref[...]
    lw = lm[:, :LANES]
    la = lm[:, LANES:2 * LANES]
    lg = lm[:, 2 * LANES:]
    w_raw = w0_ref[...] + _dot3(jnp.tanh(lw), w2_ref[...])
    decay = jnp.exp(-EXP_M05 * jax.nn.sigmoid(w_raw))
    a = jax.nn.sigmoid(a0_ref[...] + _dot3(la, a2_ref[...]))
    g = _dot3(jax.nn.sigmoid(lg), g2_ref[...])
    gsel = gsel_ref[...]
    gselt = gselt_ref[...]
    kk = k * kk_ref[...]
    nrm = jnp.sqrt(_dot_sel(kk * kk, gsel))
    inv = 1.0 / jnp.maximum(nrm, 1e-12)
    kk = kk * _dot_sel(inv, gselt)
    k_mod = k * (1.0 + (a - 1.0) * ka_ref[...])
    sb = _dot_sel(r * k_mod * rk_ref[...], gsel)
    bonus_o[...] = _dot_sel(sb, gselt) * v
    g_o[...] = g
    b = kk * a
    kk_o[...] = kk
    br = _dot_sel(_dot_sel(b * r, gsel), gselt)
    wr_o[...] = decay * r - kk * br
    w_o[...] = decay
    b_o[...] = b
    k_o[...] = k_mod
    v_o[...] = v
    kr_o[...] = _dot_sel(_dot_sel(k_mod * r, gsel), gselt)


RWKV_ROWS = 7


def rwkv_prep(rkv, rkv_prev, lora, lora_prev, p, tm):
    t = rkv.shape[0]
    row = lambda w: pl.BlockSpec((tm, w), lambda i: (i, 0))
    full = lambda a: pl.BlockSpec(a.shape, lambda i: (0,) * a.ndim)
    consts = [p["mu_rkv"], p["mu_lora"], p["w0"], p["w2"], p["a0"], p["a2"], p["g2"], p["k_k"], p["k_a"], p["r_k"],
              p["gsel"], p["gselt"]]
    n_out = RWKV_ROWS + 2
    return pl.pallas_call(
        _rwkv_prep_kernel,
        grid=(t // tm,),
        in_specs=[row(3 * C_A), row(3 * C_A), row(LORA_PAD), row(LORA_PAD)] + [full(c) for c in consts],
        out_specs=[row(C_A)] * n_out,
        out_shape=[jax.ShapeDtypeStruct((t, C_A), F32)] * n_out,
        compiler_params=_cparams(("parallel",)),
        name="rwkv_prep",
    )(rkv, rkv_prev, lora, lora_prev, *consts)


TILE_HEADS = 4
TILE_W = TILE_HEADS * HS_A
N_TILES = H_A // TILE_HEADS
LHS_PER_TILE = 3
WKV_UNROLL = 4


def _wkv_steps(rows, row_base, n_steps, npar, s_scr, bd_ref, store_o):
    sub = lax.broadcasted_iota(I32, (HS_A, TILE_W), 0)
    lane = lax.broadcasted_iota(I32, (HS_A, TILE_W), 1)
    diag = (lane % HS_A) == sub
    bd = bd_ref[...]
    blk_rows = LHS_PER_TILE * HS_A

    def step(t, carry):
        vec = [[ref[pl.ds(row_base[s] + t, 1), :] for ref in rows[s]] for s in range(npar)]
        results = []
        for s in range(npar):
            kk, wr2, _, _, _, v, _ = vec[s]
            for p in range(N_TILES):
                ls = slice(p * TILE_W, (p + 1) * TILE_W)
                st = s_scr[s * N_TILES + p]
                q1 = (st * kk[:, ls]).astype(BF16)
                q2 = (st * wr2[:, ls]).astype(BF16)
                ve = jnp.where(diag, v[:, ls], 0.0).astype(BF16)
                results.append(_dot(jnp.concatenate([q1, q2, ve], axis=0), bd))
        for s in range(npar):
            _, _, w, b, k, v, kr = vec[s]
            o_parts = []
            for p in range(N_TILES):
                ls = slice(p * TILE_W, (p + 1) * TILE_W)
                res = results[s * N_TILES + p]
                z1 = res[:HS_A]
                z2 = res[HS_A:2 * HS_A]
                vb = res[2 * HS_A:]
                st = s_scr[s * N_TILES + p]
                s_scr[s * N_TILES + p] = st * w[:, ls] - z1 * b[:, ls] + vb * k[:, ls]
                z2r = jnp.sum(jnp.where(diag, z2, 0.0), axis=0, keepdims=True)
                o_parts.append(z2r + v[:, ls] * kr[:, ls])
            store_o(s, t, jnp.concatenate(o_parts, axis=1))
        return carry

    lax.fori_loop(0, n_steps, step, 0, unroll=WKV_UNROLL)


def _pack_state(s_ref, q, s_scr, slot):
    for p in range(N_TILES):
        s_scr[slot * N_TILES + p] = jnp.concatenate([s_ref[q, TILE_HEADS * p + h] for h in range(TILE_HEADS)], axis=1)


def _unpack_state(s_scr, slot, s_ref, q):
    for p in range(N_TILES):
        st = s_scr[slot * N_TILES + p]
        for h in range(TILE_HEADS):
            s_ref[q, TILE_HEADS * p + h] = st[:, h * HS_A:(h + 1) * HS_A]


def _wkv_long_kernel(*refs, npar, blk):
    n_in = npar * RWKV_ROWS
    rows = [refs[s * RWKV_ROWS:(s + 1) * RWKV_ROWS] for s in range(npar)]
    bd_ref, o_ref, sout_ref, s_scr = refs[n_in:n_in + 4]
    c = pl.program_id(0)

    @pl.when(c == 0)
    def _():
        s_scr[...] = jnp.zeros(s_scr.shape, F32)

    def store_o(s, t, o_row):
        o_ref[s, pl.ds(t, 1), :] = o_row

    _wkv_steps(rows, [0] * npar, blk, npar, s_scr, bd_ref, store_o)

    @pl.when(c == pl.num_programs(0) - 1)
    def _():
        for s in range(npar):
            _unpack_state(s_scr, s, sout_ref, s)


WKV_NPAR_SHORT = 2
WKV_SPB = 8


def _wkv_short_kernel(*refs, seq_len, spb):
    rows_refs = refs[:RWKV_ROWS]
    s0_ref, bd_ref, o_ref, sout_ref, s_scr = refs[RWKV_ROWS:RWKV_ROWS + 5]
    c = pl.program_id(1)
    npar = WKV_NPAR_SHORT

    def group(gq, carry):
        qs = [gq * npar + s for s in range(npar)]
        for s in range(npar):
            _pack_state(s0_ref, qs[s], s_scr, s)

        def store_o(s, t, o_row):
            o_ref[pl.ds((c * spb + qs[s]) * seq_len + t, 1), :] = o_row

        _wkv_steps([rows_refs] * npar, [(c * spb + q) * seq_len for q in qs], seq_len, npar, s_scr, bd_ref, store_o)
        for s in range(npar):
            _unpack_state(s_scr, s, sout_ref, qs[s])
        return carry

    lax.fori_loop(0, spb // npar, group, 0)


def _head_ones():
    head = jnp.arange(TILE_W) // HS_A
    return (head[:, None] == head[None, :]).astype(BF16)


def wkv_long(rows, *, n_seq, seq_len, blk):
    cps = seq_len // blk
    in_specs = [pl.BlockSpec((blk, C_A), functools.partial(lambda c, s: (s * cps + c, 0), s=s))
                for s in range(n_seq) for _ in range(RWKV_ROWS)]
    return pl.pallas_call(
        functools.partial(_wkv_long_kernel, npar=n_seq, blk=blk),
        grid=(cps,),
        in_specs=in_specs + [pl.BlockSpec((TILE_W, TILE_W), lambda c: (0, 0))],
        out_specs=[pl.BlockSpec((n_seq, blk, C_A), lambda c: (0, c, 0)),
                   pl.BlockSpec((n_seq, H_A, HS_A, HS_A), lambda c: (0, 0, 0, 0))],
        out_shape=[jax.ShapeDtypeStruct((n_seq, seq_len, C_A), F32),
                   jax.ShapeDtypeStruct((n_seq, H_A, HS_A, HS_A), F32)],
        scratch_shapes=[pltpu.VMEM((n_seq * N_TILES, HS_A, TILE_W), F32)],
        compiler_params=_cparams(("arbitrary",)),
        name="wkv_long",
    )(*(list(rows) * n_seq), _head_ones())


def wkv_short(rows, s0, *, row0, n_seq, seq_len, blk, s0_seq0):
    assert row0 % blk == 0 and blk % seq_len == 0
    b0 = row0 // blk
    nsb = blk // seq_len
    spb = min(WKV_SPB, nsb)
    assert n_seq % nsb == 0 and nsb % spb == 0 and spb % WKV_NPAR_SHORT == 0 and s0_seq0 % spb == 0
    sub = nsb // spb
    sspec = lambda off: pl.BlockSpec((spb, H_A, HS_A, HS_A), lambda s, c: (off + s * sub + c, 0, 0, 0))
    return pl.pallas_call(
        functools.partial(_wkv_short_kernel, seq_len=seq_len, spb=spb),
        grid=(n_seq // nsb, sub),
        in_specs=[pl.BlockSpec((blk, C_A), lambda s, c: (b0 + s, 0))] * RWKV_ROWS
        + [sspec(s0_seq0 // spb), pl.BlockSpec((TILE_W, TILE_W), lambda s, c: (0, 0))],
        out_specs=[pl.BlockSpec((blk, C_A), lambda s, c: (s, 0)), sspec(0)],
        out_shape=[jax.ShapeDtypeStruct((n_seq * seq_len, C_A), F32),
                   jax.ShapeDtypeStruct((n_seq, H_A, HS_A, HS_A), F32)],
        scratch_shapes=[pltpu.VMEM((WKV_NPAR_SHORT * N_TILES, HS_A, TILE_W), F32)],
        compiler_params=_cparams(("parallel", "arbitrary")),
        name="wkv_short",
    )(*rows, s0, _head_ones())


def _rwkv_post_kernel(o_in_ref, bonus_ref, g_ref, lng_ref, lnb_ref, gsel_ref, gselt_ref, o_ref):
    o = o_in_ref[...]
    gsel = gsel_ref[...]
    gselt = gselt_ref[...]
    mu = _dot_sel(o, gsel) * (1.0 / HS_A)
    d = o - _dot_sel(mu, gselt)
    var = _dot_sel(d * d, gsel) * (1.0 / HS_A)
    rstd = lax.rsqrt(var + GN_EPS_A)
    y = d * _dot_sel(rstd, gselt) * lng_ref[...] + lnb_ref[...]
    o_ref[...] = ((y + bonus_ref[...]) * g_ref[...]).astype(o_ref.dtype)


def rwkv_post(o_wkv, bonus, g, p, tm):
    t = bonus.shape[0]
    row = pl.BlockSpec((tm, C_A), lambda i: (i, 0))
    full = lambda a: pl.BlockSpec(a.shape, lambda i: (0,) * a.ndim)
    consts = [p["ln_g"], p["ln_b"], p["gsel"], p["gselt"]]
    return pl.pallas_call(
        _rwkv_post_kernel,
        grid=(t // tm,),
        in_specs=[row, row, row] + [full(c) for c in consts],
        out_specs=row,
        out_shape=jax.ShapeDtypeStruct((t, C_A), BF16),
        compiler_params=_cparams(("parallel",)),
        name="rwkv_post",
    )(o_wkv, bonus, g, *consts)


S5_KT = 4
S5_KU = C_B // S5_KT
S5_KH = S5_STATE // S5_KT
S5_LC = 512


def _gelu_tanh(x):
    return 0.5 * x * (1.0 + jnp.tanh(math.sqrt(2.0 / math.pi) * (x + 0.044715 * (x * x * x))))


def _s5_kernel(u_ref, wbr_ref, wbi_ref, lamk_ref, pre_ref, pim_ref, h0r_ref, h0i_ref, wcr_ref, wci_ref,
               dsk_ref, wglu_ref, bglu_ref, ob_ref, hr_out, hi_out, hre, him, car_re, car_im, y_scr,
               *, seq_len, blk):
    long_mode = seq_len >= blk
    period = SUBLANES if long_mode else seq_len
    u = u_ref[...]
    ub = u.astype(BF16)
    for kt in range(S5_KT):
        uk = ub[:, kt * S5_KU:(kt + 1) * S5_KU]
        hre[:, kt * S5_KH:(kt + 1) * S5_KH] = _dot(uk, wbr_ref[kt])
        him[:, kt * S5_KH:(kt + 1) * S5_KH] = _dot(uk, wbi_ref[kt])

    if long_mode:
        @pl.when(pl.program_id(1) == 0)
        def _():
            car_re[...] = jnp.zeros(car_re.shape, F32)
            car_im[...] = jnp.zeros(car_im.shape, F32)

    rowi = lax.broadcasted_iota(I32, (SUBLANES, 1), 0) % period
    for lc in range(S5_STATE // S5_LC):
        ls = slice(lc * S5_LC, (lc + 1) * S5_LC)
        lam = [(lamk_ref[2 * i:2 * i + 1, ls], lamk_ref[2 * i + 1:2 * i + 2, ls]) for i in range(3)]
        pre = pre_ref[:, ls]
        pim = pim_ref[:, ls]

        def tile(i, carry, ls=ls, lam=lam, pre=pre, pim=pim):
            r0 = pl.multiple_of(i * SUBLANES, SUBLANES)
            xr = hre[pl.ds(r0, SUBLANES), ls]
            xi = him[pl.ds(r0, SUBLANES), ls]
            for lvl, sh in enumerate((1, 2, 4)):
                if sh >= period:
                    break
                lr, li = lam[lvl]
                keep = rowi >= sh
                sr = jnp.where(keep, pltpu.roll(xr, sh, 0), 0.0)
                si = jnp.where(keep, pltpu.roll(xi, sh, 0), 0.0)
                xr, xi = xr + lr * sr - li * si, xi + lr * si + li * sr
            if long_mode:
                hr, hi = carry
            else:
                hr = h0r_ref[pl.ds(r0, SUBLANES), ls]
                hi = h0i_ref[pl.ds(r0, SUBLANES), ls]
            xr, xi = xr + pre * hr - pim * hi, xi + pre * hi + pim * hr
            hre[pl.ds(r0, SUBLANES), ls] = xr
            him[pl.ds(r0, SUBLANES), ls] = xi
            if long_mode:
                last_r = jnp.broadcast_to(xr[SUBLANES - 1:SUBLANES, :], xr.shape)
                last_i = jnp.broadcast_to(xi[SUBLANES - 1:SUBLANES, :], xi.shape)
                return (last_r, last_i)
            return carry

        if long_mode:
            init = (car_re[:, ls], car_im[:, ls])
            fin = lax.fori_loop(0, blk // SUBLANES, tile, init, unroll=2)
            car_re[:, ls] = fin[0]
            car_im[:, ls] = fin[1]
        else:
            lax.fori_loop(0, blk // SUBLANES, tile, 0)

    if long_mode:
        hr_out[0] = car_re[...]
        hi_out[0] = car_im[...]
    else:
        hr_out[...] = hre[...]
        hi_out[...] = him[...]

    for kt in range(S5_KT):
        hs = slice(kt * S5_KH, (kt + 1) * S5_KH)
        y_scr[:, kt * S5_KU:(kt + 1) * S5_KU] = (_dot(hre[:, hs].astype(BF16), wcr_ref[kt])
                                                 - _dot(him[:, hs].astype(BF16), wci_ref[kt]))
    y = y_scr[...] + dsk_ref[...] * u
    z = _gelu_tanh(y)
    gate = jax.nn.sigmoid(_dot(z.astype(BF16), wglu_ref[...]) + bglu_ref[...])
    ob_ref[...] = (z * gate).astype(ob_ref.dtype)


def s5_mix(u, p, h0r_rows, h0i_rows, *, row0, n_seq, seq_len, blk):
    long_mode = seq_len >= blk
    assert row0 % blk == 0
    b0 = row0 // blk
    t_out = n_seq * seq_len
    if long_mode:
        cps = seq_len // blk
        grid = (n_seq, cps)
        umap = lambda s, c: (b0 + s * cps + c, 0)
        omap = lambda s, c: (s * cps + c, 0)
        hspec = pl.BlockSpec((1, SUBLANES, S5_STATE), lambda s, c: (s, 0, 0))
        hshape = jax.ShapeDtypeStruct((n_seq, SUBLANES, S5_STATE), F32)
        h0spec = pl.BlockSpec((SUBLANES, S5_STATE), lambda s, c: (0, 0))
    else:
        assert t_out % blk == 0
        grid = (t_out // blk, 1)
        umap = lambda s, c: (b0 + s, 0)
        omap = lambda s, c: (s, 0)
        hspec = pl.BlockSpec((blk, S5_STATE), omap)
        hshape = jax.ShapeDtypeStruct((t_out, S5_STATE), F32)
        h0spec = pl.BlockSpec((blk, S5_STATE), omap)
    full = lambda a: pl.BlockSpec(a.shape, lambda s, c: (0,) * a.ndim)
    pre, pim = (p["pow_re8"], p["pow_im8"]) if long_mode else (p["pow_re_s"], p["pow_im_s"])
    args = [u, p["wb_re"], p["wb_im"], p["lamk"], pre, pim, h0r_rows, h0i_rows, p["wc_re"], p["wc_im"],
            p["d_skip"], p["w_glu"], p["b_glu"]]
    in_specs = [pl.BlockSpec((blk, C_B), umap)] + [full(a) for a in args[1:6]] + [h0spec, h0spec] \
        + [full(a) for a in args[8:]]
    return pl.pallas_call(
        functools.partial(_s5_kernel, seq_len=seq_len, blk=blk),
        grid=grid,
        in_specs=in_specs,
        out_specs=[pl.BlockSpec((blk, C_B), omap), hspec, hspec],
        out_shape=[jax.ShapeDtypeStruct((t_out, C_B), BF16), hshape, hshape],
        scratch_shapes=[pltpu.VMEM((blk, S5_STATE), F32), pltpu.VMEM((blk, S5_STATE), F32),
                        pltpu.VMEM((SUBLANES, S5_STATE), F32), pltpu.VMEM((SUBLANES, S5_STATE), F32),
                        pltpu.VMEM((blk, C_B), F32)],
        compiler_params=_cparams(("parallel", "arbitrary")),
        name="s5_long" if long_mode else "s5_short",
    )(*args)


def _pad_cols(a, n):
    return jnp.pad(a, ((0, 0), (0, n - a.shape[1])))


def _pad_rows(a, n):
    return jnp.pad(a, ((0, n - a.shape[0]), (0, 0)))


def _lora_layout(a):
    o = 3 * C_A
    return jnp.concatenate([_pad_cols(a[:, o:o + LORA_W], LANES),
                            _pad_cols(a[:, o + LORA_W:o + LORA_W + LORA_A], LANES),
                            _pad_cols(a[:, o + LORA_W + LORA_A:A_PROJ], 2 * LANES)], axis=1)


def _lora_unlayout(a):
    return jnp.concatenate([a[:, :LORA_W], a[:, LANES:LANES + LORA_A], a[:, 2 * LANES:2 * LANES + LORA_G]], axis=1)


def _cpow_table(lbr, lbi, n):
    res_r, res_i = [lbr], [lbi]
    for _ in range(n - 1):
        pr, pi = res_r[-1], res_i[-1]
        res_r.append(pr * lbr - pi * lbi)
        res_i.append(pr * lbi + pi * lbr)
    return (jnp.stack([x.reshape(-1) for x in res_r]), jnp.stack([x.reshape(-1) for x in res_i]))


def ab_params(li, w, sample_len):
    p = {}
    w_in = w["ab_w_in"][li]
    p["w_rkv"] = w_in[:, :3 * C_A].astype(BF16)
    p["w_lora"] = _lora_layout(w_in).astype(BF16)
    p["w_u"] = w_in[:, A_PROJ:].astype(BF16)
    mu = w["ab_mu"][li][None, :]
    p["mu_rkv"] = mu[:, :3 * C_A]
    p["mu_lora"] = _lora_layout(mu)
    row = lambda a: a.reshape(1, -1)
    p["w0"] = row(w["rwkv_w0"][li])
    p["w2"] = _pad_rows(w["rwkv_w2"][li], LANES)
    p["a0"] = row(w["rwkv_a0"][li])
    p["a2"] = _pad_rows(w["rwkv_a2"][li], LANES)
    p["g2"] = _pad_rows(w["rwkv_g2"][li], 2 * LANES)
    p["k_k"] = row(w["rwkv_k_k"][li])
    p["k_a"] = row(w["rwkv_k_a"][li])
    p["r_k"] = row(w["rwkv_r_k"][li])
    p["ln_g"] = row(w["rwkv_ln_g"][li])
    p["ln_b"] = row(w["rwkv_ln_b"][li])
    head_of_col = jnp.arange(C_A) // HS_A
    gsel = (head_of_col[:, None] == jnp.arange(LANES)[None, :])
    p["gsel"] = gsel.astype(BF16)
    p["gselt"] = gsel.T.astype(BF16)
    lr = jnp.minimum(w["s5_lam_re"][li], -1e-4)
    lim = w["s5_lam_im"][li]
    dt = jnp.exp(w["s5_log_dt"][li])[:, None]
    mag = jnp.exp(lr * dt)
    lbr, lbi = mag * jnp.cos(lim * dt), mag * jnp.sin(lim * dt)
    den = lr * lr + lim * lim
    pr, pim = lbr - 1.0, lbi
    fr = (pr * lr + pim * lim) / den
    fi = (pim * lr - pr * lim) / den
    br, bi = w["s5_b_re"][li], w["s5_b_im"][li]
    bbr = fr[..., None] * br - fi[..., None] * bi
    bbi = fr[..., None] * bi + fi[..., None] * br
    gpc = S5_GROUPS // S5_KT
    eye = jnp.eye(gpc, dtype=F32)

    def blockdiag_in(bb):
        b4 = bb.reshape(S5_KT, gpc, S5_N, S5_P)
        return jnp.einsum("kgnp,gh->kgphn", b4, eye).reshape(S5_KT, S5_KU, S5_KH).astype(BF16)

    def blockdiag_out(cc):
        c4 = cc.reshape(S5_KT, gpc, S5_P, S5_N)
        return jnp.einsum("kgpn,gh->kgnhp", c4, eye).reshape(S5_KT, S5_KH, S5_KU).astype(BF16)

    p["wb_re"], p["wb_im"] = blockdiag_in(bbr), blockdiag_in(bbi)
    p["wc_re"], p["wc_im"] = blockdiag_out(w["s5_c_re"][li]), blockdiag_out(w["s5_c_im"][li])
    pw_r, pw_i = _cpow_table(lbr, lbi, SUBLANES)
    p["pow_re8"], p["pow_im8"] = pw_r, pw_i
    reps = SUBLANES // sample_len
    p["pow_re_s"] = jnp.tile(pw_r[:sample_len], (reps, 1))
    p["pow_im_s"] = jnp.tile(pw_i[:sample_len], (reps, 1))
    p["lamk"] = jnp.stack([pw_r[0], pw_i[0], pw_r[1], pw_i[1], pw_r[3], pw_i[3]])
    p["d_skip"] = row(w["s5_d"][li])
    p["w_glu"] = w["s5_w_glu"][li].astype(BF16)
    p["b_glu"] = row(w["s5_b_glu"][li])
    w_out = w["ab_w_out"][li]
    p["w_out_a"] = w_out[:C_A].astype(BF16)
    p["w_out_b"] = w_out[C_A:].astype(BF16)
    return p


def _shift_rows(a, first_sample, geo):
    n_p, l_p, n_s, l_s = geo
    wd = a.shape[1]
    ap = a[:n_p * l_p].reshape(n_p, l_p, wd)
    prev_p = jnp.concatenate([jnp.zeros((n_p, 1, wd), a.dtype), ap[:, :-1]], axis=1)
    as_ = a[n_p * l_p:].reshape(n_s, l_s, wd)
    prev_s = jnp.concatenate([first_sample[:, None, :], as_[:, :-1]], axis=1)
    return jnp.concatenate([prev_p.reshape(-1, wd), prev_s.reshape(-1, wd)], axis=0)


def _last_rows(a, geo):
    n_p, l_p, n_s, l_s = geo
    wd = a.shape[1]
    return (a[:n_p * l_p].reshape(n_p, l_p, wd)[:, -1], a[n_p * l_p:].reshape(n_s, l_s, wd)[:, -1])


def ab_layer(xb, geo, p, state_shift, state_wkv, wkv_seq0, state_re, state_im, *, tm, blk):
    n_p, l_p, n_s, l_s = geo
    tp, ts = n_p * l_p, n_s * l_s
    rkv = matmul([xb], [p["w_rkv"]], tm, 1024, name="ab_in_rkv")
    lora = matmul([xb], [p["w_lora"]], tm, LORA_PAD, name="ab_in_lora")
    u = matmul([xb], [p["w_u"]], tm, C_B, name="ab_in_u")
    rkv_prev = _shift_rows(rkv, state_shift[:, :3 * C_A], geo)
    lora_prev = _shift_rows(lora, _lora_layout(state_shift), geo)
    prep = rwkv_prep(rkv, rkv_prev, lora, lora_prev, p, tm)
    rows, g, bonus = prep[:RWKV_ROWS], prep[RWKV_ROWS], prep[RWKV_ROWS + 1]
    o_p, wkv_p = wkv_long(rows, n_seq=n_p, seq_len=l_p, blk=blk)
    o_s, wkv_s = wkv_short(rows, state_wkv, row0=tp, n_seq=n_s, seq_len=l_s, blk=blk, s0_seq0=wkv_seq0)
    o_wkv = jnp.concatenate([o_p.reshape(tp, C_A), o_s], axis=0)
    o_a = rwkv_post(o_wkv, bonus, g, p, tm)
    dummy = jnp.zeros((SUBLANES, S5_STATE), F32)
    ob_p, hr_p, hi_p = s5_mix(u, p, dummy, dummy, row0=0, n_seq=n_p, seq_len=l_p, blk=blk)
    h0r = jnp.repeat(state_re.reshape(n_s, S5_STATE), l_s, axis=0)
    h0i = jnp.repeat(state_im.reshape(n_s, S5_STATE), l_s, axis=0)
    ob_s, hr_s, hi_s = s5_mix(u, p, h0r, h0i, row0=tp, n_seq=n_s, seq_len=l_s, blk=blk)
    o_b = jnp.concatenate([ob_p, ob_s], axis=0)
    h = matmul([o_a, o_b], [p["w_out_a"], p["w_out_b"]], tm, 1024, name="ab_out")
    rkv_lp, rkv_ls = _last_rows(rkv, geo)
    lo_lp, lo_ls = _last_rows(lora, geo)
    shift_p = jnp.concatenate([rkv_lp, _lora_unlayout(lo_lp)], axis=1)
    shift_s = jnp.concatenate([rkv_ls, _lora_unlayout(lo_ls)], axis=1)
    grp = lambda a, n: a.reshape(n, S5_GROUPS, S5_N)
    ssm_p = (grp(hr_p[:, SUBLANES - 1], n_p), grp(hi_p[:, SUBLANES - 1], n_p))
    ssm_s = (grp(hr_s[l_s - 1::l_s], n_s), grp(hi_s[l_s - 1::l_s], n_s))
    return h, (shift_p, shift_s), (wkv_p, wkv_s), ssm_p, ssm_s


def _router_kernel(x_ref, wr_ref, br_ref, ltri_ref, eidx_o, gate_o, rank_o, cnt_o, carry):
    i = pl.program_id(0)

    @pl.when(i == 0)
    def _():
        carry[...] = jnp.zeros(carry.shape, F32)

    tm = x_ref.shape[0]
    neg = -jnp.inf
    lane_i = lax.broadcasted_iota(I32, (tm, LANES), 1)
    lane = lane_i.astype(F32)
    per_group = N_EXPERTS // N_EXPERT_GROUPS
    grp = (lane_i // per_group).astype(F32)
    scores = jax.nn.sigmoid(_dot3(x_ref[...], wr_ref[...]))
    biased = jnp.where(lane_i < N_EXPERTS, scores + br_ref[...], neg)
    rmax = lambda a: jnp.max(a, axis=-1, keepdims=True)
    rmin = lambda a: jnp.min(a, axis=-1, keepdims=True)
    rsum = lambda a: jnp.sum(a, axis=-1, keepdims=True)

    gsc = jnp.full((tm, LANES), neg, F32)
    for g in range(N_EXPERT_GROUPS):
        xg = jnp.where(grp == g, biased, neg)
        m1 = rmax(xg)
        twice = rsum(jnp.where(xg == m1, 1.0, 0.0)) >= 2.0
        m2 = jnp.where(twice, m1, rmax(jnp.where(xg < m1, xg, neg)))
        gsc = jnp.where(lane == g, m1 + m2, gsc)
    keep = jnp.zeros((tm, LANES), jnp.bool_)
    for _ in range(TOPK_GROUPS):
        m = rmax(gsc)
        gi = rmin(jnp.where(gsc == m, lane, float(LANES)))
        keep = keep | (grp == gi)
        gsc = jnp.where(lane == gi, neg, gsc)
    cur = jnp.where(keep, biased, neg)

    sel = jnp.zeros((tm, LANES), F32)
    es, ss = [], []
    for _ in range(TOP_K):
        m = rmax(cur)
        ik = rmin(jnp.where(cur == m, lane, float(LANES)))
        hit = lane == ik
        ss.append(rsum(jnp.where(hit, scores, 0.0)))
        es.append(ik)
        cur = jnp.where(hit, neg, cur)
        sel = jnp.where(hit, 1.0, sel)
    tot = ss[0]
    for k in range(1, TOP_K):
        tot = tot + ss[k]

    prefix = _dot(ltri_ref[...], sel.astype(BF16)) + carry[0:1, :]
    eidx = jnp.zeros((tm, LANES), F32)
    gate = jnp.zeros((tm, LANES), F32)
    rank = jnp.zeros((tm, LANES), F32)
    for k in range(TOP_K):
        slot = lane == k
        eidx = jnp.where(slot, es[k], eidx)
        gate = jnp.where(slot, ss[k] / tot * ROUTED_SCALE, gate)
        rank = jnp.where(slot, rsum(jnp.where(lane == es[k], prefix, 0.0)), rank)
    eidx_o[...] = eidx.astype(I32)
    gate_o[...] = gate
    rank_o[...] = rank.astype(I32)
    carry[...] = carry[...] + jnp.sum(sel, axis=0, keepdims=True)
    cnt_o[...] = carry[...]


def moe_router(x, w_router_pad, b_router_pad, tm):
    t, d = x.shape
    ltri = (jnp.arange(tm)[:, None] > jnp.arange(tm)[None, :]).astype(BF16)
    row = pl.BlockSpec((tm, LANES), lambda i: (i, 0))
    out = jax.ShapeDtypeStruct((t, LANES), I32)
    return pl.pallas_call(
        _router_kernel,
        grid=(t // tm,),
        in_specs=[pl.BlockSpec((tm, d), lambda i: (i, 0)), pl.BlockSpec((d, LANES), lambda i: (0, 0)),
                  pl.BlockSpec((1, LANES), lambda i: (0, 0)), pl.BlockSpec((tm, tm), lambda i: (0, 0))],
        out_specs=[row, row, row, pl.BlockSpec((SUBLANES, LANES), lambda i: (0, 0))],
        out_shape=[out, jax.ShapeDtypeStruct((t, LANES), F32), out,
                   jax.ShapeDtypeStruct((SUBLANES, LANES), F32)],
        scratch_shapes=[pltpu.VMEM((SUBLANES, LANES), F32)],
        compiler_params=_cparams(("arbitrary",)),
        name="moe_router",
    )(x, w_router_pad, b_router_pad, ltri)


MOE_BM = 256


def _gather_rows_start(src_hbm, idx_ref, idx0, dst_buf, n_rows, sem):
    for j in range(n_rows):
        pltpu.make_async_copy(src_hbm.at[pl.ds(idx_ref[idx0 + j], 1), :], dst_buf.at[pl.ds(j, 1), :], sem).start()


def _gather_rows_wait(src_hbm, dst_buf, n_rows, sem):
    pltpu.make_async_copy(src_hbm.at[pl.ds(0, n_rows), :], dst_buf, sem).wait()


def _expert_kernel(blk_e_ref, nused_ref, tok_ref, x_hbm, wg_ref, wu_ref, wd_ref, ys_ref, xbuf_a, xbuf_b, sems,
                   wgb, wub, wdb):
    i = pl.program_id(0)
    n_used = nused_ref[0]
    active = i < n_used

    @pl.when(i == 0)
    def _():
        _gather_rows_start(x_hbm, tok_ref, 0, xbuf_a, MOE_BM, sems.at[0])

    @pl.when(active)
    def _():
        e = blk_e_ref[i]
        prev = blk_e_ref[jnp.maximum(i - 1, 0)]

        @pl.when((i == 0) | (e != prev))
        def _():
            wgb[...] = wg_ref[0].astype(BF16)
            wub[...] = wu_ref[0].astype(BF16)
            wdb[...] = wd_ref[0].astype(BF16)

    def block(cur_buf, cur_sem, nxt_buf, nxt_sem):
        _gather_rows_wait(x_hbm, cur_buf, MOE_BM, cur_sem)
        nxt_row0 = jnp.minimum(i + 1, n_used - 1) * MOE_BM
        _gather_rows_start(x_hbm, tok_ref, nxt_row0, nxt_buf, MOE_BM, nxt_sem)
        x = cur_buf[...].astype(BF16)
        hg = _dot(x, wgb[...])
        hu = _dot(x, wub[...])
        h = hg * jax.nn.sigmoid(hg) * hu
        ys_ref[...] = _dot(h.astype(BF16), wdb[...])

        @pl.when(i == n_used - 1)
        def _():
            _gather_rows_wait(x_hbm, nxt_buf, MOE_BM, nxt_sem)

    @pl.when(active & (i % 2 == 0))
    def _():
        block(xbuf_a, sems.at[0], xbuf_b, sems.at[1])

    @pl.when(active & (i % 2 == 1))
    def _():
        block(xbuf_b, sems.at[1], xbuf_a, sems.at[0])

    @pl.when(i >= nused_ref[0])
    def _():
        ys_ref[...] = jnp.zeros(ys_ref.shape, F32)


def moe_experts(x, blk_e, n_used, tok_of_row, w_gate, w_up, w_down):
    d = x.shape[1]
    n_rows = tok_of_row.shape[0]
    n_blocks = n_rows // MOE_BM
    de = w_gate.shape[2]
    wmap = lambda i, be, nu, tk: (be[jnp.minimum(i, nu[0] - 1)], 0, 0)
    return pl.pallas_call(
        _expert_kernel,
        grid_spec=pltpu.PrefetchScalarGridSpec(
            num_scalar_prefetch=3,
            grid=(n_blocks,),
            in_specs=[pl.BlockSpec(memory_space=pl.ANY),
                      pl.BlockSpec((1, d, de), wmap), pl.BlockSpec((1, d, de), wmap), pl.BlockSpec((1, de, d), wmap)],
            out_specs=pl.BlockSpec((MOE_BM, d), lambda i, be, nu, tk: (i, 0)),
            scratch_shapes=[pltpu.VMEM((MOE_BM, d), F32), pltpu.VMEM((MOE_BM, d), F32), pltpu.SemaphoreType.DMA((2,)),
                            pltpu.VMEM((d, de), BF16), pltpu.VMEM((d, de), BF16), pltpu.VMEM((de, d), BF16)],
        ),
        out_shape=jax.ShapeDtypeStruct((n_rows, d), F32),
        compiler_params=_cparams(("arbitrary",)),
        name="moe_experts",
    )(blk_e, n_used, tok_of_row, x, w_gate, w_up, w_down)


def _shared_ffn_kernel(x_ref, wg_ref, wu_ref, wd_ref, o_ref):
    x = x_ref[...]
    hg = _dot(x, wg_ref[...])
    hu = _dot(x, wu_ref[...])
    h = hg * jax.nn.sigmoid(hg) * hu
    o_ref[...] = _dot(h.astype(BF16), wd_ref[...])


def shared_ffn(xb, wg, wu, wd, tm):
    t, d = xb.shape
    full = lambda a: pl.BlockSpec(a.shape, lambda i: (0, 0))
    row = pl.BlockSpec((tm, d), lambda i: (i, 0))
    return pl.pallas_call(
        _shared_ffn_kernel,
        grid=(t // tm,),
        in_specs=[row, full(wg), full(wu), full(wd)],
        out_specs=row,
        out_shape=jax.ShapeDtypeStruct((t, d), F32),
        compiler_params=_cparams(("parallel",)),
        name="shared_ffn",
    )(xb, wg, wu, wd)


def _combine_kernel(dest_ref, ys_hbm, gate_ref, x_ref, sh_ref, g_ref, b_ref, o_ref, ob_ref, buf, sems, *, tm):
    i = pl.program_id(0)
    n_asg = tm * TOP_K

    @pl.when(i == 0)
    def _():
        _gather_rows_start(ys_hbm, dest_ref, 0, buf.at[0], n_asg, sems.at[0])

    @pl.when(i + 1 < pl.num_programs(0))
    def _():
        nxt = (i + 1) % 2
        _gather_rows_start(ys_hbm, dest_ref, (i + 1) * n_asg, buf.at[nxt], n_asg, sems.at[nxt])

    cur = i % 2
    _gather_rows_wait(ys_hbm, buf.at[cur], n_asg, sems.at[cur])
    gate = gate_ref[...]
    f = sh_ref[...]
    for k in range(TOP_K):
        f = f + gate[:, k:k + 1] * buf[cur, k * tm:(k + 1) * tm, :]
    z = ALPHA * x_ref[...] + f
    inv_d = 1.0 / z.shape[1]
    mu = _row_sum(z) * inv_d
    dlt = z - mu
    var = _row_sum(dlt * dlt) * inv_d
    y = dlt * lax.rsqrt(var + LN_EPS) * g_ref[...] + b_ref[...]
    o_ref[...] = y
    ob_ref[...] = y.astype(BF16)


def moe_combine(ys, dest_flat, gates, x, shared, g, b, tm):
    t, d = x.shape
    row = pl.BlockSpec((tm, d), lambda i, dr: (i, 0))
    vec = pl.BlockSpec((1, d), lambda i, dr: (0, 0))
    return pl.pallas_call(
        functools.partial(_combine_kernel, tm=tm),
        grid_spec=pltpu.PrefetchScalarGridSpec(
            num_scalar_prefetch=1,
            grid=(t // tm,),
            in_specs=[pl.BlockSpec(memory_space=pl.ANY), pl.BlockSpec((tm, LANES), lambda i, dr: (i, 0)),
                      row, row, vec, vec],
            out_specs=[row, row],
            scratch_shapes=[pltpu.VMEM((2, TOP_K * tm, d), F32), pltpu.SemaphoreType.DMA((2,))],
        ),
        out_shape=[jax.ShapeDtypeStruct((t, d), F32), jax.ShapeDtypeStruct((t, d), BF16)],
        compiler_params=_cparams(("arbitrary",)),
        name="moe_combine",
    )(dest_flat, ys, gates, x, shared, g.reshape(1, d), b.reshape(1, d))


def moe_params(layer, w):
    p = {}
    p["w_router"] = _pad_cols(w["moe_w_router"][layer], LANES)
    p["b_router"] = _pad_cols(w["moe_b_router"][layer][None, :], LANES)
    stack = lambda a: a.reshape((-1,) + a.shape[2:])
    p["w_gate"], p["w_up"], p["w_down"] = stack(w["moe_w_gate"]), stack(w["moe_w_up"]), stack(w["moe_w_down"])
    p["e0"] = layer * N_EXPERTS
    p["ws_gate"] = w["moe_ws_gate"][layer].astype(BF16)
    p["ws_up"] = w["moe_ws_up"][layer].astype(BF16)
    p["ws_down"] = w["moe_ws_down"][layer].astype(BF16)
    return p


def moe_layer(y, yb, p, ln_g, ln_b, *, tm, tm_dma):
    t = y.shape[0]
    eidx, gates, rank, cnt = moe_router(y, p["w_router"], p["b_router"], tm)
    counts = cnt[0, :N_EXPERTS].astype(I32)
    padded = (counts + MOE_BM - 1) // MOE_BM * MOE_BM
    pad_end = jnp.cumsum(padded)
    pad_start = pad_end - padded
    e8 = eidx[:, :TOP_K]
    dest = pad_start[e8] + rank[:, :TOP_K]
    n_blocks = -(-(t * TOP_K + N_EXPERTS * (MOE_BM - 1)) // MOE_BM)
    n_rows = n_blocks * MOE_BM
    blk_start = jnp.arange(n_blocks, dtype=I32) * MOE_BM
    blk_e = jnp.sum((pad_end[None, :] <= blk_start[:, None]).astype(I32), axis=1)
    blk_e = jnp.minimum(blk_e, N_EXPERTS - 1) + p["e0"]
    n_used = (pad_end[-1:] // MOE_BM).astype(I32)
    tok = jnp.broadcast_to(jnp.arange(t, dtype=I32)[:, None], (t, TOP_K))
    tok_of_row = jnp.zeros((n_rows,), I32).at[dest.reshape(-1)].set(tok.reshape(-1))
    ys = moe_experts(y, blk_e, n_used, tok_of_row, p["w_gate"], p["w_up"], p["w_down"])
    shared = shared_ffn(yb, p["ws_gate"], p["ws_up"], p["ws_down"], tm)
    dest_tiles = dest.reshape(t // tm_dma, tm_dma, TOP_K).transpose(0, 2, 1).reshape(-1)
    return moe_combine(ys, dest_tiles, gates, y, shared, ln_g, ln_b, tm_dma)


C_QKV = (H_C + 2 * KVH_C) * DH_C
C_QI = H_IDX * D_IDX
IDX_SCALE = (H_IDX ** -0.5) * (D_IDX ** -0.5)


def _rot_half64(x, lane):
    return jnp.where(lane % D_IDX < D_IDX // 2, pltpu.roll(x, LANES - D_IDX // 2, 1), pltpu.roll(x, D_IDX // 2, 1))


def _c_post_kernel(qkv_ref, qi_ref, kw_ref, c128_ref, s128_ref, c64_ref, s64_ref, lng_ref, lnb_ref,
                   qb_o, kf_o, kb_o, vb_o, qib_o, kwo_o):
    tm = qkv_ref.shape[0]
    c128, s128 = c128_ref[...], s128_ref[...]
    c64, s64 = c64_ref[...], s64_ref[...]
    lane = lax.broadcasted_iota(I32, (tm, LANES), 1)
    for h in range(H_C + KVH_C):
        x = qkv_ref[:, h * DH_C:(h + 1) * DH_C]
        y = x * c128 + pltpu.roll(x, DH_C // 2, 1) * s128
        if h < H_C:
            qb_o[:, h * DH_C:(h + 1) * DH_C] = (y * (DH_C ** -0.5)).astype(BF16)
        else:
            j = h - H_C
            kf_o[:, j * DH_C:(j + 1) * DH_C] = y
            kb_o[:, j * DH_C:(j + 1) * DH_C] = y.astype(BF16)
    vb_o[...] = qkv_ref[:, (H_C + KVH_C) * DH_C:].astype(BF16)
    for j in range(C_QI // LANES):
        x = qi_ref[:, j * LANES:(j + 1) * LANES]
        qib_o[:, j * LANES:(j + 1) * LANES] = (x * c64 + _rot_half64(x, lane) * s64).astype(BF16)
    x = kw_ref[...]
    is_k = lane < D_IDX
    mu = jnp.sum(jnp.where(is_k, x, 0.0), axis=-1, keepdims=True) * (1.0 / D_IDX)
    d = jnp.where(is_k, x - mu, 0.0)
    var = jnp.sum(d * d, axis=-1, keepdims=True) * (1.0 / D_IDX)
    y = d * lax.rsqrt(var + LN_EPS) * lng_ref[...] + lnb_ref[...]
    y = y * c64 + _rot_half64(y, lane) * s64
    kwo_o[...] = jnp.where(is_k, y, x * IDX_SCALE)


def c_post(qkv, qi, kw, tabs, lng, lnb, tm):
    t = qkv.shape[0]
    row = lambda wd: pl.BlockSpec((tm, wd), lambda i: (i, 0))
    vec = pl.BlockSpec((1, LANES), lambda i: (0, 0))
    sh = lambda wd, dt: jax.ShapeDtypeStruct((t, wd), dt)
    return pl.pallas_call(
        _c_post_kernel,
        grid=(t // tm,),
        in_specs=[row(C_QKV), row(C_QI), row(LANES), row(LANES), row(LANES), row(LANES), row(LANES), vec, vec],
        out_specs=[row(H_C * DH_C), row(KVH_C * DH_C), row(KVH_C * DH_C), row(KVH_C * DH_C), row(C_QI), row(LANES)],
        out_shape=[sh(H_C * DH_C, BF16), sh(KVH_C * DH_C, F32), sh(KVH_C * DH_C, BF16), sh(KVH_C * DH_C, BF16),
                   sh(C_QI, BF16), sh(LANES, F32)],
        compiler_params=_cparams(("parallel",)),
        name="c_post",
    )(qkv, qi, kw, *tabs, lng, lnb)


def _rope_tables(pos):
    def tab(d):
        inv = ROPE_THETA ** (-jnp.arange(0, d, 2, dtype=F32) / d)
        ang = pos.astype(F32)[:, None] * inv[None, :]
        c, s = jnp.cos(ang), jnp.sin(ang)
        return jnp.concatenate([c, c], axis=1), jnp.concatenate([-s, s], axis=1)

    c128, s128 = tab(DH_C)
    c64, s64 = tab(D_IDX)
    return c128, s128, jnp.tile(c64, (1, 2)), jnp.tile(s64, (1, 2))


def _index_heads(qi, wi, kb):
    acc = None
    for h in range(H_IDX):
        d = lax.dot_general(qi[:, h * D_IDX:(h + 1) * D_IDX], kb, (((1,), (1,)), ((), ())),
                            preferred_element_type=F32)
        term = wi[:, D_IDX + h:D_IDX + h + 1] * jnp.maximum(d, 0.0)
        acc = term if acc is None else acc + term
    return acc + 0.0


SROWS = 16


def _topk_mask(sc_ref, utri_ref, o_ref, key_scr, *, topk, chunk, bits_per_step=1):
    rows, s_len = sc_ref.shape
    bits = lax.bitcast_convert_type(sc_ref[...], I32)
    key_scr[...] = bits ^ ((bits >> 31) & jnp.int32(0x7FFFFFFF))
    sign = jnp.int32(-2 ** 31)
    kf = jnp.float32(topk)

    def enough(cand):
        return _row_sum(jnp.where(key_scr[...] >= (cand ^ sign), 1.0, 0.0)) >= kf

    def bisect(it, t_u):
        low = (32 - bits_per_step) - bits_per_step * it
        for pattern in range(1, 2 ** bits_per_step):
            cand = t_u | (jnp.int32(pattern) << low)
            best = jnp.where(enough(cand), cand, t_u if pattern == 1 else best)
        return best

    t_u = lax.fori_loop(0, 32 // bits_per_step, bisect, jnp.zeros((rows, 1), I32))
    thr = t_u ^ sign
    need = kf - _row_sum(jnp.where(key_scr[...] > thr, 1.0, 0.0))
    utri = utri_ref[...]
    run = jnp.zeros((rows, 1), F32)
    for c in range(s_len // chunk):
        cs = slice(c * chunk, (c + 1) * chunk)
        keyc = key_scr[:, cs]
        eqc = jnp.where(keyc == thr, 1.0, 0.0)
        before = _dot(eqc.astype(BF16), utri) + run
        pick = (keyc > thr) | ((keyc == thr) & (before < need))
        o_ref[:, cs] = jnp.where(pick & (sc_ref[:, cs] > -jnp.inf), 1.0, 0.0).astype(o_ref.dtype)
        run = run + _row_sum(eqc)
    if o_ref.shape[1] > s_len:
        o_ref[:, s_len:] = jnp.zeros((rows, o_ref.shape[1] - s_len), o_ref.dtype)


def _strict_upper(chunk):
    return (jnp.arange(chunk)[:, None] < jnp.arange(chunk)[None, :]).astype(BF16)


SELECT_CLASSES = 4


def _index_select_kernel(qi_ref, wi_ref, keys_ref, utri_ref, o_ref, sc_scr, key_scr, *, q0, topk, chunk):
    qblk, width = sc_scr.shape
    kb = keys_ref[:width, :D_IDX].astype(BF16)
    sc = _index_heads(qi_ref[...], wi_ref[...], kb)
    row = lax.broadcasted_iota(I32, (qblk, width), 0) + (q0 + pl.program_id(1) * qblk)
    col = lax.broadcasted_iota(I32, (qblk, width), 1)
    sc_scr[...] = jnp.where(col <= row, sc, -jnp.inf)
    _topk_mask(sc_scr, utri_ref, o_ref, key_scr, topk=topk, chunk=chunk)


def index_select_causal(qib, kwo, topk, n_seq, seq_len, qblk, chunk):
    n_cls = SELECT_CLASSES if seq_len % (SELECT_CLASSES * max(chunk, qblk)) == 0 else 1
    cw = seq_len // n_cls
    utri = _strict_upper(chunk)
    parts = []
    for c in range(n_cls):
        width = (c + 1) * cw
        qmap = functools.partial(lambda b, i, c: (b * (seq_len // qblk) + c * (cw // qblk) + i, 0), c=c)
        parts.append(pl.pallas_call(
            functools.partial(_index_select_kernel, q0=c * cw, topk=topk, chunk=chunk),
            grid=(n_seq, cw // qblk),
            in_specs=[pl.BlockSpec((qblk, C_QI), qmap), pl.BlockSpec((qblk, LANES), qmap),
                      pl.BlockSpec((seq_len, LANES), lambda b, i: (b, 0)),
                      pl.BlockSpec((chunk, chunk), lambda b, i: (0, 0))],
            out_specs=pl.BlockSpec((qblk, seq_len), lambda b, i: (b * (cw // qblk) + i, 0)),
            out_shape=jax.ShapeDtypeStruct((n_seq * cw, seq_len), BF16),
            scratch_shapes=[pltpu.VMEM((qblk, width), F32), pltpu.VMEM((qblk, width), I32)],
            compiler_params=_cparams(("parallel", "parallel")),
            name="index_select_%d" % c,
        )(qib, kwo, kwo, utri).reshape(n_seq, cw, seq_len))
    return jnp.concatenate(parts, axis=1).reshape(n_seq * seq_len, seq_len)


NEG_BIG = -1e30


def _attn_init(m_scr, l_scr, acc_scr):
    m_scr[...] = jnp.full(m_scr.shape, NEG_BIG, F32)
    l_scr[...] = jnp.zeros(l_scr.shape, F32)
    acc_scr[...] = jnp.zeros(acc_scr.shape, F32)


def _attn_block(q, k, v, mask, m_scr, l_scr, acc_scr):
    bias = jnp.concatenate([(mask.astype(F32) - 1.0) * (-NEG_BIG)] * GROUP_C, axis=0)
    heads = range(KVH_C)
    ss = []
    for kh in heads:
        qh = jnp.concatenate([q[:, (kh * GROUP_C + g) * DH_C:(kh * GROUP_C + g + 1) * DH_C]
                              for g in range(GROUP_C)], axis=0)
        ks = k[:, kh * DH_C:(kh + 1) * DH_C]
        ss.append(lax.dot_general(qh, ks, (((1,), (1,)), ((), ())), preferred_element_type=F32) + bias)
    n_tiles = ss[0].shape[1] // LANES
    m_old = [m_scr[kh] for kh in heads]
    m_new = [jnp.maximum(m_old[kh], jnp.broadcast_to(_row_max(ss[kh]), m_old[kh].shape)) for kh in heads]
    ps = [[jnp.exp(ss[kh][:, i * LANES:(i + 1) * LANES] - m_new[kh]) for i in range(n_tiles)] for kh in heads]
    for kh in heads:
        alpha = jnp.exp(m_old[kh] - m_new[kh])
        l_scr[kh] = alpha * l_scr[kh] + functools.reduce(jnp.add, ps[kh])
        p = jnp.concatenate(ps[kh], axis=1).astype(BF16)
        acc_scr[kh] = alpha * acc_scr[kh] + _dot(p, v[:, kh * DH_C:(kh + 1) * DH_C])
        m_scr[kh] = m_new[kh]


def _attn_finish(o_ref, nq, l_scr, acc_scr, lead=None):
    for kh in range(KVH_C):
        out = acc_scr[kh] / jnp.sum(l_scr[kh], axis=-1, keepdims=True)
        for g in range(GROUP_C):
            hs = slice((kh * GROUP_C + g) * DH_C, (kh * GROUP_C + g + 1) * DH_C)
            val = out[g * nq:(g + 1) * nq, :].astype(o_ref.dtype)
            if lead is None:
                o_ref[:, hs] = val
            else:
                o_ref[lead, :, hs] = val


def _attn_prompt_kernel(q_ref, k_ref, v_ref, mask_ref, o_ref, m_scr, l_scr, acc_scr, *, qblk, kblk):
    qb = pl.program_id(1)
    kb = pl.program_id(2)

    @pl.when(kb == 0)
    def _():
        _attn_init(m_scr, l_scr, acc_scr)

    @pl.when(kb * kblk <= qb * qblk + qblk - 1)
    def _():
        _attn_block(q_ref[...], k_ref[...], v_ref[...], mask_ref[...], m_scr, l_scr, acc_scr)

    @pl.when(kb == pl.num_programs(2) - 1)
    def _():
        _attn_finish(o_ref, qblk, l_scr, acc_scr)


def attn_prompt(qb_, kb_, vb_, mask, n_p, l_p, qblk, kblk):
    nq, nk = l_p // qblk, l_p // kblk
    last = lambda q, k: jnp.minimum(k, (q * qblk + qblk - 1) // kblk)
    rows = GROUP_C * qblk
    return pl.pallas_call(
        functools.partial(_attn_prompt_kernel, qblk=qblk, kblk=kblk),
        grid=(n_p, nq, nk),
        in_specs=[pl.BlockSpec((qblk, H_C * DH_C), lambda b, q, k: (b * nq + q, 0)),
                  pl.BlockSpec((kblk, KVH_C * DH_C), lambda b, q, k: (b * nk + last(q, k), 0)),
                  pl.BlockSpec((kblk, KVH_C * DH_C), lambda b, q, k: (b * nk + last(q, k), 0)),
                  pl.BlockSpec((qblk, kblk), lambda b, q, k: (b * nq + q, last(q, k)))],
        out_specs=pl.BlockSpec((qblk, H_C * DH_C), lambda b, q, k: (b * nq + q, 0)),
        out_shape=jax.ShapeDtypeStruct((n_p * l_p, H_C * DH_C), BF16),
        scratch_shapes=[pltpu.VMEM((KVH_C, rows, LANES), F32), pltpu.VMEM((KVH_C, rows, LANES), F32),
                        pltpu.VMEM((KVH_C, rows, DH_C), F32)],
        compiler_params=_cparams(("parallel", "parallel", "arbitrary")),
        name="attn_prompt",
    )(qb_, kb_, vb_, mask)


def _dsa_sample_kernel(pt_ref, qi_ref, kw_ref, q_ref, kn_ref, vn_ref, utri_ref, kidx_hbm, k_hbm, v_hbm, o_ref,
                       kibuf, kbuf, vbuf, sems, sc_scr, key_scr, mask_scr, *, n_pages, page, l_s, topk, pool0):
    b = pl.program_id(0)
    past = n_pages * page
    s_len = past + page

    def page_copies(bb, slot, pg):
        phys = pt_ref[bb * n_pages + pg] + pool0
        rows = pl.ds(pg * page, page)
        cps = [pltpu.make_async_copy(kidx_hbm.at[phys], kibuf.at[slot, rows, :], sems.at[slot, 0])]
        for kh in range(KVH_C):
            cps.append(pltpu.make_async_copy(k_hbm.at[phys, :, kh, :], kbuf.at[slot, kh, rows, :], sems.at[slot, 1]))
            cps.append(pltpu.make_async_copy(v_hbm.at[phys, :, kh, :], vbuf.at[slot, kh, rows, :], sems.at[slot, 2]))
        return cps

    def fetch(bb, slot):
        for pg in range(n_pages):
            for cp in page_copies(bb, slot, pg):
                cp.start()

    @pl.when(b == 0)
    def _():
        kibuf[:, past:, :] = jnp.zeros((2, page, D_IDX), F32)
        kbuf[:, :, past:, :] = jnp.zeros((2, KVH_C, page, DH_C), F32)
        vbuf[:, :, past:, :] = jnp.zeros((2, KVH_C, page, DH_C), F32)
        fetch(0, 0)

    @pl.when(b + 1 < pl.num_programs(0))
    def _():
        fetch(b + 1, (b + 1) % 2)

    cur = b % 2
    for pg in range(n_pages):
        for cp in page_copies(b, cur, pg):
            cp.wait()
    kw = kw_ref[0]
    kibuf[cur, past:past + SROWS, :] = kw[:, :D_IDX]
    for kh in range(KVH_C):
        hs = slice(kh * DH_C, (kh + 1) * DH_C)
        kbuf[cur, kh, past:past + SROWS, :] = kn_ref[0][:, hs]
        vbuf[cur, kh, past:past + SROWS, :] = vn_ref[0][:, hs]

    dots = lax.dot_general(qi_ref[0], kibuf[cur].astype(BF16), (((1,), (1,)), ((), ())),
                           preferred_element_type=F32)
    sc = None
    for h in range(H_IDX):
        term = kw[:, D_IDX + h:D_IDX + h + 1] * jnp.maximum(dots[h * SROWS:(h + 1) * SROWS, :], 0.0)
        sc = term if sc is None else sc + term
    row = lax.broadcasted_iota(I32, (SROWS, s_len), 0)
    col = lax.broadcasted_iota(I32, (SROWS, s_len), 1)
    visible = (col < past) | ((col - past <= row) & (col - past < l_s))
    sc_scr[...] = jnp.where(visible, sc + 0.0, -jnp.inf)
    _topk_mask(sc_scr, utri_ref, mask_scr, key_scr, topk=topk, chunk=page, bits_per_step=2)

    bias = jnp.concatenate([(mask_scr[...] - 1.0) * (-NEG_BIG)] * GROUP_C, axis=0)
    q = q_ref[0]
    for kh in range(KVH_C):
        qh = jnp.concatenate([q[:, (kh * GROUP_C + g) * DH_C:(kh * GROUP_C + g + 1) * DH_C]
                              for g in range(GROUP_C)], axis=0)
        s = lax.dot_general(qh, kbuf[cur, kh].astype(BF16), (((1,), (1,)), ((), ())),
                            preferred_element_type=F32) + bias
        p = jnp.exp(s - _row_max(s))
        out = _dot(p.astype(BF16), vbuf[cur, kh].astype(BF16)) / _row_sum(p)
        for g in range(GROUP_C):
            o_ref[0, :, (kh * GROUP_C + g) * DH_C:(kh * GROUP_C + g + 1) * DH_C] = (
                out[g * SROWS:(g + 1) * SROWS, :].astype(o_ref.dtype))


def dsa_sample(page_table, qi_hm, kw3, q3, knew3, vnew3, cache_kidx3, cache_k4, cache_v4, *, pool0, l_s, topk):
    n_s, n_pages = page_table.shape
    page = cache_k4.shape[1]
    kvw = KVH_C * DH_C
    s_len = (n_pages + 1) * page
    own = lambda wd: pl.BlockSpec((1, SROWS, wd), lambda b, pt: (b, 0, 0))
    hbm = pl.BlockSpec(memory_space=pl.ANY)
    return pl.pallas_call(
        functools.partial(_dsa_sample_kernel, n_pages=n_pages, page=page, l_s=l_s, topk=topk, pool0=pool0),
        grid_spec=pltpu.PrefetchScalarGridSpec(
            num_scalar_prefetch=1,
            grid=(n_s,),
            in_specs=[pl.BlockSpec((1, H_IDX * SROWS, D_IDX), lambda b, pt: (b, 0, 0)),
                      own(LANES), own(H_C * DH_C), own(kvw), own(kvw),
                      pl.BlockSpec((page, page), lambda b, pt: (0, 0)), hbm, hbm, hbm],
            out_specs=own(H_C * DH_C),
            scratch_shapes=[pltpu.VMEM((2, s_len, D_IDX), F32), pltpu.VMEM((2, KVH_C, s_len, DH_C), F32),
                            pltpu.VMEM((2, KVH_C, s_len, DH_C), F32), pltpu.SemaphoreType.DMA((2, 3)),
                            pltpu.VMEM((SROWS, s_len), F32), pltpu.VMEM((SROWS, s_len), I32),
                            pltpu.VMEM((SROWS, s_len), F32)],
        ),
        out_shape=jax.ShapeDtypeStruct((n_s, SROWS, H_C * DH_C), BF16),
        compiler_params=_cparams(("arbitrary",)),
        name="dsa_sample",
    )(page_table.reshape(-1), qi_hm, kw3, q3, knew3, vnew3, _strict_upper(page), cache_kidx3, cache_k4, cache_v4)


def c_params(li, w):
    p = {}
    w_in = w["c_w_in"][li]
    p["w_qkv"] = w_in[:, :C_QKV].astype(BF16)
    p["w_qi"] = w_in[:, C_QKV:C_QKV + C_QI].astype(BF16)
    p["w_kw"] = _pad_cols(w_in[:, C_QKV + C_QI:], LANES).astype(BF16)
    p["ln_g"] = _pad_cols(w["c_kidx_ln_g"][li][None, :], LANES)
    p["ln_b"] = _pad_cols(w["c_kidx_ln_b"][li][None, :], LANES)
    p["w_out"] = w["c_w_out"][li].astype(BF16)
    return p


def _pad_sample_rows(a, n_s, l_s):
    a3 = a.reshape(n_s, l_s, a.shape[1])
    return jnp.pad(a3, ((0, 0), (0, SROWS - l_s), (0, 0)))


def c_layer(xb, geo, p, li, cache_k, cache_v, cache_kidx, page_table, *, tm, qblk, kblk):
    n_p, l_p, n_s, l_s = geo
    tp = n_p * l_p
    n_pool, page = cache_k.shape[1], cache_k.shape[2]
    n_pages = page_table.shape[1]
    past = n_pages * page
    qkv = matmul([xb], [p["w_qkv"]], tm, 1024, name="c_in_qkv")
    qi = matmul([xb], [p["w_qi"]], tm, C_QI, name="c_in_qi")
    kw = matmul([xb], [p["w_kw"]], tm, LANES, name="c_in_kw")
    pos = jnp.concatenate([jnp.tile(jnp.arange(l_p, dtype=I32), n_p),
                           jnp.tile(past + jnp.arange(l_s, dtype=I32), n_s)])
    q_b, k_f, k_b, v_b, qi_b, kwo = c_post(qkv, qi, kw, _rope_tables(pos), p["ln_g"], p["ln_b"], tm)
    v_f = qkv[:, (H_C + KVH_C) * DH_C:]
    mask_p = index_select_causal(qi_b, kwo, min(TOPK_MAX, l_p // 4), n_p, l_p, qblk, 256)
    o_p = attn_prompt(q_b, k_b, v_b, mask_p, n_p, l_p, qblk, kblk)
    pad = lambda a: _pad_sample_rows(a[tp:], n_s, l_s)
    qi_hm = pad(qi_b).reshape(n_s, SROWS, H_IDX, D_IDX).transpose(0, 2, 1, 3).reshape(n_s, H_IDX * SROWS, D_IDX)
    pool = lambda a: a.reshape((-1,) + a.shape[2:])
    o_s3 = dsa_sample(page_table, qi_hm, pad(kwo), pad(q_b), pad(k_f), pad(v_f),
                      pool(cache_kidx), pool(cache_k), pool(cache_v), pool0=li * n_pool, l_s=l_s,
                      topk=min(TOPK_MAX, (past + l_s) // 4))
    o = jnp.concatenate([o_p, o_s3[:, :l_s].reshape(n_s * l_s, -1)], axis=0)
    h = matmul([o], [p["w_out"]], tm, 1024, name="c_out")
    heads = lambda a, n, l: a.reshape(n, l, KVH_C, DH_C)
    new_p = (heads(k_f[:tp], n_p, l_p), heads(v_f[:tp], n_p, l_p), kwo[:tp, :D_IDX].reshape(n_p, l_p, D_IDX))
    new_s = (heads(k_f[tp:], n_s, l_s), heads(v_f[tp:], n_s, l_s), kwo[tp:, :D_IDX].reshape(n_s, l_s, D_IDX))
    return h, new_p, new_s


TM = 256
BLK = 128
TM_DMA = 64
QBLK = 128
KBLK = 512


def kernel(x_prompt, x_sample, state_shift, state_wkv, state_ssm_re, state_ssm_im, cache_k, cache_v, cache_kidx, page_table, ab_w_in, ab_mu, rwkv_w0, rwkv_w2, rwkv_a0, rwkv_a2, rwkv_g2, rwkv_k_k, rwkv_k_a, rwkv_r_k, rwkv_ln_g, rwkv_ln_b, s5_lam_re, s5_lam_im, s5_log_dt, s5_b_re, s5_b_im, s5_c_re, s5_c_im, s5_d, s5_w_glu, s5_b_glu, ab_w_out, c_w_in, c_kidx_ln_g, c_kidx_ln_b, c_w_out, ln_mix_g, ln_mix_b, ln_ffn_g, ln_ffn_b, moe_w_router, moe_b_router, moe_w_gate, moe_w_up, moe_w_down, moe_ws_gate, moe_ws_up, moe_ws_down):
    w = dict(locals())
    n_p, l_p, _ = x_prompt.shape
    n_s, l_s, _ = x_sample.shape
    geo = (n_p, l_p, n_s, l_s)
    tp = n_p * l_p
    y = jnp.concatenate([x_prompt.reshape(-1, D_MODEL), x_sample.reshape(-1, D_MODEL)], axis=0)
    yb = y.astype(BF16)
    outs = {k: [] for k in ("kp", "vp", "kip", "ks", "vs", "kis", "shp", "shs", "wkvp", "wkvs",
                            "srp", "sip", "srs", "sis")}
    for layer in range(DEPTH):
        li = layer // 2
        if layer % 2 == 0:
            wkv_all = state_wkv.reshape((-1,) + state_wkv.shape[2:])
            h, shift, wkv, ssm_p, ssm_s = ab_layer(yb, geo, ab_params(li, w, l_s), state_shift[li], wkv_all, li * n_s,
                                                   state_ssm_re[li], state_ssm_im[li], tm=TM, blk=BLK)
            for k, v in zip(("shp", "shs", "wkvp", "wkvs", "srp", "sip", "srs", "sis"),
                            (*shift, *wkv, *ssm_p, *ssm_s)):
                outs[k].append(v)
        else:
            h, new_p, new_s = c_layer(yb, geo, c_params(li, w), li, cache_k, cache_v, cache_kidx,
                                      page_table, tm=TM, qblk=QBLK, kblk=KBLK)
            for k, v in zip(("kp", "vp", "kip", "ks", "vs", "kis"), (*new_p, *new_s)):
                outs[k].append(v)
        y, yb = add_ln(y, h, ln_mix_g[layer], ln_mix_b[layer], TM)
        y, yb = moe_layer(y, yb, moe_params(layer, w), ln_ffn_g[layer], ln_ffn_b[layer], tm=TM, tm_dma=TM_DMA)
    st = lambda k: jnp.stack(outs[k])
    return (y[:tp].reshape(n_p, l_p, D_MODEL), y[tp:].reshape(n_s, l_s, D_MODEL),
            st("kp"), st("vp"), st("kip"), st("ks"), st("vs"), st("kis"),
            st("shp"), st("shs"), st("wkvp"), st("wkvs"), st("srp"), st("sip"), st("srs"), st("sis"))
```

```python
import functools
import math

import jax
import jax.numpy as jnp
from jax import lax
from jax.experimental import pallas as pl
from jax.experimental.pallas import tpu as pltpu

F32, BF16, I32 = jnp.float32, jnp.bfloat16, jnp.int32

D_MODEL = 2048
DEPTH = 2
ALPHA = (2.0 * DEPTH) ** 0.25
LN_EPS = 1e-5
C_A = 1024
HS_A = 64
H_A = 16
LORA_W, LORA_A, LORA_G = 64, 64, 160
A_PROJ = 3 * C_A + LORA_W + LORA_A + LORA_G
GN_EPS_A = 64e-5
C_B = 1024
S5_P = 16
S5_GROUPS = 64
S5_N = 64
S5_STATE = S5_GROUPS * S5_N
DH_C = 128
H_C = 16
KVH_C = 4
GROUP_C = 4
H_IDX = 16
D_IDX = 64
TOPK_MAX = 256
ROPE_THETA = 10000.0
N_EXPERTS = 64
TOP_K = 8
N_EXPERT_GROUPS = 8
TOPK_GROUPS = 4
D_EXPERT = 512
ROUTED_SCALE = 2.5

LANES = 128
SUBLANES = 8
VMEM_LIMIT = 56 * 1024 * 1024

LORA_PAD = 512
EXP_M05 = math.exp(-0.5)


def _cparams(sem):
    return pltpu.CompilerParams(dimension_semantics=sem, vmem_limit_bytes=VMEM_LIMIT)


def _dot(a, b):
    return jnp.dot(a, b, preferred_element_type=F32)


def _lane_fold(x, op):
    tiles = [x[:, i * LANES:(i + 1) * LANES] for i in range(x.shape[1] // LANES)]
    return functools.reduce(op, tiles)


def _row_sum(x):
    return jnp.sum(_lane_fold(x, jnp.add), axis=-1, keepdims=True)


def _row_max(x):
    return jnp.max(_lane_fold(x, jnp.maximum), axis=-1, keepdims=True)


def _split3(a):
    a1 = a.astype(BF16)
    r1 = a - a1.astype(F32)
    a2 = r1.astype(BF16)
    a3 = (r1 - a2.astype(F32)).astype(BF16)
    return a1, a2, a3


def _dot_sel(a, sel_bf16):
    a1, a2, a3 = _split3(a)
    return _dot(a1, sel_bf16) + _dot(a2, sel_bf16) + _dot(a3, sel_bf16)


def _dot3(a, b):
    a1 = a.astype(BF16)
    a2 = (a - a1.astype(F32)).astype(BF16)
    b1 = b.astype(BF16)
    b2 = (b - b1.astype(F32)).astype(BF16)
    return _dot(a1, b1) + _dot(a1, b2) + _dot(a2, b1)


def _mm_kernel(*refs, n_lhs):
    o_ref = refs[-1]
    acc = None
    for i in range(n_lhs):
        d = _dot(refs[i][...], refs[n_lhs + i][...])
        acc = d if acc is None else acc + d
    o_ref[...] = acc.astype(o_ref.dtype)


MM_TM = 512


def matmul(xs, ws, tm, tn, out_dtype=F32, name="mm"):
    m = xs[0].shape[0]
    n = ws[0].shape[1]
    tm = MM_TM if m % MM_TM == 0 else tm
    assert m % tm == 0 and n % tn == 0, (m, tm, n, tn)
    in_specs = ([pl.BlockSpec((tm, x.shape[1]), lambda j, i: (i, 0)) for x in xs]
                + [pl.BlockSpec((w.shape[0], tn), lambda j, i: (0, j)) for w in ws])
    return pl.pallas_call(
        functools.partial(_mm_kernel, n_lhs=len(xs)),
        grid=(n // tn, m // tm),
        in_specs=in_specs,
        out_specs=pl.BlockSpec((tm, tn), lambda j, i: (i, j)),
        out_shape=jax.ShapeDtypeStruct((m, n), out_dtype),
        compiler_params=_cparams(("parallel", "parallel")),
        name=name,
    )(*xs, *ws)


def _add_ln_kernel(x_ref, h_ref, g_ref, b_ref, o_ref, ob_ref):
    z = ALPHA * x_ref[...] + h_ref[...]
    inv_d = 1.0 / z.shape[1]
    mu = _row_sum(z) * inv_d
    d = z - mu
    var = _row_sum(d * d) * inv_d
    y = d * lax.rsqrt(var + LN_EPS) * g_ref[...] + b_ref[...]
    o_ref[...] = y
    ob_ref[...] = y.astype(BF16)


def add_ln(x, h, g, b, tm):
    t, d = x.shape
    row = pl.BlockSpec((tm, d), lambda i: (i, 0))
    vec = pl.BlockSpec((1, d), lambda i: (0, 0))
    return pl.pallas_call(
        _add_ln_kernel,
        grid=(t // tm,),
        in_specs=[row, row, vec, vec],
        out_specs=[row, row],
        out_shape=[jax.ShapeDtypeStruct((t, d), F32), jax.ShapeDtypeStruct((t, d), BF16)],
        compiler_params=_cparams(("parallel",)),
        name="add_ln",
    )(x, h, g.reshape(1, d), b.reshape(1, d))


def _rwkv_prep_kernel(rkv_ref, rkvp_ref, lo_ref, lop_ref, mu1_ref, mu2_ref, w0_ref, w2_ref, a0_ref, a2_ref,
                      g2_ref, kk_ref, ka_ref, rk_ref, gsel_ref, gselt_ref,
                      kk_o, wr_o, w_o, b_o, k_o, v_o, kr_o, g_o, bonus_o):
    x = rkv_ref[...]
    m = x + (rkvp_ref[...] - x) * mu1_ref[...]
    r = m[:, :C_A]
    k = m[:, C_A:2 * C_A]
    v = m[:, 2 * C_A:]
    l = lo_ref[...]
    lm = l + (lop_ref[...] - l) * mu2_ref[...]
    lw = lm[:, :LANES]
    la = lm[:, LANES:2 * LANES]
    lg = lm[:, 2 * LANES:]
    w_raw = w0_ref[...] + _dot3(jnp.tanh(lw), w2_ref[...])
    decay = jnp.exp(-EXP_M05 * jax.nn.sigmoid(w_raw))
    a = jax.nn.sigmoid(a0_ref[...] + _dot3(la, a2_ref[...]))
    g = _dot3(jax.nn.sigmoid(lg), g2_ref[...])
    gsel = gsel_ref[...]
    gselt = gselt_ref[...]
    kk = k * kk_ref[...]
    nrm = jnp.sqrt(_dot_sel(kk * kk, gsel))
    inv = 1.0 / jnp.maximum(nrm, 1e-12)
    kk = kk * _dot_sel(inv, gselt)
    k_mod = k * (1.0 + (a - 1.0) * ka_ref[...])
    sb = _dot_sel(r * k_mod * rk_ref[...], gsel)
    bonus_o[...] = _dot_sel(sb, gselt) * v
    g_o[...] = g
    b = kk * a
    kk_o[...] = kk
    br = _dot_sel(_dot_sel(b * r, gsel), gselt)
    wr_o[...] = decay * r - kk * br
    w_o[...] = decay
    b_o[...] = b
    k_o[...] = k_mod
    v_o[...] = v
    kr_o[...] = _dot_sel(_dot_sel(k_mod * r, gsel), gselt)


RWKV_ROWS = 7


def rwkv_prep(rkv, rkv_prev, lora, lora_prev, p, tm):
    t = rkv.shape[0]
    row = lambda w: pl.BlockSpec((tm, w), lambda i: (i, 0))
    full = lambda a: pl.BlockSpec(a.shape, lambda i: (0,) * a.ndim)
    consts = [p["mu_rkv"], p["mu_lora"], p["w0"], p["w2"], p["a0"], p["a2"], p["g2"], p["k_k"], p["k_a"], p["r_k"],
              p["gsel"], p["gselt"]]
    n_out = RWKV_ROWS + 2
    return pl.pallas_call(
        _rwkv_prep_kernel,
        grid=(t // tm,),
        in_specs=[row(3 * C_A), row(3 * C_A), row(LORA_PAD), row(LORA_PAD)] + [full(c) for c in consts],
        out_specs=[row(C_A)] * n_out,
        out_shape=[jax.ShapeDtypeStruct((t, C_A), F32)] * n_out,
        compiler_params=_cparams(("parallel",)),
        name="rwkv_prep",
    )(rkv, rkv_prev, lora, lora_prev, *consts)


TILE_HEADS = 4
TILE_W = TILE_HEADS * HS_A
N_TILES = H_A // TILE_HEADS
LHS_PER_TILE = 3
WKV_UNROLL = 4


def _wkv_steps(rows, row_base, n_steps, npar, s_scr, bd_ref, store_o):
    sub = lax.broadcasted_iota(I32, (HS_A, TILE_W), 0)
    lane = lax.broadcasted_iota(I32, (HS_A, TILE_W), 1)
    diag = (lane % HS_A) == sub
    bd = bd_ref[...]
    blk_rows = LHS_PER_TILE * HS_A

    def step(t, carry):
        vec = [[ref[pl.ds(row_base[s] + t, 1), :] for ref in rows[s]] for s in range(npar)]
        results = []
        for s in range(npar):
            kk, wr2, _, _, _, v, _ = vec[s]
            for p in range(N_TILES):
                ls = slice(p * TILE_W, (p + 1) * TILE_W)
                st = s_scr[s * N_TILES + p]
                q1 = (st * kk[:, ls]).astype(BF16)
                q2 = (st * wr2[:, ls]).astype(BF16)
                ve = jnp.where(diag, v[:, ls], 0.0).astype(BF16)
                results.append(_dot(jnp.concatenate([q1, q2, ve], axis=0), bd))
        for s in range(npar):
            _, _, w, b, k, v, kr = vec[s]
            o_parts = []
            for p in range(N_TILES):
                ls = slice(p * TILE_W, (p + 1) * TILE_W)
                res = results[s * N_TILES + p]
                z1 = res[:HS_A]
                z2 = res[HS_A:2 * HS_A]
                vb = res[2 * HS_A:]
                st = s_scr[s * N_TILES + p]
                s_scr[s * N_TILES + p] = st * w[:, ls] - z1 * b[:, ls] + vb * k[:, ls]
                z2r = jnp.sum(jnp.where(diag, z2, 0.0), axis=0, keepdims=True)
                o_parts.append(z2r + v[:, ls] * kr[:, ls])
            store_o(s, t, jnp.concatenate(o_parts, axis=1))
        return carry

    lax.fori_loop(0, n_steps, step, 0, unroll=WKV_UNROLL)


def _pack_state(s_ref, q, s_scr, slot):
    for p in range(N_TILES):
        s_scr[slot * N_TILES + p] = jnp.concatenate([s_ref[q, TILE_HEADS * p + h] for h in range(TILE_HEADS)], axis=1)


def _unpack_state(s_scr, slot, s_ref, q):
    for p in range(N_TILES):
        st = s_scr[slot * N_TILES + p]
        for h in range(TILE_HEADS):
            s_ref[q, TILE_HEADS * p + h] = st[:, h * HS_A:(h + 1) * HS_A]


def _wkv_long_kernel(*refs, npar, blk):
    n_in = npar * RWKV_ROWS
    rows = [refs[s * RWKV_ROWS:(s + 1) * RWKV_ROWS] for s in range(npar)]
    bd_ref, o_ref, sout_ref, s_scr = refs[n_in:n_in + 4]
    c = pl.program_id(0)

    @pl.when(c == 0)
    def _():
        s_scr[...] = jnp.zeros(s_scr.shape, F32)

    def store_o(s, t, o_row):
        o_ref[s, pl.ds(t, 1), :] = o_row

    _wkv_steps(rows, [0] * npar, blk, npar, s_scr, bd_ref, store_o)

    @pl.when(c == pl.num_programs(0) - 1)
    def _():
        for s in range(npar):
            _unpack_state(s_scr, s, sout_ref, s)


WKV_NPAR_SHORT = 2
WKV_SPB = 8


def _wkv_short_kernel(*refs, seq_len, spb):
    rows_refs = refs[:RWKV_ROWS]
    s0_ref, bd_ref, o_ref, sout_ref, s_scr = refs[RWKV_ROWS:RWKV_ROWS + 5]
    c = pl.program_id(1)
    npar = WKV_NPAR_SHORT

    def group(gq, carry):
        qs = [gq * npar + s for s in range(npar)]
        for s in range(npar):
            _pack_state(s0_ref, qs[s], s_scr, s)

        def store_o(s, t, o_row):
            o_ref[pl.ds((c * spb + qs[s]) * seq_len + t, 1), :] = o_row

        _wkv_steps([rows_refs] * npar, [(c * spb + q) * seq_len for q in qs], seq_len, npar, s_scr, bd_ref, store_o)
        for s in range(npar):
            _unpack_state(s_scr, s, sout_ref, qs[s])
        return carry

    lax.fori_loop(0, spb // npar, group, 0)


def _head_ones():
    head = jnp.arange(TILE_W) // HS_A
    return (head[:, None] == head[None, :]).astype(BF16)


def wkv_long(rows, *, n_seq, seq_len, blk):
    cps = seq_len // blk
    in_specs = [pl.BlockSpec((blk, C_A), functools.partial(lambda c, s: (s * cps + c, 0), s=s))
                for s in range(n_seq) for _ in range(RWKV_ROWS)]
    return pl.pallas_call(
        functools.partial(_wkv_long_kernel, npar=n_seq, blk=blk),
        grid=(cps,),
        in_specs=in_specs + [pl.BlockSpec((TILE_W, TILE_W), lambda c: (0, 0))],
        out_specs=[pl.BlockSpec((n_seq, blk, C_A), lambda c: (0, c, 0)),
                   pl.BlockSpec((n_seq, H_A, HS_A, HS_A), lambda c: (0, 0, 0, 0))],
        out_shape=[jax.ShapeDtypeStruct((n_seq, seq_len, C_A), F32),
                   jax.ShapeDtypeStruct((n_seq, H_A, HS_A, HS_A), F32)],
        scratch_shapes=[pltpu.VMEM((n_seq * N_TILES, HS_A, TILE_W), F32)],
        compiler_params=_cparams(("arbitrary",)),
        name="wkv_long",
    )(*(list(rows) * n_seq), _head_ones())


def wkv_short(rows, s0, *, row0, n_seq, seq_len, blk, s0_seq0):
    assert row0 % blk == 0 and blk % seq_len == 0
    b0 = row0 // blk
    nsb = blk // seq_len
    spb = min(WKV_SPB, nsb)
    assert n_seq % nsb == 0 and nsb % spb == 0 and spb % WKV_NPAR_SHORT == 0 and s0_seq0 % spb == 0
    sub = nsb // spb
    sspec = lambda off: pl.BlockSpec((spb, H_A, HS_A, HS_A), lambda s, c: (off + s * sub + c, 0, 0, 0))
    return pl.pallas_call(
        functools.partial(_wkv_short_kernel, seq_len=seq_len, spb=spb),
        grid=(n_seq // nsb, sub),
        in_specs=[pl.BlockSpec((blk, C_A), lambda s, c: (b0 + s, 0))] * RWKV_ROWS
        + [sspec(s0_seq0 // spb), pl.BlockSpec((TILE_W, TILE_W), lambda s, c: (0, 0))],
        out_specs=[pl.BlockSpec((blk, C_A), lambda s, c: (s, 0)), sspec(0)],
        out_shape=[jax.ShapeDtypeStruct((n_seq * seq_len, C_A), F32),
                   jax.ShapeDtypeStruct((n_seq, H_A, HS_A, HS_A), F32)],
        scratch_shapes=[pltpu.VMEM((WKV_NPAR_SHORT * N_TILES, HS_A, TILE_W), F32)],
        compiler_params=_cparams(("parallel", "arbitrary")),
        name="wkv_short",
    )(*rows, s0, _head_ones())


def _rwkv_post_kernel(o_in_ref, bonus_ref, g_ref, lng_ref, lnb_ref, gsel_ref, gselt_ref, o_ref):
    o = o_in_ref[...]
    gsel = gsel_ref[...]
    gselt = gselt_ref[...]
    mu = _dot_sel(o, gsel) * (1.0 / HS_A)
    d = o - _dot_sel(mu, gselt)
    var = _dot_sel(d * d, gsel) * (1.0 / HS_A)
    rstd = lax.rsqrt(var + GN_EPS_A)
    y = d * _dot_sel(rstd, gselt) * lng_ref[...] + lnb_ref[...]
    o_ref[...] = ((y + bonus_ref[...]) * g_ref[...]).astype(o_ref.dtype)


def rwkv_post(o_wkv, bonus, g, p, tm):
    t = bonus.shape[0]
    row = pl.BlockSpec((tm, C_A), lambda i: (i, 0))
    full = lambda a: pl.BlockSpec(a.shape, lambda i: (0,) * a.ndim)
    consts = [p["ln_g"], p["ln_b"], p["gsel"], p["gselt"]]
    return pl.pallas_call(
        _rwkv_post_kernel,
        grid=(t // tm,),
        in_specs=[row, row, row] + [full(c) for c in consts],
        out_specs=row,
        out_shape=jax.ShapeDtypeStruct((t, C_A), BF16),
        compiler_params=_cparams(("parallel",)),
        name="rwkv_post",
    )(o_wkv, bonus, g, *consts)


S5_KT = 4
S5_KU = C_B // S5_KT
S5_KH = S5_STATE // S5_KT
S5_LC = 512


def _gelu_tanh(x):
    return 0.5 * x * (1.0 + jnp.tanh(math.sqrt(2.0 / math.pi) * (x + 0.044715 * (x * x * x))))


def _s5_kernel(u_ref, wbr_ref, wbi_ref, lamk_ref, pre_ref, pim_ref, h0r_ref, h0i_ref, wcr_ref, wci_ref,
               dsk_ref, wglu_ref, bglu_ref, ob_ref, hr_out, hi_out, hre, him, car_re, car_im, y_scr,
               *, seq_len, blk):
    long_mode = seq_len >= blk
    period = SUBLANES if long_mode else seq_len
    u = u_ref[...]
    ub = u.astype(BF16)
    for kt in range(S5_KT):
        uk = ub[:, kt * S5_KU:(kt + 1) * S5_KU]
        hre[:, kt * S5_KH:(kt + 1) * S5_KH] = _dot(uk, wbr_ref[kt])
        him[:, kt * S5_KH:(kt + 1) * S5_KH] = _dot(uk, wbi_ref[kt])

    if long_mode:
        @pl.when(pl.program_id(1) == 0)
        def _():
            car_re[...] = jnp.zeros(car_re.shape, F32)
            car_im[...] = jnp.zeros(car_im.shape, F32)

    rowi = lax.broadcasted_iota(I32, (SUBLANES, 1), 0) % period
    for lc in range(S5_STATE // S5_LC):
        ls = slice(lc * S5_LC, (lc + 1) * S5_LC)
        lam = [(lamk_ref[2 * i:2 * i + 1, ls], lamk_ref[2 * i + 1:2 * i + 2, ls]) for i in range(3)]
        pre = pre_ref[:, ls]
        pim = pim_ref[:, ls]

        def tile(i, carry, ls=ls, lam=lam, pre=pre, pim=pim):
            r0 = pl.multiple_of(i * SUBLANES, SUBLANES)
            xr = hre[pl.ds(r0, SUBLANES), ls]
            xi = him[pl.ds(r0, SUBLANES), ls]
            for lvl, sh in enumerate((1, 2, 4)):
                if sh >= period:
                    break
                lr, li = lam[lvl]
                keep = rowi >= sh
                sr = jnp.where(keep, pltpu.roll(xr, sh, 0), 0.0)
                si = jnp.where(keep, pltpu.roll(xi, sh, 0), 0.0)
                xr, xi = xr + lr * sr - li * si, xi + lr * si + li * sr
            if long_mode:
                hr, hi = carry
            else:
                hr = h0r_ref[pl.ds(r0, SUBLANES), ls]
                hi = h0i_ref[pl.ds(r0, SUBLANES), ls]
            xr, xi = xr + pre * hr - pim * hi, xi + pre * hi + pim * hr
            hre[pl.ds(r0, SUBLANES), ls] = xr
            him[pl.ds(r0, SUBLANES), ls] = xi
            if long_mode:
                last_r = jnp.broadcast_to(xr[SUBLANES - 1:SUBLANES, :], xr.shape)
                last_i = jnp.broadcast_to(xi[SUBLANES - 1:SUBLANES, :], xi.shape)
                return (last_r, last_i)
            return carry

        if long_mode:
            init = (car_re[:, ls], car_im[:, ls])
            fin = lax.fori_loop(0, blk // SUBLANES, tile, init, unroll=2)
            car_re[:, ls] = fin[0]
            car_im[:, ls] = fin[1]
        else:
            lax.fori_loop(0, blk // SUBLANES, tile, 0)

    if long_mode:
        hr_out[0] = car_re[...]
        hi_out[0] = car_im[...]
    else:
        hr_out[...] = hre[...]
        hi_out[...] = him[...]

    for kt in range(S5_KT):
        hs = slice(kt * S5_KH, (kt + 1) * S5_KH)
        y_scr[:, kt * S5_KU:(kt + 1) * S5_KU] = (_dot(hre[:, hs].astype(BF16), wcr_ref[kt])
                                                 - _dot(him[:, hs].astype(BF16), wci_ref[kt]))
    y = y_scr[...] + dsk_ref[...] * u
    z = _gelu_tanh(y)
    gate = jax.nn.sigmoid(_dot(z.astype(BF16), wglu_ref[...]) + bglu_ref[...])
    ob_ref[...] = (z * gate).astype(ob_ref.dtype)


def s5_mix(u, p, h0r_rows, h0i_rows, *, row0, n_seq, seq_len, blk):
    long_mode = seq_len >= blk
    assert row0 % blk == 0
    b0 = row0 // blk
    t_out = n_seq * seq_len
    if long_mode:
        cps = seq_len // blk
        grid = (n_seq, cps)
        umap = lambda s, c: (b0 + s * cps + c, 0)
        omap = lambda s, c: (s * cps + c, 0)
        hspec = pl.BlockSpec((1, SUBLANES, S5_STATE), lambda s, c: (s, 0, 0))
        hshape = jax.ShapeDtypeStruct((n_seq, SUBLANES, S5_STATE), F32)
        h0spec = pl.BlockSpec((SUBLANES, S5_STATE), lambda s, c: (0, 0))
    else:
        assert t_out % blk == 0
        grid = (t_out // blk, 1)
        umap = lambda s, c: (b0 + s, 0)
        omap = lambda s, c: (s, 0)
        hspec = pl.BlockSpec((blk, S5_STATE), omap)
        hshape = jax.ShapeDtypeStruct((t_out, S5_STATE), F32)
        h0spec = pl.BlockSpec((blk, S5_STATE), omap)
    full = lambda a: pl.BlockSpec(a.shape, lambda s, c: (0,) * a.ndim)
    pre, pim = (p["pow_re8"], p["pow_im8"]) if long_mode else (p["pow_re_s"], p["pow_im_s"])
    args = [u, p["wb_re"], p["wb_im"], p["lamk"], pre, pim, h0r_rows, h0i_rows, p["wc_re"], p["wc_im"],
            p["d_skip"], p["w_glu"], p["b_glu"]]
    in_specs = [pl.BlockSpec((blk, C_B), umap)] + [full(a) for a in args[1:6]] + [h0spec, h0spec] \
        + [full(a) for a in args[8:]]
    return pl.pallas_call(
        functools.partial(_s5_kernel, seq_len=seq_len, blk=blk),
        grid=grid,
        in_specs=in_specs,
        out_specs=[pl.BlockSpec((blk, C_B), omap), hspec, hspec],
        out_shape=[jax.ShapeDtypeStruct((t_out, C_B), BF16), hshape, hshape],
        scratch_shapes=[pltpu.VMEM((blk, S5_STATE), F32), pltpu.VMEM((blk, S5_STATE), F32),
                        pltpu.VMEM((SUBLANES, S5_STATE), F32), pltpu.VMEM((SUBLANES, S5_STATE), F32),
                        pltpu.VMEM((blk, C_B), F32)],
        compiler_params=_cparams(("parallel", "arbitrary")),
        name="s5_long" if long_mode else "s5_short",
    )(*args)


def _pad_cols(a, n):
    return jnp.pad(a, ((0, 0), (0, n - a.shape[1])))


def _pad_rows(a, n):
    return jnp.pad(a, ((0, n - a.shape[0]), (0, 0)))


def _lora_layout(a):
    o = 3 * C_A
    return jnp.concatenate([_pad_cols(a[:, o:o + LORA_W], LANES),
                            _pad_cols(a[:, o + LORA_W:o + LORA_W + LORA_A], LANES),
                            _pad_cols(a[:, o + LORA_W + LORA_A:A_PROJ], 2 * LANES)], axis=1)


def _lora_unlayout(a):
    return jnp.concatenate([a[:, :LORA_W], a[:, LANES:LANES + LORA_A], a[:, 2 * LANES:2 * LANES + LORA_G]], axis=1)


def _cpow_table(lbr, lbi, n):
    res_r, res_i = [lbr], [lbi]
    for _ in range(n - 1):
        pr, pi = res_r[-1], res_i[-1]
        res_r.append(pr * lbr - pi * lbi)
        res_i.append(pr * lbi + pi * lbr)
    return (jnp.stack([x.reshape(-1) for x in res_r]), jnp.stack([x.reshape(-1) for x in res_i]))


def ab_params(li, w, sample_len):
    p = {}
    w_in = w["ab_w_in"][li]
    p["w_rkv"] = w_in[:, :3 * C_A].astype(BF16)
    p["w_lora"] = _lora_layout(w_in).astype(BF16)
    p["w_u"] = w_in[:, A_PROJ:].astype(BF16)
    mu = w["ab_mu"][li][None, :]
    p["mu_rkv"] = mu[:, :3 * C_A]
    p["mu_lora"] = _lora_layout(mu)
    row = lambda a: a.reshape(1, -1)
    p["w0"] = row(w["rwkv_w0"][li])
    p["w2"] = _pad_rows(w["rwkv_w2"][li], LANES)
    p["a0"] = row(w["rwkv_a0"][li])
    p["a2"] = _pad_rows(w["rwkv_a2"][li], LANES)
    p["g2"] = _pad_rows(w["rwkv_g2"][li], 2 * LANES)
    p["k_k"] = row(w["rwkv_k_k"][li])
    p["k_a"] = row(w["rwkv_k_a"][li])
    p["r_k"] = row(w["rwkv_r_k"][li])
    p["ln_g"] = row(w["rwkv_ln_g"][li])
    p["ln_b"] = row(w["rwkv_ln_b"][li])
    head_of_col = jnp.arange(C_A) // HS_A
    gsel = (head_of_col[:, None] == jnp.arange(LANES)[None, :])
    p["gsel"] = gsel.astype(BF16)
    p["gselt"] = gsel.T.astype(BF16)
    lr = jnp.minimum(w["s5_lam_re"][li], -1e-4)
    lim = w["s5_lam_im"][li]
    dt = jnp.exp(w["s5_log_dt"][li])[:, None]
    mag = jnp.exp(lr * dt)
    lbr, lbi = mag * jnp.cos(lim * dt), mag * jnp.sin(lim * dt)
    den = lr * lr + lim * lim
    pr, pim = lbr - 1.0, lbi
    fr = (pr * lr + pim * lim) / den
    fi = (pim * lr - pr * lim) / den
    br, bi = w["s5_b_re"][li], w["s5_b_im"][li]
    bbr = fr[..., None] * br - fi[..., None] * bi
    bbi = fr[..., None] * bi + fi[..., None] * br
    gpc = S5_GROUPS // S5_KT
    eye = jnp.eye(gpc, dtype=F32)

    def blockdiag_in(bb):
        b4 = bb.reshape(S5_KT, gpc, S5_N, S5_P)
        return jnp.einsum("kgnp,gh->kgphn", b4, eye).reshape(S5_KT, S5_KU, S5_KH).astype(BF16)

    def blockdiag_out(cc):
        c4 = cc.reshape(S5_KT, gpc, S5_P, S5_N)
        return jnp.einsum("kgpn,gh->kgnhp", c4, eye).reshape(S5_KT, S5_KH, S5_KU).astype(BF16)

    p["wb_re"], p["wb_im"] = blockdiag_in(bbr), blockdiag_in(bbi)
    p["wc_re"], p["wc_im"] = blockdiag_out(w["s5_c_re"][li]), blockdiag_out(w["s5_c_im"][li])
    pw_r, pw_i = _cpow_table(lbr, lbi, SUBLANES)
    p["pow_re8"], p["pow_im8"] = pw_r, pw_i
    reps = SUBLANES // sample_len
    p["pow_re_s"] = jnp.tile(pw_r[:sample_len], (reps, 1))
    p["pow_im_s"] = jnp.tile(pw_i[:sample_len], (reps, 1))
    p["lamk"] = jnp.stack([pw_r[0], pw_i[0], pw_r[1], pw_i[1], pw_r[3], pw_i[3]])
    p["d_skip"] = row(w["s5_d"][li])
    p["w_glu"] = w["s5_w_glu"][li].astype(BF16)
    p["b_glu"] = row(w["s5_b_glu"][li])
    w_out = w["ab_w_out"][li]
    p["w_out_a"] = w_out[:C_A].astype(BF16)
    p["w_out_b"] = w_out[C_A:].astype(BF16)
    return p


def _shift_rows(a, first_sample, geo):
    n_p, l_p, n_s, l_s = geo
    wd = a.shape[1]
    ap = a[:n_p * l_p].reshape(n_p, l_p, wd)
    prev_p = jnp.concatenate([jnp.zeros((n_p, 1, wd), a.dtype), ap[:, :-1]], axis=1)
    as_ = a[n_p * l_p:].reshape(n_s, l_s, wd)
    prev_s = jnp.concatenate([first_sample[:, None, :], as_[:, :-1]], axis=1)
    return jnp.concatenate([prev_p.reshape(-1, wd), prev_s.reshape(-1, wd)], axis=0)


def _last_rows(a, geo):
    n_p, l_p, n_s, l_s = geo
    wd = a.shape[1]
    return (a[:n_p * l_p].reshape(n_p, l_p, wd)[:, -1], a[n_p * l_p:].reshape(n_s, l_s, wd)[:, -1])


def ab_layer(xb, geo, p, state_shift, state_wkv, wkv_seq0, state_re, state_im, *, tm, blk):
    n_p, l_p, n_s, l_s = geo
    tp, ts = n_p * l_p, n_s * l_s
    rkv = matmul([xb], [p["w_rkv"]], tm, 1024, name="ab_in_rkv")
    lora = matmul([xb], [p["w_lora"]], tm, LORA_PAD, name="ab_in_lora")
    u = matmul([xb], [p["w_u"]], tm, C_B, name="ab_in_u")
    rkv_prev = _shift_rows(rkv, state_shift[:, :3 * C_A], geo)
    lora_prev = _shift_rows(lora, _lora_layout(state_shift), geo)
    prep = rwkv_prep(rkv, rkv_prev, lora, lora_prev, p, tm)
    rows, g, bonus = prep[:RWKV_ROWS], prep[RWKV_ROWS], prep[RWKV_ROWS + 1]
    o_p, wkv_p = wkv_long(rows, n_seq=n_p, seq_len=l_p, blk=blk)
    o_s, wkv_s = wkv_short(rows, state_wkv, row0=tp, n_seq=n_s, seq_len=l_s, blk=blk, s0_seq0=wkv_seq0)
    o_wkv = jnp.concatenate([o_p.reshape(tp, C_A), o_s], axis=0)
    o_a = rwkv_post(o_wkv, bonus, g, p, tm)
    dummy = jnp.zeros((SUBLANES, S5_STATE), F32)
    ob_p, hr_p, hi_p = s5_mix(u, p, dummy, dummy, row0=0, n_seq=n_p, seq_len=l_p, blk=blk)
    h0r = jnp.repeat(state_re.reshape(n_s, S5_STATE), l_s, axis=0)
    h0i = jnp.repeat(state_im.reshape(n_s, S5_STATE), l_s, axis=0)
    ob_s, hr_s, hi_s = s5_mix(u, p, h0r, h0i, row0=tp, n_seq=n_s, seq_len=l_s, blk=blk)
    o_b = jnp.concatenate([ob_p, ob_s], axis=0)
    h = matmul([o_a, o_b], [p["w_out_a"], p["w_out_b"]], tm, 1024, name="ab_out")
    rkv_lp, rkv_ls = _last_rows(rkv, geo)
    lo_lp, lo_ls = _last_rows(lora, geo)
    shift_p = jnp.concatenate([rkv_lp, _lora_unlayout(lo_lp)], axis=1)
    shift_s = jnp.concatenate([rkv_ls, _lora_unlayout(lo_ls)], axis=1)
    grp = lambda a, n: a.reshape(n, S5_GROUPS, S5_N)
    ssm_p = (grp(hr_p[:, SUBLANES - 1], n_p), grp(hi_p[:, SUBLANES - 1], n_p))
    ssm_s = (grp(hr_s[l_s - 1::l_s], n_s), grp(hi_s[l_s - 1::l_s], n_s))
    return h, (shift_p, shift_s), (wkv_p, wkv_s), ssm_p, ssm_s


def _router_kernel(x_ref, wr_ref, br_ref, ltri_ref, eidx_o, gate_o, rank_o, cnt_o, carry):
    i = pl.program_id(0)

    @pl.when(i == 0)
    def _():
        carry[...] = jnp.zeros(carry.shape, F32)

    tm = x_ref.shape[0]
    neg = -jnp.inf
    lane_i = lax.broadcasted_iota(I32, (tm, LANES), 1)
    lane = lane_i.astype(F32)
    per_group = N_EXPERTS // N_EXPERT_GROUPS
    grp = (lane_i // per_group).astype(F32)
    scores = jax.nn.sigmoid(_dot3(x_ref[...], wr_ref[...]))
    biased = jnp.where(lane_i < N_EXPERTS, scores + br_ref[...], neg)
    rmax = lambda a: jnp.max(a, axis=-1, keepdims=True)
    rmin = lambda a: jnp.min(a, axis=-1, keepdims=True)
    rsum = lambda a: jnp.sum(a, axis=-1, keepdims=True)

    gsc = jnp.full((tm, LANES), neg, F32)
    for g in range(N_EXPERT_GROUPS):
        xg = jnp.where(grp == g, biased, neg)
        m1 = rmax(xg)
        twice = rsum(jnp.where(xg == m1, 1.0, 0.0)) >= 2.0
        m2 = jnp.where(twice, m1, rmax(jnp.where(xg < m1, xg, neg)))
        gsc = jnp.where(lane == g, m1 + m2, gsc)
    keep = jnp.zeros((tm, LANES), jnp.bool_)
    for _ in range(TOPK_GROUPS):
        m = rmax(gsc)
        gi = rmin(jnp.where(gsc == m, lane, float(LANES)))
        keep = keep | (grp == gi)
        gsc = jnp.where(lane == gi, neg, gsc)
    cur = jnp.where(keep, biased, neg)

    sel = jnp.zeros((tm, LANES), F32)
    es, ss = [], []
    for _ in range(TOP_K):
        m = rmax(cur)
        ik = rmin(jnp.where(cur == m, lane, float(LANES)))
        hit = lane == ik
        ss.append(rsum(jnp.where(hit, scores, 0.0)))
        es.append(ik)
        cur = jnp.where(hit, neg, cur)
        sel = jnp.where(hit, 1.0, sel)
    tot = ss[0]
    for k in range(1, TOP_K):
        tot = tot + ss[k]

    prefix = _dot(ltri_ref[...], sel.astype(BF16)) + carry[0:1, :]
    eidx = jnp.zeros((tm, LANES), F32)
    gate = jnp.zeros((tm, LANES), F32)
    rank = jnp.zeros((tm, LANES), F32)
    for k in range(TOP_K):
        slot = lane == k
        eidx = jnp.where(slot, es[k], eidx)
        gate = jnp.where(slot, ss[k] / tot * ROUTED_SCALE, gate)
        rank = jnp.where(slot, rsum(jnp.where(lane == es[k], prefix, 0.0)), rank)
    eidx_o[...] = eidx.astype(I32)
    gate_o[...] = gate
    rank_o[...] = rank.astype(I32)
    carry[...] = carry[...] + jnp.sum(sel, axis=0, keepdims=True)
    cnt_o[...] = carry[...]


def moe_router(x, w_router_pad, b_router_pad, tm):
    t, d = x.shape
    ltri = (jnp.arange(tm)[:, None] > jnp.arange(tm)[None, :]).astype(BF16)
    row = pl.BlockSpec((tm, LANES), lambda i: (i, 0))
    out = jax.ShapeDtypeStruct((t, LANES), I32)
    return pl.pallas_call(
        _router_kernel,
        grid=(t // tm,),
        in_specs=[pl.BlockSpec((tm, d), lambda i: (i, 0)), pl.BlockSpec((d, LANES), lambda i: (0, 0)),
                  pl.BlockSpec((1, LANES), lambda i: (0, 0)), pl.BlockSpec((tm, tm), lambda i: (0, 0))],
        out_specs=[row, row, row, pl.BlockSpec((SUBLANES, LANES), lambda i: (0, 0))],
        out_shape=[out, jax.ShapeDtypeStruct((t, LANES), F32), out,
                   jax.ShapeDtypeStruct((SUBLANES, LANES), F32)],
        scratch_shapes=[pltpu.VMEM((SUBLANES, LANES), F32)],
        compiler_params=_cparams(("arbitrary",)),
        name="moe_router",
    )(x, w_router_pad, b_router_pad, ltri)


MOE_BM = 512


def _gather_rows_start(src_hbm, idx_ref, idx0, dst_buf, n_rows, sem):
    for j in range(n_rows):
        pltpu.make_async_copy(src_hbm.at[pl.ds(idx_ref[idx0 + j], 1), :], dst_buf.at[pl.ds(j, 1), :], sem).start()


def _gather_rows_wait(src_hbm, dst_buf, n_rows, sem):
    pltpu.make_async_copy(src_hbm.at[pl.ds(0, n_rows), :], dst_buf, sem).wait()


def _expert_kernel(blk_e_ref, nused_ref, tok_ref, x_hbm, wg_ref, wu_ref, wd_ref, ys_ref, xbuf_a, xbuf_b, sems,
                   wgb, wub, wdb):
    i = pl.program_id(0)
    n_used = nused_ref[0]
    active = i < n_used

    @pl.when(i == 0)
    def _():
        _gather_rows_start(x_hbm, tok_ref, 0, xbuf_a, MOE_BM, sems.at[0])

    @pl.when(active)
    def _():
        e = blk_e_ref[i]
        prev = blk_e_ref[jnp.maximum(i - 1, 0)]

        @pl.when((i == 0) | (e != prev))
        def _():
            wgb[...] = wg_ref[0].astype(BF16)
            wub[...] = wu_ref[0].astype(BF16)
            wdb[...] = wd_ref[0].astype(BF16)

    def block(cur_buf, cur_sem, nxt_buf, nxt_sem):
        _gather_rows_wait(x_hbm, cur_buf, MOE_BM, cur_sem)
        nxt_row0 = jnp.minimum(i + 1, n_used - 1) * MOE_BM
        _gather_rows_start(x_hbm, tok_ref, nxt_row0, nxt_buf, MOE_BM, nxt_sem)
        x = cur_buf[...].astype(BF16)
        hg = _dot(x, wgb[...])
        hu = _dot(x, wub[...])
        h = hg * jax.nn.sigmoid(hg) * hu
        ys_ref[...] = _dot(h.astype(BF16), wdb[...])

        @pl.when(i == n_used - 1)
        def _():
            _gather_rows_wait(x_hbm, nxt_buf, MOE_BM, nxt_sem)

    @pl.when(active & (i % 2 == 0))
    def _():
        block(xbuf_a, sems.at[0], xbuf_b, sems.at[1])

    @pl.when(active & (i % 2 == 1))
    def _():
        block(xbuf_b, sems.at[1], xbuf_a, sems.at[0])

    @pl.when(i >= nused_ref[0])
    def _():
        ys_ref[...] = jnp.zeros(ys_ref.shape, F32)


def moe_experts(x, blk_e, n_used, tok_of_row, w_gate, w_up, w_down):
    d = x.shape[1]
    n_rows = tok_of_row.shape[0]
    n_blocks = n_rows // MOE_BM
    de = w_gate.shape[2]
    wmap = lambda i, be, nu, tk: (be[jnp.minimum(i, nu[0] - 1)], 0, 0)
    return pl.pallas_call(
        _expert_kernel,
        grid_spec=pltpu.PrefetchScalarGridSpec(
            num_scalar_prefetch=3,
            grid=(n_blocks,),
            in_specs=[pl.BlockSpec(memory_space=pl.ANY),
                      pl.BlockSpec((1, d, de), wmap), pl.BlockSpec((1, d, de), wmap), pl.BlockSpec((1, de, d), wmap)],
            out_specs=pl.BlockSpec((MOE_BM, d), lambda i, be, nu, tk: (i, 0)),
            scratch_shapes=[pltpu.VMEM((MOE_BM, d), F32), pltpu.VMEM((MOE_BM, d), F32), pltpu.SemaphoreType.DMA((2,)),
                            pltpu.VMEM((d, de), BF16), pltpu.VMEM((d, de), BF16), pltpu.VMEM((de, d), BF16)],
        ),
        out_shape=jax.ShapeDtypeStruct((n_rows, d), F32),
        compiler_params=_cparams(("arbitrary",)),
        name="moe_experts",
    )(blk_e, n_used, tok_of_row, x, w_gate, w_up, w_down)


def _shared_ffn_kernel(x_ref, wg_ref, wu_ref, wd_ref, o_ref):
    x = x_ref[...]
    hg = _dot(x, wg_ref[...])
    hu = _dot(x, wu_ref[...])
    h = hg * jax.nn.sigmoid(hg) * hu
    o_ref[...] = _dot(h.astype(BF16), wd_ref[...])


def shared_ffn(xb, wg, wu, wd, tm):
    t, d = xb.shape
    full = lambda a: pl.BlockSpec(a.shape, lambda i: (0, 0))
    row = pl.BlockSpec((tm, d), lambda i: (i, 0))
    return pl.pallas_call(
        _shared_ffn_kernel,
        grid=(t // tm,),
        in_specs=[row, full(wg), full(wu), full(wd)],
        out_specs=row,
        out_shape=jax.ShapeDtypeStruct((t, d), F32),
        compiler_params=_cparams(("parallel",)),
        name="shared_ffn",
    )(xb, wg, wu, wd)


def _combine_kernel(dest_ref, ys_hbm, gate_ref, x_ref, sh_ref, g_ref, b_ref, o_ref, ob_ref, buf, sems, *, tm):
    i = pl.program_id(0)
    n_asg = tm * TOP_K

    @pl.when(i == 0)
    def _():
        _gather_rows_start(ys_hbm, dest_ref, 0, buf.at[0], n_asg, sems.at[0])

    @pl.when(i + 1 < pl.num_programs(0))
    def _():
        nxt = (i + 1) % 2
        _gather_rows_start(ys_hbm, dest_ref, (i + 1) * n_asg, buf.at[nxt], n_asg, sems.at[nxt])

    cur = i % 2
    _gather_rows_wait(ys_hbm, buf.at[cur], n_asg, sems.at[cur])
    gate = gate_ref[...]
    f = sh_ref[...]
    for k in range(TOP_K):
        f = f + gate[:, k:k + 1] * buf[cur, k * tm:(k + 1) * tm, :]
    z = ALPHA * x_ref[...] + f
    inv_d = 1.0 / z.shape[1]
    mu = _row_sum(z) * inv_d
    dlt = z - mu
    var = _row_sum(dlt * dlt) * inv_d
    y = dlt * lax.rsqrt(var + LN_EPS) * g_ref[...] + b_ref[...]
    o_ref[...] = y
    ob_ref[...] = y.astype(BF16)


def moe_combine(ys, dest_flat, gates, x, shared, g, b, tm):
    t, d = x.shape
    row = pl.BlockSpec((tm, d), lambda i, dr: (i, 0))
    vec = pl.BlockSpec((1, d), lambda i, dr: (0, 0))
    return pl.pallas_call(
        functools.partial(_combine_kernel, tm=tm),
        grid_spec=pltpu.PrefetchScalarGridSpec(
            num_scalar_prefetch=1,
            grid=(t // tm,),
            in_specs=[pl.BlockSpec(memory_space=pl.ANY), pl.BlockSpec((tm, LANES), lambda i, dr: (i, 0)),
                      row, row, vec, vec],
            out_specs=[row, row],
            scratch_shapes=[pltpu.VMEM((2, TOP_K * tm, d), F32), pltpu.SemaphoreType.DMA((2,))],
        ),
        out_shape=[jax.ShapeDtypeStruct((t, d), F32), jax.ShapeDtypeStruct((t, d), BF16)],
        compiler_params=_cparams(("arbitrary",)),
        name="moe_combine",
    )(dest_flat, ys, gates, x, shared, g.reshape(1, d), b.reshape(1, d))


def moe_params(layer, w):
    p = {}
    p["w_router"] = _pad_cols(w["moe_w_router"][layer], LANES)
    p["b_router"] = _pad_cols(w["moe_b_router"][layer][None, :], LANES)
    stack = lambda a: a.reshape((-1,) + a.shape[2:])
    p["w_gate"], p["w_up"], p["w_down"] = stack(w["moe_w_gate"]), stack(w["moe_w_up"]), stack(w["moe_w_down"])
    p["e0"] = layer * N_EXPERTS
    p["ws_gate"] = w["moe_ws_gate"][layer].astype(BF16)
    p["ws_up"] = w["moe_ws_up"][layer].astype(BF16)
    p["ws_down"] = w["moe_ws_down"][layer].astype(BF16)
    return p


def moe_layer(y, yb, p, ln_g, ln_b, *, tm, tm_dma):
    t = y.shape[0]
    eidx, gates, rank, cnt = moe_router(y, p["w_router"], p["b_router"], tm)
    counts = cnt[0, :N_EXPERTS].astype(I32)
    padded = (counts + MOE_BM - 1) // MOE_BM * MOE_BM
    pad_end = jnp.cumsum(padded)
    pad_start = pad_end - padded
    e8 = eidx[:, :TOP_K]
    dest = pad_start[e8] + rank[:, :TOP_K]
    n_blocks = -(-(t * TOP_K + N_EXPERTS * (MOE_BM - 1)) // MOE_BM)
    n_rows = n_blocks * MOE_BM
    blk_start = jnp.arange(n_blocks, dtype=I32) * MOE_BM
    blk_e = jnp.sum((pad_end[None, :] <= blk_start[:, None]).astype(I32), axis=1)
    blk_e = jnp.minimum(blk_e, N_EXPERTS - 1) + p["e0"]
    n_used = (pad_end[-1:] // MOE_BM).astype(I32)
    tok = jnp.broadcast_to(jnp.arange(t, dtype=I32)[:, None], (t, TOP_K))
    tok_of_row = jnp.zeros((n_rows,), I32).at[dest.reshape(-1)].set(tok.reshape(-1))
    ys = moe_experts(y, blk_e, n_used, tok_of_row, p["w_gate"], p["w_up"], p["w_down"])
    shared = shared_ffn(yb, p["ws_gate"], p["ws_up"], p["ws_down"], tm)
    dest_tiles = dest.reshape(t // tm_dma, tm_dma, TOP_K).transpose(0, 2, 1).reshape(-1)
    return moe_combine(ys, dest_tiles, gates, y, shared, ln_g, ln_b, tm_dma)


C_QKV = (H_C + 2 * KVH_C) * DH_C
C_QI = H_IDX * D_IDX
IDX_SCALE = (H_IDX ** -0.5) * (D_IDX ** -0.5)


def _rot_half64(x, lane):
    return jnp.where(lane % D_IDX < D_IDX // 2, pltpu.roll(x, LANES - D_IDX // 2, 1), pltpu.roll(x, D_IDX // 2, 1))


def _c_post_kernel(qkv_ref, qi_ref, kw_ref, c128_ref, s128_ref, c64_ref, s64_ref, lng_ref, lnb_ref,
                   qb_o, kf_o, kb_o, vb_o, qib_o, kwo_o):
    tm = qkv_ref.shape[0]
    c128, s128 = c128_ref[...], s128_ref[...]
    c64, s64 = c64_ref[...], s64_ref[...]
    lane = lax.broadcasted_iota(I32, (tm, LANES), 1)
    for h in range(H_C + KVH_C):
        x = qkv_ref[:, h * DH_C:(h + 1) * DH_C]
        y = x * c128 + pltpu.roll(x, DH_C // 2, 1) * s128
        if h < H_C:
            qb_o[:, h * DH_C:(h + 1) * DH_C] = (y * (DH_C ** -0.5)).astype(BF16)
        else:
            j = h - H_C
            kf_o[:, j * DH_C:(j + 1) * DH_C] = y
            kb_o[:, j * DH_C:(j + 1) * DH_C] = y.astype(BF16)
    vb_o[...] = qkv_ref[:, (H_C + KVH_C) * DH_C:].astype(BF16)
    for j in range(C_QI // LANES):
        x = qi_ref[:, j * LANES:(j + 1) * LANES]
        qib_o[:, j * LANES:(j + 1) * LANES] = (x * c64 + _rot_half64(x, lane) * s64).astype(BF16)
    x = kw_ref[...]
    is_k = lane < D_IDX
    mu = jnp.sum(jnp.where(is_k, x, 0.0), axis=-1, keepdims=True) * (1.0 / D_IDX)
    d = jnp.where(is_k, x - mu, 0.0)
    var = jnp.sum(d * d, axis=-1, keepdims=True) * (1.0 / D_IDX)
    y = d * lax.rsqrt(var + LN_EPS) * lng_ref[...] + lnb_ref[...]
    y = y * c64 + _rot_half64(y, lane) * s64
    kwo_o[...] = jnp.where(is_k, y, x * IDX_SCALE)


def c_post(qkv, qi, kw, tabs, lng, lnb, tm):
    t = qkv.shape[0]
    row = lambda wd: pl.BlockSpec((tm, wd), lambda i: (i, 0))
    vec = pl.BlockSpec((1, LANES), lambda i: (0, 0))
    sh = lambda wd, dt: jax.ShapeDtypeStruct((t, wd), dt)
    return pl.pallas_call(
        _c_post_kernel,
        grid=(t // tm,),
        in_specs=[row(C_QKV), row(C_QI), row(LANES), row(LANES), row(LANES), row(LANES), row(LANES), vec, vec],
        out_specs=[row(H_C * DH_C), row(KVH_C * DH_C), row(KVH_C * DH_C), row(KVH_C * DH_C), row(C_QI), row(LANES)],
        out_shape=[sh(H_C * DH_C, BF16), sh(KVH_C * DH_C, F32), sh(KVH_C * DH_C, BF16), sh(KVH_C * DH_C, BF16),
                   sh(C_QI, BF16), sh(LANES, F32)],
        compiler_params=_cparams(("parallel",)),
        name="c_post",
    )(qkv, qi, kw, *tabs, lng, lnb)


def _rope_tables(pos):
    def tab(d):
        inv = ROPE_THETA ** (-jnp.arange(0, d, 2, dtype=F32) / d)
        ang = pos.astype(F32)[:, None] * inv[None, :]
        c, s = jnp.cos(ang), jnp.sin(ang)
        return jnp.concatenate([c, c], axis=1), jnp.concatenate([-s, s], axis=1)

    c128, s128 = tab(DH_C)
    c64, s64 = tab(D_IDX)
    return c128, s128, jnp.tile(c64, (1, 2)), jnp.tile(s64, (1, 2))


def _index_heads(qi, wi, kb):
    acc = None
    for h in range(H_IDX):
        d = lax.dot_general(qi[:, h * D_IDX:(h + 1) * D_IDX], kb, (((1,), (1,)), ((), ())),
                            preferred_element_type=F32)
        term = wi[:, D_IDX + h:D_IDX + h + 1] * jnp.maximum(d, 0.0)
        acc = term if acc is None else acc + term
    return acc + 0.0


SROWS = 16


def _topk_mask(sc_ref, utri_ref, o_ref, key_scr, *, topk, chunk, bits_per_step=1):
    rows, s_len = sc_ref.shape
    bits = lax.bitcast_convert_type(sc_ref[...], I32)
    key_scr[...] = bits ^ ((bits >> 31) & jnp.int32(0x7FFFFFFF))
    sign = jnp.int32(-2 ** 31)
    kf = jnp.float32(topk)

    def enough(cand):
        return _row_sum(jnp.where(key_scr[...] >= (cand ^ sign), 1.0, 0.0)) >= kf

    def bisect(it, t_u):
        low = (32 - bits_per_step) - bits_per_step * it
        for pattern in range(1, 2 ** bits_per_step):
            cand = t_u | (jnp.int32(pattern) << low)
            best = jnp.where(enough(cand), cand, t_u if pattern == 1 else best)
        return best

    t_u = lax.fori_loop(0, 32 // bits_per_step, bisect, jnp.zeros((rows, 1), I32))
    thr = t_u ^ sign
    need = kf - _row_sum(jnp.where(key_scr[...] > thr, 1.0, 0.0))
    utri = utri_ref[...]
    run = jnp.zeros((rows, 1), F32)
    for c in range(s_len // chunk):
        cs = slice(c * chunk, (c + 1) * chunk)
        keyc = key_scr[:, cs]
        eqc = jnp.where(keyc == thr, 1.0, 0.0)
        before = _dot(eqc.astype(BF16), utri) + run
        pick = (keyc > thr) | ((keyc == thr) & (before < need))
        o_ref[:, cs] = jnp.where(pick & (sc_ref[:, cs] > -jnp.inf), 1.0, 0.0).astype(o_ref.dtype)
        run = run + _row_sum(eqc)
    if o_ref.shape[1] > s_len:
        o_ref[:, s_len:] = jnp.zeros((rows, o_ref.shape[1] - s_len), o_ref.dtype)


def _strict_upper(chunk):
    return (jnp.arange(chunk)[:, None] < jnp.arange(chunk)[None, :]).astype(BF16)


SELECT_CLASSES = 4


def _index_select_kernel(qi_ref, wi_ref, keys_ref, utri_ref, o_ref, sc_scr, key_scr, *, q0, topk, chunk):
    qblk, width = sc_scr.shape
    kb = keys_ref[:width, :D_IDX].astype(BF16)
    sc = _index_heads(qi_ref[...], wi_ref[...], kb)
    row = lax.broadcasted_iota(I32, (qblk, width), 0) + (q0 + pl.program_id(1) * qblk)
    col = lax.broadcasted_iota(I32, (qblk, width), 1)
    sc_scr[...] = jnp.where(col <= row, sc, -jnp.inf)
    _topk_mask(sc_scr, utri_ref, o_ref, key_scr, topk=topk, chunk=chunk)


def index_select_causal(qib, kwo, topk, n_seq, seq_len, qblk, chunk):
    n_cls = SELECT_CLASSES if seq_len % (SELECT_CLASSES * max(chunk, qblk)) == 0 else 1
    cw = seq_len // n_cls
    utri = _strict_upper(chunk)
    parts = []
    for c in range(n_cls):
        width = (c + 1) * cw
        qmap = functools.partial(lambda b, i, c: (b * (seq_len // qblk) + c * (cw // qblk) + i, 0), c=c)
        parts.append(pl.pallas_call(
            functools.partial(_index_select_kernel, q0=c * cw, topk=topk, chunk=chunk),
            grid=(n_seq, cw // qblk),
            in_specs=[pl.BlockSpec((qblk, C_QI), qmap), pl.BlockSpec((qblk, LANES), qmap),
                      pl.BlockSpec((seq_len, LANES), lambda b, i: (b, 0)),
                      pl.BlockSpec((chunk, chunk), lambda b, i: (0, 0))],
            out_specs=pl.BlockSpec((qblk, seq_len), lambda b, i: (b * (cw // qblk) + i, 0)),
            out_shape=jax.ShapeDtypeStruct((n_seq * cw, seq_len), BF16),
            scratch_shapes=[pltpu.VMEM((qblk, width), F32), pltpu.VMEM((qblk, width), I32)],
            compiler_params=_cparams(("parallel", "parallel")),
            name="index_select_%d" % c,
        )(qib, kwo, kwo, utri).reshape(n_seq, cw, seq_len))
    return jnp.concatenate(parts, axis=1).reshape(n_seq * seq_len, seq_len)


NEG_BIG = -1e30


def _attn_init(m_scr, l_scr, acc_scr):
    m_scr[...] = jnp.full(m_scr.shape, NEG_BIG, F32)
    l_scr[...] = jnp.zeros(l_scr.shape, F32)
    acc_scr[...] = jnp.zeros(acc_scr.shape, F32)


def _attn_block(q, k, v, mask, m_scr, l_scr, acc_scr):
    bias = jnp.concatenate([(mask.astype(F32) - 1.0) * (-NEG_BIG)] * GROUP_C, axis=0)
    heads = range(KVH_C)
    ss = []
    for kh in heads:
        qh = jnp.concatenate([q[:, (kh * GROUP_C + g) * DH_C:(kh * GROUP_C + g + 1) * DH_C]
                              for g in range(GROUP_C)], axis=0)
        ks = k[:, kh * DH_C:(kh + 1) * DH_C]
        ss.append(lax.dot_general(qh, ks, (((1,), (1,)), ((), ())), preferred_element_type=F32) + bias)
    n_tiles = ss[0].shape[1] // LANES
    m_old = [m_scr[kh] for kh in heads]
    m_new = [jnp.maximum(m_old[kh], jnp.broadcast_to(_row_max(ss[kh]), m_old[kh].shape)) for kh in heads]
    ps = [[jnp.exp(ss[kh][:, i * LANES:(i + 1) * LANES] - m_new[kh]) for i in range(n_tiles)] for kh in heads]
    for kh in heads:
        alpha = jnp.exp(m_old[kh] - m_new[kh])
        l_scr[kh] = alpha * l_scr[kh] + functools.reduce(jnp.add, ps[kh])
        p = jnp.concatenate(ps[kh], axis=1).astype(BF16)
        acc_scr[kh] = alpha * acc_scr[kh] + _dot(p, v[:, kh * DH_C:(kh + 1) * DH_C])
        m_scr[kh] = m_new[kh]


def _attn_finish(o_ref, nq, l_scr, acc_scr, lead=None):
    for kh in range(KVH_C):
        out = acc_scr[kh] / jnp.sum(l_scr[kh], axis=-1, keepdims=True)
        for g in range(GROUP_C):
            hs = slice((kh * GROUP_C + g) * DH_C, (kh * GROUP_C + g + 1) * DH_C)
            val = out[g * nq:(g + 1) * nq, :].astype(o_ref.dtype)
            if lead is None:
                o_ref[:, hs] = val
            else:
                o_ref[lead, :, hs] = val


def _attn_prompt_kernel(q_ref, k_ref, v_ref, mask_ref, o_ref, m_scr, l_scr, acc_scr, *, qblk, kblk):
    qb = pl.program_id(1)
    kb = pl.program_id(2)

    @pl.when(kb == 0)
    def _():
        _attn_init(m_scr, l_scr, acc_scr)

    @pl.when(kb * kblk <= qb * qblk + qblk - 1)
    def _():
        _attn_block(q_ref[...], k_ref[...], v_ref[...], mask_ref[...], m_scr, l_scr, acc_scr)

    @pl.when(kb == pl.num_programs(2) - 1)
    def _():
        _attn_finish(o_ref, qblk, l_scr, acc_scr)


def attn_prompt(qb_, kb_, vb_, mask, n_p, l_p, qblk, kblk):
    nq, nk = l_p // qblk, l_p // kblk
    last = lambda q, k: jnp.minimum(k, (q * qblk + qblk - 1) // kblk)
    rows = GROUP_C * qblk
    return pl.pallas_call(
        functools.partial(_attn_prompt_kernel, qblk=qblk, kblk=kblk),
        grid=(n_p, nq, nk),
        in_specs=[pl.BlockSpec((qblk, H_C * DH_C), lambda b, q, k: (b * nq + q, 0)),
                  pl.BlockSpec((kblk, KVH_C * DH_C), lambda b, q, k: (b * nk + last(q, k), 0)),
                  pl.BlockSpec((kblk, KVH_C * DH_C), lambda b, q, k: (b * nk + last(q, k), 0)),
                  pl.BlockSpec((qblk, kblk), lambda b, q, k: (b * nq + q, last(q, k)))],
        out_specs=pl.BlockSpec((qblk, H_C * DH_C), lambda b, q, k: (b * nq + q, 0)),
        out_shape=jax.ShapeDtypeStruct((n_p * l_p, H_C * DH_C), BF16),
        scratch_shapes=[pltpu.VMEM((KVH_C, rows, LANES), F32), pltpu.VMEM((KVH_C, rows, LANES), F32),
                        pltpu.VMEM((KVH_C, rows, DH_C), F32)],
        compiler_params=_cparams(("parallel", "parallel", "arbitrary")),
        name="attn_prompt",
    )(qb_, kb_, vb_, mask)


def _dsa_sample_kernel(pt_ref, qi_ref, kw_ref, q_ref, kn_ref, vn_ref, utri_ref, kidx_hbm, k_hbm, v_hbm, o_ref,
                       kibuf, kbuf, vbuf, sems, sc_scr, key_scr, mask_scr, *, n_pages, page, l_s, topk, pool0):
    b = pl.program_id(0)
    past = n_pages * page
    s_len = past + page

    def page_copies(bb, slot, pg):
        phys = pt_ref[bb * n_pages + pg] + pool0
        rows = pl.ds(pg * page, page)
        cps = [pltpu.make_async_copy(kidx_hbm.at[phys], kibuf.at[slot, rows, :], sems.at[slot, 0])]
        for kh in range(KVH_C):
            cps.append(pltpu.make_async_copy(k_hbm.at[phys, :, kh, :], kbuf.at[slot, kh, rows, :], sems.at[slot, 1]))
            cps.append(pltpu.make_async_copy(v_hbm.at[phys, :, kh, :], vbuf.at[slot, kh, rows, :], sems.at[slot, 2]))
        return cps

    def fetch(bb, slot):
        for pg in range(n_pages):
            for cp in page_copies(bb, slot, pg):
                cp.start()

    @pl.when(b == 0)
    def _():
        kibuf[:, past:, :] = jnp.zeros((2, page, D_IDX), F32)
        kbuf[:, :, past:, :] = jnp.zeros((2, KVH_C, page, DH_C), F32)
        vbuf[:, :, past:, :] = jnp.zeros((2, KVH_C, page, DH_C), F32)
        fetch(0, 0)

    @pl.when(b + 1 < pl.num_programs(0))
    def _():
        fetch(b + 1, (b + 1) % 2)

    cur = b % 2
    for pg in range(n_pages):
        for cp in page_copies(b, cur, pg):
            cp.wait()
    kw = kw_ref[0]
    kibuf[cur, past:past + SROWS, :] = kw[:, :D_IDX]
    for kh in range(KVH_C):
        hs = slice(kh * DH_C, (kh + 1) * DH_C)
        kbuf[cur, kh, past:past + SROWS, :] = kn_ref[0][:, hs]
        vbuf[cur, kh, past:past + SROWS, :] = vn_ref[0][:, hs]

    dots = lax.dot_general(qi_ref[0], kibuf[cur].astype(BF16), (((1,), (1,)), ((), ())),
                           preferred_element_type=F32)
    sc = None
    for h in range(H_IDX):
        term = kw[:, D_IDX + h:D_IDX + h + 1] * jnp.maximum(dots[h * SROWS:(h + 1) * SROWS, :], 0.0)
        sc = term if sc is None else sc + term
    row = lax.broadcasted_iota(I32, (SROWS, s_len), 0)
    col = lax.broadcasted_iota(I32, (SROWS, s_len), 1)
    visible = (col < past) | ((col - past <= row) & (col - past < l_s))
    sc_scr[...] = jnp.where(visible, sc + 0.0, -jnp.inf)
    _topk_mask(sc_scr, utri_ref, mask_scr, key_scr, topk=topk, chunk=page, bits_per_step=2)

    bias = jnp.concatenate([(mask_scr[...] - 1.0) * (-NEG_BIG)] * GROUP_C, axis=0)
    q = q_ref[0]
    for kh in range(KVH_C):
        qh = jnp.concatenate([q[:, (kh * GROUP_C + g) * DH_C:(kh * GROUP_C + g + 1) * DH_C]
                              for g in range(GROUP_C)], axis=0)
        s = lax.dot_general(qh, kbuf[cur, kh].astype(BF16), (((1,), (1,)), ((), ())),
                            preferred_element_type=F32) + bias
        p = jnp.exp(s - _row_max(s))
        out = _dot(p.astype(BF16), vbuf[cur, kh].astype(BF16)) / _row_sum(p)
        for g in range(GROUP_C):
            o_ref[0, :, (kh * GROUP_C + g) * DH_C:(kh * GROUP_C + g + 1) * DH_C] = (
                out[g * SROWS:(g + 1) * SROWS, :].astype(o_ref.dtype))


def dsa_sample(page_table, qi_hm, kw3, q3, knew3, vnew3, cache_kidx3, cache_k4, cache_v4, *, pool0, l_s, topk):
    n_s, n_pages = page_table.shape
    page = cache_k4.shape[1]
    kvw = KVH_C * DH_C
    s_len = (n_pages + 1) * page
    own = lambda wd: pl.BlockSpec((1, SROWS, wd), lambda b, pt: (b, 0, 0))
    hbm = pl.BlockSpec(memory_space=pl.ANY)
    return pl.pallas_call(
        functools.partial(_dsa_sample_kernel, n_pages=n_pages, page=page, l_s=l_s, topk=topk, pool0=pool0),
        grid_spec=pltpu.PrefetchScalarGridSpec(
            num_scalar_prefetch=1,
            grid=(n_s,),
            in_specs=[pl.BlockSpec((1, H_IDX * SROWS, D_IDX), lambda b, pt: (b, 0, 0)),
                      own(LANES), own(H_C * DH_C), own(kvw), own(kvw),
                      pl.BlockSpec((page, page), lambda b, pt: (0, 0)), hbm, hbm, hbm],
            out_specs=own(H_C * DH_C),
            scratch_shapes=[pltpu.VMEM((2, s_len, D_IDX), F32), pltpu.VMEM((2, KVH_C, s_len, DH_C), F32),
                            pltpu.VMEM((2, KVH_C, s_len, DH_C), F32), pltpu.SemaphoreType.DMA((2, 3)),
                            pltpu.VMEM((SROWS, s_len), F32), pltpu.VMEM((SROWS, s_len), I32),
                            pltpu.VMEM((SROWS, s_len), F32)],
        ),
        out_shape=jax.ShapeDtypeStruct((n_s, SROWS, H_C * DH_C), BF16),
        compiler_params=_cparams(("arbitrary",)),
        name="dsa_sample",
    )(page_table.reshape(-1), qi_hm, kw3, q3, knew3, vnew3, _strict_upper(page), cache_kidx3, cache_k4, cache_v4)


def c_params(li, w):
    p = {}
    w_in = w["c_w_in"][li]
    p["w_qkv"] = w_in[:, :C_QKV].astype(BF16)
    p["w_qi"] = w_in[:, C_QKV:C_QKV + C_QI].astype(BF16)
    p["w_kw"] = _pad_cols(w_in[:, C_QKV + C_QI:], LANES).astype(BF16)
    p["ln_g"] = _pad_cols(w["c_kidx_ln_g"][li][None, :], LANES)
    p["ln_b"] = _pad_cols(w["c_kidx_ln_b"][li][None, :], LANES)
    p["w_out"] = w["c_w_out"][li].astype(BF16)
    return p


def _pad_sample_rows(a, n_s, l_s):
    a3 = a.reshape(n_s, l_s, a.shape[1])
    return jnp.pad(a3, ((0, 0), (0, SROWS - l_s), (0, 0)))


def c_layer(xb, geo, p, li, cache_k, cache_v, cache_kidx, page_table, *, tm, qblk, kblk):
    n_p, l_p, n_s, l_s = geo
    tp = n_p * l_p
    n_pool, page = cache_k.shape[1], cache_k.shape[2]
    n_pages = page_table.shape[1]
    past = n_pages * page
    qkv = matmul([xb], [p["w_qkv"]], tm, 1024, name="c_in_qkv")
    qi = matmul([xb], [p["w_qi"]], tm, C_QI, name="c_in_qi")
    kw = matmul([xb], [p["w_kw"]], tm, LANES, name="c_in_kw")
    pos = jnp.concatenate([jnp.tile(jnp.arange(l_p, dtype=I32), n_p),
                           jnp.tile(past + jnp.arange(l_s, dtype=I32), n_s)])
    q_b, k_f, k_b, v_b, qi_b, kwo = c_post(qkv, qi, kw, _rope_tables(pos), p["ln_g"], p["ln_b"], tm)
    v_f = qkv[:, (H_C + KVH_C) * DH_C:]
    mask_p = index_select_causal(qi_b, kwo, min(TOPK_MAX, l_p // 4), n_p, l_p, qblk, 256)
    o_p = attn_prompt(q_b, k_b, v_b, mask_p, n_p, l_p, qblk, kblk)
    pad = lambda a: _pad_sample_rows(a[tp:], n_s, l_s)
    qi_hm = pad(qi_b).reshape(n_s, SROWS, H_IDX, D_IDX).transpose(0, 2, 1, 3).reshape(n_s, H_IDX * SROWS, D_IDX)
    pool = lambda a: a.reshape((-1,) + a.shape[2:])
    o_s3 = dsa_sample(page_table, qi_hm, pad(kwo), pad(q_b), pad(k_f), pad(v_f),
                      pool(cache_kidx), pool(cache_k), pool(cache_v), pool0=li * n_pool, l_s=l_s,
                      topk=min(TOPK_MAX, (past + l_s) // 4))
    o = jnp.concatenate([o_p, o_s3[:, :l_s].reshape(n_s * l_s, -1)], axis=0)
    h = matmul([o], [p["w_out"]], tm, 1024, name="c_out")
    heads = lambda a, n, l: a.reshape(n, l, KVH_C, DH_C)
    new_p = (heads(k_f[:tp], n_p, l_p), heads(v_f[:tp], n_p, l_p), kwo[:tp, :D_IDX].reshape(n_p, l_p, D_IDX))
    new_s = (heads(k_f[tp:], n_s, l_s), heads(v_f[tp:], n_s, l_s), kwo[tp:, :D_IDX].reshape(n_s, l_s, D_IDX))
    return h, new_p, new_s


TM = 256
BLK = 128
TM_DMA = 64
QBLK = 128
KBLK = 512


def kernel(x_prompt, x_sample, state_shift, state_wkv, state_ssm_re, state_ssm_im, cache_k, cache_v, cache_kidx, page_table, ab_w_in, ab_mu, rwkv_w0, rwkv_w2, rwkv_a0, rwkv_a2, rwkv_g2, rwkv_k_k, rwkv_k_a, rwkv_r_k, rwkv_ln_g, rwkv_ln_b, s5_lam_re, s5_lam_im, s5_log_dt, s5_b_re, s5_b_im, s5_c_re, s5_c_im, s5_d, s5_w_glu, s5_b_glu, ab_w_out, c_w_in, c_kidx_ln_g, c_kidx_ln_b, c_w_out, ln_mix_g, ln_mix_b, ln_ffn_g, ln_ffn_b, moe_w_router, moe_b_router, moe_w_gate, moe_w_up, moe_w_down, moe_ws_gate, moe_ws_up, moe_ws_down):
    w = dict(locals())
    n_p, l_p, _ = x_prompt.shape
    n_s, l_s, _ = x_sample.shape
    geo = (n_p, l_p, n_s, l_s)
    tp = n_p * l_p
    y = jnp.concatenate([x_prompt.reshape(-1, D_MODEL), x_sample.reshape(-1, D_MODEL)], axis=0)
    yb = y.astype(BF16)
    outs = {k: [] for k in ("kp", "vp", "kip", "ks", "vs", "kis", "shp", "shs", "wkvp", "wkvs",
                            "srp", "sip", "srs", "sis")}
    for layer in range(DEPTH):
        li = layer // 2
        if layer % 2 == 0:
            wkv_all = state_wkv.reshape((-1,) + state_wkv.shape[2:])
            h, shift, wkv, ssm_p, ssm_s = ab_layer(yb, geo, ab_params(li, w, l_s), state_shift[li], wkv_all, li * n_s,
                                                   state_ssm_re[li], state_ssm_im[li], tm=TM, blk=BLK)
            for k, v in zip(("shp", "shs", "wkvp", "wkvs", "srp", "sip", "srs", "sis"),
                            (*shift, *wkv, *ssm_p, *ssm_s)):
                outs[k].append(v)
        else:
            h, new_p, new_s = c_layer(yb, geo, c_params(li, w), li, cache_k, cache_v, cache_kidx,
                                      page_table, tm=TM, qblk=QBLK, kblk=KBLK)
            for k, v in zip(("kp", "vp", "kip", "ks", "vs", "kis"), (*new_p, *new_s)):
                outs[k].append(v)
        y, yb = add_ln(y, h, ln_mix_g[layer], ln_mix_b[layer], TM)
        y, yb = moe_layer(y, yb, moe_params(layer, w), ln_ffn_g[layer], ln_ffn_b[layer], tm=TM, tm_dma=TM_DMA)
    st = lambda k: jnp.stack(outs[k])
    return (y[:tp].reshape(n_p, l_p, D_MODEL), y[tp:].reshape(n_s, l_s, D_MODEL),
            st("kp"), st("vp"), st("kip"), st("ks"), st("vs"), st("kis"),
            st("shp"), st("shs"), st("wkvp"), st("wkvs"), st("srp"), st("sip"), st("srs"), st("sis"))
```

```python
import functools
import math

import jax
import jax.numpy as jnp
from jax import lax
from jax.experimental import pallas as pl
from jax.experimental.pallas import tpu as pltpu

F32, BF16, I32 = jnp.float32, jnp.bfloat16, jnp.int32

D_MODEL = 2048
DEPTH = 2
ALPHA = (2.0 * DEPTH) ** 0.25
LN_EPS = 1e-5
C_A = 1024
HS_A = 64
H_A = 16
LORA_W, LORA_A, LORA_G = 64, 64, 160
A_PROJ = 3 * C_A + LORA_W + LORA_A + LORA_G
GN_EPS_A = 64e-5
C_B = 1024
S5_P = 16
S5_GROUPS = 64
S5_N = 64
S5_STATE = S5_GROUPS * S5_N
DH_C = 128
H_C = 16
KVH_C = 4
GROUP_C = 4
H_IDX = 16
D_IDX = 64
TOPK_MAX = 256
ROPE_THETA = 10000.0
N_EXPERTS = 64
TOP_K = 8
N_EXPERT_GROUPS = 8
TOPK_GROUPS = 4
D_EXPERT = 512
ROUTED_SCALE = 2.5

LANES = 128
SUBLANES = 8
VMEM_LIMIT = 56 * 1024 * 1024

LORA_PAD = 512
EXP_M05 = math.exp(-0.5)


def _cparams(sem):
    return pltpu.CompilerParams(dimension_semantics=sem, vmem_limit_bytes=VMEM_LIMIT)


def _dot(a, b):
    return jnp.dot(a, b, preferred_element_type=F32)


def _lane_fold(x, op):
    tiles = [x[:, i * LANES:(i + 1) * LANES] for i in range(x.shape[1] // LANES)]
    return functools.reduce(op, tiles)


def _row_sum(x):
    return jnp.sum(_lane_fold(x, jnp.add), axis=-1, keepdims=True)


def _row_max(x):
    return jnp.max(_lane_fold(x, jnp.maximum), axis=-1, keepdims=True)


def _split3(a):
    a1 = a.astype(BF16)
    r1 = a - a1.astype(F32)
    a2 = r1.astype(BF16)
    a3 = (r1 - a2.astype(F32)).astype(BF16)
    return a1, a2, a3


def _dot_sel(a, sel_bf16):
    a1, a2, a3 = _split3(a)
    return _dot(a1, sel_bf16) + _dot(a2, sel_bf16) + _dot(a3, sel_bf16)


def _dot3(a, b):
    a1 = a.astype(BF16)
    a2 = (a - a1.astype(F32)).astype(BF16)
    b1 = b.astype(BF16)
    b2 = (b - b1.astype(F32)).astype(BF16)
    return _dot(a1, b1) + _dot(a1, b2) + _dot(a2, b1)


def _mm_kernel(*refs, n_lhs):
    o_ref = refs[-1]
    acc = None
    for i in range(n_lhs):
        d = _dot(refs[i][...], refs[n_lhs + i][...])
        acc = d if acc is None else acc + d
    o_ref[...] = acc.astype(o_ref.dtype)


MM_TM = 512


def matmul(xs, ws, tm, tn, out_dtype=F32, name="mm"):
    m = xs[0].shape[0]
    n = ws[0].shape[1]
    tm = MM_TM if m % MM_TM == 0 else tm
    assert m % tm == 0 and n % tn == 0, (m, tm, n, tn)
    in_specs = ([pl.BlockSpec((tm, x.shape[1]), lambda j, i: (i, 0)) for x in xs]
                + [pl.BlockSpec((w.shape[0], tn), lambda j, i: (0, j)) for w in ws])
    return pl.pallas_call(
        functools.partial(_mm_kernel, n_lhs=len(xs)),
        grid=(n // tn, m // tm),
        in_specs=in_specs,
        out_specs=pl.BlockSpec((tm, tn), lambda j, i: (i, j)),
        out_shape=jax.ShapeDtypeStruct((m, n), out_dtype),
        compiler_params=_cparams(("parallel", "parallel")),
        name=name,
    )(*xs, *ws)


def _add_ln_kernel(x_ref, h_ref, g_ref, b_ref, o_ref, ob_ref):
    z = ALPHA * x_ref[...] + h_ref[...]
    inv_d = 1.0 / z.shape[1]
    mu = _row_sum(z) * inv_d
    d = z - mu
    var = _row_sum(d * d) * inv_d
    y = d * lax.rsqrt(var + LN_EPS) * g_ref[...] + b_ref[...]
    o_ref[...] = y
    ob_ref[...] = y.astype(BF16)


def add_ln(x, h, g, b, tm):
    t, d = x.shape
    row = pl.BlockSpec((tm, d), lambda i: (i, 0))
    vec = pl.BlockSpec((1, d), lambda i: (0, 0))
    return pl.pallas_call(
        _add_ln_kernel,
        grid=(t // tm,),
        in_specs=[row, row, vec, vec],
        out_specs=[row, row],
        out_shape=[jax.ShapeDtypeStruct((t, d), F32), jax.ShapeDtypeStruct((t, d), BF16)],
        compiler_params=_cparams(("parallel",)),
        name="add_ln",
    )(x, h, g.reshape(1, d), b.reshape(1, d))


def _rwkv_prep_kernel(rkv_ref, rkvp_ref, lo_ref, lop_ref, mu1_ref, mu2_ref, w0_ref, w2_ref, a0_ref, a2_ref,
                      g2_ref, kk_ref, ka_ref, rk_ref, gsel_ref, gselt_ref,
                      kk_o, wr_o, w_o, b_o, k_o, v_o, kr_o, g_o, bonus_o):
    x = rkv_ref[...]
    m = x + (rkvp_ref[...] - x) * mu1_ref[...]
    r = m[:, :C_A]
    k = m[:, C_A:2 * C_A]
    v = m[:, 2 * C_A:]
    l = lo_ref[...]
    lm = l + (lop_ref[...] - l) * mu2_ref[...]
    lw = lm[:, :LANES]
    la = lm[:, LANES:2 * LANES]
    lg = lm[:, 2 * LANES:]
    w_raw = w0_ref[...] + _dot3(jnp.tanh(lw), w2_ref[...])
    decay = jnp.exp(-EXP_M05 * jax.nn.sigmoid(w_raw))
    a = jax.nn.sigmoid(a0_ref[...] + _dot3(la, a2_ref[...]))
    g = _dot3(jax.nn.sigmoid(lg), g2_ref[...])
    gsel = gsel_ref[...]
    gselt = gselt_ref[...]
    kk = k * kk_ref[...]
    nrm = jnp.sqrt(_dot_sel(kk * kk, gsel))
    inv = 1.0 / jnp.maximum(nrm, 1e-12)
    kk = kk * _dot_sel(inv, gselt)
    k_mod = k * (1.0 + (a - 1.0) * ka_ref[...])
    sb = _dot_sel(r * k_mod * rk_ref[...], gsel)
    bonus_o[...] = _dot_sel(sb, gselt) * v
    g_o[...] = g
    b = kk * a
    kk_o[...] = kk
    br = _dot_sel(_dot_sel(b * r, gsel), gselt)
    wr_o[...] = decay * r - kk * br
    w_o[...] = decay
    b_o[...] = b
    k_o[...] = k_mod
    v_o[...] = v
    kr_o[...] = _dot_sel(_dot_sel(k_mod * r, gsel), gselt)


RWKV_ROWS = 7


def rwkv_prep(rkv, rkv_prev, lora, lora_prev, p, tm):
    t = rkv.shape[0]
    row = lambda w: pl.BlockSpec((tm, w), lambda i: (i, 0))
    full = lambda a: pl.BlockSpec(a.shape, lambda i: (0,) * a.ndim)
    consts = [p["mu_rkv"], p["mu_lora"], p["w0"], p["w2"], p["a0"], p["a2"], p["g2"], p["k_k"], p["k_a"], p["r_k"],
              p["gsel"], p["gselt"]]
    n_out = RWKV_ROWS + 2
    return pl.pallas_call(
        _rwkv_prep_kernel,
        grid=(t // tm,),
        in_specs=[row(3 * C_A), row(3 * C_A), row(LORA_PAD), row(LORA_PAD)] + [full(c) for c in consts],
        out_specs=[row(C_A)] * n_out,
        out_shape=[jax.ShapeDtypeStruct((t, C_A), F32)] * n_out,
        compiler_params=_cparams(("parallel",)),
        name="rwkv_prep",
    )(rkv, rkv_prev, lora, lora_prev, *consts)


TILE_HEADS = 4
TILE_W = TILE_HEADS * HS_A
N_TILES = H_A // TILE_HEADS
LHS_PER_TILE = 3
WKV_UNROLL = 4


def _wkv_steps(rows, row_base, n_steps, npar, s_scr, bd_ref, store_o):
    sub = lax.broadcasted_iota(I32, (HS_A, TILE_W), 0)
    lane = lax.broadcasted_iota(I32, (HS_A, TILE_W), 1)
    diag = (lane % HS_A) == sub
    bd = bd_ref[...]
    blk_rows = LHS_PER_TILE * HS_A

    def step(t, carry):
        vec = [[ref[pl.ds(row_base[s] + t, 1), :] for ref in rows[s]] for s in range(npar)]
        results = []
        for s in range(npar):
            kk, wr2, _, _, _, v, _ = vec[s]
            for p in range(N_TILES):
                ls = slice(p * TILE_W, (p + 1) * TILE_W)
                st = s_scr[s * N_TILES + p]
                q1 = (st * kk[:, ls]).astype(BF16)
                q2 = (st * wr2[:, ls]).astype(BF16)
                ve = jnp.where(diag, v[:, ls], 0.0).astype(BF16)
                results.append(_dot(jnp.concatenate([q1, q2, ve], axis=0), bd))
        for s in range(npar):
            _, _, w, b, k, v, kr = vec[s]
            o_parts = []
            for p in range(N_TILES):
                ls = slice(p * TILE_W, (p + 1) * TILE_W)
                res = results[s * N_TILES + p]
                z1 = res[:HS_A]
                z2 = res[HS_A:2 * HS_A]
                vb = res[2 * HS_A:]
                st = s_scr[s * N_TILES + p]
                s_scr[s * N_TILES + p] = st * w[:, ls] - z1 * b[:, ls] + vb * k[:, ls]
                z2r = jnp.sum(jnp.where(diag, z2, 0.0), axis=0, keepdims=True)
                o_parts.append(z2r + v[:, ls] * kr[:, ls])
            store_o(s, t, jnp.concatenate(o_parts, axis=1))
        return carry

    lax.fori_loop(0, n_steps, step, 0, unroll=WKV_UNROLL)


def _pack_state(s_ref, q, s_scr, slot):
    for p in range(N_TILES):
        s_scr[slot * N_TILES + p] = jnp.concatenate([s_ref[q, TILE_HEADS * p + h] for h in range(TILE_HEADS)], axis=1)


def _unpack_state(s_scr, slot, s_ref, q):
    for p in range(N_TILES):
        st = s_scr[slot * N_TILES + p]
        for h in range(TILE_HEADS):
            s_ref[q, TILE_HEADS * p + h] = st[:, h * HS_A:(h + 1) * HS_A]


def _wkv_long_kernel(*refs, npar, blk):
    n_in = npar * RWKV_ROWS
    rows = [refs[s * RWKV_ROWS:(s + 1) * RWKV_ROWS] for s in range(npar)]
    bd_ref, o_ref, sout_ref, s_scr = refs[n_in:n_in + 4]
    c = pl.program_id(0)

    @pl.when(c == 0)
    def _():
        s_scr[...] = jnp.zeros(s_scr.shape, F32)

    def store_o(s, t, o_row):
        o_ref[s, pl.ds(t, 1), :] = o_row

    _wkv_steps(rows, [0] * npar, blk, npar, s_scr, bd_ref, store_o)

    @pl.when(c == pl.num_programs(0) - 1)
    def _():
        for s in range(npar):
            _unpack_state(s_scr, s, sout_ref, s)


WKV_NPAR_SHORT = 2
WKV_SPB = 8


def _wkv_short_kernel(*refs, seq_len, spb):
    rows_refs = refs[:RWKV_ROWS]
    s0_ref, bd_ref, o_ref, sout_ref, s_scr = refs[RWKV_ROWS:RWKV_ROWS + 5]
    c = pl.program_id(1)
    npar = WKV_NPAR_SHORT

    def group(gq, carry):
        qs = [gq * npar + s for s in range(npar)]
        for s in range(npar):
            _pack_state(s0_ref, qs[s], s_scr, s)

        def store_o(s, t, o_row):
            o_ref[pl.ds((c * spb + qs[s]) * seq_len + t, 1), :] = o_row

        _wkv_steps([rows_refs] * npar, [(c * spb + q) * seq_len for q in qs], seq_len, npar, s_scr, bd_ref, store_o)
        for s in range(npar):
            _unpack_state(s_scr, s, sout_ref, qs[s])
        return carry

    lax.fori_loop(0, spb // npar, group, 0)


def _head_ones():
    head = jnp.arange(TILE_W) // HS_A
    return (head[:, None] == head[None, :]).astype(BF16)


def wkv_long(rows, *, n_seq, seq_len, blk):
    cps = seq_len // blk
    in_specs = [pl.BlockSpec((blk, C_A), functools.partial(lambda c, s: (s * cps + c, 0), s=s))
                for s in range(n_seq) for _ in range(RWKV_ROWS)]
    return pl.pallas_call(
        functools.partial(_wkv_long_kernel, npar=n_seq, blk=blk),
        grid=(cps,),
        in_specs=in_specs + [pl.BlockSpec((TILE_W, TILE_W), lambda c: (0, 0))],
        out_specs=[pl.BlockSpec((n_seq, blk, C_A), lambda c: (0, c, 0)),
                   pl.BlockSpec((n_seq, H_A, HS_A, HS_A), lambda c: (0, 0, 0, 0))],
        out_shape=[jax.ShapeDtypeStruct((n_seq, seq_len, C_A), F32),
                   jax.ShapeDtypeStruct((n_seq, H_A, HS_A, HS_A), F32)],
        scratch_shapes=[pltpu.VMEM((n_seq * N_TILES, HS_A, TILE_W), F32)],
        compiler_params=_cparams(("arbitrary",)),
        name="wkv_long",
    )(*(list(rows) * n_seq), _head_ones())


def wkv_short(rows, s0, *, row0, n_seq, seq_len, blk, s0_seq0):
    assert row0 % blk == 0 and blk % seq_len == 0
    b0 = row0 // blk
    nsb = blk // seq_len
    spb = min(WKV_SPB, nsb)
    assert n_seq % nsb == 0 and nsb % spb == 0 and spb % WKV_NPAR_SHORT == 0 and s0_seq0 % spb == 0
    sub = nsb // spb
    sspec = lambda off: pl.BlockSpec((spb, H_A, HS_A, HS_A), lambda s, c: (off + s * sub + c, 0, 0, 0))
    return pl.pallas_call(
        functools.partial(_wkv_short_kernel, seq_len=seq_len, spb=spb),
        grid=(n_seq // nsb, sub),
        in_specs=[pl.BlockSpec((blk, C_A), lambda s, c: (b0 + s, 0))] * RWKV_ROWS
        + [sspec(s0_seq0 // spb), pl.BlockSpec((TILE_W, TILE_W), lambda s, c: (0, 0))],
        out_specs=[pl.BlockSpec((blk, C_A), lambda s, c: (s, 0)), sspec(0)],
        out_shape=[jax.ShapeDtypeStruct((n_seq * seq_len, C_A), F32),
                   jax.ShapeDtypeStruct((n_seq, H_A, HS_A, HS_A), F32)],
        scratch_shapes=[pltpu.VMEM((WKV_NPAR_SHORT * N_TILES, HS_A, TILE_W), F32)],
        compiler_params=_cparams(("parallel", "arbitrary")),
        name="wkv_short",
    )(*rows, s0, _head_ones())


def _rwkv_post_kernel(o_in_ref, bonus_ref, g_ref, lng_ref, lnb_ref, gsel_ref, gselt_ref, o_ref):
    o = o_in_ref[...]
    gsel = gsel_ref[...]
    gselt = gselt_ref[...]
    mu = _dot_sel(o, gsel) * (1.0 / HS_A)
    d = o - _dot_sel(mu, gselt)
    var = _dot_sel(d * d, gsel) * (1.0 / HS_A)
    rstd = lax.rsqrt(var + GN_EPS_A)
    y = d * _dot_sel(rstd, gselt) * lng_ref[...] + lnb_ref[...]
    o_ref[...] = ((y + bonus_ref[...]) * g_ref[...]).astype(o_ref.dtype)


def rwkv_post(o_wkv, bonus, g, p, tm):
    t = bonus.shape[0]
    row = pl.BlockSpec((tm, C_A), lambda i: (i, 0))
    full = lambda a: pl.BlockSpec(a.shape, lambda i: (0,) * a.ndim)
    consts = [p["ln_g"], p["ln_b"], p["gsel"], p["gselt"]]
    return pl.pallas_call(
        _rwkv_post_kernel,
        grid=(t // tm,),
        in_specs=[row, row, row] + [full(c) for c in consts],
        out_specs=row,
        out_shape=jax.ShapeDtypeStruct((t, C_A), BF16),
        compiler_params=_cparams(("parallel",)),
        name="rwkv_post",
    )(o_wkv, bonus, g, *consts)


S5_KT = 4
S5_KU = C_B // S5_KT
S5_KH = S5_STATE // S5_KT
S5_LC = 512


def _gelu_tanh(x):
    return 0.5 * x * (1.0 + jnp.tanh(math.sqrt(2.0 / math.pi) * (x + 0.044715 * (x * x * x))))


def _s5_kernel(u_ref, wbr_ref, wbi_ref, lamk_ref, pre_ref, pim_ref, h0r_ref, h0i_ref, wcr_ref, wci_ref,
               dsk_ref, wglu_ref, bglu_ref, ob_ref, hr_out, hi_out, hre, him, car_re, car_im, y_scr,
               *, seq_len, blk):
    long_mode = seq_len >= blk
    period = SUBLANES if long_mode else seq_len
    u = u_ref[...]
    ub = u.astype(BF16)
    for kt in range(S5_KT):
        uk = ub[:, kt * S5_KU:(kt + 1) * S5_KU]
        hre[:, kt * S5_KH:(kt + 1) * S5_KH] = _dot(uk, wbr_ref[kt])
        him[:, kt * S5_KH:(kt + 1) * S5_KH] = _dot(uk, wbi_ref[kt])

    if long_mode:
        @pl.when(pl.program_id(1) == 0)
        def _():
            car_re[...] = jnp.zeros(car_re.shape, F32)
            car_im[...] = jnp.zeros(car_im.shape, F32)

    rowi = lax.broadcasted_iota(I32, (SUBLANES, 1), 0) % period
    for lc in range(S5_STATE // S5_LC):
        ls = slice(lc * S5_LC, (lc + 1) * S5_LC)
        lam = [(lamk_ref[2 * i:2 * i + 1, ls], lamk_ref[2 * i + 1:2 * i + 2, ls]) for i in range(3)]
        pre = pre_ref[:, ls]
        pim = pim_ref[:, ls]

        def tile(i, carry, ls=ls, lam=lam, pre=pre, pim=pim):
            r0 = pl.multiple_of(i * SUBLANES, SUBLANES)
            xr = hre[pl.ds(r0, SUBLANES), ls]
            xi = him[pl.ds(r0, SUBLANES), ls]
            for lvl, sh in enumerate((1, 2, 4)):
                if sh >= period:
                    break
                lr, li = lam[lvl]
                keep = rowi >= sh
                sr = jnp.where(keep, pltpu.roll(xr, sh, 0), 0.0)
                si = jnp.where(keep, pltpu.roll(xi, sh, 0), 0.0)
                xr, xi = xr + lr * sr - li * si, xi + lr * si + li * sr
            if long_mode:
                hr, hi = carry
            else:
                hr = h0r_ref[pl.ds(r0, SUBLANES), ls]
                hi = h0i_ref[pl.ds(r0, SUBLANES), ls]
            xr, xi = xr + pre * hr - pim * hi, xi + pre * hi + pim * hr
            hre[pl.ds(r0, SUBLANES), ls] = xr
            him[pl.ds(r0, SUBLANES), ls] = xi
            if long_mode:
                last_r = jnp.broadcast_to(xr[SUBLANES - 1:SUBLANES, :], xr.shape)
                last_i = jnp.broadcast_to(xi[SUBLANES - 1:SUBLANES, :], xi.shape)
                return (last_r, last_i)
            return carry

        if long_mode:
            init = (car_re[:, ls], car_im[:, ls])
            fin = lax.fori_loop(0, blk // SUBLANES, tile, init, unroll=2)
            car_re[:, ls] = fin[0]
            car_im[:, ls] = fin[1]
        else:
            lax.fori_loop(0, blk // SUBLANES, tile, 0)

    if long_mode:
        hr_out[0] = car_re[...]
        hi_out[0] = car_im[...]
    else:
        hr_out[...] = hre[...]
        hi_out[...] = him[...]

    for kt in range(S5_KT):
        hs = slice(kt * S5_KH, (kt + 1) * S5_KH)
        y_scr[:, kt * S5_KU:(kt + 1) * S5_KU] = (_dot(hre[:, hs].astype(BF16), wcr_ref[kt])
                                                 - _dot(him[:, hs].astype(BF16), wci_ref[kt]))
    y = y_scr[...] + dsk_ref[...] * u
    z = _gelu_tanh(y)
    gate = jax.nn.sigmoid(_dot(z.astype(BF16), wglu_ref[...]) + bglu_ref[...])
    ob_ref[...] = (z * gate).astype(ob_ref.dtype)


def s5_mix(u, p, h0r_rows, h0i_rows, *, row0, n_seq, seq_len, blk):
    long_mode = seq_len >= blk
    assert row0 % blk == 0
    b0 = row0 // blk
    t_out = n_seq * seq_len
    if long_mode:
        cps = seq_len // blk
        grid = (n_seq, cps)
        umap = lambda s, c: (b0 + s * cps + c, 0)
        omap = lambda s, c: (s * cps + c, 0)
        hspec = pl.BlockSpec((1, SUBLANES, S5_STATE), lambda s, c: (s, 0, 0))
        hshape = jax.ShapeDtypeStruct((n_seq, SUBLANES, S5_STATE), F32)
        h0spec = pl.BlockSpec((SUBLANES, S5_STATE), lambda s, c: (0, 0))
    else:
        assert t_out % blk == 0
        grid = (t_out // blk, 1)
        umap = lambda s, c: (b0 + s, 0)
        omap = lambda s, c: (s, 0)
        hspec = pl.BlockSpec((blk, S5_STATE), omap)
        hshape = jax.ShapeDtypeStruct((t_out, S5_STATE), F32)
        h0spec = pl.BlockSpec((blk, S5_STATE), omap)
    full = lambda a: pl.BlockSpec(a.shape, lambda s, c: (0,) * a.ndim)
    pre, pim = (p["pow_re8"], p["pow_im8"]) if long_mode else (p["pow_re_s"], p["pow_im_s"])
    args = [u, p["wb_re"], p["wb_im"], p["lamk"], pre, pim, h0r_rows, h0i_rows, p["wc_re"], p["wc_im"],
            p["d_skip"], p["w_glu"], p["b_glu"]]
    in_specs = [pl.BlockSpec((blk, C_B), umap)] + [full(a) for a in args[1:6]] + [h0spec, h0spec] \
        + [full(a) for a in args[8:]]
    return pl.pallas_call(
        functools.partial(_s5_kernel, seq_len=seq_len, blk=blk),
        grid=grid,
        in_specs=in_specs,
        out_specs=[pl.BlockSpec((blk, C_B), omap), hspec, hspec],
        out_shape=[jax.ShapeDtypeStruct((t_out, C_B), BF16), hshape, hshape],
        scratch_shapes=[pltpu.VMEM((blk, S5_STATE), F32), pltpu.VMEM((blk, S5_STATE), F32),
                        pltpu.VMEM((SUBLANES, S5_STATE), F32), pltpu.VMEM((SUBLANES, S5_STATE), F32),
                        pltpu.VMEM((blk, C_B), F32)],
        compiler_params=_cparams(("parallel", "arbitrary")),
        name="s5_long" if long_mode else "s5_short",
    )(*args)


def _pad_cols(a, n):
    return jnp.pad(a, ((0, 0), (0, n - a.shape[1])))


def _pad_rows(a, n):
    return jnp.pad(a, ((0, n - a.shape[0]), (0, 0)))


def _lora_layout(a):
    o = 3 * C_A
    return jnp.concatenate([_pad_cols(a[:, o:o + LORA_W], LANES),
                            _pad_cols(a[:, o + LORA_W:o + LORA_W + LORA_A], LANES),
                            _pad_cols(a[:, o + LORA_W + LORA_A:A_PROJ], 2 * LANES)], axis=1)


def _lora_unlayout(a):
    return jnp.concatenate([a[:, :LORA_W], a[:, LANES:LANES + LORA_A], a[:, 2 * LANES:2 * LANES + LORA_G]], axis=1)


def _cpow_table(lbr, lbi, n):
    res_r, res_i = [lbr], [lbi]
    for _ in range(n - 1):
        pr, pi = res_r[-1], res_i[-1]
        res_r.append(pr * lbr - pi * lbi)
        res_i.append(pr * lbi + pi * lbr)
    return (jnp.stack([x.reshape(-1) for x in res_r]), jnp.stack([x.reshape(-1) for x in res_i]))


def ab_params(li, w, sample_len):
    p = {}
    w_in = w["ab_w_in"][li]
    p["w_rkv"] = w_in[:, :3 * C_A].astype(BF16)
    p["w_lora"] = _lora_layout(w_in).astype(BF16)
    p["w_u"] = w_in[:, A_PROJ:].astype(BF16)
    mu = w["ab_mu"][li][None, :]
    p["mu_rkv"] = mu[:, :3 * C_A]
    p["mu_lora"] = _lora_layout(mu)
    row = lambda a: a.reshape(1, -1)
    p["w0"] = row(w["rwkv_w0"][li])
    p["w2"] = _pad_rows(w["rwkv_w2"][li], LANES)
    p["a0"] = row(w["rwkv_a0"][li])
    p["a2"] = _pad_rows(w["rwkv_a2"][li], LANES)
    p["g2"] = _pad_rows(w["rwkv_g2"][li], 2 * LANES)
    p["k_k"] = row(w["rwkv_k_k"][li])
    p["k_a"] = row(w["rwkv_k_a"][li])
    p["r_k"] = row(w["rwkv_r_k"][li])
    p["ln_g"] = row(w["rwkv_ln_g"][li])
    p["ln_b"] = row(w["rwkv_ln_b"][li])
    head_of_col = jnp.arange(C_A) // HS_A
    gsel = (head_of_col[:, None] == jnp.arange(LANES)[None, :])
    p["gsel"] = gsel.astype(BF16)
    p["gselt"] = gsel.T.astype(BF16)
    lr = jnp.minimum(w["s5_lam_re"][li], -1e-4)
    lim = w["s5_lam_im"][li]
    dt = jnp.exp(w["s5_log_dt"][li])[:, None]
    mag = jnp.exp(lr * dt)
    lbr, lbi = mag * jnp.cos(lim * dt), mag * jnp.sin(lim * dt)
    den = lr * lr + lim * lim
    pr, pim = lbr - 1.0, lbi
    fr = (pr * lr + pim * lim) / den
    fi = (pim * lr - pr * lim) / den
    br, bi = w["s5_b_re"][li], w["s5_b_im"][li]
    bbr = fr[..., None] * br - fi[..., None] * bi
    bbi = fr[..., None] * bi + fi[..., None] * br
    gpc = S5_GROUPS // S5_KT
    eye = jnp.eye(gpc, dtype=F32)

    def blockdiag_in(bb):
        b4 = bb.reshape(S5_KT, gpc, S5_N, S5_P)
        return jnp.einsum("kgnp,gh->kgphn", b4, eye).reshape(S5_KT, S5_KU, S5_KH).astype(BF16)

    def blockdiag_out(cc):
        c4 = cc.reshape(S5_KT, gpc, S5_P, S5_N)
        return jnp.einsum("kgpn,gh->kgnhp", c4, eye).reshape(S5_KT, S5_KH, S5_KU).astype(BF16)

    p["wb_re"], p["wb_im"] = blockdiag_in(bbr), blockdiag_in(bbi)
    p["wc_re"], p["wc_im"] = blockdiag_out(w["s5_c_re"][li]), blockdiag_out(w["s5_c_im"][li])
    pw_r, pw_i = _cpow_table(lbr, lbi, SUBLANES)
    p["pow_re8"], p["pow_im8"] = pw_r, pw_i
    reps = SUBLANES // sample_len
    p["pow_re_s"] = jnp.tile(pw_r[:sample_len], (reps, 1))
    p["pow_im_s"] = jnp.tile(pw_i[:sample_len], (reps, 1))
    p["lamk"] = jnp.stack([pw_r[0], pw_i[0], pw_r[1], pw_i[1], pw_r[3], pw_i[3]])
    p["d_skip"] = row(w["s5_d"][li])
    p["w_glu"] = w["s5_w_glu"][li].astype(BF16)
    p["b_glu"] = row(w["s5_b_glu"][li])
    w_out = w["ab_w_out"][li]
    p["w_out_a"] = w_out[:C_A].astype(BF16)
    p["w_out_b"] = w_out[C_A:].astype(BF16)
    return p


def _shift_rows(a, first_sample, geo):
    n_p, l_p, n_s, l_s = geo
    wd = a.shape[1]
    ap = a[:n_p * l_p].reshape(n_p, l_p, wd)
    prev_p = jnp.concatenate([jnp.zeros((n_p, 1, wd), a.dtype), ap[:, :-1]], axis=1)
    as_ = a[n_p * l_p:].reshape(n_s, l_s, wd)
    prev_s = jnp.concatenate([first_sample[:, None, :], as_[:, :-1]], axis=1)
    return jnp.concatenate([prev_p.reshape(-1, wd), prev_s.reshape(-1, wd)], axis=0)


def _last_rows(a, geo):
    n_p, l_p, n_s, l_s = geo
    wd = a.shape[1]
    return (a[:n_p * l_p].reshape(n_p, l_p, wd)[:, -1], a[n_p * l_p:].reshape(n_s, l_s, wd)[:, -1])


def ab_layer(xb, geo, p, state_shift, state_wkv, wkv_seq0, state_re, state_im, *, tm, blk):
    n_p, l_p, n_s, l_s = geo
    tp, ts = n_p * l_p, n_s * l_s
    rkv = matmul([xb], [p["w_rkv"]], tm, 1024, name="ab_in_rkv")
    lora = matmul([xb], [p["w_lora"]], tm, LORA_PAD, name="ab_in_lora")
    u = matmul([xb], [p["w_u"]], tm, C_B, name="ab_in_u")
    rkv_prev = _shift_rows(rkv, state_shift[:, :3 * C_A], geo)
    lora_prev = _shift_rows(lora, _lora_layout(state_shift), geo)
    prep = rwkv_prep(rkv, rkv_prev, lora, lora_prev, p, tm)
    rows, g, bonus = prep[:RWKV_ROWS], prep[RWKV_ROWS], prep[RWKV_ROWS + 1]
    o_p, wkv_p = wkv_long(rows, n_seq=n_p, seq_len=l_p, blk=blk)
    o_s, wkv_s = wkv_short(rows, state_wkv, row0=tp, n_seq=n_s, seq_len=l_s, blk=blk, s0_seq0=wkv_seq0)
    o_wkv = jnp.concatenate([o_p.reshape(tp, C_A), o_s], axis=0)
    o_a = rwkv_post(o_wkv, bonus, g, p, tm)
    dummy = jnp.zeros((SUBLANES, S5_STATE), F32)
    ob_p, hr_p, hi_p = s5_mix(u, p, dummy, dummy, row0=0, n_seq=n_p, seq_len=l_p, blk=blk)
    h0r = jnp.repeat(state_re.reshape(n_s, S5_STATE), l_s, axis=0)
    h0i = jnp.repeat(state_im.reshape(n_s, S5_STATE), l_s, axis=0)
    ob_s, hr_s, hi_s = s5_mix(u, p, h0r, h0i, row0=tp, n_seq=n_s, seq_len=l_s, blk=blk)
    o_b = jnp.concatenate([ob_p, ob_s], axis=0)
    h = matmul([o_a, o_b], [p["w_out_a"], p["w_out_b"]], tm, 1024, name="ab_out")
    rkv_lp, rkv_ls = _last_rows(rkv, geo)
    lo_lp, lo_ls = _last_rows(lora, geo)
    shift_p = jnp.concatenate([rkv_lp, _lora_unlayout(lo_lp)], axis=1)
    shift_s = jnp.concatenate([rkv_ls, _lora_unlayout(lo_ls)], axis=1)
    grp = lambda a, n: a.reshape(n, S5_GROUPS, S5_N)
    ssm_p = (grp(hr_p[:, SUBLANES - 1], n_p), grp(hi_p[:, SUBLANES - 1], n_p))
    ssm_s = (grp(hr_s[l_s - 1::l_s], n_s), grp(hi_s[l_s - 1::l_s], n_s))
    return h, (shift_p, shift_s), (wkv_p, wkv_s), ssm_p, ssm_s


def _router_kernel(x_ref, wr_ref, br_ref, ltri_ref, eidx_o, gate_o, rank_o, cnt_o, carry):
    i = pl.program_id(0)

    @pl.when(i == 0)
    def _():
        carry[...] = jnp.zeros(carry.shape, F32)

    tm = x_ref.shape[0]
    neg = -jnp.inf
    lane_i = lax.broadcasted_iota(I32, (tm, LANES), 1)
    lane = lane_i.astype(F32)
    per_group = N_EXPERTS // N_EXPERT_GROUPS
    grp = (lane_i // per_group).astype(F32)
    scores = jax.nn.sigmoid(_dot3(x_ref[...], wr_ref[...]))
    biased = jnp.where(lane_i < N_EXPERTS, scores + br_ref[...], neg)
    rmax = lambda a: jnp.max(a, axis=-1, keepdims=True)
    rmin = lambda a: jnp.min(a, axis=-1, keepdims=True)
    rsum = lambda a: jnp.sum(a, axis=-1, keepdims=True)

    gsc = jnp.full((tm, LANES), neg, F32)
    for g in range(N_EXPERT_GROUPS):
        xg = jnp.where(grp == g, biased, neg)
        m1 = rmax(xg)
        twice = rsum(jnp.where(xg == m1, 1.0, 0.0)) >= 2.0
        m2 = jnp.where(twice, m1, rmax(jnp.where(xg < m1, xg, neg)))
        gsc = jnp.where(lane == g, m1 + m2, gsc)
    keep = jnp.zeros((tm, LANES), jnp.bool_)
    for _ in range(TOPK_GROUPS):
        m = rmax(gsc)
        gi = rmin(jnp.where(gsc == m, lane, float(LANES)))
        keep = keep | (grp == gi)
        gsc = jnp.where(lane == gi, neg, gsc)
    cur = jnp.where(keep, biased, neg)

    sel = jnp.zeros((tm, LANES), F32)
    es, ss = [], []
    for _ in range(TOP_K):
        m = rmax(cur)
        ik = rmin(jnp.where(cur == m, lane, float(LANES)))
        hit = lane == ik
        ss.append(rsum(jnp.where(hit, scores, 0.0)))
        es.append(ik)
        cur = jnp.where(hit, neg, cur)
        sel = jnp.where(hit, 1.0, sel)
    tot = ss[0]
    for k in range(1, TOP_K):
        tot = tot + ss[k]

    prefix = _dot(ltri_ref[...], sel.astype(BF16)) + carry[0:1, :]
    eidx = jnp.zeros((tm, LANES), F32)
    gate = jnp.zeros((tm, LANES), F32)
    rank = jnp.zeros((tm, LANES), F32)
    for k in range(TOP_K):
        slot = lane == k
        eidx = jnp.where(slot, es[k], eidx)
        gate = jnp.where(slot, ss[k] / tot * ROUTED_SCALE, gate)
        rank = jnp.where(slot, rsum(jnp.where(lane == es[k], prefix, 0.0)), rank)
    eidx_o[...] = eidx.astype(I32)
    gate_o[...] = gate
    rank_o[...] = rank.astype(I32)
    carry[...] = carry[...] + jnp.sum(sel, axis=0, keepdims=True)
    cnt_o[...] = carry[...]


def moe_router(x, w_router_pad, b_router_pad, tm):
    t, d = x.shape
    ltri = (jnp.arange(tm)[:, None] > jnp.arange(tm)[None, :]).astype(BF16)
    row = pl.BlockSpec((tm, LANES), lambda i: (i, 0))
    out = jax.ShapeDtypeStruct((t, LANES), I32)
    return pl.pallas_call(
        _router_kernel,
        grid=(t // tm,),
        in_specs=[pl.BlockSpec((tm, d), lambda i: (i, 0)), pl.BlockSpec((d, LANES), lambda i: (0, 0)),
                  pl.BlockSpec((1, LANES), lambda i: (0, 0)), pl.BlockSpec((tm, tm), lambda i: (0, 0))],
        out_specs=[row, row, row, pl.BlockSpec((SUBLANES, LANES), lambda i: (0, 0))],
        out_shape=[out, jax.ShapeDtypeStruct((t, LANES), F32), out,
                   jax.ShapeDtypeStruct((SUBLANES, LANES), F32)],
        scratch_shapes=[pltpu.VMEM((SUBLANES, LANES), F32)],
        compiler_params=_cparams(("arbitrary",)),
        name="moe_router",
    )(x, w_router_pad, b_router_pad, ltri)


MOE_BM = 128


def _gather_rows_start(src_hbm, idx_ref, idx0, dst_buf, n_rows, sem):
    for j in range(n_rows):
        pltpu.make_async_copy(src_hbm.at[pl.ds(idx_ref[idx0 + j], 1), :], dst_buf.at[pl.ds(j, 1), :], sem).start()


def _gather_rows_wait(src_hbm, dst_buf, n_rows, sem):
    pltpu.make_async_copy(src_hbm.at[pl.ds(0, n_rows), :], dst_buf, sem).wait()


def _expert_kernel(blk_e_ref, nused_ref, tok_ref, x_hbm, wg_ref, wu_ref, wd_ref, ys_ref, xbuf_a, xbuf_b, sems,
                   wgb, wub, wdb):
    i = pl.program_id(0)
    n_used = nused_ref[0]
    active = i < n_used

    @pl.when(i == 0)
    def _():
        _gather_rows_start(x_hbm, tok_ref, 0, xbuf_a, MOE_BM, sems.at[0])

    @pl.when(active)
    def _():
        e = blk_e_ref[i]
        prev = blk_e_ref[jnp.maximum(i - 1, 0)]

        @pl.when((i == 0) | (e != prev))
        def _():
            wgb[...] = wg_ref[0].astype(BF16)
            wub[...] = wu_ref[0].astype(BF16)
            wdb[...] = wd_ref[0].astype(BF16)

    def block(cur_buf, cur_sem, nxt_buf, nxt_sem):
        _gather_rows_wait(x_hbm, cur_buf, MOE_BM, cur_sem)
        nxt_row0 = jnp.minimum(i + 1, n_used - 1) * MOE_BM
        _gather_rows_start(x_hbm, tok_ref, nxt_row0, nxt_buf, MOE_BM, nxt_sem)
        x = cur_buf[...].astype(BF16)
        hg = _dot(x, wgb[...])
        hu = _dot(x, wub[...])
        h = hg * jax.nn.sigmoid(hg) * hu
        ys_ref[...] = _dot(h.astype(BF16), wdb[...])

        @pl.when(i == n_used - 1)
        def _():
            _gather_rows_wait(x_hbm, nxt_buf, MOE_BM, nxt_sem)

    @pl.when(active & (i % 2 == 0))
    def _():
        block(xbuf_a, sems.at[0], xbuf_b, sems.at[1])

    @pl.when(active & (i % 2 == 1))
    def _():
        block(xbuf_b, sems.at[1], xbuf_a, sems.at[0])

    @pl.when(i >= nused_ref[0])
    def _():
        ys_ref[...] = jnp.zeros(ys_ref.shape, F32)


def moe_experts(x, blk_e, n_used, tok_of_row, w_gate, w_up, w_down):
    d = x.shape[1]
    n_rows = tok_of_row.shape[0]
    n_blocks = n_rows // MOE_BM
    de = w_gate.shape[2]
    wmap = lambda i, be, nu, tk: (be[jnp.minimum(i, nu[0] - 1)], 0, 0)
    return pl.pallas_call(
        _expert_kernel,
        grid_spec=pltpu.PrefetchScalarGridSpec(
            num_scalar_prefetch=3,
            grid=(n_blocks,),
            in_specs=[pl.BlockSpec(memory_space=pl.ANY),
                      pl.BlockSpec((1, d, de), wmap), pl.BlockSpec((1, d, de), wmap), pl.BlockSpec((1, de, d), wmap)],
            out_specs=pl.BlockSpec((MOE_BM, d), lambda i, be, nu, tk: (i, 0)),
            scratch_shapes=[pltpu.VMEM((MOE_BM, d), F32), pltpu.VMEM((MOE_BM, d), F32), pltpu.SemaphoreType.DMA((2,)),
                            pltpu.VMEM((d, de), BF16), pltpu.VMEM((d, de), BF16), pltpu.VMEM((de, d), BF16)],
        ),
        out_shape=jax.ShapeDtypeStruct((n_rows, d), F32),
        compiler_params=_cparams(("arbitrary",)),
        name="moe_experts",
    )(blk_e, n_used, tok_of_row, x, w_gate, w_up, w_down)


def _shared_ffn_kernel(x_ref, wg_ref, wu_ref, wd_ref, o_ref):
    x = x_ref[...]
    hg = _dot(x, wg_ref[...])
    hu = _dot(x, wu_ref[...])
    h = hg * jax.nn.sigmoid(hg) * hu
    o_ref[...] = _dot(h.astype(BF16), wd_ref[...])


def shared_ffn(xb, wg, wu, wd, tm):
    t, d = xb.shape
    full = lambda a: pl.BlockSpec(a.shape, lambda i: (0, 0))
    row = pl.BlockSpec((tm, d), lambda i: (i, 0))
    return pl.pallas_call(
        _shared_ffn_kernel,
        grid=(t // tm,),
        in_specs=[row, full(wg), full(wu), full(wd)],
        out_specs=row,
        out_shape=jax.ShapeDtypeStruct((t, d), F32),
        compiler_params=_cparams(("parallel",)),
        name="shared_ffn",
    )(xb, wg, wu, wd)


def _combine_kernel(dest_ref, ys_hbm, gate_ref, x_ref, sh_ref, g_ref, b_ref, o_ref, ob_ref, buf, sems, *, tm):
    i = pl.program_id(0)
    n_asg = tm * TOP_K

    @pl.when(i == 0)
    def _():
        _gather_rows_start(ys_hbm, dest_ref, 0, buf.at[0], n_asg, sems.at[0])

    @pl.when(i + 1 < pl.num_programs(0))
    def _():
        nxt = (i + 1) % 2
        _gather_rows_start(ys_hbm, dest_ref, (i + 1) * n_asg, buf.at[nxt], n_asg, sems.at[nxt])

    cur = i % 2
    _gather_rows_wait(ys_hbm, buf.at[cur], n_asg, sems.at[cur])
    gate = gate_ref[...]
    f = sh_ref[...]
    for k in range(TOP_K):
        f = f + gate[:, k:k + 1] * buf[cur, k * tm:(k + 1) * tm, :]
    z = ALPHA * x_ref[...] + f
    inv_d = 1.0 / z.shape[1]
    mu = _row_sum(z) * inv_d
    dlt = z - mu
    var = _row_sum(dlt * dlt) * inv_d
    y = dlt * lax.rsqrt(var + LN_EPS) * g_ref[...] + b_ref[...]
    o_ref[...] = y
    ob_ref[...] = y.astype(BF16)


def moe_combine(ys, dest_flat, gates, x, shared, g, b, tm):
    t, d = x.shape
    row = pl.BlockSpec((tm, d), lambda i, dr: (i, 0))
    vec = pl.BlockSpec((1, d), lambda i, dr: (0, 0))
    return pl.pallas_call(
        functools.partial(_combine_kernel, tm=tm),
        grid_spec=pltpu.PrefetchScalarGridSpec(
            num_scalar_prefetch=1,
            grid=(t // tm,),
            in_specs=[pl.BlockSpec(memory_space=pl.ANY), pl.BlockSpec((tm, LANES), lambda i, dr: (i, 0)),
                      row, row, vec, vec],
            out_specs=[row, row],
            scratch_shapes=[pltpu.VMEM((2, TOP_K * tm, d), F32), pltpu.SemaphoreType.DMA((2,))],
        ),
        out_shape=[jax.ShapeDtypeStruct((t, d), F32), jax.ShapeDtypeStruct((t, d), BF16)],
        compiler_params=_cparams(("arbitrary",)),
        name="moe_combine",
    )(dest_flat, ys, gates, x, shared, g.reshape(1, d), b.reshape(1, d))


def moe_params(layer, w):
    p = {}
    p["w_router"] = _pad_cols(w["moe_w_router"][layer], LANES)
    p["b_router"] = _pad_cols(w["moe_b_router"][layer][None, :], LANES)
    stack = lambda a: a.reshape((-1,) + a.shape[2:])
    p["w_gate"], p["w_up"], p["w_down"] = stack(w["moe_w_gate"]), stack(w["moe_w_up"]), stack(w["moe_w_down"])
    p["e0"] = layer * N_EXPERTS
    p["ws_gate"] = w["moe_ws_gate"][layer].astype(BF16)
    p["ws_up"] = w["moe_ws_up"][layer].astype(BF16)
    p["ws_down"] = w["moe_ws_down"][layer].astype(BF16)
    return p


def moe_layer(y, yb, p, ln_g, ln_b, *, tm, tm_dma):
    t = y.shape[0]
    eidx, gates, rank, cnt = moe_router(y, p["w_router"], p["b_router"], tm)
    counts = cnt[0, :N_EXPERTS].astype(I32)
    padded = (counts + MOE_BM - 1) // MOE_BM * MOE_BM
    pad_end = jnp.cumsum(padded)
    pad_start = pad_end - padded
    e8 = eidx[:, :TOP_K]
    dest = pad_start[e8] + rank[:, :TOP_K]
    n_blocks = -(-(t * TOP_K + N_EXPERTS * (MOE_BM - 1)) // MOE_BM)
    n_rows = n_blocks * MOE_BM
    blk_start = jnp.arange(n_blocks, dtype=I32) * MOE_BM
    blk_e = jnp.sum((pad_end[None, :] <= blk_start[:, None]).astype(I32), axis=1)
    blk_e = jnp.minimum(blk_e, N_EXPERTS - 1) + p["e0"]
    n_used = (pad_end[-1:] // MOE_BM).astype(I32)
    tok = jnp.broadcast_to(jnp.arange(t, dtype=I32)[:, None], (t, TOP_K))
    tok_of_row = jnp.zeros((n_rows,), I32).at[dest.reshape(-1)].set(tok.reshape(-1))
    ys = moe_experts(y, blk_e, n_used, tok_of_row, p["w_gate"], p["w_up"], p["w_down"])
    shared = shared_ffn(yb, p["ws_gate"], p["ws_up"], p["ws_down"], tm)
    dest_tiles = dest.reshape(t // tm_dma, tm_dma, TOP_K).transpose(0, 2, 1).reshape(-1)
    return moe_combine(ys, dest_tiles, gates, y, shared, ln_g, ln_b, tm_dma)


C_QKV = (H_C + 2 * KVH_C) * DH_C
C_QI = H_IDX * D_IDX
IDX_SCALE = (H_IDX ** -0.5) * (D_IDX ** -0.5)


def _rot_half64(x, lane):
    return jnp.where(lane % D_IDX < D_IDX // 2, pltpu.roll(x, LANES - D_IDX // 2, 1), pltpu.roll(x, D_IDX // 2, 1))


def _c_post_kernel(qkv_ref, qi_ref, kw_ref, c128_ref, s128_ref, c64_ref, s64_ref, lng_ref, lnb_ref,
                   qb_o, kf_o, kb_o, vb_o, qib_o, kwo_o):
    tm = qkv_ref.shape[0]
    c128, s128 = c128_ref[...], s128_ref[...]
    c64, s64 = c64_ref[...], s64_ref[...]
    lane = lax.broadcasted_iota(I32, (tm, LANES), 1)
    for h in range(H_C + KVH_C):
        x = qkv_ref[:, h * DH_C:(h + 1) * DH_C]
        y = x * c128 + pltpu.roll(x, DH_C // 2, 1) * s128
        if h < H_C:
            qb_o[:, h * DH_C:(h + 1) * DH_C] = (y * (DH_C ** -0.5)).astype(BF16)
        else:
            j = h - H_C
            kf_o[:, j * DH_C:(j + 1) * DH_C] = y
            kb_o[:, j * DH_C:(j + 1) * DH_C] = y.astype(BF16)
    vb_o[...] = qkv_ref[:, (H_C + KVH_C) * DH_C:].astype(BF16)
    for j in range(C_QI // LANES):
        x = qi_ref[:, j * LANES:(j + 1) * LANES]
        qib_o[:, j * LANES:(j + 1) * LANES] = (x * c64 + _rot_half64(x, lane) * s64).astype(BF16)
    x = kw_ref[...]
    is_k = lane < D_IDX
    mu = jnp.sum(jnp.where(is_k, x, 0.0), axis=-1, keepdims=True) * (1.0 / D_IDX)
    d = jnp.where(is_k, x - mu, 0.0)
    var = jnp.sum(d * d, axis=-1, keepdims=True) * (1.0 / D_IDX)
    y = d * lax.rsqrt(var + LN_EPS) * lng_ref[...] + lnb_ref[...]
    y = y * c64 + _rot_half64(y, lane) * s64
    kwo_o[...] = jnp.where(is_k, y, x * IDX_SCALE)


def c_post(qkv, qi, kw, tabs, lng, lnb, tm):
    t = qkv.shape[0]
    row = lambda wd: pl.BlockSpec((tm, wd), lambda i: (i, 0))
    vec = pl.BlockSpec((1, LANES), lambda i: (0, 0))
    sh = lambda wd, dt: jax.ShapeDtypeStruct((t, wd), dt)
    return pl.pallas_call(
        _c_post_kernel,
        grid=(t // tm,),
        in_specs=[row(C_QKV), row(C_QI), row(LANES), row(LANES), row(LANES), row(LANES), row(LANES), vec, vec],
        out_specs=[row(H_C * DH_C), row(KVH_C * DH_C), row(KVH_C * DH_C), row(KVH_C * DH_C), row(C_QI), row(LANES)],
        out_shape=[sh(H_C * DH_C, BF16), sh(KVH_C * DH_C, F32), sh(KVH_C * DH_C, BF16), sh(KVH_C * DH_C, BF16),
                   sh(C_QI, BF16), sh(LANES, F32)],
        compiler_params=_cparams(("parallel",)),
        name="c_post",
    )(qkv, qi, kw, *tabs, lng, lnb)


def _rope_tables(pos):
    def tab(d):
        inv = ROPE_THETA ** (-jnp.arange(0, d, 2, dtype=F32) / d)
        ang = pos.astype(F32)[:, None] * inv[None, :]
        c, s = jnp.cos(ang), jnp.sin(ang)
        return jnp.concatenate([c, c], axis=1), jnp.concatenate([-s, s], axis=1)

    c128, s128 = tab(DH_C)
    c64, s64 = tab(D_IDX)
    return c128, s128, jnp.tile(c64, (1, 2)), jnp.tile(s64, (1, 2))


def _index_heads(qi, wi, kb):
    acc = None
    for h in range(H_IDX):
        d = lax.dot_general(qi[:, h * D_IDX:(h + 1) * D_IDX], kb, (((1,), (1,)), ((), ())),
                            preferred_element_type=F32)
        term = wi[:, D_IDX + h:D_IDX + h + 1] * jnp.maximum(d, 0.0)
        acc = term if acc is None else acc + term
    return acc + 0.0


SROWS = 16


def _topk_mask(sc_ref, utri_ref, o_ref, key_scr, *, topk, chunk, bits_per_step=1):
    rows, s_len = sc_ref.shape
    bits = lax.bitcast_convert_type(sc_ref[...], I32)
    key_scr[...] = bits ^ ((bits >> 31) & jnp.int32(0x7FFFFFFF))
    sign = jnp.int32(-2 ** 31)
    kf = jnp.float32(topk)

    def enough(cand):
        return _row_sum(jnp.where(key_scr[...] >= (cand ^ sign), 1.0, 0.0)) >= kf

    def bisect(it, t_u):
        low = (32 - bits_per_step) - bits_per_step * it
        for pattern in range(1, 2 ** bits_per_step):
            cand = t_u | (jnp.int32(pattern) << low)
            best = jnp.where(enough(cand), cand, t_u if pattern == 1 else best)
        return best

    t_u = lax.fori_loop(0, 32 // bits_per_step, bisect, jnp.zeros((rows, 1), I32))
    thr = t_u ^ sign
    need = kf - _row_sum(jnp.where(key_scr[...] > thr, 1.0, 0.0))
    utri = utri_ref[...]
    run = jnp.zeros((rows, 1), F32)
    for c in range(s_len // chunk):
        cs = slice(c * chunk, (c + 1) * chunk)
        keyc = key_scr[:, cs]
        eqc = jnp.where(keyc == thr, 1.0, 0.0)
        before = _dot(eqc.astype(BF16), utri) + run
        pick = (keyc > thr) | ((keyc == thr) & (before < need))
        o_ref[:, cs] = jnp.where(pick & (sc_ref[:, cs] > -jnp.inf), 1.0, 0.0).astype(o_ref.dtype)
        run = run + _row_sum(eqc)
    if o_ref.shape[1] > s_len:
        o_ref[:, s_len:] = jnp.zeros((rows, o_ref.shape[1] - s_len), o_ref.dtype)


def _strict_upper(chunk):
    return (jnp.arange(chunk)[:, None] < jnp.arange(chunk)[None, :]).astype(BF16)


SELECT_CLASSES = 4


def _index_select_kernel(qi_ref, wi_ref, keys_ref, utri_ref, o_ref, sc_scr, key_scr, *, q0, topk, chunk):
    qblk, width = sc_scr.shape
    kb = keys_ref[:width, :D_IDX].astype(BF16)
    sc = _index_heads(qi_ref[...], wi_ref[...], kb)
    row = lax.broadcasted_iota(I32, (qblk, width), 0) + (q0 + pl.program_id(1) * qblk)
    col = lax.broadcasted_iota(I32, (qblk, width), 1)
    sc_scr[...] = jnp.where(col <= row, sc, -jnp.inf)
    _topk_mask(sc_scr, utri_ref, o_ref, key_scr, topk=topk, chunk=chunk)


def index_select_causal(qib, kwo, topk, n_seq, seq_len, qblk, chunk):
    n_cls = SELECT_CLASSES if seq_len % (SELECT_CLASSES * max(chunk, qblk)) == 0 else 1
    cw = seq_len // n_cls
    utri = _strict_upper(chunk)
    parts = []
    for c in range(n_cls):
        width = (c + 1) * cw
        qmap = functools.partial(lambda b, i, c: (b * (seq_len // qblk) + c * (cw // qblk) + i, 0), c=c)
        parts.append(pl.pallas_call(
            functools.partial(_index_select_kernel, q0=c * cw, topk=topk, chunk=chunk),
            grid=(n_seq, cw // qblk),
            in_specs=[pl.BlockSpec((qblk, C_QI), qmap), pl.BlockSpec((qblk, LANES), qmap),
                      pl.BlockSpec((seq_len, LANES), lambda b, i: (b, 0)),
                      pl.BlockSpec((chunk, chunk), lambda b, i: (0, 0))],
            out_specs=pl.BlockSpec((qblk, seq_len), lambda b, i: (b * (cw // qblk) + i, 0)),
            out_shape=jax.ShapeDtypeStruct((n_seq * cw, seq_len), BF16),
            scratch_shapes=[pltpu.VMEM((qblk, width), F32), pltpu.VMEM((qblk, width), I32)],
            compiler_params=_cparams(("parallel", "parallel")),
            name="index_select_%d" % c,
        )(qib, kwo, kwo, utri).reshape(n_seq, cw, seq_len))
    return jnp.concatenate(parts, axis=1).reshape(n_seq * seq_len, seq_len)


NEG_BIG = -1e30


def _attn_init(m_scr, l_scr, acc_scr):
    m_scr[...] = jnp.full(m_scr.shape, NEG_BIG, F32)
    l_scr[...] = jnp.zeros(l_scr.shape, F32)
    acc_scr[...] = jnp.zeros(acc_scr.shape, F32)


def _attn_block(q, k, v, mask, m_scr, l_scr, acc_scr):
    bias = jnp.concatenate([(mask.astype(F32) - 1.0) * (-NEG_BIG)] * GROUP_C, axis=0)
    heads = range(KVH_C)
    ss = []
    for kh in heads:
        qh = jnp.concatenate([q[:, (kh * GROUP_C + g) * DH_C:(kh * GROUP_C + g + 1) * DH_C]
                              for g in range(GROUP_C)], axis=0)
        ks = k[:, kh * DH_C:(kh + 1) * DH_C]
        ss.append(lax.dot_general(qh, ks, (((1,), (1,)), ((), ())), preferred_element_type=F32) + bias)
    n_tiles = ss[0].shape[1] // LANES
    m_old = [m_scr[kh] for kh in heads]
    m_new = [jnp.maximum(m_old[kh], jnp.broadcast_to(_row_max(ss[kh]), m_old[kh].shape)) for kh in heads]
    ps = [[jnp.exp(ss[kh][:, i * LANES:(i + 1) * LANES] - m_new[kh]) for i in range(n_tiles)] for kh in heads]
    for kh in heads:
        alpha = jnp.exp(m_old[kh] - m_new[kh])
        l_scr[kh] = alpha * l_scr[kh] + functools.reduce(jnp.add, ps[kh])
        p = jnp.concatenate(ps[kh], axis=1).astype(BF16)
        acc_scr[kh] = alpha * acc_scr[kh] + _dot(p, v[:, kh * DH_C:(kh + 1) * DH_C])
        m_scr[kh] = m_new[kh]


def _attn_finish(o_ref, nq, l_scr, acc_scr, lead=None):
    for kh in range(KVH_C):
        out = acc_scr[kh] / jnp.sum(l_scr[kh], axis=-1, keepdims=True)
        for g in range(GROUP_C):
            hs = slice((kh * GROUP_C + g) * DH_C, (kh * GROUP_C + g + 1) * DH_C)
            val = out[g * nq:(g + 1) * nq, :].astype(o_ref.dtype)
            if lead is None:
                o_ref[:, hs] = val
            else:
                o_ref[lead, :, hs] = val


def _attn_prompt_kernel(q_ref, k_ref, v_ref, mask_ref, o_ref, m_scr, l_scr, acc_scr, *, qblk, kblk):
    qb = pl.program_id(1)
    kb = pl.program_id(2)

    @pl.when(kb == 0)
    def _():
        _attn_init(m_scr, l_scr, acc_scr)

    @pl.when(kb * kblk <= qb * qblk + qblk - 1)
    def _():
        _attn_block(q_ref[...], k_ref[...], v_ref[...], mask_ref[...], m_scr, l_scr, acc_scr)

    @pl.when(kb == pl.num_programs(2) - 1)
    def _():
        _attn_finish(o_ref, qblk, l_scr, acc_scr)


def attn_prompt(qb_, kb_, vb_, mask, n_p, l_p, qblk, kblk):
    nq, nk = l_p // qblk, l_p // kblk
    last = lambda q, k: jnp.minimum(k, (q * qblk + qblk - 1) // kblk)
    rows = GROUP_C * qblk
    return pl.pallas_call(
        functools.partial(_attn_prompt_kernel, qblk=qblk, kblk=kblk),
        grid=(n_p, nq, nk),
        in_specs=[pl.BlockSpec((qblk, H_C * DH_C), lambda b, q, k: (b * nq + q, 0)),
                  pl.BlockSpec((kblk, KVH_C * DH_C), lambda b, q, k: (b * nk + last(q, k), 0)),
                  pl.BlockSpec((kblk, KVH_C * DH_C), lambda b, q, k: (b * nk + last(q, k), 0)),
                  pl.BlockSpec((qblk, kblk), lambda b, q, k: (b * nq + q, last(q, k)))],
        out_specs=pl.BlockSpec((qblk, H_C * DH_C), lambda b, q, k: (b * nq + q, 0)),
        out_shape=jax.ShapeDtypeStruct((n_p * l_p, H_C * DH_C), BF16),
        scratch_shapes=[pltpu.VMEM((KVH_C, rows, LANES), F32), pltpu.VMEM((KVH_C, rows, LANES), F32),
                        pltpu.VMEM((KVH_C, rows, DH_C), F32)],
        compiler_params=_cparams(("parallel", "parallel", "arbitrary")),
        name="attn_prompt",
    )(qb_, kb_, vb_, mask)


def _dsa_sample_kernel(pt_ref, qi_ref, kw_ref, q_ref, kn_ref, vn_ref, utri_ref, kidx_hbm, k_hbm, v_hbm, o_ref,
                       kibuf, kbuf, vbuf, sems, sc_scr, key_scr, mask_scr, *, n_pages, page, l_s, topk, pool0):
    b = pl.program_id(0)
    past = n_pages * page
    s_len = past + page

    def page_copies(bb, slot, pg):
        phys = pt_ref[bb * n_pages + pg] + pool0
        rows = pl.ds(pg * page, page)
        cps = [pltpu.make_async_copy(kidx_hbm.at[phys], kibuf.at[slot, rows, :], sems.at[slot, 0])]
        for kh in range(KVH_C):
            cps.append(pltpu.make_async_copy(k_hbm.at[phys, :, kh, :], kbuf.at[slot, kh, rows, :], sems.at[slot, 1]))
            cps.append(pltpu.make_async_copy(v_hbm.at[phys, :, kh, :], vbuf.at[slot, kh, rows, :], sems.at[slot, 2]))
        return cps

    def fetch(bb, slot):
        for pg in range(n_pages):
            for cp in page_copies(bb, slot, pg):
                cp.start()

    @pl.when(b == 0)
    def _():
        kibuf[:, past:, :] = jnp.zeros((2, page, D_IDX), F32)
        kbuf[:, :, past:, :] = jnp.zeros((2, KVH_C, page, DH_C), F32)
        vbuf[:, :, past:, :] = jnp.zeros((2, KVH_C, page, DH_C), F32)
        fetch(0, 0)

    @pl.when(b + 1 < pl.num_programs(0))
    def _():
        fetch(b + 1, (b + 1) % 2)

    cur = b % 2
    for pg in range(n_pages):
        for cp in page_copies(b, cur, pg):
            cp.wait()
    kw = kw_ref[0]
    kibuf[cur, past:past + SROWS, :] = kw[:, :D_IDX]
    for kh in range(KVH_C):
        hs = slice(kh * DH_C, (kh + 1) * DH_C)
        kbuf[cur, kh, past:past + SROWS, :] = kn_ref[0][:, hs]
        vbuf[cur, kh, past:past + SROWS, :] = vn_ref[0][:, hs]

    dots = lax.dot_general(qi_ref[0], kibuf[cur].astype(BF16), (((1,), (1,)), ((), ())),
                           preferred_element_type=F32)
    sc = None
    for h in range(H_IDX):
        term = kw[:, D_IDX + h:D_IDX + h + 1] * jnp.maximum(dots[h * SROWS:(h + 1) * SROWS, :], 0.0)
        sc = term if sc is None else sc + term
    row = lax.broadcasted_iota(I32, (SROWS, s_len), 0)
    col = lax.broadcasted_iota(I32, (SROWS, s_len), 1)
    visible = (col < past) | ((col - past <= row) & (col - past < l_s))
    sc_scr[...] = jnp.where(visible, sc + 0.0, -jnp.inf)
    _topk_mask(sc_scr, utri_ref, mask_scr, key_scr, topk=topk, chunk=page, bits_per_step=2)

    bias = jnp.concatenate([(mask_scr[...] - 1.0) * (-NEG_BIG)] * GROUP_C, axis=0)
    q = q_ref[0]
    for kh in range(KVH_C):
        qh = jnp.concatenate([q[:, (kh * GROUP_C + g) * DH_C:(kh * GROUP_C + g + 1) * DH_C]
                              for g in range(GROUP_C)], axis=0)
        s = lax.dot_general(qh, kbuf[cur, kh].astype(BF16), (((1,), (1,)), ((), ())),
                            preferred_element_type=F32) + bias
        p = jnp.exp(s - _row_max(s))
        out = _dot(p.astype(BF16), vbuf[cur, kh].astype(BF16)) / _row_sum(p)
        for g in range(GROUP_C):
            o_ref[0, :, (kh * GROUP_C + g) * DH_C:(kh * GROUP_C + g + 1) * DH_C] = (
                out[g * SROWS:(g + 1) * SROWS, :].astype(o_ref.dtype))


def dsa_sample(page_table, qi_hm, kw3, q3, knew3, vnew3, cache_kidx3, cache_k4, cache_v4, *, pool0, l_s, topk):
    n_s, n_pages = page_table.shape
    page = cache_k4.shape[1]
    kvw = KVH_C * DH_C
    s_len = (n_pages + 1) * page
    own = lambda wd: pl.BlockSpec((1, SROWS, wd), lambda b, pt: (b, 0, 0))
    hbm = pl.BlockSpec(memory_space=pl.ANY)
    return pl.pallas_call(
        functools.partial(_dsa_sample_kernel, n_pages=n_pages, page=page, l_s=l_s, topk=topk, pool0=pool0),
        grid_spec=pltpu.PrefetchScalarGridSpec(
            num_scalar_prefetch=1,
            grid=(n_s,),
            in_specs=[pl.BlockSpec((1, H_IDX * SROWS, D_IDX), lambda b, pt: (b, 0, 0)),
                      own(LANES), own(H_C * DH_C), own(kvw), own(kvw),
                      pl.BlockSpec((page, page), lambda b, pt: (0, 0)), hbm, hbm, hbm],
            out_specs=own(H_C * DH_C),
            scratch_shapes=[pltpu.VMEM((2, s_len, D_IDX), F32), pltpu.VMEM((2, KVH_C, s_len, DH_C), F32),
                            pltpu.VMEM((2, KVH_C, s_len, DH_C), F32), pltpu.SemaphoreType.DMA((2, 3)),
                            pltpu.VMEM((SROWS, s_len), F32), pltpu.VMEM((SROWS, s_len), I32),
                            pltpu.VMEM((SROWS, s_len), F32)],
        ),
        out_shape=jax.ShapeDtypeStruct((n_s, SROWS, H_C * DH_C), BF16),
        compiler_params=_cparams(("arbitrary",)),
        name="dsa_sample",
    )(page_table.reshape(-1), qi_hm, kw3, q3, knew3, vnew3, _strict_upper(page), cache_kidx3, cache_k4, cache_v4)


def c_params(li, w):
    p = {}
    w_in = w["c_w_in"][li]
    p["w_qkv"] = w_in[:, :C_QKV].astype(BF16)
    p["w_qi"] = w_in[:, C_QKV:C_QKV + C_QI].astype(BF16)
    p["w_kw"] = _pad_cols(w_in[:, C_QKV + C_QI:], LANES).astype(BF16)
    p["ln_g"] = _pad_cols(w["c_kidx_ln_g"][li][None, :], LANES)
    p["ln_b"] = _pad_cols(w["c_kidx_ln_b"][li][None, :], LANES)
    p["w_out"] = w["c_w_out"][li].astype(BF16)
    return p


def _pad_sample_rows(a, n_s, l_s):
    a3 = a.reshape(n_s, l_s, a.shape[1])
    return jnp.pad(a3, ((0, 0), (0, SROWS - l_s), (0, 0)))


def c_layer(xb, geo, p, li, cache_k, cache_v, cache_kidx, page_table, *, tm, qblk, kblk):
    n_p, l_p, n_s, l_s = geo
    tp = n_p * l_p
    n_pool, page = cache_k.shape[1], cache_k.shape[2]
    n_pages = page_table.shape[1]
    past = n_pages * page
    qkv = matmul([xb], [p["w_qkv"]], tm, 1024, name="c_in_qkv")
    qi = matmul([xb], [p["w_qi"]], tm, C_QI, name="c_in_qi")
    kw = matmul([xb], [p["w_kw"]], tm, LANES, name="c_in_kw")
    pos = jnp.concatenate([jnp.tile(jnp.arange(l_p, dtype=I32), n_p),
                           jnp.tile(past + jnp.arange(l_s, dtype=I32), n_s)])
    q_b, k_f, k_b, v_b, qi_b, kwo = c_post(qkv, qi, kw, _rope_tables(pos), p["ln_g"], p["ln_b"], tm)
    v_f = qkv[:, (H_C + KVH_C) * DH_C:]
    mask_p = index_select_causal(qi_b, kwo, min(TOPK_MAX, l_p // 4), n_p, l_p, qblk, 256)
    o_p = attn_prompt(q_b, k_b, v_b, mask_p, n_p, l_p, qblk, kblk)
    pad = lambda a: _pad_sample_rows(a[tp:], n_s, l_s)
    qi_hm = pad(qi_b).reshape(n_s, SROWS, H_IDX, D_IDX).transpose(0, 2, 1, 3).reshape(n_s, H_IDX * SROWS, D_IDX)
    pool = lambda a: a.reshape((-1,) + a.shape[2:])
    o_s3 = dsa_sample(page_table, qi_hm, pad(kwo), pad(q_b), pad(k_f), pad(v_f),
                      pool(cache_kidx), pool(cache_k), pool(cache_v), pool0=li * n_pool, l_s=l_s,
                      topk=min(TOPK_MAX, (past + l_s) // 4))
    o = jnp.concatenate([o_p, o_s3[:, :l_s].reshape(n_s * l_s, -1)], axis=0)
    h = matmul([o], [p["w_out"]], tm, 1024, name="c_out")
    heads = lambda a, n, l: a.reshape(n, l, KVH_C, DH_C)
    new_p = (heads(k_f[:tp], n_p, l_p), heads(v_f[:tp], n_p, l_p), kwo[:tp, :D_IDX].reshape(n_p, l_p, D_IDX))
    new_s = (heads(k_f[tp:], n_s, l_s), heads(v_f[tp:], n_s, l_s), kwo[tp:, :D_IDX].reshape(n_s, l_s, D_IDX))
    return h, new_p, new_s


TM = 256
BLK = 128
TM_DMA = 64
QBLK = 128
KBLK = 512


def kernel(x_prompt, x_sample, state_shift, state_wkv, state_ssm_re, state_ssm_im, cache_k, cache_v, cache_kidx, page_table, ab_w_in, ab_mu, rwkv_w0, rwkv_w2, rwkv_a0, rwkv_a2, rwkv_g2, rwkv_k_k, rwkv_k_a, rwkv_r_k, rwkv_ln_g, rwkv_ln_b, s5_lam_re, s5_lam_im, s5_log_dt, s5_b_re, s5_b_im, s5_c_re, s5_c_im, s5_d, s5_w_glu, s5_b_glu, ab_w_out, c_w_in, c_kidx_ln_g, c_kidx_ln_b, c_w_out, ln_mix_g, ln_mix_b, ln_ffn_g, ln_ffn_b, moe_w_router, moe_b_router, moe_w_gate, moe_w_up, moe_w_down, moe_ws_gate, moe_ws_up, moe_ws_down):
    w = dict(locals())
    n_p, l_p, _ = x_prompt.shape
    n_s, l_s, _ = x_sample.shape
    geo = (n_p, l_p, n_s, l_s)
    tp = n_p * l_p
    y = jnp.concatenate([x_prompt.reshape(-1, D_MODEL), x_sample.reshape(-1, D_MODEL)], axis=0)
    yb = y.astype(BF16)
    outs = {k: [] for k in ("kp", "vp", "kip", "ks", "vs", "kis", "shp", "shs", "wkvp", "wkvs",
                            "srp", "sip", "srs", "sis")}
    for layer in range(DEPTH):
        li = layer // 2
        if layer % 2 == 0:
            wkv_all = state_wkv.reshape((-1,) + state_wkv.shape[2:])
            h, shift, wkv, ssm_p, ssm_s = ab_layer(yb, geo, ab_params(li, w, l_s), state_shift[li], wkv_all, li * n_s,
                                                   state_ssm_re[li], state_ssm_im[li], tm=TM, blk=BLK)
            for k, v in zip(("shp", "shs", "wkvp", "wkvs", "srp", "sip", "srs", "sis"),
                            (*shift, *wkv, *ssm_p, *ssm_s)):
                outs[k].append(v)
        else:
            h, new_p, new_s = c_layer(yb, geo, c_params(li, w), li, cache_k, cache_v, cache_kidx,
                                      page_table, tm=TM, qblk=QBLK, kblk=KBLK)
            for k, v in zip(("kp", "vp", "kip", "ks", "vs", "kis"), (*new_p, *new_s)):
                outs[k].append(v)
        y, yb = add_ln(y, h, ln_mix_g[layer], ln_mix_b[layer], TM)
        y, yb = moe_layer(y, yb, moe_params(layer, w), ln_ffn_g[layer], ln_ffn_b[layer], tm=TM, tm_dma=TM_DMA)
    st = lambda k: jnp.stack(outs[k])
    return (y[:tp].reshape(n_p, l_p, D_MODEL), y[tp:].reshape(n_s, l_s, D_MODEL),
            st("kp"), st("vp"), st("kip"), st("ks"), st("vs"), st("kis"),
            st("shp"), st("shs"), st("wkvp"), st("wkvs"), st("srp"), st("sip"), st("srs"), st("sis"))
```

```python
import functools
import math

import jax
import jax.numpy as jnp
from jax import lax
from jax.experimental import pallas as pl
from jax.experimental.pallas import tpu as pltpu

F32, BF16, I32 = jnp.float32, jnp.bfloat16, jnp.int32

D_MODEL = 2048
DEPTH = 2
ALPHA = (2.0 * DEPTH) ** 0.25
LN_EPS = 1e-5
C_A = 1024
HS_A = 64
H_A = 16
LORA_W, LORA_A, LORA_G = 64, 64, 160
A_PROJ = 3 * C_A + LORA_W + LORA_A + LORA_G
GN_EPS_A = 64e-5
C_B = 1024
S5_P = 16
S5_GROUPS = 64
S5_N = 64
S5_STATE = S5_GROUPS * S5_N
DH_C = 128
H_C = 16
KVH_C = 4
GROUP_C = 4
H_IDX = 16
D_IDX = 64
TOPK_MAX = 256
ROPE_THETA = 10000.0
N_EXPERTS = 64
TOP_K = 8
N_EXPERT_GROUPS = 8
TOPK_GROUPS = 4
D_EXPERT = 512
ROUTED_SCALE = 2.5

LANES = 128
SUBLANES = 8
VMEM_LIMIT = 56 * 1024 * 1024

LORA_PAD = 512
EXP_M05 = math.exp(-0.5)


def _cparams(sem):
    return pltpu.CompilerParams(dimension_semantics=sem, vmem_limit_bytes=VMEM_LIMIT)


def _dot(a, b):
    return jnp.dot(a, b, preferred_element_type=F32)


def _lane_fold(x, op):
    tiles = [x[:, i * LANES:(i + 1) * LANES] for i in range(x.shape[1] // LANES)]
    return functools.reduce(op, tiles)


def _row_sum(x):
    return jnp.sum(_lane_fold(x, jnp.add), axis=-1, keepdims=True)


def _row_max(x):
    return jnp.max(_lane_fold(x, jnp.maximum), axis=-1, keepdims=True)


def _split3(a):
    a1 = a.astype(BF16)
    r1 = a - a1.astype(F32)
    a2 = r1.astype(BF16)
    a3 = (r1 - a2.astype(F32)).astype(BF16)
    return a1, a2, a3


def _dot_sel(a, sel_bf16):
    a1, a2, a3 = _split3(a)
    return _dot(a1, sel_bf16) + _dot(a2, sel_bf16) + _dot(a3, sel_bf16)


def _dot3(a, b):
    a1 = a.astype(BF16)
    a2 = (a - a1.astype(F32)).astype(BF16)
    b1 = b.astype(BF16)
    b2 = (b - b1.astype(F32)).astype(BF16)
    return _dot(a1, b1) + _dot(a1, b2) + _dot(a2, b1)


def _mm_kernel(*refs, n_lhs):
    o_ref = refs[-1]
    acc = None
    for i in range(n_lhs):
        d = _dot(refs[i][...], refs[n_lhs + i][...])
        acc = d if acc is None else acc + d
    o_ref[...] = acc.astype(o_ref.dtype)


MM_TM = 512


def matmul(xs, ws, tm, tn, out_dtype=F32, name="mm"):
    m = xs[0].shape[0]
    n = ws[0].shape[1]
    tm = MM_TM if m % MM_TM == 0 else tm
    assert m % tm == 0 and n % tn == 0, (m, tm, n, tn)
    in_specs = ([pl.BlockSpec((tm, x.shape[1]), lambda j, i: (i, 0)) for x in xs]
                + [pl.BlockSpec((w.shape[0], tn), lambda j, i: (0, j)) for w in ws])
    return pl.pallas_call(
        functools.partial(_mm_kernel, n_lhs=len(xs)),
        grid=(n // tn, m // tm),
        in_specs=in_specs,
        out_specs=pl.BlockSpec((tm, tn), lambda j, i: (i, j)),
        out_shape=jax.ShapeDtypeStruct((m, n), out_dtype),
        compiler_params=_cparams(("parallel", "parallel")),
        name=name,
    )(*xs, *ws)


def _add_ln_kernel(x_ref, h_ref, g_ref, b_ref, o_ref, ob_ref):
    z = ALPHA * x_ref[...] + h_ref[...]
    inv_d = 1.0 / z.shape[1]
    mu = _row_sum(z) * inv_d
    d = z - mu
    var = _row_sum(d * d) * inv_d
    y = d * lax.rsqrt(var + LN_EPS) * g_ref[...] + b_ref[...]
    o_ref[...] = y
    ob_ref[...] = y.astype(BF16)


def add_ln(x, h, g, b, tm):
    t, d = x.shape
    row = pl.BlockSpec((tm, d), lambda i: (i, 0))
    vec = pl.BlockSpec((1, d), lambda i: (0, 0))
    return pl.pallas_call(
        _add_ln_kernel,
        grid=(t // tm,),
        in_specs=[row, row, vec, vec],
        out_specs=[row, row],
        out_shape=[jax.ShapeDtypeStruct((t, d), F32), jax.ShapeDtypeStruct((t, d), BF16)],
        compiler_params=_cparams(("parallel",)),
        name="add_ln",
    )(x, h, g.reshape(1, d), b.reshape(1, d))


def _rwkv_prep_kernel(rkv_ref, rkvp_ref, lo_ref, lop_ref, mu1_ref, mu2_ref, w0_ref, w2_ref, a0_ref, a2_ref,
                      g2_ref, kk_ref, ka_ref, rk_ref, gsel_ref, gselt_ref,
                      kk_o, wr_o, w_o, b_o, k_o, v_o, kr_o, g_o, bonus_o):
    x = rkv_ref[...]
    m = x + (rkvp_ref[...] - x) * mu1_ref[...]
    r = m[:, :C_A]
    k = m[:, C_A:2 * C_A]
    v = m[:, 2 * C_A:]
    l = lo_ref[...]
    lm = l + (lop_ref[...] - l) * mu2_ref[...]
    lw = lm[:, :LANES]
    la = lm[:, LANES:2 * LANES]
    lg = lm[:, 2 * LANES:]
    w_raw = w0_ref[...] + _dot3(jnp.tanh(lw), w2_ref[...])
    decay = jnp.exp(-EXP_M05 * jax.nn.sigmoid(w_raw))
    a = jax.nn.sigmoid(a0_ref[...] + _dot3(la, a2_ref[...]))
    g = _dot3(jax.nn.sigmoid(lg), g2_ref[...])
    gsel = gsel_ref[...]
    gselt = gselt_ref[...]
    kk = k * kk_ref[...]
    nrm = jnp.sqrt(_dot_sel(kk * kk, gsel))
    inv = 1.0 / jnp.maximum(nrm, 1e-12)
    kk = kk * _dot_sel(inv, gselt)
    k_mod = k * (1.0 + (a - 1.0) * ka_ref[...])
    sb = _dot_sel(r * k_mod * rk_ref[...], gsel)
    bonus_o[...] = _dot_sel(sb, gselt) * v
    g_o[...] = g
    b = kk * a
    kk_o[...] = kk
    br = _dot_sel(_dot_sel(b * r, gsel), gselt)
    wr_o[...] = decay * r - kk * br
    w_o[...] = decay
    b_o[...] = b
    k_o[...] = k_mod
    v_o[...] = v
    kr_o[...] = _dot_sel(_dot_sel(k_mod * r, gsel), gselt)


RWKV_ROWS = 7


def rwkv_prep(rkv, rkv_prev, lora, lora_prev, p, tm):
    t = rkv.shape[0]
    row = lambda w: pl.BlockSpec((tm, w), lambda i: (i, 0))
    full = lambda a: pl.BlockSpec(a.shape, lambda i: (0,) * a.ndim)
    consts = [p["mu_rkv"], p["mu_lora"], p["w0"], p["w2"], p["a0"], p["a2"], p["g2"], p["k_k"], p["k_a"], p["r_k"],
              p["gsel"], p["gselt"]]
    n_out = RWKV_ROWS + 2
    return pl.pallas_call(
        _rwkv_prep_kernel,
        grid=(t // tm,),
        in_specs=[row(3 * C_A), row(3 * C_A), row(LORA_PAD), row(LORA_PAD)] + [full(c) for c in consts],
        out_specs=[row(C_A)] * n_out,
        out_shape=[jax.ShapeDtypeStruct((t, C_A), F32)] * n_out,
        compiler_params=_cparams(("parallel",)),
        name="rwkv_prep",
    )(rkv, rkv_prev, lora, lora_prev, *consts)


TILE_HEADS = 4
TILE_W = TILE_HEADS * HS_A
N_TILES = H_A // TILE_HEADS
LHS_PER_TILE = 3
WKV_UNROLL = 4


def _wkv_steps(rows, row_base, n_steps, npar, s_scr, bd_ref, store_o):
    sub = lax.broadcasted_iota(I32, (HS_A, TILE_W), 0)
    lane = lax.broadcasted_iota(I32, (HS_A, TILE_W), 1)
    diag = (lane % HS_A) == sub
    bd = bd_ref[...]
    blk_rows = LHS_PER_TILE * HS_A

    def step(t, carry):
        vec = [[ref[pl.ds(row_base[s] + t, 1), :] for ref in rows[s]] for s in range(npar)]
        results = []
        for s in range(npar):
            kk, wr2, _, _, _, v, _ = vec[s]
            for p in range(N_TILES):
                ls = slice(p * TILE_W, (p + 1) * TILE_W)
                st = s_scr[s * N_TILES + p]
                q1 = (st * kk[:, ls]).astype(BF16)
                q2 = (st * wr2[:, ls]).astype(BF16)
                ve = jnp.where(diag, v[:, ls], 0.0).astype(BF16)
                results.append(_dot(jnp.concatenate([q1, q2, ve], axis=0), bd))
        for s in range(npar):
            _, _, w, b, k, v, kr = vec[s]
            o_parts = []
            for p in range(N_TILES):
                ls = slice(p * TILE_W, (p + 1) * TILE_W)
                res = results[s * N_TILES + p]
                z1 = res[:HS_A]
                z2 = res[HS_A:2 * HS_A]
                vb = res[2 * HS_A:]
                st = s_scr[s * N_TILES + p]
                s_scr[s * N_TILES + p] = st * w[:, ls] - z1 * b[:, ls] + vb * k[:, ls]
                z2r = jnp.sum(jnp.where(diag, z2, 0.0), axis=0, keepdims=True)
                o_parts.append(z2r + v[:, ls] * kr[:, ls])
            store_o(s, t, jnp.concatenate(o_parts, axis=1))
        return carry

    lax.fori_loop(0, n_steps, step, 0, unroll=WKV_UNROLL)


def _pack_state(s_ref, q, s_scr, slot):
    for p in range(N_TILES):
        s_scr[slot * N_TILES + p] = jnp.concatenate([s_ref[q, TILE_HEADS * p + h] for h in range(TILE_HEADS)], axis=1)


def _unpack_state(s_scr, slot, s_ref, q):
    for p in range(N_TILES):
        st = s_scr[slot * N_TILES + p]
        for h in range(TILE_HEADS):
            s_ref[q, TILE_HEADS * p + h] = st[:, h * HS_A:(h + 1) * HS_A]


def _wkv_long_kernel(*refs, npar, blk):
    n_in = npar * RWKV_ROWS
    rows = [refs[s * RWKV_ROWS:(s + 1) * RWKV_ROWS] for s in range(npar)]
    bd_ref, o_ref, sout_ref, s_scr = refs[n_in:n_in + 4]
    c = pl.program_id(0)

    @pl.when(c == 0)
    def _():
        s_scr[...] = jnp.zeros(s_scr.shape, F32)

    def store_o(s, t, o_row):
        o_ref[s, pl.ds(t, 1), :] = o_row

    _wkv_steps(rows, [0] * npar, blk, npar, s_scr, bd_ref, store_o)

    @pl.when(c == pl.num_programs(0) - 1)
    def _():
        for s in range(npar):
            _unpack_state(s_scr, s, sout_ref, s)


WKV_NPAR_SHORT = 2
WKV_SPB = 8


def _wkv_short_kernel(*refs, seq_len, spb):
    rows_refs = refs[:RWKV_ROWS]
    s0_ref, bd_ref, o_ref, sout_ref, s_scr = refs[RWKV_ROWS:RWKV_ROWS + 5]
    c = pl.program_id(1)
    npar = WKV_NPAR_SHORT

    def group(gq, carry):
        qs = [gq * npar + s for s in range(npar)]
        for s in range(npar):
            _pack_state(s0_ref, qs[s], s_scr, s)

        def store_o(s, t, o_row):
            o_ref[pl.ds((c * spb + qs[s]) * seq_len + t, 1), :] = o_row

        _wkv_steps([rows_refs] * npar, [(c * spb + q) * seq_len for q in qs], seq_len, npar, s_scr, bd_ref, store_o)
        for s in range(npar):
            _unpack_state(s_scr, s, sout_ref, qs[s])
        return carry

    lax.fori_loop(0, spb // npar, group, 0)


def _head_ones():
    head = jnp.arange(TILE_W) // HS_A
    return (head[:, None] == head[None, :]).astype(BF16)


def wkv_long(rows, *, n_seq, seq_len, blk):
    cps = seq_len // blk
    in_specs = [pl.BlockSpec((blk, C_A), functools.partial(lambda c, s: (s * cps + c, 0), s=s))
                for s in range(n_seq) for _ in range(RWKV_ROWS)]
    return pl.pallas_call(
        functools.partial(_wkv_long_kernel, npar=n_seq, blk=blk),
        grid=(cps,),
        in_specs=in_specs + [pl.BlockSpec((TILE_W, TILE_W), lambda c: (0, 0))],
        out_specs=[pl.BlockSpec((n_seq, blk, C_A), lambda c: (0, c, 0)),
                   pl.BlockSpec((n_seq, H_A, HS_A, HS_A), lambda c: (0, 0, 0, 0))],
        out_shape=[jax.ShapeDtypeStruct((n_seq, seq_len, C_A), F32),
                   jax.ShapeDtypeStruct((n_seq, H_A, HS_A, HS_A), F32)],
        scratch_shapes=[pltpu.VMEM((n_seq * N_TILES, HS_A, TILE_W), F32)],
        compiler_params=_cparams(("arbitrary",)),
        name="wkv_long",
    )(*(list(rows) * n_seq), _head_ones())


def wkv_short(rows, s0, *, row0, n_seq, seq_len, blk, s0_seq0):
    assert row0 % blk == 0 and blk % seq_len == 0
    b0 = row0 // blk
    nsb = blk // seq_len
    spb = min(WKV_SPB, nsb)
    assert n_seq % nsb == 0 and nsb % spb == 0 and spb % WKV_NPAR_SHORT == 0 and s0_seq0 % spb == 0
    sub = nsb // spb
    sspec = lambda off: pl.BlockSpec((spb, H_A, HS_A, HS_A), lambda s, c: (off + s * sub + c, 0, 0, 0))
    return pl.pallas_call(
        functools.partial(_wkv_short_kernel, seq_len=seq_len, spb=spb),
        grid=(n_seq // nsb, sub),
        in_specs=[pl.BlockSpec((blk, C_A), lambda s, c: (b0 + s, 0))] * RWKV_ROWS
        + [sspec(s0_seq0 // spb), pl.BlockSpec((TILE_W, TILE_W), lambda s, c: (0, 0))],
        out_specs=[pl.BlockSpec((blk, C_A), lambda s, c: (s, 0)), sspec(0)],
        out_shape=[jax.ShapeDtypeStruct((n_seq * seq_len, C_A), F32),
                   jax.ShapeDtypeStruct((n_seq, H_A, HS_A, HS_A), F32)],
        scratch_shapes=[pltpu.VMEM((WKV_NPAR_SHORT * N_TILES, HS_A, TILE_W), F32)],
        compiler_params=_cparams(("parallel", "arbitrary")),
        name="wkv_short",
    )(*rows, s0, _head_ones())


def _rwkv_post_kernel(o_in_ref, bonus_ref, g_ref, lng_ref, lnb_ref, gsel_ref, gselt_ref, o_ref):
    o = o_in_ref[...]
    gsel = gsel_ref[...]
    gselt = gselt_ref[...]
    mu = _dot_sel(o, gsel) * (1.0 / HS_A)
    d = o - _dot_sel(mu, gselt)
    var = _dot_sel(d * d, gsel) * (1.0 / HS_A)
    rstd = lax.rsqrt(var + GN_EPS_A)
    y = d * _dot_sel(rstd, gselt) * lng_ref[...] + lnb_ref[...]
    o_ref[...] = ((y + bonus_ref[...]) * g_ref[...]).astype(o_ref.dtype)


def rwkv_post(o_wkv, bonus, g, p, tm):
    t = bonus.shape[0]
    row = pl.BlockSpec((tm, C_A), lambda i: (i, 0))
    full = lambda a: pl.BlockSpec(a.shape, lambda i: (0,) * a.ndim)
    consts = [p["ln_g"], p["ln_b"], p["gsel"], p["gselt"]]
    return pl.pallas_call(
        _rwkv_post_kernel,
        grid=(t // tm,),
        in_specs=[row, row, row] + [full(c) for c in consts],
        out_specs=row,
        out_shape=jax.ShapeDtypeStruct((t, C_A), BF16),
        compiler_params=_cparams(("parallel",)),
        name="rwkv_post",
    )(o_wkv, bonus, g, *consts)


S5_KT = 4
S5_KU = C_B // S5_KT
S5_KH = S5_STATE // S5_KT
S5_LC = 512


def _gelu_tanh(x):
    return 0.5 * x * (1.0 + jnp.tanh(math.sqrt(2.0 / math.pi) * (x + 0.044715 * (x * x * x))))


def _s5_kernel(u_ref, wbr_ref, wbi_ref, lamk_ref, pre_ref, pim_ref, h0r_ref, h0i_ref, wcr_ref, wci_ref,
               dsk_ref, wglu_ref, bglu_ref, ob_ref, hr_out, hi_out, hre, him, car_re, car_im, y_scr,
               *, seq_len, blk):
    long_mode = seq_len >= blk
    period = SUBLANES if long_mode else seq_len
    u = u_ref[...]
    ub = u.astype(BF16)
    for kt in range(S5_KT):
        uk = ub[:, kt * S5_KU:(kt + 1) * S5_KU]
        hre[:, kt * S5_KH:(kt + 1) * S5_KH] = _dot(uk, wbr_ref[kt])
        him[:, kt * S5_KH:(kt + 1) * S5_KH] = _dot(uk, wbi_ref[kt])

    if long_mode:
        @pl.when(pl.program_id(1) == 0)
        def _():
            car_re[...] = jnp.zeros(car_re.shape, F32)
            car_im[...] = jnp.zeros(car_im.shape, F32)

    rowi = lax.broadcasted_iota(I32, (SUBLANES, 1), 0) % period
    for lc in range(S5_STATE // S5_LC):
        ls = slice(lc * S5_LC, (lc + 1) * S5_LC)
        lam = [(lamk_ref[2 * i:2 * i + 1, ls], lamk_ref[2 * i + 1:2 * i + 2, ls]) for i in range(3)]
        pre = pre_ref[:, ls]
        pim = pim_ref[:, ls]

        def tile(i, carry, ls=ls, lam=lam, pre=pre, pim=pim):
            r0 = pl.multiple_of(i * SUBLANES, SUBLANES)
            xr = hre[pl.ds(r0, SUBLANES), ls]
            xi = him[pl.ds(r0, SUBLANES), ls]
            for lvl, sh in enumerate((1, 2, 4)):
                if sh >= period:
                    break
                lr, li = lam[lvl]
                keep = rowi >= sh
                sr = jnp.where(keep, pltpu.roll(xr, sh, 0), 0.0)
                si = jnp.where(keep, pltpu.roll(xi, sh, 0), 0.0)
                xr, xi = xr + lr * sr - li * si, xi + lr * si + li * sr
            if long_mode:
                hr, hi = carry
            else:
                hr = h0r_ref[pl.ds(r0, SUBLANES), ls]
                hi = h0i_ref[pl.ds(r0, SUBLANES), ls]
            xr, xi = xr + pre * hr - pim * hi, xi + pre * hi + pim * hr
            hre[pl.ds(r0, SUBLANES), ls] = xr
            him[pl.ds(r0, SUBLANES), ls] = xi
            if long_mode:
                last_r = jnp.broadcast_to(xr[SUBLANES - 1:SUBLANES, :], xr.shape)
                last_i = jnp.broadcast_to(xi[SUBLANES - 1:SUBLANES, :], xi.shape)
                return (last_r, last_i)
            return carry

        if long_mode:
            init = (car_re[:, ls], car_im[:, ls])
            fin = lax.fori_loop(0, blk // SUBLANES, tile, init, unroll=2)
            car_re[:, ls] = fin[0]
            car_im[:, ls] = fin[1]
        else:
            lax.fori_loop(0, blk // SUBLANES, tile, 0)

    if long_mode:
        hr_out[0] = car_re[...]
        hi_out[0] = car_im[...]
    else:
        hr_out[...] = hre[...]
        hi_out[...] = him[...]

    for kt in range(S5_KT):
        hs = slice(kt * S5_KH, (kt + 1) * S5_KH)
        y_scr[:, kt * S5_KU:(kt + 1) * S5_KU] = (_dot(hre[:, hs].astype(BF16), wcr_ref[kt])
                                                 - _dot(him[:, hs].astype(BF16), wci_ref[kt]))
    y = y_scr[...] + dsk_ref[...] * u
    z = _gelu_tanh(y)
    gate = jax.nn.sigmoid(_dot(z.astype(BF16), wglu_ref[...]) + bglu_ref[...])
    ob_ref[...] = (z * gate).astype(ob_ref.dtype)


def s5_mix(u, p, h0r_rows, h0i_rows, *, row0, n_seq, seq_len, blk):
    long_mode = seq_len >= blk
    assert row0 % blk == 0
    b0 = row0 // blk
    t_out = n_seq * seq_len
    if long_mode:
        cps = seq_len // blk
        grid = (n_seq, cps)
        umap = lambda s, c: (b0 + s * cps + c, 0)
        omap = lambda s, c: (s * cps + c, 0)
        hspec = pl.BlockSpec((1, SUBLANES, S5_STATE), lambda s, c: (s, 0, 0))
        hshape = jax.ShapeDtypeStruct((n_seq, SUBLANES, S5_STATE), F32)
        h0spec = pl.BlockSpec((SUBLANES, S5_STATE), lambda s, c: (0, 0))
    else:
        assert t_out % blk == 0
        grid = (t_out // blk, 1)
        umap = lambda s, c: (b0 + s, 0)
        omap = lambda s, c: (s, 0)
        hspec = pl.BlockSpec((blk, S5_STATE), omap)
        hshape = jax.ShapeDtypeStruct((t_out, S5_STATE), F32)
        h0spec = pl.BlockSpec((blk, S5_STATE), omap)
    full = lambda a: pl.BlockSpec(a.shape, lambda s, c: (0,) * a.ndim)
    pre, pim = (p["pow_re8"], p["pow_im8"]) if long_mode else (p["pow_re_s"], p["pow_im_s"])
    args = [u, p["wb_re"], p["wb_im"], p["lamk"], pre, pim, h0r_rows, h0i_rows, p["wc_re"], p["wc_im"],
            p["d_skip"], p["w_glu"], p["b_glu"]]
    in_specs = [pl.BlockSpec((blk, C_B), umap)] + [full(a) for a in args[1:6]] + [h0spec, h0spec] \
        + [full(a) for a in args[8:]]
    return pl.pallas_call(
        functools.partial(_s5_kernel, seq_len=seq_len, blk=blk),
        grid=grid,
        in_specs=in_specs,
        out_specs=[pl.BlockSpec((blk, C_B), omap), hspec, hspec],
        out_shape=[jax.ShapeDtypeStruct((t_out, C_B), BF16), hshape, hshape],
        scratch_shapes=[pltpu.VMEM((blk, S5_STATE), F32), pltpu.VMEM((blk, S5_STATE), F32),
                        pltpu.VMEM((SUBLANES, S5_STATE), F32), pltpu.VMEM((SUBLANES, S5_STATE), F32),
                        pltpu.VMEM((blk, C_B), F32)],
        compiler_params=_cparams(("parallel", "arbitrary")),
        name="s5_long" if long_mode else "s5_short",
    )(*args)


def _pad_cols(a, n):
    return jnp.pad(a, ((0, 0), (0, n - a.shape[1])))


def _pad_rows(a, n):
    return jnp.pad(a, ((0, n - a.shape[0]), (0, 0)))


def _lora_layout(a):
    o = 3 * C_A
    return jnp.concatenate([_pad_cols(a[:, o:o + LORA_W], LANES),
                            _pad_cols(a[:, o + LORA_W:o + LORA_W + LORA_A], LANES),
                            _pad_cols(a[:, o + LORA_W + LORA_A:A_PROJ], 2 * LANES)], axis=1)


def _lora_unlayout(a):
    return jnp.concatenate([a[:, :LORA_W], a[:, LANES:LANES + LORA_A], a[:, 2 * LANES:2 * LANES + LORA_G]], axis=1)


def _cpow_table(lbr, lbi, n):
    res_r, res_i = [lbr], [lbi]
    for _ in range(n - 1):
        pr, pi = res_r[-1], res_i[-1]
        res_r.append(pr * lbr - pi * lbi)
        res_i.append(pr * lbi + pi * lbr)
    return (jnp.stack([x.reshape(-1) for x in res_r]), jnp.stack([x.reshape(-1) for x in res_i]))


def ab_params(li, w, sample_len):
    p = {}
    w_in = w["ab_w_in"][li]
    p["w_rkv"] = w_in[:, :3 * C_A].astype(BF16)
    p["w_lora"] = _lora_layout(w_in).astype(BF16)
    p["w_u"] = w_in[:, A_PROJ:].astype(BF16)
    mu = w["ab_mu"][li][None, :]
    p["mu_rkv"] = mu[:, :3 * C_A]
    p["mu_lora"] = _lora_layout(mu)
    row = lambda a: a.reshape(1, -1)
    p["w0"] = row(w["rwkv_w0"][li])
    p["w2"] = _pad_rows(w["rwkv_w2"][li], LANES)
    p["a0"] = row(w["rwkv_a0"][li])
    p["a2"] = _pad_rows(w["rwkv_a2"][li], LANES)
    p["g2"] = _pad_rows(w["rwkv_g2"][li], 2 * LANES)
    p["k_k"] = row(w["rwkv_k_k"][li])
    p["k_a"] = row(w["rwkv_k_a"][li])
    p["r_k"] = row(w["rwkv_r_k"][li])
    p["ln_g"] = row(w["rwkv_ln_g"][li])
    p["ln_b"] = row(w["rwkv_ln_b"][li])
    head_of_col = jnp.arange(C_A) // HS_A
    gsel = (head_of_col[:, None] == jnp.arange(LANES)[None, :])
    p["gsel"] = gsel.astype(BF16)
    p["gselt"] = gsel.T.astype(BF16)
    lr = jnp.minimum(w["s5_lam_re"][li], -1e-4)
    lim = w["s5_lam_im"][li]
    dt = jnp.exp(w["s5_log_dt"][li])[:, None]
    mag = jnp.exp(lr * dt)
    lbr, lbi = mag * jnp.cos(lim * dt), mag * jnp.sin(lim * dt)
    den = lr * lr + lim * lim
    pr, pim = lbr - 1.0, lbi
    fr = (pr * lr + pim * lim) / den
    fi = (pim * lr - pr * lim) / den
    br, bi = w["s5_b_re"][li], w["s5_b_im"][li]
    bbr = fr[..., None] * br - fi[..., None] * bi
    bbi = fr[..., None] * bi + fi[..., None] * br
    gpc = S5_GROUPS // S5_KT
    eye = jnp.eye(gpc, dtype=F32)

    def blockdiag_in(bb):
        b4 = bb.reshape(S5_KT, gpc, S5_N, S5_P)
        return jnp.einsum("kgnp,gh->kgphn", b4, eye).reshape(S5_KT, S5_KU, S5_KH).astype(BF16)

    def blockdiag_out(cc):
        c4 = cc.reshape(S5_KT, gpc, S5_P, S5_N)
        return jnp.einsum("kgpn,gh->kgnhp", c4, eye).reshape(S5_KT, S5_KH, S5_KU).astype(BF16)

    p["wb_re"], p["wb_im"] = blockdiag_in(bbr), blockdiag_in(bbi)
    p["wc_re"], p["wc_im"] = blockdiag_out(w["s5_c_re"][li]), blockdiag_out(w["s5_c_im"][li])
    pw_r, pw_i = _cpow_table(lbr, lbi, SUBLANES)
    p["pow_re8"], p["pow_im8"] = pw_r, pw_i
    reps = SUBLANES // sample_len
    p["pow_re_s"] = jnp.tile(pw_r[:sample_len], (reps, 1))
    p["pow_im_s"] = jnp.tile(pw_i[:sample_len], (reps, 1))
    p["lamk"] = jnp.stack([pw_r[0], pw_i[0], pw_r[1], pw_i[1], pw_r[3], pw_i[3]])
    p["d_skip"] = row(w["s5_d"][li])
    p["w_glu"] = w["s5_w_glu"][li].astype(BF16)
    p["b_glu"] = row(w["s5_b_glu"][li])
    w_out = w["ab_w_out"][li]
    p["w_out_a"] = w_out[:C_A].astype(BF16)
    p["w_out_b"] = w_out[C_A:].astype(BF16)
    return p


def _shift_rows(a, first_sample, geo):
    n_p, l_p, n_s, l_s = geo
    wd = a.shape[1]
    ap = a[:n_p * l_p].reshape(n_p, l_p, wd)
    prev_p = jnp.concatenate([jnp.zeros((n_p, 1, wd), a.dtype), ap[:, :-1]], axis=1)
    as_ = a[n_p * l_p:].reshape(n_s, l_s, wd)
    prev_s = jnp.concatenate([first_sample[:, None, :], as_[:, :-1]], axis=1)
    return jnp.concatenate([prev_p.reshape(-1, wd), prev_s.reshape(-1, wd)], axis=0)


def _last_rows(a, geo):
    n_p, l_p, n_s, l_s = geo
    wd = a.shape[1]
    return (a[:n_p * l_p].reshape(n_p, l_p, wd)[:, -1], a[n_p * l_p:].reshape(n_s, l_s, wd)[:, -1])


def ab_layer(xb, geo, p, state_shift, state_wkv, wkv_seq0, state_re, state_im, *, tm, blk):
    n_p, l_p, n_s, l_s = geo
    tp, ts = n_p * l_p, n_s * l_s
    rkv = matmul([xb], [p["w_rkv"]], tm, 1024, name="ab_in_rkv")
    lora = matmul([xb], [p["w_lora"]], tm, LORA_PAD, name="ab_in_lora")
    u = matmul([xb], [p["w_u"]], tm, C_B, name="ab_in_u")
    rkv_prev = _shift_rows(rkv, state_shift[:, :3 * C_A], geo)
    lora_prev = _shift_rows(lora, _lora_layout(state_shift), geo)
    prep = rwkv_prep(rkv, rkv_prev, lora, lora_prev, p, tm)
    rows, g, bonus = prep[:RWKV_ROWS], prep[RWKV_ROWS], prep[RWKV_ROWS + 1]
    o_p, wkv_p = wkv_long(rows, n_seq=n_p, seq_len=l_p, blk=blk)
    o_s, wkv_s = wkv_short(rows, state_wkv, row0=tp, n_seq=n_s, seq_len=l_s, blk=blk, s0_seq0=wkv_seq0)
    o_wkv = jnp.concatenate([o_p.reshape(tp, C_A), o_s], axis=0)
    o_a = rwkv_post(o_wkv, bonus, g, p, tm)
    dummy = jnp.zeros((SUBLANES, S5_STATE), F32)
    ob_p, hr_p, hi_p = s5_mix(u, p, dummy, dummy, row0=0, n_seq=n_p, seq_len=l_p, blk=blk)
    h0r = jnp.repeat(state_re.reshape(n_s, S5_STATE), l_s, axis=0)
    h0i = jnp.repeat(state_im.reshape(n_s, S5_STATE), l_s, axis=0)
    ob_s, hr_s, hi_s = s5_mix(u, p, h0r, h0i, row0=tp, n_seq=n_s, seq_len=l_s, blk=blk)
    o_b = jnp.concatenate([ob_p, ob_s], axis=0)
    h = matmul([o_a, o_b], [p["w_out_a"], p["w_out_b"]], tm, 1024, name="ab_out")
    rkv_lp, rkv_ls = _last_rows(rkv, geo)
    lo_lp, lo_ls = _last_rows(lora, geo)
    shift_p = jnp.concatenate([rkv_lp, _lora_unlayout(lo_lp)], axis=1)
    shift_s = jnp.concatenate([rkv_ls, _lora_unlayout(lo_ls)], axis=1)
    grp = lambda a, n: a.reshape(n, S5_GROUPS, S5_N)
    ssm_p = (grp(hr_p[:, SUBLANES - 1], n_p), grp(hi_p[:, SUBLANES - 1], n_p))
    ssm_s = (grp(hr_s[l_s - 1::l_s], n_s), grp(hi_s[l_s - 1::l_s], n_s))
    return h, (shift_p, shift_s), (wkv_p, wkv_s), ssm_p, ssm_s


def _router_kernel(x_ref, wr_ref, br_ref, ltri_ref, eidx_o, gate_o, rank_o, cnt_o, carry):
    i = pl.program_id(0)

    @pl.when(i == 0)
    def _():
        carry[...] = jnp.zeros(carry.shape, F32)

    tm = x_ref.shape[0]
    neg = -jnp.inf
    lane_i = lax.broadcasted_iota(I32, (tm, LANES), 1)
    lane = lane_i.astype(F32)
    per_group = N_EXPERTS // N_EXPERT_GROUPS
    grp = (lane_i // per_group).astype(F32)
    scores = jax.nn.sigmoid(_dot3(x_ref[...], wr_ref[...]))
    biased = jnp.where(lane_i < N_EXPERTS, scores + br_ref[...], neg)
    rmax = lambda a: jnp.max(a, axis=-1, keepdims=True)
    rmin = lambda a: jnp.min(a, axis=-1, keepdims=True)
    rsum = lambda a: jnp.sum(a, axis=-1, keepdims=True)

    gsc = jnp.full((tm, LANES), neg, F32)
    for g in range(N_EXPERT_GROUPS):
        xg = jnp.where(grp == g, biased, neg)
        m1 = rmax(xg)
        twice = rsum(jnp.where(xg == m1, 1.0, 0.0)) >= 2.0
        m2 = jnp.where(twice, m1, rmax(jnp.where(xg < m1, xg, neg)))
        gsc = jnp.where(lane == g, m1 + m2, gsc)
    keep = jnp.zeros((tm, LANES), jnp.bool_)
    for _ in range(TOPK_GROUPS):
        m = rmax(gsc)
        gi = rmin(jnp.where(gsc == m, lane, float(LANES)))
        keep = keep | (grp == gi)
        gsc = jnp.where(lane == gi, neg, gsc)
    cur = jnp.where(keep, biased, neg)

    sel = jnp.zeros((tm, LANES), F32)
    es, ss = [], []
    for _ in range(TOP_K):
        m = rmax(cur)
        ik = rmin(jnp.where(cur == m, lane, float(LANES)))
        hit = lane == ik
        ss.append(rsum(jnp.where(hit, scores, 0.0)))
        es.append(ik)
        cur = jnp.where(hit, neg, cur)
        sel = jnp.where(hit, 1.0, sel)
    tot = ss[0]
    for k in range(1, TOP_K):
        tot = tot + ss[k]

    prefix = _dot(ltri_ref[...], sel.astype(BF16)) + carry[0:1, :]
    eidx = jnp.zeros((tm, LANES), F32)
    gate = jnp.zeros((tm, LANES), F32)
    rank = jnp.zeros((tm, LANES), F32)
    for k in range(TOP_K):
        slot = lane == k
        eidx = jnp.where(slot, es[k], eidx)
        gate = jnp.where(slot, ss[k] / tot * ROUTED_SCALE, gate)
        rank = jnp.where(slot, rsum(jnp.where(lane == es[k], prefix, 0.0)), rank)
    eidx_o[...] = eidx.astype(I32)
    gate_o[...] = gate
    rank_o[...] = rank.astype(I32)
    carry[...] = carry[...] + jnp.sum(sel, axis=0, keepdims=True)
    cnt_o[...] = carry[...]


def moe_router(x, w_router_pad, b_router_pad, tm):
    t, d = x.shape
    ltri = (jnp.arange(tm)[:, None] > jnp.arange(tm)[None, :]).astype(BF16)
    row = pl.BlockSpec((tm, LANES), lambda i: (i, 0))
    out = jax.ShapeDtypeStruct((t, LANES), I32)
    return pl.pallas_call(
        _router_kernel,
        grid=(t // tm,),
        in_specs=[pl.BlockSpec((tm, d), lambda i: (i, 0)), pl.BlockSpec((d, LANES), lambda i: (0, 0)),
                  pl.BlockSpec((1, LANES), lambda i: (0, 0)), pl.BlockSpec((tm, tm), lambda i: (0, 0))],
        out_specs=[row, row, row, pl.BlockSpec((SUBLANES, LANES), lambda i: (0, 0))],
        out_shape=[out, jax.ShapeDtypeStruct((t, LANES), F32), out,
                   jax.ShapeDtypeStruct((SUBLANES, LANES), F32)],
        scratch_shapes=[pltpu.VMEM((SUBLANES, LANES), F32)],
        compiler_params=_cparams(("arbitrary",)),
        name="moe_router",
    )(x, w_router_pad, b_router_pad, ltri)


MOE_BM = 256


DMA_PRIORITIES = 2


def _gather_rows_start(src_hbm, idx_ref, idx0, dst_buf, n_rows, sem, n_threads=1):
    for j in range(n_rows):
        pltpu.make_async_copy(src_hbm.at[pl.ds(idx_ref[idx0 + j], 1), :], dst_buf.at[pl.ds(j, 1), :],
                              sem).start(priority=j % n_threads)


def _gather_rows_wait(src_hbm, dst_buf, n_rows, sem):
    pltpu.make_async_copy(src_hbm.at[pl.ds(0, n_rows), :], dst_buf, sem).wait()


def _expert_kernel(blk_e_ref, nused_ref, tok_ref, x_hbm, wg_ref, wu_ref, wd_ref, ys_ref, xbuf_a, xbuf_b, sems,
                   wgb, wub, wdb):
    i = pl.program_id(0)
    n_used = nused_ref[0]
    active = i < n_used

    @pl.when(i == 0)
    def _():
        _gather_rows_start(x_hbm, tok_ref, 0, xbuf_a, MOE_BM, sems.at[0])

    @pl.when(active)
    def _():
        e = blk_e_ref[i]
        prev = blk_e_ref[jnp.maximum(i - 1, 0)]

        @pl.when((i == 0) | (e != prev))
        def _():
            wgb[...] = wg_ref[0].astype(BF16)
            wub[...] = wu_ref[0].astype(BF16)
            wdb[...] = wd_ref[0].astype(BF16)

    def block(cur_buf, cur_sem, nxt_buf, nxt_sem):
        _gather_rows_wait(x_hbm, cur_buf, MOE_BM, cur_sem)
        nxt_row0 = jnp.minimum(i + 1, n_used - 1) * MOE_BM
        _gather_rows_start(x_hbm, tok_ref, nxt_row0, nxt_buf, MOE_BM, nxt_sem)
        x = cur_buf[...].astype(BF16)
        hg = _dot(x, wgb[...])
        hu = _dot(x, wub[...])
        h = hg * jax.nn.sigmoid(hg) * hu
        ys_ref[...] = _dot(h.astype(BF16), wdb[...])

        @pl.when(i == n_used - 1)
        def _():
            _gather_rows_wait(x_hbm, nxt_buf, MOE_BM, nxt_sem)

    @pl.when(active & (i % 2 == 0))
    def _():
        block(xbuf_a, sems.at[0], xbuf_b, sems.at[1])

    @pl.when(active & (i % 2 == 1))
    def _():
        block(xbuf_b, sems.at[1], xbuf_a, sems.at[0])

    @pl.when(i >= nused_ref[0])
    def _():
        ys_ref[...] = jnp.zeros(ys_ref.shape, F32)


def moe_experts(x, blk_e, n_used, tok_of_row, w_gate, w_up, w_down):
    d = x.shape[1]
    n_rows = tok_of_row.shape[0]
    n_blocks = n_rows // MOE_BM
    de = w_gate.shape[2]
    wmap = lambda i, be, nu, tk: (be[jnp.minimum(i, nu[0] - 1)], 0, 0)
    return pl.pallas_call(
        _expert_kernel,
        grid_spec=pltpu.PrefetchScalarGridSpec(
            num_scalar_prefetch=3,
            grid=(n_blocks,),
            in_specs=[pl.BlockSpec(memory_space=pl.ANY),
                      pl.BlockSpec((1, d, de), wmap), pl.BlockSpec((1, d, de), wmap), pl.BlockSpec((1, de, d), wmap)],
            out_specs=pl.BlockSpec((MOE_BM, d), lambda i, be, nu, tk: (i, 0)),
            scratch_shapes=[pltpu.VMEM((MOE_BM, d), F32), pltpu.VMEM((MOE_BM, d), F32), pltpu.SemaphoreType.DMA((2,)),
                            pltpu.VMEM((d, de), BF16), pltpu.VMEM((d, de), BF16), pltpu.VMEM((de, d), BF16)],
        ),
        out_shape=jax.ShapeDtypeStruct((n_rows, d), F32),
        compiler_params=_cparams(("arbitrary",)),
        name="moe_experts",
    )(blk_e, n_used, tok_of_row, x, w_gate, w_up, w_down)


def _shared_ffn_kernel(x_ref, wg_ref, wu_ref, wd_ref, o_ref):
    x = x_ref[...]
    hg = _dot(x, wg_ref[...])
    hu = _dot(x, wu_ref[...])
    h = hg * jax.nn.sigmoid(hg) * hu
    o_ref[...] = _dot(h.astype(BF16), wd_ref[...])


def shared_ffn(xb, wg, wu, wd, tm):
    t, d = xb.shape
    full = lambda a: pl.BlockSpec(a.shape, lambda i: (0, 0))
    row = pl.BlockSpec((tm, d), lambda i: (i, 0))
    return pl.pallas_call(
        _shared_ffn_kernel,
        grid=(t // tm,),
        in_specs=[row, full(wg), full(wu), full(wd)],
        out_specs=row,
        out_shape=jax.ShapeDtypeStruct((t, d), F32),
        compiler_params=_cparams(("parallel",)),
        name="shared_ffn",
    )(xb, wg, wu, wd)


def _combine_kernel(dest_ref, ys_hbm, gate_ref, x_ref, sh_ref, g_ref, b_ref, o_ref, ob_ref, buf, sems, *, tm):
    i = pl.program_id(0)
    n_asg = tm * TOP_K

    @pl.when(i == 0)
    def _():
        _gather_rows_start(ys_hbm, dest_ref, 0, buf.at[0], n_asg, sems.at[0], DMA_PRIORITIES)

    @pl.when(i + 1 < pl.num_programs(0))
    def _():
        nxt = (i + 1) % 2
        _gather_rows_start(ys_hbm, dest_ref, (i + 1) * n_asg, buf.at[nxt], n_asg, sems.at[nxt], DMA_PRIORITIES)

    cur = i % 2
    _gather_rows_wait(ys_hbm, buf.at[cur], n_asg, sems.at[cur])
    gate = gate_ref[...]
    f = sh_ref[...]
    for k in range(TOP_K):
        f = f + gate[:, k:k + 1] * buf[cur, k * tm:(k + 1) * tm, :]
    z = ALPHA * x_ref[...] + f
    inv_d = 1.0 / z.shape[1]
    mu = _row_sum(z) * inv_d
    dlt = z - mu
    var = _row_sum(dlt * dlt) * inv_d
    y = dlt * lax.rsqrt(var + LN_EPS) * g_ref[...] + b_ref[...]
    o_ref[...] = y
    ob_ref[...] = y.astype(BF16)


def moe_combine(ys, dest_flat, gates, x, shared, g, b, tm):
    t, d = x.shape
    row = pl.BlockSpec((tm, d), lambda i, dr: (i, 0))
    vec = pl.BlockSpec((1, d), lambda i, dr: (0, 0))
    return pl.pallas_call(
        functools.partial(_combine_kernel, tm=tm),
        grid_spec=pltpu.PrefetchScalarGridSpec(
            num_scalar_prefetch=1,
            grid=(t // tm,),
            in_specs=[pl.BlockSpec(memory_space=pl.ANY), pl.BlockSpec((tm, LANES), lambda i, dr: (i, 0)),
                      row, row, vec, vec],
            out_specs=[row, row],
            scratch_shapes=[pltpu.VMEM((2, TOP_K * tm, d), F32), pltpu.SemaphoreType.DMA((2,))],
        ),
        out_shape=[jax.ShapeDtypeStruct((t, d), F32), jax.ShapeDtypeStruct((t, d), BF16)],
        compiler_params=_cparams(("arbitrary",)),
        name="moe_combine",
    )(dest_flat, ys, gates, x, shared, g.reshape(1, d), b.reshape(1, d))


def moe_params(layer, w):
    p = {}
    p["w_router"] = _pad_cols(w["moe_w_router"][layer], LANES)
    p["b_router"] = _pad_cols(w["moe_b_router"][layer][None, :], LANES)
    stack = lambda a: a.reshape((-1,) + a.shape[2:])
    p["w_gate"], p["w_up"], p["w_down"] = stack(w["moe_w_gate"]), stack(w["moe_w_up"]), stack(w["moe_w_down"])
    p["e0"] = layer * N_EXPERTS
    p["ws_gate"] = w["moe_ws_gate"][layer].astype(BF16)
    p["ws_up"] = w["moe_ws_up"][layer].astype(BF16)
    p["ws_down"] = w["moe_ws_down"][layer].astype(BF16)
    return p


def moe_layer(y, yb, p, ln_g, ln_b, *, tm, tm_dma):
    t = y.shape[0]
    eidx, gates, rank, cnt = moe_router(y, p["w_router"], p["b_router"], tm)
    counts = cnt[0, :N_EXPERTS].astype(I32)
    padded = (counts + MOE_BM - 1) // MOE_BM * MOE_BM
    pad_end = jnp.cumsum(padded)
    pad_start = pad_end - padded
    e8 = eidx[:, :TOP_K]
    dest = pad_start[e8] + rank[:, :TOP_K]
    n_blocks = -(-(t * TOP_K + N_EXPERTS * (MOE_BM - 1)) // MOE_BM)
    n_rows = n_blocks * MOE_BM
    blk_start = jnp.arange(n_blocks, dtype=I32) * MOE_BM
    blk_e = jnp.sum((pad_end[None, :] <= blk_start[:, None]).astype(I32), axis=1)
    blk_e = jnp.minimum(blk_e, N_EXPERTS - 1) + p["e0"]
    n_used = (pad_end[-1:] // MOE_BM).astype(I32)
    tok = jnp.broadcast_to(jnp.arange(t, dtype=I32)[:, None], (t, TOP_K))
    tok_of_row = jnp.zeros((n_rows,), I32).at[dest.reshape(-1)].set(tok.reshape(-1))
    ys = moe_experts(y, blk_e, n_used, tok_of_row, p["w_gate"], p["w_up"], p["w_down"])
    shared = shared_ffn(yb, p["ws_gate"], p["ws_up"], p["ws_down"], tm)
    dest_tiles = dest.reshape(t // tm_dma, tm_dma, TOP_K).transpose(0, 2, 1).reshape(-1)
    return moe_combine(ys, dest_tiles, gates, y, shared, ln_g, ln_b, tm_dma)


C_QKV = (H_C + 2 * KVH_C) * DH_C
C_QI = H_IDX * D_IDX
IDX_SCALE = (H_IDX ** -0.5) * (D_IDX ** -0.5)


def _rot_half64(x, lane):
    return jnp.where(lane % D_IDX < D_IDX // 2, pltpu.roll(x, LANES - D_IDX // 2, 1), pltpu.roll(x, D_IDX // 2, 1))


def _c_post_kernel(qkv_ref, qi_ref, kw_ref, c128_ref, s128_ref, c64_ref, s64_ref, lng_ref, lnb_ref,
                   qb_o, kf_o, kb_o, vb_o, qib_o, kwo_o):
    tm = qkv_ref.shape[0]
    c128, s128 = c128_ref[...], s128_ref[...]
    c64, s64 = c64_ref[...], s64_ref[...]
    lane = lax.broadcasted_iota(I32, (tm, LANES), 1)
    for h in range(H_C + KVH_C):
        x = qkv_ref[:, h * DH_C:(h + 1) * DH_C]
        y = x * c128 + pltpu.roll(x, DH_C // 2, 1) * s128
        if h < H_C:
            qb_o[:, h * DH_C:(h + 1) * DH_C] = (y * (DH_C ** -0.5)).astype(BF16)
        else:
            j = h - H_C
            kf_o[:, j * DH_C:(j + 1) * DH_C] = y
            kb_o[:, j * DH_C:(j + 1) * DH_C] = y.astype(BF16)
    vb_o[...] = qkv_ref[:, (H_C + KVH_C) * DH_C:].astype(BF16)
    for j in range(C_QI // LANES):
        x = qi_ref[:, j * LANES:(j + 1) * LANES]
        qib_o[:, j * LANES:(j + 1) * LANES] = (x * c64 + _rot_half64(x, lane) * s64).astype(BF16)
    x = kw_ref[...]
    is_k = lane < D_IDX
    mu = jnp.sum(jnp.where(is_k, x, 0.0), axis=-1, keepdims=True) * (1.0 / D_IDX)
    d = jnp.where(is_k, x - mu, 0.0)
    var = jnp.sum(d * d, axis=-1, keepdims=True) * (1.0 / D_IDX)
    y = d * lax.rsqrt(var + LN_EPS) * lng_ref[...] + lnb_ref[...]
    y = y * c64 + _rot_half64(y, lane) * s64
    kwo_o[...] = jnp.where(is_k, y, x * IDX_SCALE)


def c_post(qkv, qi, kw, tabs, lng, lnb, tm):
    t = qkv.shape[0]
    row = lambda wd: pl.BlockSpec((tm, wd), lambda i: (i, 0))
    vec = pl.BlockSpec((1, LANES), lambda i: (0, 0))
    sh = lambda wd, dt: jax.ShapeDtypeStruct((t, wd), dt)
    return pl.pallas_call(
        _c_post_kernel,
        grid=(t // tm,),
        in_specs=[row(C_QKV), row(C_QI), row(LANES), row(LANES), row(LANES), row(LANES), row(LANES), vec, vec],
        out_specs=[row(H_C * DH_C), row(KVH_C * DH_C), row(KVH_C * DH_C), row(KVH_C * DH_C), row(C_QI), row(LANES)],
        out_shape=[sh(H_C * DH_C, BF16), sh(KVH_C * DH_C, F32), sh(KVH_C * DH_C, BF16), sh(KVH_C * DH_C, BF16),
                   sh(C_QI, BF16), sh(LANES, F32)],
        compiler_params=_cparams(("parallel",)),
        name="c_post",
    )(qkv, qi, kw, *tabs, lng, lnb)


def _rope_tables(pos):
    def tab(d):
        inv = ROPE_THETA ** (-jnp.arange(0, d, 2, dtype=F32) / d)
        ang = pos.astype(F32)[:, None] * inv[None, :]
        c, s = jnp.cos(ang), jnp.sin(ang)
        return jnp.concatenate([c, c], axis=1), jnp.concatenate([-s, s], axis=1)

    c128, s128 = tab(DH_C)
    c64, s64 = tab(D_IDX)
    return c128, s128, jnp.tile(c64, (1, 2)), jnp.tile(s64, (1, 2))


def _index_heads(qi, wi, kb):
    acc = None
    for h in range(H_IDX):
        d = lax.dot_general(qi[:, h * D_IDX:(h + 1) * D_IDX], kb, (((1,), (1,)), ((), ())),
                            preferred_element_type=F32)
        term = wi[:, D_IDX + h:D_IDX + h + 1] * jnp.maximum(d, 0.0)
        acc = term if acc is None else acc + term
    return acc + 0.0


SROWS = 16


def _topk_mask(sc_ref, utri_ref, o_ref, key_scr, *, topk, chunk, bits_per_step=1):
    rows, s_len = sc_ref.shape
    bits = lax.bitcast_convert_type(sc_ref[...], I32)
    key_scr[...] = bits ^ ((bits >> 31) & jnp.int32(0x7FFFFFFF))
    sign = jnp.int32(-2 ** 31)
    kf = jnp.float32(topk)

    def enough(cand):
        return _row_sum(jnp.where(key_scr[...] >= (cand ^ sign), 1.0, 0.0)) >= kf

    def bisect(it, t_u):
        low = (32 - bits_per_step) - bits_per_step * it
        for pattern in range(1, 2 ** bits_per_step):
            cand = t_u | (jnp.int32(pattern) << low)
            best = jnp.where(enough(cand), cand, t_u if pattern == 1 else best)
        return best

    t_u = lax.fori_loop(0, 32 // bits_per_step, bisect, jnp.zeros((rows, 1), I32))
    thr = t_u ^ sign
    need = kf - _row_sum(jnp.where(key_scr[...] > thr, 1.0, 0.0))
    utri = utri_ref[...]
    run = jnp.zeros((rows, 1), F32)
    for c in range(s_len // chunk):
        cs = slice(c * chunk, (c + 1) * chunk)
        keyc = key_scr[:, cs]
        eqc = jnp.where(keyc == thr, 1.0, 0.0)
        before = _dot(eqc.astype(BF16), utri) + run
        pick = (keyc > thr) | ((keyc == thr) & (before < need))
        o_ref[:, cs] = jnp.where(pick & (sc_ref[:, cs] > -jnp.inf), 1.0, 0.0).astype(o_ref.dtype)
        run = run + _row_sum(eqc)
    if o_ref.shape[1] > s_len:
        o_ref[:, s_len:] = jnp.zeros((rows, o_ref.shape[1] - s_len), o_ref.dtype)


def _strict_upper(chunk):
    return (jnp.arange(chunk)[:, None] < jnp.arange(chunk)[None, :]).astype(BF16)


SELECT_CLASSES = 4


def _index_select_kernel(qi_ref, wi_ref, keys_ref, utri_ref, o_ref, sc_scr, key_scr, *, q0, topk, chunk):
    qblk, width = sc_scr.shape
    kb = keys_ref[:width, :D_IDX].astype(BF16)
    sc = _index_heads(qi_ref[...], wi_ref[...], kb)
    row = lax.broadcasted_iota(I32, (qblk, width), 0) + (q0 + pl.program_id(1) * qblk)
    col = lax.broadcasted_iota(I32, (qblk, width), 1)
    sc_scr[...] = jnp.where(col <= row, sc, -jnp.inf)
    _topk_mask(sc_scr, utri_ref, o_ref, key_scr, topk=topk, chunk=chunk)


def index_select_causal(qib, kwo, topk, n_seq, seq_len, qblk, chunk):
    n_cls = SELECT_CLASSES if seq_len % (SELECT_CLASSES * max(chunk, qblk)) == 0 else 1
    cw = seq_len // n_cls
    utri = _strict_upper(chunk)
    parts = []
    for c in range(n_cls):
        width = (c + 1) * cw
        qmap = functools.partial(lambda b, i, c: (b * (seq_len // qblk) + c * (cw // qblk) + i, 0), c=c)
        parts.append(pl.pallas_call(
            functools.partial(_index_select_kernel, q0=c * cw, topk=topk, chunk=chunk),
            grid=(n_seq, cw // qblk),
            in_specs=[pl.BlockSpec((qblk, C_QI), qmap), pl.BlockSpec((qblk, LANES), qmap),
                      pl.BlockSpec((seq_len, LANES), lambda b, i: (b, 0)),
                      pl.BlockSpec((chunk, chunk), lambda b, i: (0, 0))],
            out_specs=pl.BlockSpec((qblk, seq_len), lambda b, i: (b * (cw // qblk) + i, 0)),
            out_shape=jax.ShapeDtypeStruct((n_seq * cw, seq_len), BF16),
            scratch_shapes=[pltpu.VMEM((qblk, width), F32), pltpu.VMEM((qblk, width), I32)],
            compiler_params=_cparams(("parallel", "parallel")),
            name="index_select_%d" % c,
        )(qib, kwo, kwo, utri).reshape(n_seq, cw, seq_len))
    return jnp.concatenate(parts, axis=1).reshape(n_seq * seq_len, seq_len)


NEG_BIG = -1e30


def _attn_init(m_scr, l_scr, acc_scr):
    m_scr[...] = jnp.full(m_scr.shape, NEG_BIG, F32)
    l_scr[...] = jnp.zeros(l_scr.shape, F32)
    acc_scr[...] = jnp.zeros(acc_scr.shape, F32)


def _attn_block(q, k, v, mask, m_scr, l_scr, acc_scr):
    bias = jnp.concatenate([(mask.astype(F32) - 1.0) * (-NEG_BIG)] * GROUP_C, axis=0)
    heads = range(KVH_C)
    ss = []
    for kh in heads:
        qh = jnp.concatenate([q[:, (kh * GROUP_C + g) * DH_C:(kh * GROUP_C + g + 1) * DH_C]
                              for g in range(GROUP_C)], axis=0)
        ks = k[:, kh * DH_C:(kh + 1) * DH_C]
        ss.append(lax.dot_general(qh, ks, (((1,), (1,)), ((), ())), preferred_element_type=F32) + bias)
    n_tiles = ss[0].shape[1] // LANES
    m_old = [m_scr[kh] for kh in heads]
    m_new = [jnp.maximum(m_old[kh], jnp.broadcast_to(_row_max(ss[kh]), m_old[kh].shape)) for kh in heads]
    ps = [[jnp.exp(ss[kh][:, i * LANES:(i + 1) * LANES] - m_new[kh]) for i in range(n_tiles)] for kh in heads]
    for kh in heads:
        alpha = jnp.exp(m_old[kh] - m_new[kh])
        l_scr[kh] = alpha * l_scr[kh] + functools.reduce(jnp.add, ps[kh])
        p = jnp.concatenate(ps[kh], axis=1).astype(BF16)
        acc_scr[kh] = alpha * acc_scr[kh] + _dot(p, v[:, kh * DH_C:(kh + 1) * DH_C])
        m_scr[kh] = m_new[kh]


def _attn_finish(o_ref, nq, l_scr, acc_scr, lead=None):
    for kh in range(KVH_C):
        out = acc_scr[kh] / jnp.sum(l_scr[kh], axis=-1, keepdims=True)
        for g in range(GROUP_C):
            hs = slice((kh * GROUP_C + g) * DH_C, (kh * GROUP_C + g + 1) * DH_C)
            val = out[g * nq:(g + 1) * nq, :].astype(o_ref.dtype)
            if lead is None:
                o_ref[:, hs] = val
            else:
                o_ref[lead, :, hs] = val


def _attn_prompt_kernel(q_ref, k_ref, v_ref, mask_ref, o_ref, m_scr, l_scr, acc_scr, *, qblk, kblk):
    qb = pl.program_id(1)
    kb = pl.program_id(2)

    @pl.when(kb == 0)
    def _():
        _attn_init(m_scr, l_scr, acc_scr)

    @pl.when(kb * kblk <= qb * qblk + qblk - 1)
    def _():
        _attn_block(q_ref[...], k_ref[...], v_ref[...], mask_ref[...], m_scr, l_scr, acc_scr)

    @pl.when(kb == pl.num_programs(2) - 1)
    def _():
        _attn_finish(o_ref, qblk, l_scr, acc_scr)


def attn_prompt(qb_, kb_, vb_, mask, n_p, l_p, qblk, kblk):
    nq, nk = l_p // qblk, l_p // kblk
    last = lambda q, k: jnp.minimum(k, (q * qblk + qblk - 1) // kblk)
    rows = GROUP_C * qblk
    return pl.pallas_call(
        functools.partial(_attn_prompt_kernel, qblk=qblk, kblk=kblk),
        grid=(n_p, nq, nk),
        in_specs=[pl.BlockSpec((qblk, H_C * DH_C), lambda b, q, k: (b * nq + q, 0)),
                  pl.BlockSpec((kblk, KVH_C * DH_C), lambda b, q, k: (b * nk + last(q, k), 0)),
                  pl.BlockSpec((kblk, KVH_C * DH_C), lambda b, q, k: (b * nk + last(q, k), 0)),
                  pl.BlockSpec((qblk, kblk), lambda b, q, k: (b * nq + q, last(q, k)))],
        out_specs=pl.BlockSpec((qblk, H_C * DH_C), lambda b, q, k: (b * nq + q, 0)),
        out_shape=jax.ShapeDtypeStruct((n_p * l_p, H_C * DH_C), BF16),
        scratch_shapes=[pltpu.VMEM((KVH_C, rows, LANES), F32), pltpu.VMEM((KVH_C, rows, LANES), F32),
                        pltpu.VMEM((KVH_C, rows, DH_C), F32)],
        compiler_params=_cparams(("parallel", "parallel", "arbitrary")),
        name="attn_prompt",
    )(qb_, kb_, vb_, mask)


def _dsa_sample_kernel(pt_ref, qi_ref, kw_ref, q_ref, kn_ref, vn_ref, utri_ref, kidx_hbm, k_hbm, v_hbm, o_ref,
                       kibuf, kbuf, vbuf, sems, sc_scr, key_scr, mask_scr, *, n_pages, page, l_s, topk, pool0):
    b = pl.program_id(0)
    past = n_pages * page
    s_len = past + page

    def page_copies(bb, slot, pg):
        phys = pt_ref[bb * n_pages + pg] + pool0
        rows = pl.ds(pg * page, page)
        cps = [pltpu.make_async_copy(kidx_hbm.at[phys], kibuf.at[slot, rows, :], sems.at[slot, 0])]
        for kh in range(KVH_C):
            cps.append(pltpu.make_async_copy(k_hbm.at[phys, :, kh, :], kbuf.at[slot, kh, rows, :], sems.at[slot, 1]))
            cps.append(pltpu.make_async_copy(v_hbm.at[phys, :, kh, :], vbuf.at[slot, kh, rows, :], sems.at[slot, 2]))
        return cps

    def fetch(bb, slot):
        for pg in range(n_pages):
            for cp in page_copies(bb, slot, pg):
                cp.start()

    @pl.when(b == 0)
    def _():
        kibuf[:, past:, :] = jnp.zeros((2, page, D_IDX), F32)
        kbuf[:, :, past:, :] = jnp.zeros((2, KVH_C, page, DH_C), F32)
        vbuf[:, :, past:, :] = jnp.zeros((2, KVH_C, page, DH_C), F32)
        fetch(0, 0)

    @pl.when(b + 1 < pl.num_programs(0))
    def _():
        fetch(b + 1, (b + 1) % 2)

    cur = b % 2
    for pg in range(n_pages):
        for cp in page_copies(b, cur, pg):
            cp.wait()
    kw = kw_ref[0]
    kibuf[cur, past:past + SROWS, :] = kw[:, :D_IDX]
    for kh in range(KVH_C):
        hs = slice(kh * DH_C, (kh + 1) * DH_C)
        kbuf[cur, kh, past:past + SROWS, :] = kn_ref[0][:, hs]
        vbuf[cur, kh, past:past + SROWS, :] = vn_ref[0][:, hs]

    dots = lax.dot_general(qi_ref[0], kibuf[cur].astype(BF16), (((1,), (1,)), ((), ())),
                           preferred_element_type=F32)
    sc = None
    for h in range(H_IDX):
        term = kw[:, D_IDX + h:D_IDX + h + 1] * jnp.maximum(dots[h * SROWS:(h + 1) * SROWS, :], 0.0)
        sc = term if sc is None else sc + term
    row = lax.broadcasted_iota(I32, (SROWS, s_len), 0)
    col = lax.broadcasted_iota(I32, (SROWS, s_len), 1)
    visible = (col < past) | ((col - past <= row) & (col - past < l_s))
    sc_scr[...] = jnp.where(visible, sc + 0.0, -jnp.inf)
    _topk_mask(sc_scr, utri_ref, mask_scr, key_scr, topk=topk, chunk=page, bits_per_step=2)

    bias = jnp.concatenate([(mask_scr[...] - 1.0) * (-NEG_BIG)] * GROUP_C, axis=0)
    q = q_ref[0]
    for kh in range(KVH_C):
        qh = jnp.concatenate([q[:, (kh * GROUP_C + g) * DH_C:(kh * GROUP_C + g + 1) * DH_C]
                              for g in range(GROUP_C)], axis=0)
        s = lax.dot_general(qh, kbuf[cur, kh].astype(BF16), (((1,), (1,)), ((), ())),
                            preferred_element_type=F32) + bias
        p = jnp.exp(s - _row_max(s))
        out = _dot(p.astype(BF16), vbuf[cur, kh].astype(BF16)) / _row_sum(p)
        for g in range(GROUP_C):
            o_ref[0, :, (kh * GROUP_C + g) * DH_C:(kh * GROUP_C + g + 1) * DH_C] = (
                out[g * SROWS:(g + 1) * SROWS, :].astype(o_ref.dtype))


def dsa_sample(page_table, qi_hm, kw3, q3, knew3, vnew3, cache_kidx3, cache_k4, cache_v4, *, pool0, l_s, topk):
    n_s, n_pages = page_table.shape
    page = cache_k4.shape[1]
    kvw = KVH_C * DH_C
    s_len = (n_pages + 1) * page
    own = lambda wd: pl.BlockSpec((1, SROWS, wd), lambda b, pt: (b, 0, 0))
    hbm = pl.BlockSpec(memory_space=pl.ANY)
    return pl.pallas_call(
        functools.partial(_dsa_sample_kernel, n_pages=n_pages, page=page, l_s=l_s, topk=topk, pool0=pool0),
        grid_spec=pltpu.PrefetchScalarGridSpec(
            num_scalar_prefetch=1,
            grid=(n_s,),
            in_specs=[pl.BlockSpec((1, H_IDX * SROWS, D_IDX), lambda b, pt: (b, 0, 0)),
                      own(LANES), own(H_C * DH_C), own(kvw), own(kvw),
                      pl.BlockSpec((page, page), lambda b, pt: (0, 0)), hbm, hbm, hbm],
            out_specs=own(H_C * DH_C),
            scratch_shapes=[pltpu.VMEM((2, s_len, D_IDX), F32), pltpu.VMEM((2, KVH_C, s_len, DH_C), F32),
                            pltpu.VMEM((2, KVH_C, s_len, DH_C), F32), pltpu.SemaphoreType.DMA((2, 3)),
                            pltpu.VMEM((SROWS, s_len), F32), pltpu.VMEM((SROWS, s_len), I32),
                            pltpu.VMEM((SROWS, s_len), F32)],
        ),
        out_shape=jax.ShapeDtypeStruct((n_s, SROWS, H_C * DH_C), BF16),
        compiler_params=_cparams(("arbitrary",)),
        name="dsa_sample",
    )(page_table.reshape(-1), qi_hm, kw3, q3, knew3, vnew3, _strict_upper(page), cache_kidx3, cache_k4, cache_v4)


def c_params(li, w):
    p = {}
    w_in = w["c_w_in"][li]
    p["w_qkv"] = w_in[:, :C_QKV].astype(BF16)
    p["w_qi"] = w_in[:, C_QKV:C_QKV + C_QI].astype(BF16)
    p["w_kw"] = _pad_cols(w_in[:, C_QKV + C_QI:], LANES).astype(BF16)
    p["ln_g"] = _pad_cols(w["c_kidx_ln_g"][li][None, :], LANES)
    p["ln_b"] = _pad_cols(w["c_kidx_ln_b"][li][None, :], LANES)
    p["w_out"] = w["c_w_out"][li].astype(BF16)
    return p


def _pad_sample_rows(a, n_s, l_s):
    a3 = a.reshape(n_s, l_s, a.shape[1])
    return jnp.pad(a3, ((0, 0), (0, SROWS - l_s), (0, 0)))


def c_layer(xb, geo, p, li, cache_k, cache_v, cache_kidx, page_table, *, tm, qblk, kblk):
    n_p, l_p, n_s, l_s = geo
    tp = n_p * l_p
    n_pool, page = cache_k.shape[1], cache_k.shape[2]
    n_pages = page_table.shape[1]
    past = n_pages * page
    qkv = matmul([xb], [p["w_qkv"]], tm, 1024, name="c_in_qkv")
    qi = matmul([xb], [p["w_qi"]], tm, C_QI, name="c_in_qi")
    kw = matmul([xb], [p["w_kw"]], tm, LANES, name="c_in_kw")
    pos = jnp.concatenate([jnp.tile(jnp.arange(l_p, dtype=I32), n_p),
                           jnp.tile(past + jnp.arange(l_s, dtype=I32), n_s)])
    q_b, k_f, k_b, v_b, qi_b, kwo = c_post(qkv, qi, kw, _rope_tables(pos), p["ln_g"], p["ln_b"], tm)
    v_f = qkv[:, (H_C + KVH_C) * DH_C:]
    mask_p = index_select_causal(qi_b, kwo, min(TOPK_MAX, l_p // 4), n_p, l_p, qblk, 256)
    o_p = attn_prompt(q_b, k_b, v_b, mask_p, n_p, l_p, qblk, kblk)
    pad = lambda a: _pad_sample_rows(a[tp:], n_s, l_s)
    qi_hm = pad(qi_b).reshape(n_s, SROWS, H_IDX, D_IDX).transpose(0, 2, 1, 3).reshape(n_s, H_IDX * SROWS, D_IDX)
    pool = lambda a: a.reshape((-1,) + a.shape[2:])
    o_s3 = dsa_sample(page_table, qi_hm, pad(kwo), pad(q_b), pad(k_f), pad(v_f),
                      pool(cache_kidx), pool(cache_k), pool(cache_v), pool0=li * n_pool, l_s=l_s,
                      topk=min(TOPK_MAX, (past + l_s) // 4))
    o = jnp.concatenate([o_p, o_s3[:, :l_s].reshape(n_s * l_s, -1)], axis=0)
    h = matmul([o], [p["w_out"]], tm, 1024, name="c_out")
    heads = lambda a, n, l: a.reshape(n, l, KVH_C, DH_C)
    new_p = (heads(k_f[:tp], n_p, l_p), heads(v_f[:tp], n_p, l_p), kwo[:tp, :D_IDX].reshape(n_p, l_p, D_IDX))
    new_s = (heads(k_f[tp:], n_s, l_s), heads(v_f[tp:], n_s, l_s), kwo[tp:, :D_IDX].reshape(n_s, l_s, D_IDX))
    return h, new_p, new_s


TM = 256
BLK = 128
TM_DMA = 64
QBLK = 128
KBLK = 512


def kernel(x_prompt, x_sample, state_shift, state_wkv, state_ssm_re, state_ssm_im, cache_k, cache_v, cache_kidx, page_table, ab_w_in, ab_mu, rwkv_w0, rwkv_w2, rwkv_a0, rwkv_a2, rwkv_g2, rwkv_k_k, rwkv_k_a, rwkv_r_k, rwkv_ln_g, rwkv_ln_b, s5_lam_re, s5_lam_im, s5_log_dt, s5_b_re, s5_b_im, s5_c_re, s5_c_im, s5_d, s5_w_glu, s5_b_glu, ab_w_out, c_w_in, c_kidx_ln_g, c_kidx_ln_b, c_w_out, ln_mix_g, ln_mix_b, ln_ffn_g, ln_ffn_b, moe_w_router, moe_b_router, moe_w_gate, moe_w_up, moe_w_down, moe_ws_gate, moe_ws_up, moe_ws_down):
    w = dict(locals())
    n_p, l_p, _ = x_prompt.shape
    n_s, l_s, _ = x_sample.shape
    geo = (n_p, l_p, n_s, l_s)
    tp = n_p * l_p
    y = jnp.concatenate([x_prompt.reshape(-1, D_MODEL), x_sample.reshape(-1, D_MODEL)], axis=0)
    yb = y.astype(BF16)
    outs = {k: [] for k in ("kp", "vp", "kip", "ks", "vs", "kis", "shp", "shs", "wkvp", "wkvs",
                            "srp", "sip", "srs", "sis")}
    for layer in range(DEPTH):
        li = layer // 2
        if layer % 2 == 0:
            wkv_all = state_wkv.reshape((-1,) + state_wkv.shape[2:])
            h, shift, wkv, ssm_p, ssm_s = ab_layer(yb, geo, ab_params(li, w, l_s), state_shift[li], wkv_all, li * n_s,
                                                   state_ssm_re[li], state_ssm_im[li], tm=TM, blk=BLK)
            for k, v in zip(("shp", "shs", "wkvp", "wkvs", "srp", "sip", "srs", "sis"),
                            (*shift, *wkv, *ssm_p, *ssm_s)):
                outs[k].append(v)
        else:
            h, new_p, new_s = c_layer(yb, geo, c_params(li, w), li, cache_k, cache_v, cache_kidx,
                                      page_table, tm=TM, qblk=QBLK, kblk=KBLK)
            for k, v in zip(("kp", "vp", "kip", "ks", "vs", "kis"), (*new_p, *new_s)):
                outs[k].append(v)
        y, yb = add_ln(y, h, ln_mix_g[layer], ln_mix_b[layer], TM)
        y, yb = moe_layer(y, yb, moe_params(layer, w), ln_ffn_g[layer], ln_ffn_b[layer], tm=TM, tm_dma=TM_DMA)
    st = lambda k: jnp.stack(outs[k])
    return (y[:tp].reshape(n_p, l_p, D_MODEL), y[tp:].reshape(n_s, l_s, D_MODEL),
            st("kp"), st("vp"), st("kip"), st("ks"), st("vs"), st("kis"),
            st("shp"), st("shs"), st("wkvp"), st("wkvs"), st("srp"), st("sip"), st("srs"), st("sis"))
```
